```python
import math
import jax
import jax.numpy as jnp
from jax import lax
import numpy as np


D_MODEL = 2048
BATCH = 1
SEQ = 8192
DEPTH = 4

N_MIXERS = 3
RMS_EPS = 1e-6

POOL_WINDOWS = (2, 4, 8, 16)
POOL_GROUP = D_MODEL // len(POOL_WINDOWS)

SSD_EXPAND = 2
SSD_D_INNER = SSD_EXPAND * D_MODEL
SSD_HEAD_DIM = 64
SSD_HEADS = SSD_D_INNER // SSD_HEAD_DIM
SSD_STATE = 128
SSD_GROUPS = 8
SSD_HEADS_PER_GROUP = SSD_HEADS // SSD_GROUPS
SSD_CONV = 4
SSD_CHUNK = 128
SSD_CONV_DIM = SSD_D_INNER + 2 * SSD_GROUPS * SSD_STATE
SSD_IN_DIM = SSD_D_INNER + SSD_CONV_DIM + SSD_HEADS

FOX_HEAD_DIM = 128
FOX_HEADS = D_MODEL // FOX_HEAD_DIM
FOX_Q_BLOCK = 128
FOX_IN_DIM = 3 * D_MODEL + FOX_HEADS

MOE_GROUPS = 4
MOE_EXPERTS_PER_GROUP = 8
MOE_EXPERTS = MOE_GROUPS * MOE_EXPERTS_PER_GROUP
MOE_TOP_K = 2
MOE_HIDDEN = D_MODEL // 8
MOE_ROW_BLOCK = 128

N_POOL_LAYERS = (DEPTH + 2) // 3
N_SSD_LAYERS = (DEPTH + 1) // 3
N_FOX_LAYERS = DEPTH // 3

kernel_name = 'hybrid_pool_ssd_fox_hmoe'


def rmsnorm(x, g):
    xf = x.astype(jnp.float32)
    y = xf * lax.rsqrt(jnp.mean(xf * xf, axis=-1, keepdims=True) + RMS_EPS)
    return (y * g.astype(jnp.float32)).astype(x.dtype)


def pool_mixer(xn, w_pool, scale):
    bsz, s, _ = xn.shape
    xf = xn.astype(jnp.float32)
    cs = jnp.cumsum(xf, axis=1)
    cs0 = jnp.concatenate([jnp.zeros_like(cs[:, :1]), cs], axis=1)
    outs = []
    for g, w in enumerate(POOL_WINDOWS):
        sl = slice(g * POOL_GROUP, (g + 1) * POOL_GROUP)
        c = cs0[:, :, sl]
        upper = c[:, 1:]
        lower = jnp.concatenate([jnp.zeros_like(c[:, :w - 1]), c[:, :s + 1 - w]], axis=1)
        count = jnp.minimum(jnp.arange(1, s + 1), w).astype(jnp.float32)
        pooled = (upper - lower) / count[None, :, None] - xf[:, :, sl]
        outs.append(jnp.einsum('bsc,cd->bsd', pooled.astype(xn.dtype), w_pool[g]))
    return jnp.concatenate(outs, axis=-1) * scale


def ssd_scan(x, dt, a, b_mat, c_mat):
    bsz, s = x.shape[:2]
    nc, L = s // SSD_CHUNK, SSD_CHUNK
    G, K = SSD_GROUPS, SSD_HEADS_PER_GROUP
    xc = x.reshape(bsz, nc, L, G, K, SSD_HEAD_DIM)
    dtc = dt.reshape(bsz, nc, L, G, K)
    bc = b_mat.reshape(bsz, nc, L, G, SSD_STATE)
    cc = c_mat.reshape(bsz, nc, L, G, SSD_STATE)
    xdt = xc * dtc[..., None]
    a_cs = jnp.cumsum(dtc * a.reshape(G, K), axis=2)
    causal = jnp.tril(jnp.ones((L, L), dtype=bool))[None, None, :, :, None, None]
    seg = a_cs[:, :, :, None] - a_cs[:, :, None, :]
    decay = jnp.exp(jnp.where(causal, seg, -jnp.inf))
    cb = jnp.einsum('bclgn,bcsgn->bclsg', cc, bc)
    y_diag = jnp.einsum('bclsg,bclsgk,bcsgkp->bclgkp', cb, decay, xdt)
    decay_to_end = jnp.exp(a_cs[:, :, -1:] - a_cs)
    chunk_states = jnp.einsum('bclgn,bclgk,bclgkp->bcgkpn', bc, decay_to_end, xdt)
    chunk_decay = jnp.exp(a_cs[:, :, -1])

    def step(h, inp):
        st, dec = inp
        return dec[..., None, None] * h + st, h

    h0 = jnp.zeros((bsz, G, K, SSD_HEAD_DIM, SSD_STATE), x.dtype)
    _, h_in = lax.scan(step, h0, (jnp.moveaxis(chunk_states, 1, 0), jnp.moveaxis(chunk_decay, 1, 0)))
    h_in = jnp.moveaxis(h_in, 0, 1)
    y_off = jnp.einsum('bclgn,bcgkpn,bclgk->bclgkp', cc, h_in, jnp.exp(a_cs))
    return (y_diag + y_off).reshape(bsz, s, SSD_HEADS, SSD_HEAD_DIM)


def ssd_mixer(xn, in_w, conv_w, conv_b, dt_bias, a_log, d_skip, norm_w, out_w):
    bsz, s, _ = xn.shape
    f32 = jnp.float32
    proj = xn @ in_w
    z = proj[..., :SSD_D_INNER]
    xbc = proj[..., SSD_D_INNER:SSD_D_INNER + SSD_CONV_DIM]
    dt_raw = proj[..., SSD_D_INNER + SSD_CONV_DIM:]
    xbc = lax.conv_general_dilated(xbc, conv_w[:, None, :], window_strides=(1,),
                                   padding=[(SSD_CONV - 1, 0)],
                                   dimension_numbers=('NWC', 'WIO', 'NWC'),
                                   feature_group_count=SSD_CONV_DIM) + conv_b
    xbc = jax.nn.silu(xbc).astype(f32)
    gn = SSD_GROUPS * SSD_STATE
    xs = xbc[..., :SSD_D_INNER].reshape(bsz, s, SSD_HEADS, SSD_HEAD_DIM)
    b_mat = xbc[..., SSD_D_INNER:SSD_D_INNER + gn].reshape(bsz, s, SSD_GROUPS, SSD_STATE)
    c_mat = xbc[..., SSD_D_INNER + gn:].reshape(bsz, s, SSD_GROUPS, SSD_STATE)
    dt = jax.nn.softplus(dt_raw.astype(f32) + dt_bias.astype(f32))
    a = -jnp.exp(a_log.astype(f32))
    y = ssd_scan(xs, dt, a, b_mat, c_mat) + d_skip.astype(f32)[:, None] * xs
    y = y.reshape(bsz, s, SSD_D_INNER) * jax.nn.silu(z.astype(f32))
    yg = y.reshape(bsz, s, SSD_GROUPS, SSD_D_INNER // SSD_GROUPS)
    yg = yg * lax.rsqrt(jnp.mean(yg * yg, axis=-1, keepdims=True) + RMS_EPS)
    y = yg.reshape(bsz, s, SSD_D_INNER) * norm_w.astype(f32)
    return y.astype(xn.dtype) @ out_w


def fox_mixer(xn, in_w, f_bias, out_w):
    bsz, s, _ = xn.shape
    f32 = jnp.float32
    proj = xn @ in_w
    q = proj[..., :D_MODEL].reshape(bsz, s, FOX_HEADS, FOX_HEAD_DIM)
    k = proj[..., D_MODEL:2 * D_MODEL].reshape(bsz, s, FOX_HEADS, FOX_HEAD_DIM)
    v = proj[..., 2 * D_MODEL:3 * D_MODEL].reshape(bsz, s, FOX_HEADS, FOX_HEAD_DIM)
    log_f = jax.nn.log_sigmoid(proj[..., 3 * D_MODEL:].astype(f32) + f_bias.astype(f32))
    cum = jnp.cumsum(log_f, axis=1)
    cum_k = jnp.swapaxes(cum, 1, 2)
    nb = s // FOX_Q_BLOCK
    scale = FOX_HEAD_DIM ** -0.5
    key_pos = jnp.arange(s)

    def q_block(args):
        qb, cqb, start = args
        logits = jnp.einsum('bqhd,bkhd->bhqk', qb, k, preferred_element_type=f32) * scale
        logits = logits + jnp.swapaxes(cqb, 1, 2)[..., None] - cum_k[:, :, None, :]
        q_pos = start + jnp.arange(FOX_Q_BLOCK)
        logits = jnp.where(q_pos[:, None] >= key_pos[None, :], logits, -jnp.inf)
        p = jax.nn.softmax(logits, axis=-1)
        return jnp.einsum('bhqk,bkhd->bqhd', p.astype(v.dtype), v)

    qbs = jnp.swapaxes(q.reshape(bsz, nb, FOX_Q_BLOCK, FOX_HEADS, FOX_HEAD_DIM), 0, 1)
    cqbs = jnp.swapaxes(cum.reshape(bsz, nb, FOX_Q_BLOCK, FOX_HEADS), 0, 1)
    starts = jnp.arange(nb) * FOX_Q_BLOCK
    o = lax.map(q_block, (qbs, cqbs, starts))
    o = jnp.swapaxes(o, 0, 1).reshape(bsz, s, D_MODEL)
    return o @ out_w


def moe_ffn(xn, group_w, group_b, router_w, router_b, gate_w, up_w, down_w):
    bsz, s, d = xn.shape
    f32 = jnp.float32
    xt = xn.reshape(-1, d)
    n = xt.shape[0]
    grp_logits = (xt @ group_w).astype(f32) + group_b.astype(f32)
    g_sel = jnp.argmax(grp_logits, axis=-1)
    p_grp = jnp.take_along_axis(jax.nn.softmax(grp_logits, axis=-1), g_sel[:, None], axis=-1)
    exp_logits = ((xt @ router_w).astype(f32) + router_b.astype(f32)).reshape(n, MOE_GROUPS, MOE_EXPERTS_PER_GROUP)
    in_grp = jnp.take_along_axis(exp_logits, g_sel[:, None, None], axis=1)[:, 0]
    top_v, top_i = lax.top_k(in_grp, MOE_TOP_K)
    gates = p_grp * jax.nn.softmax(top_v, axis=-1)
    experts = g_sel[:, None].astype(jnp.int32) * MOE_EXPERTS_PER_GROUP + top_i.astype(jnp.int32)
    n_slots = n * MOE_TOP_K
    rb = MOE_ROW_BLOCK
    flat_e = experts.reshape(-1)
    flat_t = jnp.repeat(jnp.arange(n, dtype=jnp.int32), MOE_TOP_K)
    flat_g = gates.reshape(-1).astype(xt.dtype)
    order = jnp.argsort(flat_e)
    se, st, sg = flat_e[order], flat_t[order], flat_g[order]
    counts = jnp.bincount(flat_e, length=MOE_EXPERTS)
    padded = (counts + rb - 1) // rb * rb
    pad_end = jnp.cumsum(padded)
    pad_start = pad_end - padded
    cnt_start = jnp.cumsum(counts) - counts
    dest = pad_start[se] + jnp.arange(n_slots, dtype=jnp.int32) - cnt_start[se]
    n_rows = -(-(n_slots + MOE_EXPERTS * (rb - 1)) // rb) * rb
    n_blocks = n_rows // rb
    row_tok = jnp.full((n_rows,), n, jnp.int32).at[dest].set(st)
    row_gate = jnp.zeros((n_rows,), xt.dtype).at[dest].set(sg)
    block_exp = jnp.minimum(jnp.searchsorted(pad_end, jnp.arange(n_blocks, dtype=jnp.int32) * rb, side='right'),
                            MOE_EXPERTS - 1)
    x_pad = jnp.concatenate([xt, jnp.zeros((1, d), xt.dtype)], axis=0)

    def expert_block(args):
        tok, e = args
        xb = x_pad[tok]
        hb = jax.nn.silu(xb @ gate_w[e]) * (xb @ up_w[e])
        return hb @ down_w[e]

    out = lax.map(expert_block, (row_tok.reshape(n_blocks, rb), block_exp))
    out = out.reshape(n_rows, d) * row_gate[:, None]
    y = jnp.zeros((n + 1, d), xt.dtype).at[row_tok].add(out)[:n]
    return y.reshape(bsz, s, d)


def setup_inputs(seed: int = 0) -> dict:
    key = jax.random.key(seed)
    ks = jax.random.split(key, 24)
    f32 = jnp.float32

    def nrm(k, shape, scale):
        return jax.random.normal(k, shape, f32) * scale

    dt0 = jnp.exp(jax.random.uniform(ks[9], (N_SSD_LAYERS, SSD_HEADS), f32,
                                     minval=math.log(1e-3), maxval=math.log(1e-1)))
    return {
        'x': nrm(ks[0], (BATCH, SEQ, D_MODEL), 1.0),
        'norm_mix': 1.0 + nrm(ks[1], (DEPTH, D_MODEL), 0.02),
        'norm_ffn': 1.0 + nrm(ks[2], (DEPTH, D_MODEL), 0.02),
        'final_norm': 1.0 + nrm(ks[3], (D_MODEL,), 0.02),
        'pool_w': nrm(ks[4], (N_POOL_LAYERS, len(POOL_WINDOWS), POOL_GROUP, POOL_GROUP), POOL_GROUP ** -0.5),
        'pool_scale': 1.0 + nrm(ks[5], (N_POOL_LAYERS, D_MODEL), 0.1),
        'ssd_in_w': nrm(ks[6], (N_SSD_LAYERS, D_MODEL, SSD_IN_DIM), D_MODEL ** -0.5),
        'ssd_conv_w': nrm(ks[7], (N_SSD_LAYERS, SSD_CONV, SSD_CONV_DIM), SSD_CONV ** -0.5),
        'ssd_conv_b': nrm(ks[8], (N_SSD_LAYERS, SSD_CONV_DIM), 0.01),
        'ssd_dt_bias': dt0 + jnp.log(-jnp.expm1(-dt0)),
        'ssd_a_log': jnp.log(jax.random.uniform(ks[10], (N_SSD_LAYERS, SSD_HEADS), f32, minval=1.0, maxval=16.0)),
        'ssd_d_skip': 1.0 + nrm(ks[11], (N_SSD_LAYERS, SSD_HEADS), 0.1),
        'ssd_norm': 1.0 + nrm(ks[12], (N_SSD_LAYERS, SSD_D_INNER), 0.02),
        'ssd_out_w': nrm(ks[13], (N_SSD_LAYERS, SSD_D_INNER, D_MODEL), SSD_D_INNER ** -0.5),
        'fox_in_w': nrm(ks[14], (N_FOX_LAYERS, D_MODEL, FOX_IN_DIM), D_MODEL ** -0.5),
        'fox_f_bias': jax.random.uniform(ks[15], (N_FOX_LAYERS, FOX_HEADS), f32, minval=1.0, maxval=4.0),
        'fox_out_w': nrm(ks[16], (N_FOX_LAYERS, D_MODEL, D_MODEL), D_MODEL ** -0.5),
        'moe_group_w': nrm(ks[17], (DEPTH, D_MODEL, MOE_GROUPS), D_MODEL ** -0.5),
        'moe_group_b': nrm(ks[18], (DEPTH, MOE_GROUPS), 0.01),
        'moe_router_w': nrm(ks[19], (DEPTH, D_MODEL, MOE_EXPERTS), D_MODEL ** -0.5),
        'moe_router_b': nrm(ks[20], (DEPTH, MOE_EXPERTS), 0.01),
        'moe_gate_w': nrm(ks[21], (DEPTH, MOE_EXPERTS, D_MODEL, MOE_HIDDEN), D_MODEL ** -0.5),
        'moe_up_w': nrm(ks[22], (DEPTH, MOE_EXPERTS, D_MODEL, MOE_HIDDEN), D_MODEL ** -0.5),
        'moe_down_w': nrm(ks[23], (DEPTH, MOE_EXPERTS, MOE_HIDDEN, D_MODEL), MOE_HIDDEN ** -0.5),
    }


def reference(x, norm_mix, norm_ffn, final_norm, pool_w, pool_scale, ssd_in_w, ssd_conv_w, ssd_conv_b,
              ssd_dt_bias, ssd_a_log, ssd_d_skip, ssd_norm, ssd_out_w, fox_in_w, fox_f_bias, fox_out_w,
              moe_group_w, moe_group_b, moe_router_w, moe_router_b, moe_gate_w, moe_up_w, moe_down_w):
    h = x
    for i in range(DEPTH):
        kind, j = i % N_MIXERS, i // N_MIXERS
        hn = rmsnorm(h, norm_mix[i])
        if kind == 0:
            mix = pool_mixer(hn, pool_w[j], pool_scale[j])
        elif kind == 1:
            mix = ssd_mixer(hn, ssd_in_w[j], ssd_conv_w[j], ssd_conv_b[j], ssd_dt_bias[j], ssd_a_log[j],
                            ssd_d_skip[j], ssd_norm[j], ssd_out_w[j])
        else:
            mix = fox_mixer(hn, fox_in_w[j], fox_f_bias[j], fox_out_w[j])
        h = h + mix
        hn = rmsnorm(h, norm_ffn[i])
        h = h + moe_ffn(hn, moe_group_w[i], moe_group_b[i], moe_router_w[i], moe_router_b[i],
                        moe_gate_w[i], moe_up_w[i], moe_down_w[i])
    return rmsnorm(h, final_norm)
```

```python
import functools

import jax
import jax.numpy as jnp
from jax import lax
from jax.experimental import pallas as pl
from jax.experimental.pallas import tpu as pltpu

F32 = jnp.float32
BF16 = jnp.bfloat16
I32 = jnp.int32

RMS_EPS = 1e-6
LANES = 128
VMEM_LIMIT = 56 * 1024 * 1024

POOL_WINDOWS = (2, 4, 8, 16)
POOL_HALO = 16

SSD_HEAD_DIM = 64
SSD_STATE = 128
SSD_GROUPS = 8
SSD_CONV = 4
SSD_CHUNK = 128
CONV_HALO = 8

FOX_HEAD_DIM = 128
FOX_TQ = 512
FOX_TK = 512

MOE_GROUPS = 4
MOE_EPG = 8
MOE_EXPERTS = MOE_GROUPS * MOE_EPG
MOE_ROWS = 256
MOE_TOK_TILE = 256
ROUTE_TILE = 512


def _cparams(n_axes, vmem=VMEM_LIMIT):
    return pltpu.CompilerParams(dimension_semantics=("arbitrary",) * n_axes, vmem_limit_bytes=vmem)


def _rms(x, g):
    ms = jnp.mean(x * x, axis=-1, keepdims=True)
    return x * lax.rsqrt(ms + RMS_EPS) * g


def _split3(x):
    hi = x.astype(BF16)
    r = x - hi.astype(F32)
    mid = r.astype(BF16)
    lo = (r - mid.astype(F32)).astype(BF16)
    return hi, mid, lo


def _dot(a, b):
    return jnp.dot(a, b, preferred_element_type=F32)


def _dot_nt(a, b):
    return lax.dot_general(a, b, (((1,), (1,)), ((), ())), preferred_element_type=F32)


def _dot3_rhs_exact(x, m):
    hi, mid, lo = _split3(x)
    return _dot(hi, m) + _dot(mid, m) + _dot(lo, m)


def _dot3_lhs_exact(m, x):
    hi, mid, lo = _split3(x)
    return _dot(m, hi) + _dot(m, mid) + _dot(m, lo)


def _sigmoid(x):
    return 1.0 / (1.0 + jnp.exp(-x))


def _softplus(x):
    return jnp.maximum(x, 0.0) + jnp.log1p(jnp.exp(-jnp.abs(x)))


def _mm_kernel(*refs, has_res):
    if has_res:
        x_ref, w_ref, r_ref, o_ref, wbf = refs
    else:
        x_ref, w_ref, o_ref, wbf = refs
        r_ref = None

    @pl.when(pl.program_id(1) == 0)
    def _():
        wbf[...] = w_ref[...].astype(BF16)

    acc = _dot(x_ref[...], wbf[...])
    if r_ref is not None:
        acc = acc + r_ref[...]
    o_ref[...] = acc.astype(o_ref.dtype)


def _matmul(x, w, *, col0=0, ncols=None, tn=512, tm=512, residual=None, out_dtype=F32, name="mm"):
    m, k = x.shape
    ncols = w.shape[1] - col0 if ncols is None else ncols
    assert ncols % tn == 0 and col0 % tn == 0 and m % tm == 0
    jb = col0 // tn
    in_specs = [pl.BlockSpec((tm, k), lambda j, i: (i, 0)),
                pl.BlockSpec((k, tn), lambda j, i: (0, jb + j))]
    args = [x, w]
    if residual is not None:
        in_specs.append(pl.BlockSpec((tm, tn), lambda j, i: (i, j)))
        args.append(residual)
    return pl.pallas_call(
        functools.partial(_mm_kernel, has_res=residual is not None),
        out_shape=jax.ShapeDtypeStruct((m, ncols), out_dtype),
        grid=(ncols // tn, m // tm),
        in_specs=in_specs,
        out_specs=pl.BlockSpec((tm, tn), lambda j, i: (i, j)),
        scratch_shapes=[pltpu.VMEM((k, tn), BF16)],
        compiler_params=_cparams(2),
        name=name,
    )(*args)


def _pool_kernel(h_ref, g_ref, w_ref, scale_ref, o_ref, xbuf, wbf, *, tm, pg):
    i = pl.program_id(0)
    d = h_ref.shape[1]

    @pl.when(i == 0)
    def _():
        xbuf[0:POOL_HALO, :] = jnp.zeros((POOL_HALO, d), F32)
        wbf[...] = w_ref[...].astype(BF16)

    h = h_ref[...]
    hn = _rms(h, g_ref[...])
    xbuf[POOL_HALO:POOL_HALO + tm, :] = hn
    row = i * tm + lax.broadcasted_iota(I32, (tm, 1), 0)
    for gi, w in enumerate(POOL_WINDOWS):
        c0 = gi * pg
        cur = hn[:, c0:c0 + pg]
        acc = cur
        for j in range(1, w):
            acc = acc + xbuf[POOL_HALO - j:POOL_HALO - j + tm, c0:c0 + pg]
        cnt = jnp.minimum(row + 1, w).astype(F32)
        pooled = acc / cnt - cur
        mix = _dot(pooled.astype(BF16), wbf[gi]) * scale_ref[:, c0:c0 + pg]
        o_ref[:, c0:c0 + pg] = h[:, c0:c0 + pg] + mix
    xbuf[0:POOL_HALO, :] = xbuf[tm:tm + POOL_HALO, :]


def _pool_layer(h, g, pool_w, pool_scale, j, tm=256):
    n, d = h.shape
    nw, pg, _ = pool_w.shape[1:]
    return pl.pallas_call(
        functools.partial(_pool_kernel, tm=tm, pg=pg),
        out_shape=jax.ShapeDtypeStruct((n, d), F32),
        grid=(n // tm,),
        in_specs=[pl.BlockSpec((tm, d), lambda i: (i, 0)),
                  pl.BlockSpec((1, d), lambda i: (0, 0)),
                  pl.BlockSpec((None, nw, pg, pg), lambda i: (j, 0, 0, 0)),
                  pl.BlockSpec((1, d), lambda i: (0, 0))],
        out_specs=pl.BlockSpec((tm, d), lambda i: (i, 0)),
        scratch_shapes=[pltpu.VMEM((POOL_HALO + tm, d), F32), pltpu.VMEM((nw, pg, pg), BF16)],
        compiler_params=_cparams(1),
        name="pool_layer",
    )(h, g.reshape(1, d), pool_w, pool_scale[j].reshape(1, d))


def _route_kernel(h_ref, g_ref, whi_ref, wlo_ref, b_ref, tri_ref, hn_ref, ri_ref, rg_ref, cnt_ref,
                  run_ref, *, tm):
    i = pl.program_id(0)

    @pl.when(i == 0)
    def _():
        run_ref[...] = jnp.zeros_like(run_ref)

    hn = _rms(h_ref[...], g_ref[...])
    hn_bf = hn.astype(BF16)
    hn_ref[...] = hn_bf.astype(F32).reshape(hn_ref.shape)
    hn_lo = (hn - hn_bf.astype(F32)).astype(BF16)
    whi = whi_ref[...]
    logits = _dot(hn_bf, whi) + _dot(hn_bf, wlo_ref[...]) + _dot(hn_lo, whi) + b_ref[...]

    lane = lax.broadcasted_iota(I32, (tm, LANES), 1)
    lane_f = lane.astype(F32)
    neg = -jnp.inf
    big = float(LANES)
    gl = jnp.where(lane < MOE_GROUPS, logits, neg)
    gmax = jnp.max(gl, axis=-1, keepdims=True)
    gsel = jnp.min(jnp.where(gl == gmax, lane_f, big), axis=-1, keepdims=True).astype(I32)
    p_grp = 1.0 / jnp.sum(jnp.exp(gl - gmax), axis=-1, keepdims=True)
    lo_lane = MOE_GROUPS + MOE_EPG * gsel
    el = jnp.where((lane >= lo_lane) & (lane < lo_lane + MOE_EPG), logits, neg)
    v0 = jnp.max(el, axis=-1, keepdims=True)
    i0 = jnp.min(jnp.where(el == v0, lane_f, big), axis=-1, keepdims=True).astype(I32)
    el2 = jnp.where(lane == i0, neg, el)
    v1 = jnp.max(el2, axis=-1, keepdims=True)
    i1 = jnp.min(jnp.where(el2 == v1, lane_f, big), axis=-1, keepdims=True).astype(I32)
    t = jnp.exp(v1 - v0)
    gate0 = p_grp / (1.0 + t)
    gate1 = p_grp * t / (1.0 + t)
    e0 = i0 - MOE_GROUPS
    e1 = i1 - MOE_GROUPS

    onehot = ((lane == e0) | (lane == e1))
    c_bf = jnp.where(onehot, 1.0, 0.0).astype(BF16)
    prefix = _dot(tri_ref[...], c_bf) + run_ref[...]
    rank0 = jnp.sum(jnp.where(lane == e0, prefix, 0.0), axis=-1, keepdims=True).astype(I32)
    rank1 = jnp.sum(jnp.where(lane == e1, prefix, 0.0), axis=-1, keepdims=True).astype(I32)
    run = run_ref[...] + jnp.sum(jnp.where(onehot, 1.0, 0.0), axis=0, keepdims=True)
    run_ref[...] = run
    cnt_ref[...] = jnp.broadcast_to(run, cnt_ref.shape)

    ri = jnp.where(lane == 0, e0, jnp.where(lane == 1, e1, jnp.where(lane == 2, rank0, rank1)))
    ri_ref[...] = ri
    rg_ref[...] = jnp.where(lane == 0, gate0, gate1)


def _route(h, g, wr, br, tm=ROUTE_TILE):
    n, d = h.shape
    whi = wr.astype(BF16)
    wlo = (wr - whi.astype(F32)).astype(BF16)
    tri = jnp.tril(jnp.ones((tm, tm), BF16), -1)
    return pl.pallas_call(
        functools.partial(_route_kernel, tm=tm),
        out_shape=(jax.ShapeDtypeStruct((n, 1, d), F32),
                   jax.ShapeDtypeStruct((n, LANES), I32),
                   jax.ShapeDtypeStruct((n, LANES), F32),
                   jax.ShapeDtypeStruct((8, LANES), F32)),
        grid=(n // tm,),
        in_specs=[pl.BlockSpec((tm, d), lambda i: (i, 0)),
                  pl.BlockSpec((1, d), lambda i: (0, 0)),
                  pl.BlockSpec((d, LANES), lambda i: (0, 0)),
                  pl.BlockSpec((d, LANES), lambda i: (0, 0)),
                  pl.BlockSpec((1, LANES), lambda i: (0, 0)),
                  pl.BlockSpec((tm, tm), lambda i: (0, 0))],
        out_specs=(pl.BlockSpec((tm, 1, d), lambda i: (i, 0, 0)),
                   pl.BlockSpec((tm, LANES), lambda i: (i, 0)),
                   pl.BlockSpec((tm, LANES), lambda i: (i, 0)),
                   pl.BlockSpec((8, LANES), lambda i: (0, 0))),
        scratch_shapes=[pltpu.VMEM((1, LANES), F32)],
        compiler_params=_cparams(1),
        name="moe_route",
    )(h, g.reshape(1, d), whi, wlo, br, tri)


def _dispatch_kernel(dest_ref, hn_ref, xs_in_ref, xs_ref, sem, *, t):
    del xs_in_ref

    def row_copy(r, k):
        return pltpu.make_async_copy(hn_ref.at[r], xs_ref.at[dest_ref[0, 2 * r + k]], sem)

    def start(r, c):
        row_copy(r, 0).start()
        row_copy(r, 1).start()
        return c

    def wait(r, c):
        row_copy(r, 0).wait()
        row_copy(r, 1).wait()
        return c

    lax.fori_loop(0, t, start, 0)
    lax.fori_loop(0, t, wait, 0)


def _dispatch(hn, dest, n_rows, t=MOE_TOK_TILE):
    n, _, d = hn.shape
    xs0 = jnp.zeros((n_rows, 1, d), hn.dtype)
    return pl.pallas_call(
        functools.partial(_dispatch_kernel, t=t),
        out_shape=jax.ShapeDtypeStruct((n_rows, 1, d), hn.dtype),
        grid=(n // t,),
        in_specs=[pl.BlockSpec((None, 1, 2 * t), lambda i: (i, 0, 0), memory_space=pltpu.SMEM),
                  pl.BlockSpec((t, 1, d), lambda i: (i, 0, 0)),
                  pl.BlockSpec(memory_space=pl.ANY)],
        out_specs=pl.BlockSpec(memory_space=pl.ANY),
        scratch_shapes=[pltpu.SemaphoreType.DMA(())],
        input_output_aliases={2: 0},
        compiler_params=_cparams(1),
        name="moe_dispatch",
    )(dest.reshape(n // t, 1, 2 * t), hn, xs0)


def _expert_kernel(bexp_ref, nused_ref, xs_ref, wg_ref, wu_ref, wd_ref, o_ref, wg_bf, wu_bf, wd_bf):
    b = pl.program_id(0)
    prev = bexp_ref[jnp.maximum(b - 1, 0)]
    active = b < nused_ref[0]

    @pl.when(active & ((b == 0) | (bexp_ref[b] != prev)))
    def _():
        wg_bf[...] = wg_ref[...].astype(BF16)
        wu_bf[...] = wu_ref[...].astype(BF16)
        wd_bf[...] = wd_ref[...].astype(BF16)

    @pl.when(active)
    def _():
        rows, _, d = xs_ref.shape
        x = xs_ref[...].reshape(rows, d).astype(BF16)
        gte = _dot(x, wg_bf[...])
        up = _dot(x, wu_bf[...])
        hb = gte * _sigmoid(gte) * up
        o_ref[...] = _dot(hb.astype(BF16), wd_bf[...]).reshape(o_ref.shape)

    @pl.when(jnp.logical_not(active))
    def _():
        o_ref[...] = jnp.zeros_like(o_ref)


def _experts(xs, block_exp, n_used, gate_w, up_w, down_w, layer, rows=MOE_ROWS):
    n_rows, _, d = xs.shape
    hid = gate_w.shape[-1]
    grid_spec = pltpu.PrefetchScalarGridSpec(
        num_scalar_prefetch=2,
        grid=(n_rows // rows,),
        in_specs=[pl.BlockSpec((rows, 1, d), lambda b, be, nu: (b, 0, 0)),
                  pl.BlockSpec((None, None, d, hid), lambda b, be, nu: (layer, be[b], 0, 0)),
                  pl.BlockSpec((None, None, d, hid), lambda b, be, nu: (layer, be[b], 0, 0)),
                  pl.BlockSpec((None, None, hid, d), lambda b, be, nu: (layer, be[b], 0, 0))],
        out_specs=pl.BlockSpec((rows, 1, d), lambda b, be, nu: (b, 0, 0)),
        scratch_shapes=[pltpu.VMEM((d, hid), BF16), pltpu.VMEM((d, hid), BF16), pltpu.VMEM((hid, d), BF16)],
    )
    return pl.pallas_call(
        _expert_kernel,
        out_shape=jax.ShapeDtypeStruct((n_rows, 1, d), F32),
        grid_spec=grid_spec,
        compiler_params=_cparams(1),
        name="moe_experts",
    )(block_exp, n_used, xs, gate_w, up_w, down_w)


def _combine_kernel(*refs, t, mode):
    if mode == "plain":
        dest_ref, h_ref, rg_ref, ys_ref, o_ref, buf0, buf1, flat0, flat1, sem = refs
    elif mode == "norm_bf16":
        dest_ref, h_ref, rg_ref, ys_ref, g_ref, o_ref, hn_ref, buf0, buf1, flat0, flat1, sem = refs
    else:
        dest_ref, h_ref, rg_ref, ys_ref, g_ref, o_ref, buf0, buf1, flat0, flat1, sem = refs
    bufs = (buf0, buf1)

    def row_copy(r, k):
        return pltpu.make_async_copy(ys_ref.at[dest_ref[0, 2 * r + k]], bufs[k].at[r], sem)

    def start(r, c):
        row_copy(r, 0).start()
        row_copy(r, 1).start()
        return c

    def wait(r, c):
        row_copy(r, 0).wait()
        row_copy(r, 1).wait()
        return c

    lax.fori_loop(0, t, start, 0)
    lax.fori_loop(0, t, wait, 0)
    flat0[...] = buf0[...].reshape(flat0.shape)
    flat1[...] = buf1[...].reshape(flat1.shape)
    rg = rg_ref[...]
    y = rg[:, 0:1] * flat0[...] + rg[:, 1:2] * flat1[...]
    h_new = h_ref[...] + y
    if mode == "plain":
        o_ref[...] = h_new
    elif mode == "norm_bf16":
        o_ref[...] = h_new
        hn_ref[...] = _rms(h_new, g_ref[...]).astype(BF16)
    else:
        o_ref[...] = _rms(h_new, g_ref[...])


def _combine(h, rg, ys, dest, g_next, mode, t=MOE_TOK_TILE):
    n, d = h.shape
    in_specs = [pl.BlockSpec((None, 1, 2 * t), lambda i: (i, 0, 0), memory_space=pltpu.SMEM),
                pl.BlockSpec((t, d), lambda i: (i, 0)),
                pl.BlockSpec((t, LANES), lambda i: (i, 0)),
                pl.BlockSpec(memory_space=pl.ANY)]
    args = [dest.reshape(n // t, 1, 2 * t), h, rg, ys]
    tile = pl.BlockSpec((t, d), lambda i: (i, 0))
    if mode != "plain":
        in_specs.append(pl.BlockSpec((1, d), lambda i: (0, 0)))
        args.append(g_next.reshape(1, d))
    if mode == "norm_bf16":
        out_shape = (jax.ShapeDtypeStruct((n, d), F32), jax.ShapeDtypeStruct((n, d), BF16))
        out_specs = (tile, tile)
    else:
        out_shape = jax.ShapeDtypeStruct((n, d), F32)
        out_specs = tile
    return pl.pallas_call(
        functools.partial(_combine_kernel, t=t, mode=mode),
        out_shape=out_shape,
        grid=(n // t,),
        in_specs=in_specs,
        out_specs=out_specs,
        scratch_shapes=[pltpu.VMEM((t, 1, d), F32), pltpu.VMEM((t, 1, d), F32),
                        pltpu.VMEM((t, d), F32), pltpu.VMEM((t, d), F32), pltpu.SemaphoreType.DMA(())],
        compiler_params=_cparams(1),
        name="moe_combine",
    )(*args)


def _moe_layer(h, layer, norm_ffn, group_w, group_b, router_w, router_b, gate_w, up_w, down_w, g_next, mode):
    n, d = h.shape
    pad = LANES - MOE_GROUPS - MOE_EXPERTS
    wr = jnp.concatenate([group_w[layer], router_w[layer], jnp.zeros((d, pad), F32)], axis=1)
    br = jnp.concatenate([group_b[layer], router_b[layer], jnp.zeros((pad,), F32)]).reshape(1, LANES)
    hn, ri, rg, cnt = _route(h, norm_ffn[layer], wr, br)

    counts = cnt[0, :MOE_EXPERTS].astype(I32)
    nblk = (counts + MOE_ROWS - 1) // MOE_ROWS
    blk_end = jnp.cumsum(nblk)
    row_start = (blk_end - nblk) * MOE_ROWS
    n_rows = -(-(n * 2 + MOE_EXPERTS * (MOE_ROWS - 1)) // MOE_ROWS) * MOE_ROWS
    n_blocks = n_rows // MOE_ROWS
    dest = jnp.stack([row_start[ri[:, 0]] + ri[:, 2], row_start[ri[:, 1]] + ri[:, 3]], axis=1).astype(I32)
    block_exp = jnp.minimum(jnp.searchsorted(blk_end, jnp.arange(n_blocks, dtype=I32), side="right"),
                            MOE_EXPERTS - 1).astype(I32)
    n_used = blk_end[-1:].astype(I32)

    xs = _dispatch(hn, dest, n_rows)
    ys = _experts(xs, block_exp, n_used, gate_w, up_w, down_w, layer)
    return _combine(h, rg, ys, dest, g_next, mode)


def _ssd_kernel(z_ref, x_ref, b_ref, c_ref, dt_ref, cwx_ref, cwb_ref, cwc_ref, cbx_ref, cbb_ref, cbc_ref,
                dtb_ref, a_ref, dexp_ref, nw_ref, exp_ref, tri_ref, o_ref,
                xbuf, bbuf, cbuf, xc_s, xdt_s, xw_s, b_s, c_s, eacs_s, acs_s, acst_s, y_s, st_s,
                *, L, hp, n_state, groups):
    ci = pl.program_id(0)
    d_inner = x_ref.shape[1]
    gw = d_inner // groups
    kh = gw // hp
    H0 = CONV_HALO

    @pl.when(ci == 0)
    def _():
        xbuf[0:H0, :] = jnp.zeros((H0, xbuf.shape[1]), F32)
        bbuf[0:H0, :] = jnp.zeros((H0, bbuf.shape[1]), F32)
        cbuf[0:H0, :] = jnp.zeros((H0, cbuf.shape[1]), F32)
        st_s[...] = jnp.zeros_like(st_s)

    def conv_silu(in_ref, buf, cw_ref, cb_ref):
        buf[H0:H0 + L, :] = in_ref[...]
        acc = cb_ref[...] + cw_ref[SSD_CONV - 1:SSD_CONV, :] * buf[H0:H0 + L, :]
        for j in range(SSD_CONV - 1):
            off = H0 - (SSD_CONV - 1) + j
            acc = acc + cw_ref[j:j + 1, :] * buf[off:off + L, :]
        buf[0:H0, :] = buf[L:L + H0, :]
        return acc * _sigmoid(acc)

    xc_s[...] = conv_silu(x_ref, xbuf, cwx_ref, cbx_ref)
    b_s[...] = conv_silu(b_ref, bbuf, cwb_ref, cbb_ref).astype(BF16)
    c_s[...] = conv_silu(c_ref, cbuf, cwc_ref, cbc_ref).astype(BF16)

    dt = _softplus(dt_ref[...] + dtb_ref[...])
    d_a = dt * a_ref[...]
    acs = _dot3_lhs_exact(tri_ref[...], d_a)
    acs_s[...] = acs
    acst_s[...] = acs.T
    emat = exp_ref[...]
    acs_e = _dot3_rhs_exact(acs, emat)
    dt_e = _dot3_rhs_exact(dt, emat)
    acs_last = acs_e[L - 1:L, :]
    xdt = xc_s[...] * dt_e
    xdt_s[...] = xdt.astype(BF16)
    xw_s[...] = (xdt * jnp.exp(acs_last - acs_e)).astype(BF16)
    eacs_s[...] = jnp.exp(acs_e)
    cdecay = jnp.exp(acs_last)

    rr = lax.broadcasted_iota(I32, (L, L), 0)
    cc = lax.broadcasted_iota(I32, (L, L), 1)
    causal = rr >= cc
    lane2 = lax.broadcasted_iota(I32, (L, 2 * hp), 1)

    for g in range(groups):
        bg = b_s[:, g * n_state:(g + 1) * n_state]
        cg = c_s[:, g * n_state:(g + 1) * n_state]
        cb = _dot_nt(cg, bg)
        st_prev = st_s[g]
        y_off = _dot(cg, st_prev.astype(BF16)) * eacs_s[:, g * gw:(g + 1) * gw]
        for kp in range(kh // 2):
            c0 = g * gw + kp * 2 * hp
            slab = xdt_s[:, c0:c0 + 2 * hp]
            y_pair = None
            for half in range(2):
                hd = g * kh + kp * 2 + half
                seg = acs_s[:, hd:hd + 1] - acst_s[hd:hd + 1, :]
                decay = jnp.exp(jnp.where(causal, seg, -jnp.inf))
                m = (cb * decay).astype(BF16)
                keep = (lane2 < hp) if half == 0 else (lane2 >= hp)
                part = _dot(m, jnp.where(keep, slab, jnp.zeros_like(slab)))
                y_pair = part if y_pair is None else y_pair + part
            y_s[:, c0:c0 + 2 * hp] = y_pair + y_off[:, kp * 2 * hp:(kp + 1) * 2 * hp]
        bgt = bg.astype(F32).T.astype(BF16)
        s_new = _dot(bgt, xw_s[:, g * gw:(g + 1) * gw])
        st_s[g] = cdecay[:, g * gw:(g + 1) * gw] * st_prev + s_new

    y = y_s[...] + dexp_ref[...] * xc_s[...]
    z = z_ref[...]
    y = y * (z * _sigmoid(z))
    for g in range(groups):
        yg = y[:, g * gw:(g + 1) * gw]
        yg = yg * lax.rsqrt(jnp.mean(yg * yg, axis=-1, keepdims=True) + RMS_EPS)
        o_ref[:, g * gw:(g + 1) * gw] = (yg * nw_ref[:, g * gw:(g + 1) * gw]).astype(o_ref.dtype)


def _ssd_scan(zx, dt_raw, conv_w, conv_b, dt_bias, a_log, d_skip, norm_w, d_inner):
    s = zx.shape[0]
    L = SSD_CHUNK
    hp, n_state, groups = SSD_HEAD_DIM, SSD_STATE, SSD_GROUPS
    heads = d_inner // hp
    gn = groups * n_state
    gw = d_inner // groups
    assert d_inner % gn == 0 and heads <= LANES
    pad = LANES - heads
    dtb = jnp.pad(dt_bias, (0, pad)).reshape(1, LANES)
    a_neg = jnp.pad(-jnp.exp(a_log), (0, pad)).reshape(1, LANES)
    dexp = jnp.repeat(d_skip, hp).reshape(1, d_inner)
    emat = (jnp.arange(LANES)[:, None] == (jnp.arange(d_inner)[None, :] // hp)).astype(BF16)
    tri = jnp.tril(jnp.ones((L, L), BF16))
    cb2 = conv_b.reshape(1, -1)
    xb = d_inner // gn
    row = lambda c: (c, 0)
    const = lambda c: (0, 0)
    in_specs = [
        pl.BlockSpec((L, d_inner), row),
        pl.BlockSpec((L, d_inner), lambda c: (c, 1)),
        pl.BlockSpec((L, gn), lambda c: (c, 2 * xb)),
        pl.BlockSpec((L, gn), lambda c: (c, 2 * xb + 1)),
        pl.BlockSpec((L, LANES), row),
        pl.BlockSpec((SSD_CONV, d_inner), const),
        pl.BlockSpec((SSD_CONV, gn), lambda c: (0, xb)),
        pl.BlockSpec((SSD_CONV, gn), lambda c: (0, xb + 1)),
        pl.BlockSpec((1, d_inner), const),
        pl.BlockSpec((1, gn), lambda c: (0, xb)),
        pl.BlockSpec((1, gn), lambda c: (0, xb + 1)),
        pl.BlockSpec((1, LANES), const),
        pl.BlockSpec((1, LANES), const),
        pl.BlockSpec((1, d_inner), const),
        pl.BlockSpec((1, d_inner), const),
        pl.BlockSpec((LANES, d_inner), const),
        pl.BlockSpec((L, L), const),
    ]
    scratch = [
        pltpu.VMEM((CONV_HALO + L, d_inner), F32), pltpu.VMEM((CONV_HALO + L, gn), F32),
        pltpu.VMEM((CONV_HALO + L, gn), F32),
        pltpu.VMEM((L, d_inner), F32),
        pltpu.VMEM((L, d_inner), BF16),
        pltpu.VMEM((L, d_inner), BF16),
        pltpu.VMEM((L, gn), BF16), pltpu.VMEM((L, gn), BF16),
        pltpu.VMEM((L, d_inner), F32),
        pltpu.VMEM((L, LANES), F32), pltpu.VMEM((LANES, L), F32),
        pltpu.VMEM((L, d_inner), F32),
        pltpu.VMEM((groups, n_state, gw), F32),
    ]
    return pl.pallas_call(
        functools.partial(_ssd_kernel, L=L, hp=hp, n_state=n_state, groups=groups),
        out_shape=jax.ShapeDtypeStruct((s, d_inner), BF16),
        grid=(s // L,),
        in_specs=in_specs,
        out_specs=pl.BlockSpec((L, d_inner), row),
        scratch_shapes=scratch,
        compiler_params=_cparams(1),
        name="ssd_scan",
    )(zx, zx, zx, zx, dt_raw, conv_w, conv_w, conv_w, cb2, cb2, cb2, dtb, a_neg, dexp,
      norm_w.reshape(1, d_inner), emat, tri)


def _ssd_mixer(h, hn, in_w, conv_w, conv_b, dt_bias, a_log, d_skip, norm_w, out_w):
    d = hn.shape[1]
    d_inner = out_w.shape[0]
    heads = d_inner // SSD_HEAD_DIM
    wide = in_w.shape[1] - heads
    zx = _matmul(hn, in_w, col0=0, ncols=wide, tn=512, name="ssd_in_proj")
    w_dt = jnp.pad(in_w[:, wide:], ((0, 0), (0, LANES - heads)))
    dt_raw = _matmul(hn, w_dt, tn=LANES, name="ssd_dt_proj")
    y = _ssd_scan(zx, dt_raw, conv_w, conv_b, dt_bias, a_log, d_skip, norm_w, d_inner)
    return _matmul(y, out_w, tn=512, residual=h, name="ssd_out_proj")


def _fox_cum_kernel(f_ref, b_ref, tri_ref, o_ref, carry):
    @pl.when(pl.program_id(0) == 0)
    def _():
        carry[...] = jnp.zeros_like(carry)

    x = f_ref[...] + b_ref[...]
    log_f = jnp.minimum(x, 0.0) - jnp.log1p(jnp.exp(-jnp.abs(x)))
    cs = _dot3_rhs_exact(log_f, tri_ref[...]) + carry[...]
    o_ref[...] = cs
    carry[...] = cs[:, cs.shape[1] - 1:]


def _fox_cum(f_t, bias, tk=512):
    nh, s = f_t.shape
    triu = jnp.triu(jnp.ones((tk, tk), BF16))
    return pl.pallas_call(
        _fox_cum_kernel,
        out_shape=jax.ShapeDtypeStruct((nh, s), F32),
        grid=(s // tk,),
        in_specs=[pl.BlockSpec((nh, tk), lambda i: (0, i)),
                  pl.BlockSpec((nh, 1), lambda i: (0, 0)),
                  pl.BlockSpec((tk, tk), lambda i: (0, 0))],
        out_specs=pl.BlockSpec((nh, tk), lambda i: (0, i)),
        scratch_shapes=[pltpu.VMEM((nh, 1), F32)],
        compiler_params=_cparams(1),
        name="fox_cum",
    )(f_t, bias.reshape(nh, 1), triu)


def _fox_attn_kernel(q_ref, k_ref, v_ref, cq_ref, ck_ref, o_ref, m_s, l_s, acc_s, *, tq, tk, scale):
    qi = pl.program_id(1)
    kj = pl.program_id(2)

    @pl.when(kj == 0)
    def _():
        m_s[...] = jnp.full_like(m_s, -jnp.inf)
        l_s[...] = jnp.zeros_like(l_s)
        acc_s[...] = jnp.zeros_like(acc_s)

    @pl.when(kj <= qi)
    def _():
        s = _dot_nt(q_ref[...], k_ref[...]) * scale
        s = s + (cq_ref[:, 0:1] - ck_ref[...])
        q_pos = qi * tq + lax.broadcasted_iota(I32, (tq, tk), 0)
        k_pos = kj * tk + lax.broadcasted_iota(I32, (tq, tk), 1)
        s = jnp.where(q_pos >= k_pos, s, -jnp.inf)
        m_prev = m_s[...]
        m_new = jnp.maximum(m_prev, jnp.max(s, axis=-1, keepdims=True))
        alpha = jnp.exp(m_prev - m_new)
        p = jnp.exp(s - m_new)
        l_s[...] = alpha * l_s[...] + jnp.sum(p, axis=-1, keepdims=True)
        acc_s[...] = alpha * acc_s[...] + _dot(p.astype(BF16), v_ref[...])
        m_s[...] = m_new

    @pl.when(kj == qi)
    def _():
        o_ref[...] = (acc_s[...] / l_s[...]).astype(o_ref.dtype)


def _fox_attention(qkv, cum, nh, tq=FOX_TQ, tk=FOX_TK):
    s = qkv.shape[0]
    dh = FOX_HEAD_DIM
    assert tq == tk
    cum3 = cum.reshape(nh, 1, s)
    return pl.pallas_call(
        functools.partial(_fox_attn_kernel, tq=tq, tk=tk, scale=dh ** -0.5),
        out_shape=jax.ShapeDtypeStruct((s, nh * dh), BF16),
        grid=(nh, s // tq, s // tk),
        in_specs=[pl.BlockSpec((tq, dh), lambda h, i, j: (i, h)),
                  pl.BlockSpec((tk, dh), lambda h, i, j: (jnp.minimum(j, i), nh + h)),
                  pl.BlockSpec((tk, dh), lambda h, i, j: (jnp.minimum(j, i), 2 * nh + h)),
                  pl.BlockSpec((None, 1, tq), lambda h, i, j: (h, 0, i)),
                  pl.BlockSpec((None, 1, tk), lambda h, i, j: (h, 0, jnp.minimum(j, i)))],
        out_specs=pl.BlockSpec((tq, dh), lambda h, i, j: (i, h)),
        scratch_shapes=[pltpu.VMEM((tq, 1), F32), pltpu.VMEM((tq, 1), F32), pltpu.VMEM((tq, dh), F32)],
        compiler_params=_cparams(3),
        name="fox_attention",
    )(qkv, qkv, qkv, cum3, cum3)


def _fox_mixer(h, hn, in_w, f_bias, out_w):
    d = hn.shape[1]
    nh = d // FOX_HEAD_DIM
    qkv = _matmul(hn, in_w, col0=0, ncols=3 * d, tn=512, out_dtype=BF16, name="fox_in_proj")
    w_f = jnp.pad(in_w[:, 3 * d:], ((0, 0), (0, LANES - nh)))
    f_raw = _matmul(hn, w_f, tn=LANES, name="fox_f_proj")
    cum = _fox_cum(f_raw[:, :nh].T, f_bias)
    o = _fox_attention(qkv, cum, nh)
    return _matmul(o, out_w, tn=512, residual=h, name="fox_out_proj")


def _norm_kernel(h_ref, g_ref, o_ref):
    o_ref[...] = _rms(h_ref[...], g_ref[...]).astype(o_ref.dtype)


def _norm(h, g, out_dtype, tm=512):
    n, d = h.shape
    return pl.pallas_call(
        _norm_kernel,
        out_shape=jax.ShapeDtypeStruct((n, d), out_dtype),
        grid=(n // tm,),
        in_specs=[pl.BlockSpec((tm, d), lambda i: (i, 0)), pl.BlockSpec((1, d), lambda i: (0, 0))],
        out_specs=pl.BlockSpec((tm, d), lambda i: (i, 0)),
        compiler_params=_cparams(1),
        name="rmsnorm",
    )(h, g.reshape(1, d))


def kernel(x, norm_mix, norm_ffn, final_norm, pool_w, pool_scale, ssd_in_w, ssd_conv_w, ssd_conv_b,
           ssd_dt_bias, ssd_a_log, ssd_d_skip, ssd_norm, ssd_out_w, fox_in_w, fox_f_bias, fox_out_w,
           moe_group_w, moe_group_b, moe_router_w, moe_router_b, moe_gate_w, moe_up_w, moe_down_w):
    bsz, s, d = x.shape
    assert bsz == 1
    depth = norm_mix.shape[0]
    h = x.reshape(s, d)
    hn = None
    for i in range(depth):
        kind, j = i % 3, i // 3
        if kind == 0:
            h = _pool_layer(h, norm_mix[i], pool_w, pool_scale, j)
        else:
            if hn is None:
                hn = _norm(h, norm_mix[i], BF16)
            if kind == 1:
                h = _ssd_mixer(h, hn, ssd_in_w[j], ssd_conv_w[j], ssd_conv_b[j], ssd_dt_bias[j], ssd_a_log[j],
                               ssd_d_skip[j], ssd_norm[j], ssd_out_w[j])
            else:
                h = _fox_mixer(h, hn, fox_in_w[j], fox_f_bias[j], fox_out_w[j])
        last = i == depth - 1
        next_kind = (i + 1) % 3
        if last:
            g_next, mode = final_norm, "final"
        elif next_kind == 0:
            g_next, mode = None, "plain"
        else:
            g_next, mode = norm_mix[i + 1], "norm_bf16"
        out = _moe_layer(h, i, norm_ffn, moe_group_w, moe_group_b, moe_router_w, moe_router_b,
                         moe_gate_w, moe_up_w, moe_down_w, g_next, mode)
        if mode == "norm_bf16":
            h, hn = out
        else:
            h, hn = out, None
    return h.reshape(bsz, s, d)
```

```python
import functools

import jax
import jax.numpy as jnp
from jax import lax
from jax.experimental import pallas as pl
from jax.experimental.pallas import tpu as pltpu

F32 = jnp.float32
BF16 = jnp.bfloat16
I32 = jnp.int32
U32 = jnp.uint32

RMS_EPS = 1e-6
LOG2E = 1.4426950408889634
LANES = 128
VMEM_LIMIT = 56 * 1024 * 1024

POOL_WINDOWS = (2, 4, 8, 16)
POOL_HALO = 16

SSD_HEAD_DIM = 64
SSD_STATE = 128
SSD_GROUPS = 8
SSD_CONV = 4
SSD_CHUNK = 128
CONV_HALO = 8

FOX_HEAD_DIM = 128
FOX_TQ = 1024
FOX_TK = 512
FOX_SUB = 256

MOE_GROUPS = 4
MOE_EPG = 8
MOE_EXPERTS = MOE_GROUPS * MOE_EPG
MOE_ROWS = 256
MOE_TOK_TILE = 256
ROUTE_TILE = 512


def _cparams(n_axes, vmem=VMEM_LIMIT):
    return pltpu.CompilerParams(dimension_semantics=("arbitrary",) * n_axes, vmem_limit_bytes=vmem)


def _rms(x, g):
    ms = jnp.mean(x * x, axis=-1, keepdims=True)
    return x * lax.rsqrt(ms + RMS_EPS) * g


def _split3(x):
    hi = x.astype(BF16)
    r = x - hi.astype(F32)
    mid = r.astype(BF16)
    lo = (r - mid.astype(F32)).astype(BF16)
    return hi, mid, lo


def _dot(a, b):
    return jnp.dot(a, b, preferred_element_type=F32)


def _dot_nt(a, b):
    return lax.dot_general(a, b, (((1,), (1,)), ((), ())), preferred_element_type=F32)


def _dot3_rhs_exact(x, m):
    hi, mid, lo = _split3(x)
    return _dot(hi, m) + _dot(mid, m) + _dot(lo, m)


def _dot3_lhs_exact(m, x):
    hi, mid, lo = _split3(x)
    return _dot(m, hi) + _dot(m, mid) + _dot(m, lo)


def _sigmoid(x):
    return 1.0 / (1.0 + jnp.exp(-x))


def _softplus(x):
    return jnp.maximum(x, 0.0) + jnp.log1p(jnp.exp(-jnp.abs(x)))


def _pack_rows(x):
    half = x.shape[1] // 2
    lo = lax.bitcast_convert_type(x[:, :half].astype(BF16).astype(F32), U32)
    hi = lax.bitcast_convert_type(x[:, half:].astype(BF16).astype(F32), U32)
    return hi | (lo >> 16)


def _unpack_rows(u):
    lo = lax.bitcast_convert_type(u << 16, F32)
    hi = lax.bitcast_convert_type(u & jnp.uint32(0xFFFF0000), F32)
    return lo, hi


def _mm_kernel(*refs, has_res, has_scale):
    x_ref, w_ref = refs[:2]
    rest = list(refs[2:])
    r_ref = rest.pop(0) if has_res else None
    s_ref = rest.pop(0) if has_scale else None
    o_ref, wbf = rest

    @pl.when(pl.program_id(1) == 0)
    def _():
        wbf[...] = w_ref[...].astype(BF16)

    acc = _dot(x_ref[...], wbf[...])
    if s_ref is not None:
        acc = acc * s_ref[...]
    if r_ref is not None:
        acc = acc + r_ref[...]
    o_ref[...] = acc.astype(o_ref.dtype)


def _matmul(x, w, *, col0=0, ncols=None, tn=512, tm=512, residual=None, colscale=None, out_dtype=F32,
            name="mm"):
    m, k = x.shape
    ncols = w.shape[1] - col0 if ncols is None else ncols
    assert ncols % tn == 0 and col0 % tn == 0 and m % tm == 0
    jb = col0 // tn
    in_specs = [pl.BlockSpec((tm, k), lambda j, i: (i, 0)),
                pl.BlockSpec((k, tn), lambda j, i: (0, jb + j))]
    args = [x, w]
    if residual is not None:
        in_specs.append(pl.BlockSpec((tm, tn), lambda j, i: (i, j)))
        args.append(residual)
    if colscale is not None:
        in_specs.append(pl.BlockSpec((1, tn), lambda j, i: (0, j)))
        args.append(colscale)
    return pl.pallas_call(
        functools.partial(_mm_kernel, has_res=residual is not None, has_scale=colscale is not None),
        out_shape=jax.ShapeDtypeStruct((m, ncols), out_dtype),
        grid=(ncols // tn, m // tm),
        in_specs=in_specs,
        out_specs=pl.BlockSpec((tm, tn), lambda j, i: (i, j)),
        scratch_shapes=[pltpu.VMEM((k, tn), BF16)],
        compiler_params=_cparams(2),
        name=name,
    )(*args)


def _pool_kernel(h_ref, g_ref, w_ref, scale_ref, o_ref, xbuf, wbf, *, tm, pg):
    i = pl.program_id(0)
    d = h_ref.shape[1]

    @pl.when(i == 0)
    def _():
        xbuf[0:POOL_HALO, :] = jnp.zeros((POOL_HALO, d), F32)
        wbf[...] = w_ref[...].astype(BF16)

    h = h_ref[...]
    hn = _rms(h, g_ref[...])
    xbuf[POOL_HALO:POOL_HALO + tm, :] = hn
    row = i * tm + lax.broadcasted_iota(I32, (tm, 1), 0)
    for gi, w in enumerate(POOL_WINDOWS):
        c0 = gi * pg
        cur = hn[:, c0:c0 + pg]
        acc = cur
        for j in range(1, w):
            acc = acc + xbuf[POOL_HALO - j:POOL_HALO - j + tm, c0:c0 + pg]
        cnt = jnp.minimum(row + 1, w).astype(F32)
        pooled = acc / cnt - cur
        mix = _dot(pooled.astype(BF16), wbf[gi]) * scale_ref[:, c0:c0 + pg]
        o_ref[:, c0:c0 + pg] = h[:, c0:c0 + pg] + mix
    xbuf[0:POOL_HALO, :] = xbuf[tm:tm + POOL_HALO, :]


def _pool_layer(h, g, pool_w, pool_scale, j, tm=256):
    n, d = h.shape
    nw, pg, _ = pool_w.shape[1:]
    return pl.pallas_call(
        functools.partial(_pool_kernel, tm=tm, pg=pg),
        out_shape=jax.ShapeDtypeStruct((n, d), F32),
        grid=(n // tm,),
        in_specs=[pl.BlockSpec((tm, d), lambda i: (i, 0)),
                  pl.BlockSpec((1, d), lambda i: (0, 0)),
                  pl.BlockSpec((None, nw, pg, pg), lambda i: (j, 0, 0, 0)),
                  pl.BlockSpec((1, d), lambda i: (0, 0))],
        out_specs=pl.BlockSpec((tm, d), lambda i: (i, 0)),
        scratch_shapes=[pltpu.VMEM((POOL_HALO + tm, d), F32), pltpu.VMEM((nw, pg, pg), BF16)],
        compiler_params=_cparams(1),
        name="pool_layer",
    )(h, g.reshape(1, d), pool_w, pool_scale[j].reshape(1, d))


def _route_kernel(h_ref, g_ref, whi_ref, wlo_ref, b_ref, tri_ref, hn_ref, ri_ref, rg_ref, cnt_ref,
                  run_ref, *, tm):
    i = pl.program_id(0)

    @pl.when(i == 0)
    def _():
        run_ref[...] = jnp.zeros_like(run_ref)

    hn = _rms(h_ref[...], g_ref[...])
    hn_bf = hn.astype(BF16)
    hn_ref[...] = _pack_rows(hn).reshape(hn_ref.shape)
    hn_lo = (hn - hn_bf.astype(F32)).astype(BF16)
    whi = whi_ref[...]
    logits = _dot(hn_bf, whi) + _dot(hn_bf, wlo_ref[...]) + _dot(hn_lo, whi) + b_ref[...]

    lane = lax.broadcasted_iota(I32, (tm, LANES), 1)
    lane_f = lane.astype(F32)
    neg = -jnp.inf
    big = float(LANES)
    gl = jnp.where(lane < MOE_GROUPS, logits, neg)
    gmax = jnp.max(gl, axis=-1, keepdims=True)
    gsel = jnp.min(jnp.where(gl == gmax, lane_f, big), axis=-1, keepdims=True).astype(I32)
    p_grp = 1.0 / jnp.sum(jnp.exp(gl - gmax), axis=-1, keepdims=True)
    lo_lane = MOE_GROUPS + MOE_EPG * gsel
    el = jnp.where((lane >= lo_lane) & (lane < lo_lane + MOE_EPG), logits, neg)
    v0 = jnp.max(el, axis=-1, keepdims=True)
    i0 = jnp.min(jnp.where(el == v0, lane_f, big), axis=-1, keepdims=True).astype(I32)
    el2 = jnp.where(lane == i0, neg, el)
    v1 = jnp.max(el2, axis=-1, keepdims=True)
    i1 = jnp.min(jnp.where(el2 == v1, lane_f, big), axis=-1, keepdims=True).astype(I32)
    t = jnp.exp(v1 - v0)
    gate0 = p_grp / (1.0 + t)
    gate1 = p_grp * t / (1.0 + t)
    e0 = i0 - MOE_GROUPS
    e1 = i1 - MOE_GROUPS

    onehot = ((lane == e0) | (lane == e1))
    c_bf = jnp.where(onehot, 1.0, 0.0).astype(BF16)
    prefix = _dot(tri_ref[...], c_bf) + run_ref[...]
    rank0 = jnp.sum(jnp.where(lane == e0, prefix, 0.0), axis=-1, keepdims=True).astype(I32)
    rank1 = jnp.sum(jnp.where(lane == e1, prefix, 0.0), axis=-1, keepdims=True).astype(I32)
    run = run_ref[...] + jnp.sum(jnp.where(onehot, 1.0, 0.0), axis=0, keepdims=True)
    run_ref[...] = run
    cnt_ref[...] = jnp.broadcast_to(run, cnt_ref.shape)

    ri = jnp.where(lane == 0, e0, jnp.where(lane == 1, e1, jnp.where(lane == 2, rank0, rank1)))
    ri_ref[...] = ri.T[0:8, :]
    rg_ref[...] = jnp.where(lane == 0, gate0, gate1)


def _route(h, g, wr, br, tm=ROUTE_TILE):
    n, d = h.shape
    whi = wr.astype(BF16)
    wlo = (wr - whi.astype(F32)).astype(BF16)
    tri = jnp.tril(jnp.ones((tm, tm), BF16), -1)
    return pl.pallas_call(
        functools.partial(_route_kernel, tm=tm),
        out_shape=(jax.ShapeDtypeStruct((n, 1, d // 2), U32),
                   jax.ShapeDtypeStruct((8, n), I32),
                   jax.ShapeDtypeStruct((n, LANES), F32),
                   jax.ShapeDtypeStruct((8, LANES), F32)),
        grid=(n // tm,),
        in_specs=[pl.BlockSpec((tm, d), lambda i: (i, 0)),
                  pl.BlockSpec((1, d), lambda i: (0, 0)),
                  pl.BlockSpec((d, LANES), lambda i: (0, 0)),
                  pl.BlockSpec((d, LANES), lambda i: (0, 0)),
                  pl.BlockSpec((1, LANES), lambda i: (0, 0)),
                  pl.BlockSpec((tm, tm), lambda i: (0, 0))],
        out_specs=(pl.BlockSpec((tm, 1, d // 2), lambda i: (i, 0, 0)),
                   pl.BlockSpec((8, tm), lambda i: (0, i)),
                   pl.BlockSpec((tm, LANES), lambda i: (i, 0)),
                   pl.BlockSpec((8, LANES), lambda i: (0, 0))),
        scratch_shapes=[pltpu.VMEM((1, LANES), F32)],
        compiler_params=_cparams(1),
        name="moe_route",
    )(h, g.reshape(1, d), whi, wlo, br, tri)


def _dispatch_kernel(dest_ref, hn_ref, xs_in_ref, xs_ref, sem, *, t):
    del xs_in_ref

    def row_copy(r, k):
        return pltpu.make_async_copy(hn_ref.at[r], xs_ref.at[dest_ref[0, k * t + r]], sem)

    def start(r, c):
        row_copy(r, 0).start()
        row_copy(r, 1).start()
        return c

    def wait(r, c):
        row_copy(r, 0).wait()
        row_copy(r, 1).wait()
        return c

    lax.fori_loop(0, t, start, 0)
    lax.fori_loop(0, t, wait, 0)


def _dispatch(hn, dest, n_rows, t=MOE_TOK_TILE):
    n, _, d = hn.shape
    xs0 = jnp.zeros((n_rows, 1, d), hn.dtype)
    return pl.pallas_call(
        functools.partial(_dispatch_kernel, t=t),
        out_shape=jax.ShapeDtypeStruct((n_rows, 1, d), hn.dtype),
        grid=(n // t,),
        in_specs=[pl.BlockSpec((None, 1, 2 * t), lambda i: (i, 0, 0), memory_space=pltpu.SMEM),
                  pl.BlockSpec((t, 1, d), lambda i: (i, 0, 0)),
                  pl.BlockSpec(memory_space=pl.ANY)],
        out_specs=pl.BlockSpec(memory_space=pl.ANY),
        scratch_shapes=[pltpu.SemaphoreType.DMA(())],
        input_output_aliases={2: 0},
        compiler_params=_cparams(1),
        name="moe_dispatch",
    )(dest, hn, xs0)


def _expert_kernel(bexp_ref, nused_ref, xs_ref, wg_ref, wu_ref, wd_ref, o_ref, wg_bf, wu_bf, wd_bf):
    b = pl.program_id(0)
    prev = bexp_ref[jnp.maximum(b - 1, 0)]
    active = b < nused_ref[0]

    @pl.when(active & ((b == 0) | (bexp_ref[b] != prev)))
    def _():
        wg_bf[...] = wg_ref[...].astype(BF16)
        wu_bf[...] = wu_ref[...].astype(BF16)
        wd_bf[...] = wd_ref[...].astype(BF16)

    @pl.when(active)
    def _():
        rows, _, half = xs_ref.shape
        x_lo, x_hi = _unpack_rows(xs_ref[...].reshape(rows, half))
        x = jnp.concatenate([x_lo.astype(BF16), x_hi.astype(BF16)], axis=1)
        gte = _dot(x, wg_bf[...])
        up = _dot(x, wu_bf[...])
        hb = gte * _sigmoid(gte) * up
        o_ref[...] = _pack_rows(_dot(hb.astype(BF16), wd_bf[...])).reshape(o_ref.shape)

    @pl.when(jnp.logical_not(active))
    def _():
        o_ref[...] = jnp.zeros_like(o_ref)


def _experts(xs, block_exp, n_used, gate_w, up_w, down_w, layer, rows=MOE_ROWS):
    n_rows, _, half = xs.shape
    d = 2 * half
    hid = gate_w.shape[-1]
    grid_spec = pltpu.PrefetchScalarGridSpec(
        num_scalar_prefetch=2,
        grid=(n_rows // rows,),
        in_specs=[pl.BlockSpec((rows, 1, half), lambda b, be, nu: (b, 0, 0)),
                  pl.BlockSpec((None, None, d, hid), lambda b, be, nu: (layer, be[b], 0, 0)),
                  pl.BlockSpec((None, None, d, hid), lambda b, be, nu: (layer, be[b], 0, 0)),
                  pl.BlockSpec((None, None, hid, d), lambda b, be, nu: (layer, be[b], 0, 0))],
        out_specs=pl.BlockSpec((rows, 1, half), lambda b, be, nu: (b, 0, 0)),
        scratch_shapes=[pltpu.VMEM((d, hid), BF16), pltpu.VMEM((d, hid), BF16), pltpu.VMEM((hid, d), BF16)],
    )
    return pl.pallas_call(
        _expert_kernel,
        out_shape=jax.ShapeDtypeStruct((n_rows, 1, half), U32),
        grid_spec=grid_spec,
        compiler_params=_cparams(1),
        name="moe_experts",
    )(block_exp, n_used, xs, gate_w, up_w, down_w)


def _combine_kernel(*refs, t, mode):
    dest_ref, dest_next_ref, h_ref, rg_ref, ys_ref = refs[:5]
    rest = list(refs[5:])
    g_ref = rest.pop(0) if mode != "plain" else None
    o_ref = rest.pop(0)
    hn_ref = rest.pop(0) if mode == "norm_bf16" else None
    buf00, buf01, buf10, buf11, flat0, flat1, sems = rest
    bufs = ((buf00, buf01), (buf10, buf11))
    i = pl.program_id(0)
    n = pl.num_programs(0)

    def row_copy(dref, slot, r, k):
        return pltpu.make_async_copy(ys_ref.at[dref[0, k * t + r]], bufs[slot][k].at[r], sems.at[slot])

    def issue(dref, slot):
        def start(r, c):
            row_copy(dref, slot, r, 0).start()
            row_copy(dref, slot, r, 1).start()
            return c
        lax.fori_loop(0, t, start, 0)

    def drain(slot):
        def wait(r, c):
            row_copy(dest_ref, slot, r, 0).wait()
            row_copy(dest_ref, slot, r, 1).wait()
            return c
        lax.fori_loop(0, t, wait, 0)

    @pl.when(i == 0)
    def _():
        issue(dest_ref, 0)

    for slot in (0, 1):
        @pl.when((i % 2 == slot) & (i + 1 < n))
        def _():
            issue(dest_next_ref, 1 - slot)

        @pl.when(i % 2 == slot)
        def _():
            drain(slot)
            flat0[...] = bufs[slot][0][...].reshape(flat0.shape)
            flat1[...] = bufs[slot][1][...].reshape(flat1.shape)

    rg = rg_ref[...]
    g0 = rg[:, 0:1]
    g1 = rg[:, 1:2]
    a_lo, a_hi = _unpack_rows(flat0[...])
    b_lo, b_hi = _unpack_rows(flat1[...])
    y = jnp.concatenate([g0 * a_lo + g1 * b_lo, g0 * a_hi + g1 * b_hi], axis=1)
    h_new = h_ref[...] + y
    if mode == "plain":
        o_ref[...] = h_new
    elif mode == "norm_bf16":
        o_ref[...] = h_new
        hn_ref[...] = _rms(h_new, g_ref[...]).astype(BF16)
    else:
        o_ref[...] = _rms(h_new, g_ref[...])


def _combine(h, rg, ys, dest, g_next, mode, t=MOE_TOK_TILE):
    n, d = h.shape
    half = ys.shape[-1]
    last = n // t - 1
    in_specs = [pl.BlockSpec((None, 1, 2 * t), lambda i: (i, 0, 0), memory_space=pltpu.SMEM),
                pl.BlockSpec((None, 1, 2 * t), lambda i: (jnp.minimum(i + 1, last), 0, 0),
                             memory_space=pltpu.SMEM),
                pl.BlockSpec((t, d), lambda i: (i, 0)),
                pl.BlockSpec((t, LANES), lambda i: (i, 0)),
                pl.BlockSpec(memory_space=pl.ANY)]
    args = [dest, dest, h, rg, ys]
    tile = pl.BlockSpec((t, d), lambda i: (i, 0))
    if mode != "plain":
        in_specs.append(pl.BlockSpec((1, d), lambda i: (0, 0)))
        args.append(g_next.reshape(1, d))
    if mode == "norm_bf16":
        out_shape = (jax.ShapeDtypeStruct((n, d), F32), jax.ShapeDtypeStruct((n, d), BF16))
        out_specs = (tile, tile)
    else:
        out_shape = jax.ShapeDtypeStruct((n, d), F32)
        out_specs = tile
    return pl.pallas_call(
        functools.partial(_combine_kernel, t=t, mode=mode),
        out_shape=out_shape,
        grid=(n // t,),
        in_specs=in_specs,
        out_specs=out_specs,
        scratch_shapes=[pltpu.VMEM((t, 1, half), U32), pltpu.VMEM((t, 1, half), U32),
                        pltpu.VMEM((t, 1, half), U32), pltpu.VMEM((t, 1, half), U32),
                        pltpu.VMEM((t, half), U32), pltpu.VMEM((t, half), U32),
                        pltpu.SemaphoreType.DMA((2,))],
        compiler_params=_cparams(1),
        name="moe_combine",
    )(*args)


def _moe_layer(h, layer, norm_ffn, group_w, group_b, router_w, router_b, gate_w, up_w, down_w, g_next, mode):
    n, d = h.shape
    pad = LANES - MOE_GROUPS - MOE_EXPERTS
    wr = jnp.concatenate([group_w[layer], router_w[layer], jnp.zeros((d, pad), F32)], axis=1)
    br = jnp.concatenate([group_b[layer], router_b[layer], jnp.zeros((pad,), F32)]).reshape(1, LANES)
    hn, ri, rg, cnt = _route(h, norm_ffn[layer], wr, br)

    t = MOE_TOK_TILE
    counts = cnt[0, :MOE_EXPERTS].astype(I32)
    nblk = (counts + MOE_ROWS - 1) // MOE_ROWS
    blk_end = jnp.cumsum(nblk)
    row_start = (blk_end - nblk) * MOE_ROWS
    n_rows = -(-(n * 2 + MOE_EXPERTS * (MOE_ROWS - 1)) // MOE_ROWS) * MOE_ROWS
    n_blocks = n_rows // MOE_ROWS
    experts = ri[0:2, :]
    ranks = ri[2:4, :]
    onehot = experts[:, :, None] == jnp.arange(MOE_EXPERTS, dtype=I32)
    dest = jnp.sum(jnp.where(onehot, row_start, 0), axis=-1) + ranks
    dest = dest.reshape(2, n // t, t).transpose(1, 0, 2).reshape(n // t, 1, 2 * t).astype(I32)
    block_exp = jnp.minimum(jnp.sum(blk_end[None, :] <= jnp.arange(n_blocks, dtype=I32)[:, None], axis=1),
                            MOE_EXPERTS - 1).astype(I32)
    n_used = blk_end[-1:].astype(I32)

    xs = _dispatch(hn, dest, n_rows)
    ys = _experts(xs, block_exp, n_used, gate_w, up_w, down_w, layer)
    return _combine(h, rg, ys, dest, g_next, mode)


def _ssd_kernel(z_ref, x_ref, b_ref, c_ref, dt_ref, cwx_ref, cwb_ref, cwc_ref, cbx_ref, cbb_ref, cbc_ref,
                dtb_ref, a_ref, dexp_ref, nw_ref, exp_ref, tri_ref, o_ref,
                xbuf, bbuf, cbuf, xc_s, xdt_s, xw_s, b_s, c_s, eacs_s, acs_s, acst_s, y_s, st_s,
                *, L, hp, n_state, groups):
    ci = pl.program_id(0)
    d_inner = x_ref.shape[1]
    gw = d_inner // groups
    kh = gw // hp
    H0 = CONV_HALO

    @pl.when(ci == 0)
    def _():
        xbuf[0:H0, :] = jnp.zeros((H0, xbuf.shape[1]), F32)
        bbuf[0:H0, :] = jnp.zeros((H0, bbuf.shape[1]), F32)
        cbuf[0:H0, :] = jnp.zeros((H0, cbuf.shape[1]), F32)
        st_s[...] = jnp.zeros_like(st_s)

    def conv_silu(in_ref, buf, cw_ref, cb_ref):
        buf[H0:H0 + L, :] = in_ref[...]
        acc = cb_ref[...] + cw_ref[SSD_CONV - 1:SSD_CONV, :] * buf[H0:H0 + L, :]
        for j in range(SSD_CONV - 1):
            off = H0 - (SSD_CONV - 1) + j
            acc = acc + cw_ref[j:j + 1, :] * buf[off:off + L, :]
        buf[0:H0, :] = buf[L:L + H0, :]
        return acc * _sigmoid(acc)

    xc_s[...] = conv_silu(x_ref, xbuf, cwx_ref, cbx_ref)
    b_s[...] = conv_silu(b_ref, bbuf, cwb_ref, cbb_ref).astype(BF16)
    c_s[...] = conv_silu(c_ref, cbuf, cwc_ref, cbc_ref).astype(BF16)

    dt = _softplus(dt_ref[...] + dtb_ref[...])
    d_a = dt * a_ref[...]
    acs = _dot3_lhs_exact(tri_ref[...], d_a)
    acs_s[...] = acs
    acst_s[...] = acs.T
    emat = exp_ref[...]
    acs_e = _dot3_rhs_exact(acs, emat)
    dt_e = _dot3_rhs_exact(dt, emat)
    acs_last = acs_e[L - 1:L, :]
    xdt = xc_s[...] * dt_e
    xdt_s[...] = xdt.astype(BF16)
    xw_s[...] = (xdt * jnp.exp(acs_last - acs_e)).astype(BF16)
    eacs_s[...] = jnp.exp(acs_e)
    cdecay = jnp.exp(acs_last)

    rr = lax.broadcasted_iota(I32, (L, L), 0)
    cc = lax.broadcasted_iota(I32, (L, L), 1)
    causal = rr >= cc
    lane2 = lax.broadcasted_iota(I32, (L, 2 * hp), 1)

    for g in range(groups):
        bg = b_s[:, g * n_state:(g + 1) * n_state]
        cg = c_s[:, g * n_state:(g + 1) * n_state]
        cb = _dot_nt(cg, bg)
        st_prev = st_s[g]
        y_off = _dot(cg, st_prev.astype(BF16)) * eacs_s[:, g * gw:(g + 1) * gw]
        for kp in range(kh // 2):
            c0 = g * gw + kp * 2 * hp
            slab = xdt_s[:, c0:c0 + 2 * hp]
            y_pair = None
            for half in range(2):
                hd = g * kh + kp * 2 + half
                seg = acs_s[:, hd:hd + 1] - acst_s[hd:hd + 1, :]
                decay = jnp.exp(jnp.where(causal, seg, -jnp.inf))
                m = (cb * decay).astype(BF16)
                keep = (lane2 < hp) if half == 0 else (lane2 >= hp)
                part = _dot(m, jnp.where(keep, slab, jnp.zeros_like(slab)))
                y_pair = part if y_pair is None else y_pair + part
            y_s[:, c0:c0 + 2 * hp] = y_pair + y_off[:, kp * 2 * hp:(kp + 1) * 2 * hp]
        bgt = bg.astype(F32).T.astype(BF16)
        s_new = _dot(bgt, xw_s[:, g * gw:(g + 1) * gw])
        st_s[g] = cdecay[:, g * gw:(g + 1) * gw] * st_prev + s_new

    y = y_s[...] + dexp_ref[...] * xc_s[...]
    z = z_ref[...]
    y = y * (z * _sigmoid(z))
    for g in range(groups):
        yg = y[:, g * gw:(g + 1) * gw]
        yg = yg * lax.rsqrt(jnp.mean(yg * yg, axis=-1, keepdims=True) + RMS_EPS)
        o_ref[:, g * gw:(g + 1) * gw] = (yg * nw_ref[:, g * gw:(g + 1) * gw]).astype(o_ref.dtype)


def _ssd_scan(zx, dt_raw, conv_w, conv_b, dt_bias, a_log, d_skip, norm_w, d_inner):
    s = zx.shape[0]
    L = SSD_CHUNK
    hp, n_state, groups = SSD_HEAD_DIM, SSD_STATE, SSD_GROUPS
    heads = d_inner // hp
    gn = groups * n_state
    gw = d_inner // groups
    assert d_inner % gn == 0 and heads <= LANES
    pad = LANES - heads
    dtb = jnp.pad(dt_bias, (0, pad)).reshape(1, LANES)
    a_neg = jnp.pad(-jnp.exp(a_log), (0, pad)).reshape(1, LANES)
    dexp = jnp.repeat(d_skip, hp).reshape(1, d_inner)
    emat = (jnp.arange(LANES)[:, None] == (jnp.arange(d_inner)[None, :] // hp)).astype(BF16)
    tri = jnp.tril(jnp.ones((L, L), BF16))
    cb2 = conv_b.reshape(1, -1)
    xb = d_inner // gn
    row = lambda c: (c, 0)
    const = lambda c: (0, 0)
    in_specs = [
        pl.BlockSpec((L, d_inner), row),
        pl.BlockSpec((L, d_inner), lambda c: (c, 1)),
        pl.BlockSpec((L, gn), lambda c: (c, 2 * xb)),
        pl.BlockSpec((L, gn), lambda c: (c, 2 * xb + 1)),
        pl.BlockSpec((L, LANES), row),
        pl.BlockSpec((SSD_CONV, d_inner), const),
        pl.BlockSpec((SSD_CONV, gn), lambda c: (0, xb)),
        pl.BlockSpec((SSD_CONV, gn), lambda c: (0, xb + 1)),
        pl.BlockSpec((1, d_inner), const),
        pl.BlockSpec((1, gn), lambda c: (0, xb)),
        pl.BlockSpec((1, gn), lambda c: (0, xb + 1)),
        pl.BlockSpec((1, LANES), const),
        pl.BlockSpec((1, LANES), const),
        pl.BlockSpec((1, d_inner), const),
        pl.BlockSpec((1, d_inner), const),
        pl.BlockSpec((LANES, d_inner), const),
        pl.BlockSpec((L, L), const),
    ]
    scratch = [
        pltpu.VMEM((CONV_HALO + L, d_inner), F32), pltpu.VMEM((CONV_HALO + L, gn), F32),
        pltpu.VMEM((CONV_HALO + L, gn), F32),
        pltpu.VMEM((L, d_inner), F32),
        pltpu.VMEM((L, d_inner), BF16),
        pltpu.VMEM((L, d_inner), BF16),
        pltpu.VMEM((L, gn), BF16), pltpu.VMEM((L, gn), BF16),
        pltpu.VMEM((L, d_inner), F32),
        pltpu.VMEM((L, LANES), F32), pltpu.VMEM((LANES, L), F32),
        pltpu.VMEM((L, d_inner), F32),
        pltpu.VMEM((groups, n_state, gw), F32),
    ]
    return pl.pallas_call(
        functools.partial(_ssd_kernel, L=L, hp=hp, n_state=n_state, groups=groups),
        out_shape=jax.ShapeDtypeStruct((s, d_inner), BF16),
        grid=(s // L,),
        in_specs=in_specs,
        out_specs=pl.BlockSpec((L, d_inner), row),
        scratch_shapes=scratch,
        compiler_params=_cparams(1),
        name="ssd_scan",
    )(zx, zx, zx, zx, dt_raw, conv_w, conv_w, conv_w, cb2, cb2, cb2, dtb, a_neg, dexp,
      norm_w.reshape(1, d_inner), emat, tri)


def _ssd_mixer(h, hn, in_w, conv_w, conv_b, dt_bias, a_log, d_skip, norm_w, out_w):
    d = hn.shape[1]
    d_inner = out_w.shape[0]
    heads = d_inner // SSD_HEAD_DIM
    wide = in_w.shape[1] - heads
    zx = _matmul(hn, in_w, col0=0, ncols=wide, tn=512, name="ssd_in_proj")
    w_dt = jnp.pad(in_w[:, wide:], ((0, 0), (0, LANES - heads)))
    dt_raw = _matmul(hn, w_dt, tn=LANES, name="ssd_dt_proj")
    y = _ssd_scan(zx, dt_raw, conv_w, conv_b, dt_bias, a_log, d_skip, norm_w, d_inner)
    return _matmul(y, out_w, tn=512, residual=h, name="ssd_out_proj")


def _fox_cum_kernel(f_ref, b_ref, tri_ref, o_ref, carry):
    @pl.when(pl.program_id(0) == 0)
    def _():
        carry[...] = jnp.zeros_like(carry)

    x = f_ref[...] + b_ref[...]
    log_f = jnp.minimum(x, 0.0) - jnp.log1p(jnp.exp(-jnp.abs(x)))
    cs = _dot3_rhs_exact(log_f, tri_ref[...]) + carry[...]
    o_ref[...] = cs
    carry[...] = cs[:, cs.shape[1] - 1:]


def _fox_cum(f_t, bias, tk=512):
    nh, s = f_t.shape
    triu = jnp.triu(jnp.ones((tk, tk), BF16))
    return pl.pallas_call(
        _fox_cum_kernel,
        out_shape=jax.ShapeDtypeStruct((nh, s), F32),
        grid=(s // tk,),
        in_specs=[pl.BlockSpec((nh, tk), lambda i: (0, i)),
                  pl.BlockSpec((nh, 1), lambda i: (0, 0)),
                  pl.BlockSpec((tk, tk), lambda i: (0, 0))],
        out_specs=pl.BlockSpec((nh, tk), lambda i: (0, i)),
        scratch_shapes=[pltpu.VMEM((nh, 1), F32)],
        compiler_params=_cparams(1),
        name="fox_cum",
    )(f_t, bias.reshape(nh, 1), triu)


def _fox_attn_kernel(q_ref, k_ref, v_ref, cum_ref, o_ref, m_s, acc_s, vext, *, tq, tk, sr):
    qi = pl.program_id(1)
    dh = q_ref.shape[1]
    n_sub = tq // sr

    @pl.when(qi == 0)
    def _():
        vext[:, 0:dh] = v_ref[...]
        vext[:, dh:2 * dh] = jnp.ones((vext.shape[0], dh), BF16)

    q0 = pl.multiple_of(qi * tq, tq)
    c_ref = cum_ref[:, pl.ds(q0, LANES)][:, 0:1]
    m_s[...] = jnp.full_like(m_s, -jnp.inf)
    acc_s[...] = jnp.zeros_like(acc_s)

    def sub_step(r, k0, bias, col_off):
        rows = slice(r * sr, (r + 1) * sr)
        s = _dot_nt(q_ref[rows, :], k_ref[pl.ds(k0, tk), :]) + bias
        if col_off is not None:
            q_pos = r * sr + lax.broadcasted_iota(I32, (sr, tk), 0)
            k_pos = col_off + lax.broadcasted_iota(I32, (sr, tk), 1)
            s = jnp.where(q_pos >= k_pos, s, -jnp.inf)
        m_prev = m_s[rows, :]
        m_new = jnp.maximum(m_prev, jnp.max(s, axis=-1, keepdims=True))
        alpha = jnp.exp2(m_prev - m_new)
        p = jnp.exp2(s - jnp.concatenate([m_new] * (tk // dh), axis=1)).astype(BF16)
        pv = _dot(p, vext[pl.ds(k0, tk), :])
        acc_s[rows, :] = jnp.concatenate([alpha, alpha], axis=1) * acc_s[rows, :] + pv
        m_s[rows, :] = m_new

    def chunk_bias(k0):
        return (c_ref - cum_ref[:, pl.ds(k0, tk)]) * LOG2E

    def body(j, c):
        k0 = pl.multiple_of(j * tk, tk)
        bias = chunk_bias(k0)
        for r in range(n_sub):
            sub_step(r, k0, bias, None)
        return c

    lax.fori_loop(0, qi * (tq // tk), body, 0)
    for c in range(tq // tk):
        k0 = pl.multiple_of(q0 + c * tk, tk)
        bias = chunk_bias(k0)
        for r in range(n_sub):
            if (r + 1) * sr <= c * tk:
                continue
            straddles = r * sr < (c + 1) * tk - 1
            sub_step(r, k0, bias, c * tk if straddles else None)
    acc = acc_s[...]
    o_ref[...] = (acc[:, 0:dh] / acc[:, dh:2 * dh]).astype(o_ref.dtype)


def _fox_attention(qkv, cum, nh, tq=FOX_TQ, tk=FOX_TK, sr=FOX_SUB):
    s = qkv.shape[0]
    dh = FOX_HEAD_DIM
    assert tq % tk == 0 and tq % sr == 0 and tk % dh == 0 and dh == LANES
    cum3 = cum.reshape(nh, 1, s)
    return pl.pallas_call(
        functools.partial(_fox_attn_kernel, tq=tq, tk=tk, sr=sr),
        out_shape=jax.ShapeDtypeStruct((s, nh * dh), BF16),
        grid=(nh, s // tq),
        in_specs=[pl.BlockSpec((tq, dh), lambda h, i: (i, h)),
                  pl.BlockSpec((s, dh), lambda h, i: (0, nh + h)),
                  pl.BlockSpec((s, dh), lambda h, i: (0, 2 * nh + h)),
                  pl.BlockSpec((None, 1, s), lambda h, i: (h, 0, 0))],
        out_specs=pl.BlockSpec((tq, dh), lambda h, i: (i, h)),
        scratch_shapes=[pltpu.VMEM((tq, dh), F32), pltpu.VMEM((tq, 2 * dh), F32), pltpu.VMEM((s, 2 * dh), BF16)],
        compiler_params=_cparams(2),
        name="fox_attention",
    )(qkv, qkv, qkv, cum3)


def _fox_mixer(h, hn, in_w, f_bias, out_w):
    d = hn.shape[1]
    nh = d // FOX_HEAD_DIM
    q_scale = FOX_HEAD_DIM ** -0.5 * LOG2E
    colscale = jnp.concatenate([jnp.full((1, d), q_scale, F32), jnp.ones((1, 2 * d), F32)], axis=1)
    qkv = _matmul(hn, in_w, col0=0, ncols=3 * d, tn=512, out_dtype=BF16, colscale=colscale, name="fox_in_proj")
    w_f = jnp.pad(in_w[:, 3 * d:], ((0, 0), (0, LANES - nh)))
    f_raw = _matmul(hn, w_f, tn=LANES, name="fox_f_proj")
    cum = _fox_cum(f_raw[:, :nh].T, f_bias)
    o = _fox_attention(qkv, cum, nh)
    return _matmul(o, out_w, tn=512, residual=h, name="fox_out_proj")


def _norm_kernel(h_ref, g_ref, o_ref):
    o_ref[...] = _rms(h_ref[...], g_ref[...]).astype(o_ref.dtype)


def _norm(h, g, out_dtype, tm=512):
    n, d = h.shape
    return pl.pallas_call(
        _norm_kernel,
        out_shape=jax.ShapeDtypeStruct((n, d), out_dtype),
        grid=(n // tm,),
        in_specs=[pl.BlockSpec((tm, d), lambda i: (i, 0)), pl.BlockSpec((1, d), lambda i: (0, 0))],
        out_specs=pl.BlockSpec((tm, d), lambda i: (i, 0)),
        compiler_params=_cparams(1),
        name="rmsnorm",
    )(h, g.reshape(1, d))


def kernel(x, norm_mix, norm_ffn, final_norm, pool_w, pool_scale, ssd_in_w, ssd_conv_w, ssd_conv_b,
           ssd_dt_bias, ssd_a_log, ssd_d_skip, ssd_norm, ssd_out_w, fox_in_w, fox_f_bias, fox_out_w,
           moe_group_w, moe_group_b, moe_router_w, moe_router_b, moe_gate_w, moe_up_w, moe_down_w):
    bsz, s, d = x.shape
    assert bsz == 1
    depth = norm_mix.shape[0]
    h = x.reshape(s, d)
    hn = None
    for i in range(depth):
        kind, j = i % 3, i // 3
        if kind == 0:
            h = _pool_layer(h, norm_mix[i], pool_w, pool_scale, j)
        else:
            if hn is None:
                hn = _norm(h, norm_mix[i], BF16)
            if kind == 1:
                h = _ssd_mixer(h, hn, ssd_in_w[j], ssd_conv_w[j], ssd_conv_b[j], ssd_dt_bias[j], ssd_a_log[j],
                               ssd_d_skip[j], ssd_norm[j], ssd_out_w[j])
            else:
                h = _fox_mixer(h, hn, fox_in_w[j], fox_f_bias[j], fox_out_w[j])
        last = i == depth - 1
        next_kind = (i + 1) % 3
        if last:
            g_next, mode = final_norm, "final"
        elif next_kind == 0:
            g_next, mode = None, "plain"
        else:
            g_next, mode = norm_mix[i + 1], "norm_bf16"
        out = _moe_layer(h, i, norm_ffn, moe_group_w, moe_group_b, moe_router_w, moe_router_b,
                         moe_gate_w, moe_up_w, moe_down_w, g_next, mode)
        if mode == "norm_bf16":
            h, hn = out
        else:
            h, hn = out, None
    return h.reshape(bsz, s, d)
```

```python
import functools

import jax
import jax.numpy as jnp
from jax import lax
from jax.experimental import pallas as pl
from jax.experimental.pallas import tpu as pltpu

F32 = jnp.float32
BF16 = jnp.bfloat16
I32 = jnp.int32
U32 = jnp.uint32

RMS_EPS = 1e-6
LOG2E = 1.4426950408889634
LANES = 128
SLAB = 8
VMEM_LIMIT = 56 * 1024 * 1024

POOL_WINDOWS = (2, 4, 8, 16)
POOL_HALO = 16

SSD_HEAD_DIM = 64
SSD_STATE = 128
SSD_GROUPS = 8
SSD_CONV = 4
SSD_CHUNK = 128
CONV_HALO = 8

FOX_HEAD_DIM = 128
FOX_TQ = 1024
FOX_TK = 512
FOX_SUB = 256

MOE_GROUPS = 4
MOE_EPG = 8
MOE_EXPERTS = MOE_GROUPS * MOE_EPG
MOE_ROWS = 256
MOE_TOK_TILE = 256
ROUTE_TILE = 512


def _cparams(n_axes, vmem=VMEM_LIMIT):
    return pltpu.CompilerParams(dimension_semantics=("arbitrary",) * n_axes, vmem_limit_bytes=vmem)


def _rms(x, g):
    ms = jnp.mean(x * x, axis=-1, keepdims=True)
    return x * lax.rsqrt(ms + RMS_EPS) * g


def _split3(x):
    hi = x.astype(BF16)
    r = x - hi.astype(F32)
    mid = r.astype(BF16)
    lo = (r - mid.astype(F32)).astype(BF16)
    return hi, mid, lo


def _dot(a, b):
    return jnp.dot(a, b, preferred_element_type=F32)


def _dot_nt(a, b):
    return lax.dot_general(a, b, (((1,), (1,)), ((), ())), preferred_element_type=F32)


def _dot3_rhs_exact(x, m):
    hi, mid, lo = _split3(x)
    return _dot(hi, m) + _dot(mid, m) + _dot(lo, m)


def _dot3_lhs_exact(m, x):
    hi, mid, lo = _split3(x)
    return _dot(m, hi) + _dot(m, mid) + _dot(m, lo)


def _sigmoid(x):
    return 1.0 / (1.0 + jnp.exp(-x))


def _softplus(x):
    return jnp.maximum(x, 0.0) + jnp.log1p(jnp.exp(-jnp.abs(x)))


def _pack_rows(x):
    half = x.shape[1] // 2
    lo = lax.bitcast_convert_type(x[:, :half].astype(BF16).astype(F32), U32)
    hi = lax.bitcast_convert_type(x[:, half:].astype(BF16).astype(F32), U32)
    return hi | (lo >> 16)


def _unpack_rows(u):
    lo = lax.bitcast_convert_type(u << 16, F32)
    hi = lax.bitcast_convert_type(u & jnp.uint32(0xFFFF0000), F32)
    return lo, hi


def _load_token_slabs(ref, rows):
    return [ref[pl.ds(c, rows, stride=SLAB), :] for c in range(SLAB)]


def _store_token_slabs(ref, packed):
    rows = packed.shape[0]
    for c in range(SLAB):
        ref[pl.ds(c, rows, stride=SLAB), :] = packed[:, c * LANES:(c + 1) * LANES]


def _slab(ref, token):
    return ref.at[pl.ds(pl.multiple_of(token * SLAB, SLAB), SLAB)]


def _mm_kernel(*refs, has_res, has_scale, transposed):
    x_ref, w_ref = refs[:2]
    rest = list(refs[2:])
    r_ref = rest.pop(0) if has_res else None
    s_ref = rest.pop(0) if has_scale else None
    o_ref, wbf = rest

    @pl.when(pl.program_id(1) == 0)
    def _():
        if transposed and wbf.shape[0] != w_ref.shape[0]:
            wbf[...] = jnp.zeros_like(wbf)
            wbf[0:w_ref.shape[0], :] = w_ref[...].astype(BF16)
        else:
            wbf[...] = w_ref[...].astype(BF16)

    acc = _dot_nt(x_ref[...], wbf[...]) if transposed else _dot(x_ref[...], wbf[...])
    if s_ref is not None:
        acc = acc * s_ref[...]
    if r_ref is not None:
        acc = acc + r_ref[...]
    o_ref[...] = acc.astype(o_ref.dtype)


def _matmul(x, w, *, layer=0, col0=0, ncols=None, tn=512, tm=512, residual=None, colscale=None,
            out_dtype=F32, name="mm"):
    m, k = x.shape
    ncols = w.shape[-1] - col0 if ncols is None else ncols
    assert ncols % tn == 0 and col0 % tn == 0 and m % tm == 0
    jb = col0 // tn
    if w.ndim == 3:
        w_spec = pl.BlockSpec((None, k, tn), lambda j, i: (layer, 0, jb + j))
    else:
        w_spec = pl.BlockSpec((k, tn), lambda j, i: (0, jb + j))
    in_specs = [pl.BlockSpec((tm, k), lambda j, i: (i, 0)), w_spec]
    args = [x, w]
    if residual is not None:
        in_specs.append(pl.BlockSpec((tm, tn), lambda j, i: (i, j)))
        args.append(residual)
    if colscale is not None:
        in_specs.append(pl.BlockSpec((1, tn), lambda j, i: (0, j)))
        args.append(colscale)
    return pl.pallas_call(
        functools.partial(_mm_kernel, has_res=residual is not None, has_scale=colscale is not None,
                          transposed=False),
        out_shape=jax.ShapeDtypeStruct((m, ncols), out_dtype),
        grid=(ncols // tn, m // tm),
        in_specs=in_specs,
        out_specs=pl.BlockSpec((tm, tn), lambda j, i: (i, j)),
        scratch_shapes=[pltpu.VMEM((k, tn), BF16)],
        compiler_params=_cparams(2),
        name=name,
    )(*args)


def _matmul_t(x, wt, *, layer, row0=0, nrows=None, tn=512, tm=512, colscale=None, out_dtype=F32, name="mm_t"):
    m, k = x.shape
    nrows = wt.shape[1] - row0 if nrows is None else nrows
    assert nrows % tn == 0 and row0 % tn == 0 and m % tm == 0 and tn % 8 == 0
    assert tn % LANES == 0 or nrows == tn
    tn_out = max(tn, LANES)
    jb = row0 // tn
    in_specs = [pl.BlockSpec((tm, k), lambda j, i: (i, 0)),
                pl.BlockSpec((None, tn, k), lambda j, i: (layer, jb + j, 0))]
    args = [x, wt]
    if colscale is not None:
        in_specs.append(pl.BlockSpec((1, tn_out), lambda j, i: (0, j)))
        args.append(colscale)
    return pl.pallas_call(
        functools.partial(_mm_kernel, has_res=False, has_scale=colscale is not None, transposed=True),
        out_shape=jax.ShapeDtypeStruct((m, (nrows // tn) * tn_out), out_dtype),
        grid=(nrows // tn, m // tm),
        in_specs=in_specs,
        out_specs=pl.BlockSpec((tm, tn_out), lambda j, i: (i, j)),
        scratch_shapes=[pltpu.VMEM((tn_out, k), BF16)],
        compiler_params=_cparams(2),
        name=name,
    )(*args)


def _pool_kernel(h_ref, g_ref, w_ref, scale_ref, o_ref, xbuf, wbf, *, tm, pg):
    i = pl.program_id(0)
    d = h_ref.shape[1]

    @pl.when(i == 0)
    def _():
        xbuf[0:POOL_HALO, :] = jnp.zeros((POOL_HALO, d), F32)
        wbf[...] = w_ref[...].astype(BF16)

    h = h_ref[...]
    hn = _rms(h, g_ref[...])
    xbuf[POOL_HALO:POOL_HALO + tm, :] = hn
    row = i * tm + lax.broadcasted_iota(I32, (tm, 1), 0)
    for gi, w in enumerate(POOL_WINDOWS):
        c0 = gi * pg
        cur = hn[:, c0:c0 + pg]
        acc = cur
        for j in range(1, w):
            acc = acc + xbuf[POOL_HALO - j:POOL_HALO - j + tm, c0:c0 + pg]
        cnt = jnp.minimum(row + 1, w).astype(F32)
        pooled = acc / cnt - cur
        mix = _dot(pooled.astype(BF16), wbf[gi]) * scale_ref[:, c0:c0 + pg]
        o_ref[:, c0:c0 + pg] = h[:, c0:c0 + pg] + mix
    xbuf[0:POOL_HALO, :] = xbuf[tm:tm + POOL_HALO, :]


def _pool_layer(h, g, pool_w, pool_scale, j, tm=256):
    n, d = h.shape
    nw, pg, _ = pool_w.shape[1:]
    return pl.pallas_call(
        functools.partial(_pool_kernel, tm=tm, pg=pg),
        out_shape=jax.ShapeDtypeStruct((n, d), F32),
        grid=(n // tm,),
        in_specs=[pl.BlockSpec((tm, d), lambda i: (i, 0)),
                  pl.BlockSpec((1, d), lambda i: (0, 0)),
                  pl.BlockSpec((None, nw, pg, pg), lambda i: (j, 0, 0, 0)),
                  pl.BlockSpec((1, d), lambda i: (0, 0))],
        out_specs=pl.BlockSpec((tm, d), lambda i: (i, 0)),
        scratch_shapes=[pltpu.VMEM((POOL_HALO + tm, d), F32), pltpu.VMEM((nw, pg, pg), BF16)],
        compiler_params=_cparams(1),
        name="pool_layer",
    )(h, g.reshape(1, d), pool_w, pool_scale[j].reshape(1, d))


def _route_kernel(h_ref, g_ref, whi_ref, wlo_ref, b_ref, tri_ref, hn_ref, ri_ref, rg_ref, cnt_ref,
                  run_ref, *, tm):
    i = pl.program_id(0)

    @pl.when(i == 0)
    def _():
        run_ref[...] = jnp.zeros_like(run_ref)

    hn = _rms(h_ref[...], g_ref[...])
    hn_bf = hn.astype(BF16)
    _store_token_slabs(hn_ref, _pack_rows(hn))
    hn_lo = (hn - hn_bf.astype(F32)).astype(BF16)
    whi = whi_ref[...]
    logits = _dot(hn_bf, whi) + _dot(hn_bf, wlo_ref[...]) + _dot(hn_lo, whi) + b_ref[...]

    lane = lax.broadcasted_iota(I32, (tm, LANES), 1)
    lane_f = lane.astype(F32)
    neg = -jnp.inf
    big = float(LANES)
    gl = jnp.where(lane < MOE_GROUPS, logits, neg)
    gmax = jnp.max(gl, axis=-1, keepdims=True)
    gsel = jnp.min(jnp.where(gl == gmax, lane_f, big), axis=-1, keepdims=True).astype(I32)
    p_grp = 1.0 / jnp.sum(jnp.exp(gl - gmax), axis=-1, keepdims=True)
    lo_lane = MOE_GROUPS + MOE_EPG * gsel
    el = jnp.where((lane >= lo_lane) & (lane < lo_lane + MOE_EPG), logits, neg)
    v0 = jnp.max(el, axis=-1, keepdims=True)
    i0 = jnp.min(jnp.where(el == v0, lane_f, big), axis=-1, keepdims=True).astype(I32)
    el2 = jnp.where(lane == i0, neg, el)
    v1 = jnp.max(el2, axis=-1, keepdims=True)
    i1 = jnp.min(jnp.where(el2 == v1, lane_f, big), axis=-1, keepdims=True).astype(I32)
    t = jnp.exp(v1 - v0)
    gate0 = p_grp / (1.0 + t)
    gate1 = p_grp * t / (1.0 + t)
    e0 = i0 - MOE_GROUPS
    e1 = i1 - MOE_GROUPS

    onehot = ((lane == e0) | (lane == e1))
    c_bf = jnp.where(onehot, 1.0, 0.0).astype(BF16)
    prefix = _dot(tri_ref[...], c_bf) + run_ref[...]
    rank0 = jnp.sum(jnp.where(lane == e0, prefix, 0.0), axis=-1, keepdims=True).astype(I32)
    rank1 = jnp.sum(jnp.where(lane == e1, prefix, 0.0), axis=-1, keepdims=True).astype(I32)
    run = run_ref[...] + jnp.sum(jnp.where(onehot, 1.0, 0.0), axis=0, keepdims=True)
    run_ref[...] = run
    cnt_ref[...] = jnp.broadcast_to(run, cnt_ref.shape)

    ri = jnp.where(lane == 0, e0, jnp.where(lane == 1, e1, jnp.where(lane == 2, rank0, rank1)))
    ri_ref[...] = ri.T[0:8, :]
    rg_ref[...] = jnp.where(lane == 0, gate0, gate1)


def _route(h, g, wr, br, tm=ROUTE_TILE):
    n, d = h.shape
    whi = wr.astype(BF16)
    wlo = (wr - whi.astype(F32)).astype(BF16)
    tri = jnp.tril(jnp.ones((tm, tm), BF16), -1)
    return pl.pallas_call(
        functools.partial(_route_kernel, tm=tm),
        out_shape=(jax.ShapeDtypeStruct((n * SLAB, LANES), U32),
                   jax.ShapeDtypeStruct((8, n), I32),
                   jax.ShapeDtypeStruct((n, LANES), F32),
                   jax.ShapeDtypeStruct((8, LANES), F32)),
        grid=(n // tm,),
        in_specs=[pl.BlockSpec((tm, d), lambda i: (i, 0)),
                  pl.BlockSpec((1, d), lambda i: (0, 0)),
                  pl.BlockSpec((d, LANES), lambda i: (0, 0)),
                  pl.BlockSpec((d, LANES), lambda i: (0, 0)),
                  pl.BlockSpec((1, LANES), lambda i: (0, 0)),
                  pl.BlockSpec((tm, tm), lambda i: (0, 0))],
        out_specs=(pl.BlockSpec((tm * SLAB, LANES), lambda i: (i, 0)),
                   pl.BlockSpec((8, tm), lambda i: (0, i)),
                   pl.BlockSpec((tm, LANES), lambda i: (i, 0)),
                   pl.BlockSpec((8, LANES), lambda i: (0, 0))),
        scratch_shapes=[pltpu.VMEM((1, LANES), F32)],
        compiler_params=_cparams(1),
        name="moe_route",
    )(h, g.reshape(1, d), whi, wlo, br, tri)


def _dispatch_kernel(dest_ref, hn_ref, xs_in_ref, xs_ref, sem, *, t):
    del xs_in_ref

    def row_copy(r, k):
        return pltpu.make_async_copy(_slab(hn_ref, r), _slab(xs_ref, dest_ref[0, k * t + r]), sem)

    def start(r, c):
        row_copy(r, 0).start()
        row_copy(r, 1).start()
        return c

    lax.fori_loop(0, t, start, 0, unroll=8)
    for _ in range(2):
        pltpu.make_async_copy(hn_ref, xs_ref.at[pl.ds(0, t * SLAB)], sem).wait()


def _dispatch(hn, dest, n_rows, t=MOE_TOK_TILE):
    n = hn.shape[0] // SLAB
    xs0 = jnp.zeros((n_rows * SLAB, LANES), hn.dtype)
    return pl.pallas_call(
        functools.partial(_dispatch_kernel, t=t),
        out_shape=jax.ShapeDtypeStruct((n_rows * SLAB, LANES), hn.dtype),
        grid=(n // t,),
        in_specs=[pl.BlockSpec((None, 1, 2 * t), lambda i: (i, 0, 0), memory_space=pltpu.SMEM),
                  pl.BlockSpec((t * SLAB, LANES), lambda i: (i, 0)),
                  pl.BlockSpec(memory_space=pl.ANY)],
        out_specs=pl.BlockSpec(memory_space=pl.ANY),
        scratch_shapes=[pltpu.SemaphoreType.DMA(())],
        input_output_aliases={2: 0},
        compiler_params=_cparams(1),
        name="moe_dispatch",
    )(dest, hn, xs0)


def _expert_kernel(bexp_ref, nused_ref, xs_ref, wg_ref, wu_ref, wd_ref, o_ref, wg_bf, wu_bf, wd_bf):
    b = pl.program_id(0)
    prev = bexp_ref[jnp.maximum(b - 1, 0)]
    active = b < nused_ref[0]

    @pl.when(active & ((b == 0) | (bexp_ref[b] != prev)))
    def _():
        wg_bf[...] = wg_ref[...].astype(BF16)
        wu_bf[...] = wu_ref[...].astype(BF16)
        wd_bf[...] = wd_ref[...].astype(BF16)

    @pl.when(active)
    def _():
        rows = xs_ref.shape[0] // SLAB
        halves = [_unpack_rows(u) for u in _load_token_slabs(xs_ref, rows)]
        x = jnp.concatenate([lo.astype(BF16) for lo, _ in halves] + [hi.astype(BF16) for _, hi in halves], axis=1)
        gte = _dot(x, wg_bf[...])
        up = _dot(x, wu_bf[...])
        hb = gte * _sigmoid(gte) * up
        _store_token_slabs(o_ref, _pack_rows(_dot(hb.astype(BF16), wd_bf[...])))

    @pl.when(jnp.logical_not(active))
    def _():
        o_ref[...] = jnp.zeros_like(o_ref)


def _experts(xs, block_exp, n_used, gate_w, up_w, down_w, layer, rows=MOE_ROWS):
    n_rows = xs.shape[0] // SLAB
    d = gate_w.shape[-2]
    hid = gate_w.shape[-1]
    assert d == 2 * SLAB * LANES
    grid_spec = pltpu.PrefetchScalarGridSpec(
        num_scalar_prefetch=2,
        grid=(n_rows // rows,),
        in_specs=[pl.BlockSpec((rows * SLAB, LANES), lambda b, be, nu: (b, 0)),
                  pl.BlockSpec((None, None, d, hid), lambda b, be, nu: (layer, be[b], 0, 0)),
                  pl.BlockSpec((None, None, d, hid), lambda b, be, nu: (layer, be[b], 0, 0)),
                  pl.BlockSpec((None, None, hid, d), lambda b, be, nu: (layer, be[b], 0, 0))],
        out_specs=pl.BlockSpec((rows * SLAB, LANES), lambda b, be, nu: (b, 0)),
        scratch_shapes=[pltpu.VMEM((d, hid), BF16), pltpu.VMEM((d, hid), BF16), pltpu.VMEM((hid, d), BF16)],
    )
    return pl.pallas_call(
        _expert_kernel,
        out_shape=jax.ShapeDtypeStruct((n_rows * SLAB, LANES), U32),
        grid_spec=grid_spec,
        compiler_params=_cparams(1),
        name="moe_experts",
    )(block_exp, n_used, xs, gate_w, up_w, down_w)


def _combine_kernel(*refs, t, mode):
    dest_ref, dest_next_ref, h_ref, rg_ref, ys_ref = refs[:5]
    rest = list(refs[5:])
    g_ref = rest.pop(0) if mode != "plain" else None
    o_ref = rest.pop(0)
    hn_ref = rest.pop(0) if mode == "norm_bf16" else None
    buf00, buf01, buf10, buf11, sems = rest
    bufs = ((buf00, buf01), (buf10, buf11))
    i = pl.program_id(0)
    n = pl.num_programs(0)
    half = SLAB * LANES

    def row_copy(dref, slot, r, k):
        return pltpu.make_async_copy(_slab(ys_ref, dref[0, k * t + r]), _slab(bufs[slot][k], r), sems.at[slot])

    def issue(dref, slot):
        def start(r, c):
            row_copy(dref, slot, r, 0).start()
            row_copy(dref, slot, r, 1).start()
            return c
        lax.fori_loop(0, t, start, 0, unroll=8)

    def drain(slot):
        for k in range(2):
            pltpu.make_async_copy(ys_ref.at[pl.ds(0, t * SLAB)], bufs[slot][k], sems.at[slot]).wait()

    @pl.when(i == 0)
    def _():
        issue(dest_ref, 0)

    for slot in (0, 1):
        @pl.when((i % 2 == slot) & (i + 1 < n))
        def _():
            issue(dest_next_ref, 1 - slot)

        @pl.when(i % 2 == slot)
        def _():
            drain(slot)
            rg = rg_ref[...]
            g0 = rg[:, 0:1]
            g1 = rg[:, 1:2]
            for c, (ua, ub) in enumerate(zip(_load_token_slabs(bufs[slot][0], t), _load_token_slabs(bufs[slot][1], t))):
                a_lo, a_hi = _unpack_rows(ua)
                b_lo, b_hi = _unpack_rows(ub)
                lo_cols = slice(c * LANES, (c + 1) * LANES)
                hi_cols = slice(half + c * LANES, half + (c + 1) * LANES)
                o_ref[:, lo_cols] = h_ref[:, lo_cols] + (g0 * a_lo + g1 * b_lo)
                o_ref[:, hi_cols] = h_ref[:, hi_cols] + (g0 * a_hi + g1 * b_hi)

    if mode != "plain":
        h_new = o_ref[...]
        normed = _rms(h_new, g_ref[...])
        if mode == "norm_bf16":
            hn_ref[...] = normed.astype(BF16)
        else:
            o_ref[...] = normed


def _combine(h, rg, ys, dest, g_next, mode, t=MOE_TOK_TILE):
    n, d = h.shape
    assert d == 2 * SLAB * LANES
    last = n // t - 1
    in_specs = [pl.BlockSpec((None, 1, 2 * t), lambda i: (i, 0, 0), memory_space=pltpu.SMEM),
                pl.BlockSpec((None, 1, 2 * t), lambda i: (jnp.minimum(i + 1, last), 0, 0),
                             memory_space=pltpu.SMEM),
                pl.BlockSpec((t, d), lambda i: (i, 0)),
                pl.BlockSpec((t, LANES), lambda i: (i, 0)),
                pl.BlockSpec(memory_space=pl.ANY)]
    args = [dest, dest, h, rg, ys]
    tile = pl.BlockSpec((t, d), lambda i: (i, 0))
    if mode != "plain":
        in_specs.append(pl.BlockSpec((1, d), lambda i: (0, 0)))
        args.append(g_next.reshape(1, d))
    if mode == "norm_bf16":
        out_shape = (jax.ShapeDtypeStruct((n, d), F32), jax.ShapeDtypeStruct((n, d), BF16))
        out_specs = (tile, tile)
    else:
        out_shape = jax.ShapeDtypeStruct((n, d), F32)
        out_specs = tile
    return pl.pallas_call(
        functools.partial(_combine_kernel, t=t, mode=mode),
        out_shape=out_shape,
        grid=(n // t,),
        in_specs=in_specs,
        out_specs=out_specs,
        scratch_shapes=[pltpu.VMEM((t * SLAB, LANES), U32)] * 4 + [pltpu.SemaphoreType.DMA((2,))],
        compiler_params=_cparams(1),
        name="moe_combine",
    )(*args)


def _moe_layer(h, layer, norm_ffn, group_w, group_b, router_w, router_b, gate_w, up_w, down_w, g_next, mode):
    n, d = h.shape
    pad = LANES - MOE_GROUPS - MOE_EXPERTS
    wr = jnp.concatenate([group_w[layer], router_w[layer], jnp.zeros((d, pad), F32)], axis=1)
    br = jnp.concatenate([group_b[layer], router_b[layer], jnp.zeros((pad,), F32)]).reshape(1, LANES)
    hn, ri, rg, cnt = _route(h, norm_ffn[layer], wr, br)

    t = MOE_TOK_TILE
    counts = cnt[0, :MOE_EXPERTS].astype(I32)
    nblk = (counts + MOE_ROWS - 1) // MOE_ROWS
    blk_end = jnp.cumsum(nblk)
    row_start = (blk_end - nblk) * MOE_ROWS
    n_rows = -(-(n * 2 + MOE_EXPERTS * (MOE_ROWS - 1)) // MOE_ROWS) * MOE_ROWS
    n_blocks = n_rows // MOE_ROWS
    experts = ri[0:2, :]
    ranks = ri[2:4, :]
    onehot = experts[:, :, None] == jnp.arange(MOE_EXPERTS, dtype=I32)
    dest = jnp.sum(jnp.where(onehot, row_start, 0), axis=-1) + ranks
    dest = dest.reshape(2, n // t, t).transpose(1, 0, 2).reshape(n // t, 1, 2 * t).astype(I32)
    block_exp = jnp.minimum(jnp.sum(blk_end[None, :] <= jnp.arange(n_blocks, dtype=I32)[:, None], axis=1),
                            MOE_EXPERTS - 1).astype(I32)
    n_used = blk_end[-1:].astype(I32)

    xs = _dispatch(hn, dest, n_rows)
    ys = _experts(xs, block_exp, n_used, gate_w, up_w, down_w, layer)
    return _combine(h, rg, ys, dest, g_next, mode)


def _ssd_kernel(z_ref, x_ref, b_ref, c_ref, dt_ref, cwx_ref, cwb_ref, cwc_ref, cbx_ref, cbb_ref, cbc_ref,
                dtb_ref, a_ref, dexp_ref, nw_ref, exp_ref, tri_ref, o_ref,
                xbuf, bbuf, cbuf, xc_s, xdt_s, xw_s, b_s, c_s, eacs_s, acs_s, acst_s, y_s, st_s,
                *, L, hp, n_state, groups):
    ci = pl.program_id(0)
    d_inner = x_ref.shape[1]
    gw = d_inner // groups
    kh = gw // hp
    H0 = CONV_HALO

    @pl.when(ci == 0)
    def _():
        xbuf[0:H0, :] = jnp.zeros((H0, xbuf.shape[1]), F32)
        bbuf[0:H0, :] = jnp.zeros((H0, bbuf.shape[1]), F32)
        cbuf[0:H0, :] = jnp.zeros((H0, cbuf.shape[1]), F32)
        st_s[...] = jnp.zeros_like(st_s)

    def conv_silu(in_ref, buf, cw_ref, cb_ref):
        buf[H0:H0 + L, :] = in_ref[...]
        acc = cb_ref[...] + cw_ref[SSD_CONV - 1:SSD_CONV, :] * buf[H0:H0 + L, :]
        for j in range(SSD_CONV - 1):
            off = H0 - (SSD_CONV - 1) + j
            acc = acc + cw_ref[j:j + 1, :] * buf[off:off + L, :]
        buf[0:H0, :] = buf[L:L + H0, :]
        return acc * _sigmoid(acc)

    xc_s[...] = conv_silu(x_ref, xbuf, cwx_ref, cbx_ref)
    b_s[...] = conv_silu(b_ref, bbuf, cwb_ref, cbb_ref).astype(BF16)
    c_s[...] = conv_silu(c_ref, cbuf, cwc_ref, cbc_ref).astype(BF16)

    dt = _softplus(dt_ref[...] + dtb_ref[...])
    d_a = dt * a_ref[...]
    acs = _dot3_lhs_exact(tri_ref[...], d_a)
    acs_s[...] = acs
    acst_s[...] = acs.T
    emat = exp_ref[...]
    acs_e = _dot3_rhs_exact(acs, emat)
    dt_e = _dot3_rhs_exact(dt, emat)
    acs_last = acs_e[L - 1:L, :]
    xdt = xc_s[...] * dt_e
    xdt_s[...] = xdt.astype(BF16)
    xw_s[...] = (xdt * jnp.exp(acs_last - acs_e)).astype(BF16)
    eacs_s[...] = jnp.exp(acs_e)
    cdecay = jnp.exp(acs_last)

    rr = lax.broadcasted_iota(I32, (L, L), 0)
    cc = lax.broadcasted_iota(I32, (L, L), 1)
    causal = rr >= cc
    lane2 = lax.broadcasted_iota(I32, (L, 2 * hp), 1)

    for g in range(groups):
        bg = b_s[:, g * n_state:(g + 1) * n_state]
        cg = c_s[:, g * n_state:(g + 1) * n_state]
        cb = _dot_nt(cg, bg)
        st_prev = st_s[g]
        y_off = _dot(cg, st_prev.astype(BF16)) * eacs_s[:, g * gw:(g + 1) * gw]
        for kp in range(kh // 2):
            c0 = g * gw + kp * 2 * hp
            slab = xdt_s[:, c0:c0 + 2 * hp]
            y_pair = None
            for half in range(2):
                hd = g * kh + kp * 2 + half
                seg = acs_s[:, hd:hd + 1] - acst_s[hd:hd + 1, :]
                decay = jnp.exp(jnp.where(causal, seg, -jnp.inf))
                m = (cb * decay).astype(BF16)
                keep = (lane2 < hp) if half == 0 else (lane2 >= hp)
                part = _dot(m, jnp.where(keep, slab, jnp.zeros_like(slab)))
                y_pair = part if y_pair is None else y_pair + part
            y_s[:, c0:c0 + 2 * hp] = y_pair + y_off[:, kp * 2 * hp:(kp + 1) * 2 * hp]
        bgt = bg.astype(F32).T.astype(BF16)
        s_new = _dot(bgt, xw_s[:, g * gw:(g + 1) * gw])
        st_s[g] = cdecay[:, g * gw:(g + 1) * gw] * st_prev + s_new

    y = y_s[...] + dexp_ref[...] * xc_s[...]
    z = z_ref[...]
    y = y * (z * _sigmoid(z))
    for g in range(groups):
        yg = y[:, g * gw:(g + 1) * gw]
        yg = yg * lax.rsqrt(jnp.mean(yg * yg, axis=-1, keepdims=True) + RMS_EPS)
        o_ref[:, g * gw:(g + 1) * gw] = (yg * nw_ref[:, g * gw:(g + 1) * gw]).astype(o_ref.dtype)


def _ssd_scan(zx, dt_raw, conv_w, conv_b, dt_bias, a_log, d_skip, norm_w, d_inner):
    s = zx.shape[0]
    L = SSD_CHUNK
    hp, n_state, groups = SSD_HEAD_DIM, SSD_STATE, SSD_GROUPS
    heads = d_inner // hp
    gn = groups * n_state
    gw = d_inner // groups
    assert d_inner % gn == 0 and heads <= LANES
    pad = LANES - heads
    dtb = jnp.pad(dt_bias, (0, pad)).reshape(1, LANES)
    a_neg = jnp.pad(-jnp.exp(a_log), (0, pad)).reshape(1, LANES)
    dexp = jnp.repeat(d_skip, hp).reshape(1, d_inner)
    emat = (jnp.arange(LANES)[:, None] == (jnp.arange(d_inner)[None, :] // hp)).astype(BF16)
    tri = jnp.tril(jnp.ones((L, L), BF16))
    cb2 = conv_b.reshape(1, -1)
    xb = d_inner // gn
    row = lambda c: (c, 0)
    const = lambda c: (0, 0)
    in_specs = [
        pl.BlockSpec((L, d_inner), row),
        pl.BlockSpec((L, d_inner), lambda c: (c, 1)),
        pl.BlockSpec((L, gn), lambda c: (c, 2 * xb)),
        pl.BlockSpec((L, gn), lambda c: (c, 2 * xb + 1)),
        pl.BlockSpec((L, LANES), row),
        pl.BlockSpec((SSD_CONV, d_inner), const),
        pl.BlockSpec((SSD_CONV, gn), lambda c: (0, xb)),
        pl.BlockSpec((SSD_CONV, gn), lambda c: (0, xb + 1)),
        pl.BlockSpec((1, d_inner), const),
        pl.BlockSpec((1, gn), lambda c: (0, xb)),
        pl.BlockSpec((1, gn), lambda c: (0, xb + 1)),
        pl.BlockSpec((1, LANES), const),
        pl.BlockSpec((1, LANES), const),
        pl.BlockSpec((1, d_inner), const),
        pl.BlockSpec((1, d_inner), const),
        pl.BlockSpec((LANES, d_inner), const),
        pl.BlockSpec((L, L), const),
    ]
    scratch = [
        pltpu.VMEM((CONV_HALO + L, d_inner), F32), pltpu.VMEM((CONV_HALO + L, gn), F32),
        pltpu.VMEM((CONV_HALO + L, gn), F32),
        pltpu.VMEM((L, d_inner), F32),
        pltpu.VMEM((L, d_inner), BF16),
        pltpu.VMEM((L, d_inner), BF16),
        pltpu.VMEM((L, gn), BF16), pltpu.VMEM((L, gn), BF16),
        pltpu.VMEM((L, d_inner), F32),
        pltpu.VMEM((L, LANES), F32), pltpu.VMEM((LANES, L), F32),
        pltpu.VMEM((L, d_inner), F32),
        pltpu.VMEM((groups, n_state, gw), F32),
    ]
    return pl.pallas_call(
        functools.partial(_ssd_kernel, L=L, hp=hp, n_state=n_state, groups=groups),
        out_shape=jax.ShapeDtypeStruct((s, d_inner), BF16),
        grid=(s // L,),
        in_specs=in_specs,
        out_specs=pl.BlockSpec((L, d_inner), row),
        scratch_shapes=scratch,
        compiler_params=_cparams(1),
        name="ssd_scan",
    )(zx, zx, zx, zx, dt_raw, conv_w, conv_w, conv_w, cb2, cb2, cb2, dtb, a_neg, dexp,
      norm_w.reshape(1, d_inner), emat, tri)


def _ssd_mixer(h, hn, j, in_w, conv_w, conv_b, dt_bias, a_log, d_skip, norm_w, out_w):
    d_inner = out_w.shape[1]
    heads = d_inner // SSD_HEAD_DIM
    wide = in_w.shape[2] - heads
    wt = jnp.swapaxes(in_w, 1, 2)
    zx = _matmul_t(hn, wt, layer=j, row0=0, nrows=wide, tn=512, name="ssd_in_proj")
    dt_raw = _matmul_t(hn, wt, layer=j, row0=wide, nrows=heads, tn=heads, name="ssd_dt_proj")
    y = _ssd_scan(zx, dt_raw, conv_w, conv_b, dt_bias, a_log, d_skip, norm_w, d_inner)
    return _matmul(y, out_w, layer=j, tn=512, residual=h, name="ssd_out_proj")


def _fox_cum_kernel(f_ref, b_ref, tri_ref, o_ref, carry):
    @pl.when(pl.program_id(0) == 0)
    def _():
        carry[...] = jnp.zeros_like(carry)

    x = f_ref[...] + b_ref[...]
    log_f = jnp.minimum(x, 0.0) - jnp.log1p(jnp.exp(-jnp.abs(x)))
    cs = _dot3_rhs_exact(log_f, tri_ref[...]) + carry[...]
    o_ref[...] = cs
    carry[...] = cs[:, cs.shape[1] - 1:]


def _fox_cum(f_t, bias, tk=512):
    nh, s = f_t.shape
    triu = jnp.triu(jnp.ones((tk, tk), BF16))
    return pl.pallas_call(
        _fox_cum_kernel,
        out_shape=jax.ShapeDtypeStruct((nh, s), F32),
        grid=(s // tk,),
        in_specs=[pl.BlockSpec((nh, tk), lambda i: (0, i)),
                  pl.BlockSpec((nh, 1), lambda i: (0, 0)),
                  pl.BlockSpec((tk, tk), lambda i: (0, 0))],
        out_specs=pl.BlockSpec((nh, tk), lambda i: (0, i)),
        scratch_shapes=[pltpu.VMEM((nh, 1), F32)],
        compiler_params=_cparams(1),
        name="fox_cum",
    )(f_t, bias.reshape(nh, 1), triu)


def _fox_attn_kernel(q_ref, k_ref, v_ref, cum_ref, o_ref, m_s, acc_s, vext, *, tq, tk, sr):
    qi = pl.program_id(1)
    dh = q_ref.shape[1]
    n_sub = tq // sr

    @pl.when(qi == 0)
    def _():
        vext[:, 0:dh] = v_ref[...]
        vext[:, dh:2 * dh] = jnp.ones((vext.shape[0], dh), BF16)

    q0 = pl.multiple_of(qi * tq, tq)
    c_ref = cum_ref[:, pl.ds(q0, LANES)][:, 0:1]
    m_s[...] = jnp.full_like(m_s, -jnp.inf)
    acc_s[...] = jnp.zeros_like(acc_s)

    def sub_step(r, k0, bias, col_off):
        rows = slice(r * sr, (r + 1) * sr)
        s = _dot_nt(q_ref[rows, :], k_ref[pl.ds(k0, tk), :]) + bias
        if col_off is not None:
            q_pos = r * sr + lax.broadcasted_iota(I32, (sr, tk), 0)
            k_pos = col_off + lax.broadcasted_iota(I32, (sr, tk), 1)
            s = jnp.where(q_pos >= k_pos, s, -jnp.inf)
        m_prev = m_s[rows, :]
        m_new = jnp.maximum(m_prev, jnp.max(s, axis=-1, keepdims=True))
        alpha = jnp.exp2(m_prev - m_new)
        p = jnp.exp2(s - jnp.concatenate([m_new] * (tk // dh), axis=1)).astype(BF16)
        pv = _dot(p, vext[pl.ds(k0, tk), :])
        acc_s[rows, :] = jnp.concatenate([alpha, alpha], axis=1) * acc_s[rows, :] + pv
        m_s[rows, :] = m_new

    def chunk_bias(k0):
        return (c_ref - cum_ref[:, pl.ds(k0, tk)]) * LOG2E

    def body(j, c):
        k0 = pl.multiple_of(j * tk, tk)
        bias = chunk_bias(k0)
        for r in range(n_sub):
            sub_step(r, k0, bias, None)
        return c

    lax.fori_loop(0, qi * (tq // tk), body, 0)
    for c in range(tq // tk):
        k0 = pl.multiple_of(q0 + c * tk, tk)
        bias = chunk_bias(k0)
        for r in range(n_sub):
            if (r + 1) * sr <= c * tk:
                continue
            straddles = r * sr < (c + 1) * tk - 1
            sub_step(r, k0, bias, c * tk if straddles else None)
    acc = acc_s[...]
    o_ref[...] = (acc[:, 0:dh] / acc[:, dh:2 * dh]).astype(o_ref.dtype)


def _fox_attention(qkv, cum, nh, tq=FOX_TQ, tk=FOX_TK, sr=FOX_SUB):
    s = qkv.shape[0]
    dh = FOX_HEAD_DIM
    assert tq % tk == 0 and tq % sr == 0 and tk % dh == 0 and dh == LANES
    cum3 = cum.reshape(nh, 1, s)
    return pl.pallas_call(
        functools.partial(_fox_attn_kernel, tq=tq, tk=tk, sr=sr),
        out_shape=jax.ShapeDtypeStruct((s, nh * dh), BF16),
        grid=(nh, s // tq),
        in_specs=[pl.BlockSpec((tq, dh), lambda h, i: (i, h)),
                  pl.BlockSpec((s, dh), lambda h, i: (0, nh + h)),
                  pl.BlockSpec((s, dh), lambda h, i: (0, 2 * nh + h)),
                  pl.BlockSpec((None, 1, s), lambda h, i: (h, 0, 0))],
        out_specs=pl.BlockSpec((tq, dh), lambda h, i: (i, h)),
        scratch_shapes=[pltpu.VMEM((tq, dh), F32), pltpu.VMEM((tq, 2 * dh), F32), pltpu.VMEM((s, 2 * dh), BF16)],
        compiler_params=_cparams(2),
        name="fox_attention",
    )(qkv, qkv, qkv, cum3)


def _fox_mixer(h, hn, j, in_w, f_bias, out_w):
    d = hn.shape[1]
    nh = d // FOX_HEAD_DIM
    q_scale = FOX_HEAD_DIM ** -0.5 * LOG2E
    colscale = jnp.concatenate([jnp.full((1, d), q_scale, F32), jnp.ones((1, 2 * d), F32)], axis=1)
    wt = jnp.swapaxes(in_w, 1, 2)
    qkv = _matmul_t(hn, wt, layer=j, row0=0, nrows=3 * d, tn=512, out_dtype=BF16, colscale=colscale,
                    name="fox_in_proj")
    f_raw = _matmul_t(hn, wt, layer=j, row0=3 * d, nrows=nh, tn=nh, name="fox_f_proj")
    cum = _fox_cum(f_raw[:, :nh].T, f_bias)
    o = _fox_attention(qkv, cum, nh)
    return _matmul(o, out_w, layer=j, tn=512, residual=h, name="fox_out_proj")


def _norm_kernel(h_ref, g_ref, o_ref):
    o_ref[...] = _rms(h_ref[...], g_ref[...]).astype(o_ref.dtype)


def _norm(h, g, out_dtype, tm=512):
    n, d = h.shape
    return pl.pallas_call(
        _norm_kernel,
        out_shape=jax.ShapeDtypeStruct((n, d), out_dtype),
        grid=(n // tm,),
        in_specs=[pl.BlockSpec((tm, d), lambda i: (i, 0)), pl.BlockSpec((1, d), lambda i: (0, 0))],
        out_specs=pl.BlockSpec((tm, d), lambda i: (i, 0)),
        compiler_params=_cparams(1),
        name="rmsnorm",
    )(h, g.reshape(1, d))


def kernel(x, norm_mix, norm_ffn, final_norm, pool_w, pool_scale, ssd_in_w, ssd_conv_w, ssd_conv_b,
           ssd_dt_bias, ssd_a_log, ssd_d_skip, ssd_norm, ssd_out_w, fox_in_w, fox_f_bias, fox_out_w,
           moe_group_w, moe_group_b, moe_router_w, moe_router_b, moe_gate_w, moe_up_w, moe_down_w):
    bsz, s, d = x.shape
    assert bsz == 1
    depth = norm_mix.shape[0]
    h = x.reshape(s, d)
    hn = None
    for i in range(depth):
        kind, j = i % 3, i // 3
        if kind == 0:
            h = _pool_layer(h, norm_mix[i], pool_w, pool_scale, j)
        else:
            if hn is None:
                hn = _norm(h, norm_mix[i], BF16)
            if kind == 1:
                h = _ssd_mixer(h, hn, j, ssd_in_w, ssd_conv_w[j], ssd_conv_b[j], ssd_dt_bias[j], ssd_a_log[j],
                               ssd_d_skip[j], ssd_norm[j], ssd_out_w)
            else:
                h = _fox_mixer(h, hn, j, fox_in_w, fox_f_bias[j], fox_out_w)
        last = i == depth - 1
        next_kind = (i + 1) % 3
        if last:
            g_next, mode = final_norm, "final"
        elif next_kind == 0:
            g_next, mode = None, "plain"
        else:
            g_next, mode = norm_mix[i + 1], "norm_bf16"
        out = _moe_layer(h, i, norm_ffn, moe_group_w, moe_group_b, moe_router_w, moe_router_b,
                         moe_gate_w, moe_up_w, moe_down_w, g_next, mode)
        if mode == "norm_bf16":
            h, hn = out
        else:
            h, hn = out, None
    return h.reshape(bsz, s, d)
```

```python
import functools

import jax
import jax.numpy as jnp
from jax import lax
from jax.experimental import pallas as pl
from jax.experimental.pallas import tpu as pltpu

F32 = jnp.float32
BF16 = jnp.bfloat16
I32 = jnp.int32
U32 = jnp.uint32

RMS_EPS = 1e-6
LOG2E = 1.4426950408889634
LANES = 128
SLAB = 8
VMEM_LIMIT = 56 * 1024 * 1024

MM_TM = 1024
MM_TN = 1024

POOL_WINDOWS = (2, 4, 8, 16)
POOL_HALO = 16

SSD_HEAD_DIM = 64
SSD_STATE = 128
SSD_GROUPS = 8
SSD_CONV = 4
SSD_CHUNK = 128
CONV_HALO = 8

FOX_HEAD_DIM = 128
FOX_TQ = 1024
FOX_TK = 512
FOX_SUB = 256

MOE_GROUPS = 4
MOE_EPG = 8
MOE_EXPERTS = MOE_GROUPS * MOE_EPG
MOE_ROWS = 256
MOE_TOK_TILE = 256
ROUTE_TILE = 512


def _cparams(n_axes, vmem=VMEM_LIMIT):
    return pltpu.CompilerParams(dimension_semantics=("arbitrary",) * n_axes, vmem_limit_bytes=vmem)


def _rms(x, g):
    ms = jnp.mean(x * x, axis=-1, keepdims=True)
    return x * lax.rsqrt(ms + RMS_EPS) * g


def _split3(x):
    hi = x.astype(BF16)
    r = x - hi.astype(F32)
    mid = r.astype(BF16)
    lo = (r - mid.astype(F32)).astype(BF16)
    return hi, mid, lo


def _dot(a, b):
    return jnp.dot(a, b, preferred_element_type=F32)


def _dot_nt(a, b):
    return lax.dot_general(a, b, (((1,), (1,)), ((), ())), preferred_element_type=F32)


def _dot3_rhs_exact(x, m):
    hi, mid, lo = _split3(x)
    return _dot(hi, m) + _dot(mid, m) + _dot(lo, m)


def _dot3_lhs_exact(m, x):
    hi, mid, lo = _split3(x)
    return _dot(m, hi) + _dot(m, mid) + _dot(m, lo)


def _sigmoid(x):
    return 1.0 / (1.0 + jnp.exp(-x))


def _softplus(x):
    return jnp.maximum(x, 0.0) + jnp.log1p(jnp.exp(-jnp.abs(x)))


def _pack_rows(x):
    half = x.shape[1] // 2
    lo = lax.bitcast_convert_type(x[:, :half].astype(BF16).astype(F32), U32)
    hi = lax.bitcast_convert_type(x[:, half:].astype(BF16).astype(F32), U32)
    return hi | (lo >> 16)


def _unpack_rows(u):
    lo = lax.bitcast_convert_type(u << 16, F32)
    hi = lax.bitcast_convert_type(u & jnp.uint32(0xFFFF0000), F32)
    return lo, hi


def _load_token_slabs(ref, rows):
    return [ref[pl.ds(c, rows, stride=SLAB), :] for c in range(SLAB)]


def _store_token_slabs(ref, packed):
    rows = packed.shape[0]
    for c in range(SLAB):
        ref[pl.ds(c, rows, stride=SLAB), :] = packed[:, c * LANES:(c + 1) * LANES]


def _slab(ref, token):
    return ref.at[pl.ds(pl.multiple_of(token * SLAB, SLAB), SLAB)]


def _slab_run(ref, token, count):
    return ref.at[pl.ds(pl.multiple_of(token * SLAB, SLAB), count * SLAB)]


def _mm_kernel(*refs, has_res, has_scale, transposed):
    x_ref, w_ref = refs[:2]
    rest = list(refs[2:])
    r_ref = rest.pop(0) if has_res else None
    s_ref = rest.pop(0) if has_scale else None
    o_ref, wbf = rest

    @pl.when(pl.program_id(1) == 0)
    def _():
        if transposed and wbf.shape[0] != w_ref.shape[0]:
            wbf[...] = jnp.zeros_like(wbf)
            wbf[0:w_ref.shape[0], :] = w_ref[...].astype(BF16)
        else:
            wbf[...] = w_ref[...].astype(BF16)

    acc = _dot_nt(x_ref[...], wbf[...]) if transposed else _dot(x_ref[...], wbf[...])
    if s_ref is not None:
        acc = acc * s_ref[...]
    if r_ref is not None:
        acc = acc + r_ref[...]
    o_ref[...] = acc.astype(o_ref.dtype)


def _matmul(x, w, *, layer=0, col0=0, ncols=None, tn=512, tm=512, residual=None, colscale=None,
            out_dtype=F32, name="mm"):
    m, k = x.shape
    ncols = w.shape[-1] - col0 if ncols is None else ncols
    assert ncols % tn == 0 and col0 % tn == 0 and m % tm == 0
    jb = col0 // tn
    if w.ndim == 3:
        w_spec = pl.BlockSpec((None, k, tn), lambda j, i: (layer, 0, jb + j))
    else:
        w_spec = pl.BlockSpec((k, tn), lambda j, i: (0, jb + j))
    in_specs = [pl.BlockSpec((tm, k), lambda j, i: (i, 0)), w_spec]
    args = [x, w]
    if residual is not None:
        in_specs.append(pl.BlockSpec((tm, tn), lambda j, i: (i, j)))
        args.append(residual)
    if colscale is not None:
        in_specs.append(pl.BlockSpec((1, tn), lambda j, i: (0, j)))
        args.append(colscale)
    return pl.pallas_call(
        functools.partial(_mm_kernel, has_res=residual is not None, has_scale=colscale is not None,
                          transposed=False),
        out_shape=jax.ShapeDtypeStruct((m, ncols), out_dtype),
        grid=(ncols // tn, m // tm),
        in_specs=in_specs,
        out_specs=pl.BlockSpec((tm, tn), lambda j, i: (i, j)),
        scratch_shapes=[pltpu.VMEM((k, tn), BF16)],
        compiler_params=_cparams(2),
        name=name,
    )(*args)


def _matmul_t(x, wt, *, layer, row0=0, nrows=None, tn=512, tm=512, colscale=None, out_dtype=F32, name="mm_t"):
    m, k = x.shape
    nrows = wt.shape[1] - row0 if nrows is None else nrows
    assert nrows % tn == 0 and row0 % tn == 0 and m % tm == 0 and tn % 8 == 0
    assert tn % LANES == 0 or nrows == tn
    tn_out = max(tn, LANES)
    jb = row0 // tn
    in_specs = [pl.BlockSpec((tm, k), lambda j, i: (i, 0)),
                pl.BlockSpec((None, tn, k), lambda j, i: (layer, jb + j, 0))]
    args = [x, wt]
    if colscale is not None:
        in_specs.append(pl.BlockSpec((1, tn_out), lambda j, i: (0, j)))
        args.append(colscale)
    return pl.pallas_call(
        functools.partial(_mm_kernel, has_res=False, has_scale=colscale is not None, transposed=True),
        out_shape=jax.ShapeDtypeStruct((m, (nrows // tn) * tn_out), out_dtype),
        grid=(nrows // tn, m // tm),
        in_specs=in_specs,
        out_specs=pl.BlockSpec((tm, tn_out), lambda j, i: (i, j)),
        scratch_shapes=[pltpu.VMEM((tn_out, k), BF16)],
        compiler_params=_cparams(2),
        name=name,
    )(*args)


def _pool_kernel(h_ref, g_ref, w_ref, scale_ref, o_ref, xbuf, wbf, *, tm, pg):
    i = pl.program_id(0)
    d = h_ref.shape[1]

    @pl.when(i == 0)
    def _():
        xbuf[0:POOL_HALO, :] = jnp.zeros((POOL_HALO, d), F32)
        wbf[...] = w_ref[...].astype(BF16)

    h = h_ref[...]
    hn = _rms(h, g_ref[...])
    xbuf[POOL_HALO:POOL_HALO + tm, :] = hn
    row = i * tm + lax.broadcasted_iota(I32, (tm, 1), 0)
    for gi, w in enumerate(POOL_WINDOWS):
        c0 = gi * pg
        cur = hn[:, c0:c0 + pg]
        acc = cur
        for j in range(1, w):
            acc = acc + xbuf[POOL_HALO - j:POOL_HALO - j + tm, c0:c0 + pg]
        cnt = jnp.minimum(row + 1, w).astype(F32)
        pooled = acc / cnt - cur
        mix = _dot(pooled.astype(BF16), wbf[gi]) * scale_ref[:, c0:c0 + pg]
        o_ref[:, c0:c0 + pg] = h[:, c0:c0 + pg] + mix
    xbuf[0:POOL_HALO, :] = xbuf[tm:tm + POOL_HALO, :]


def _pool_layer(h, g, pool_w, pool_scale, j, tm=256):
    n, d = h.shape
    nw, pg, _ = pool_w.shape[1:]
    return pl.pallas_call(
        functools.partial(_pool_kernel, tm=tm, pg=pg),
        out_shape=jax.ShapeDtypeStruct((n, d), F32),
        grid=(n // tm,),
        in_specs=[pl.BlockSpec((tm, d), lambda i: (i, 0)),
                  pl.BlockSpec((1, d), lambda i: (0, 0)),
                  pl.BlockSpec((None, nw, pg, pg), lambda i: (j, 0, 0, 0)),
                  pl.BlockSpec((1, d), lambda i: (0, 0))],
        out_specs=pl.BlockSpec((tm, d), lambda i: (i, 0)),
        scratch_shapes=[pltpu.VMEM((POOL_HALO + tm, d), F32), pltpu.VMEM((nw, pg, pg), BF16)],
        compiler_params=_cparams(1),
        name="pool_layer",
    )(h, g.reshape(1, d), pool_w, pool_scale[j].reshape(1, d))


def _route_kernel(h_ref, g_ref, whi_ref, wlo_ref, b_ref, tri_ref, hn_ref, ri_ref, rg_ref, cnt_ref,
                  run_ref, *, tm):
    i = pl.program_id(0)

    @pl.when(i == 0)
    def _():
        run_ref[...] = jnp.zeros_like(run_ref)

    hn = _rms(h_ref[...], g_ref[...])
    hn_bf = hn.astype(BF16)
    _store_token_slabs(hn_ref, _pack_rows(hn))
    hn_lo = (hn - hn_bf.astype(F32)).astype(BF16)
    whi = whi_ref[...]
    logits = _dot(hn_bf, whi) + _dot(hn_bf, wlo_ref[...]) + _dot(hn_lo, whi) + b_ref[...]

    lane = lax.broadcasted_iota(I32, (tm, LANES), 1)
    lane_f = lane.astype(F32)
    neg = -jnp.inf
    big = float(LANES)
    gl = jnp.where(lane < MOE_GROUPS, logits, neg)
    gmax = jnp.max(gl, axis=-1, keepdims=True)
    gsel = jnp.min(jnp.where(gl == gmax, lane_f, big), axis=-1, keepdims=True).astype(I32)
    p_grp = 1.0 / jnp.sum(jnp.exp(gl - gmax), axis=-1, keepdims=True)
    lo_lane = MOE_GROUPS + MOE_EPG * gsel
    el = jnp.where((lane >= lo_lane) & (lane < lo_lane + MOE_EPG), logits, neg)
    v0 = jnp.max(el, axis=-1, keepdims=True)
    i0 = jnp.min(jnp.where(el == v0, lane_f, big), axis=-1, keepdims=True).astype(I32)
    el2 = jnp.where(lane == i0, neg, el)
    v1 = jnp.max(el2, axis=-1, keepdims=True)
    i1 = jnp.min(jnp.where(el2 == v1, lane_f, big), axis=-1, keepdims=True).astype(I32)
    t = jnp.exp(v1 - v0)
    gate0 = p_grp / (1.0 + t)
    gate1 = p_grp * t / (1.0 + t)
    e0 = i0 - MOE_GROUPS
    e1 = i1 - MOE_GROUPS

    onehot = ((lane == e0) | (lane == e1))
    c_bf = jnp.where(onehot, 1.0, 0.0).astype(BF16)
    prefix = _dot(tri_ref[...], c_bf) + run_ref[...]
    rank0 = jnp.sum(jnp.where(lane == e0, prefix, 0.0), axis=-1, keepdims=True).astype(I32)
    rank1 = jnp.sum(jnp.where(lane == e1, prefix, 0.0), axis=-1, keepdims=True).astype(I32)
    run = run_ref[...] + jnp.sum(jnp.where(onehot, 1.0, 0.0), axis=0, keepdims=True)
    run_ref[...] = run
    cnt_ref[...] = jnp.broadcast_to(run, cnt_ref.shape)

    ri = jnp.where(lane == 0, e0, jnp.where(lane == 1, e1, jnp.where(lane == 2, rank0, rank1)))
    ri_ref[...] = ri.T[0:8, :]
    rg_ref[...] = jnp.where(lane == 0, gate0, gate1)


def _route(h, g, wr, br, tm=ROUTE_TILE):
    n, d = h.shape
    whi = wr.astype(BF16)
    wlo = (wr - whi.astype(F32)).astype(BF16)
    tri = jnp.tril(jnp.ones((tm, tm), BF16), -1)
    return pl.pallas_call(
        functools.partial(_route_kernel, tm=tm),
        out_shape=(jax.ShapeDtypeStruct((n * SLAB, LANES), U32),
                   jax.ShapeDtypeStruct((8, n), I32),
                   jax.ShapeDtypeStruct((n, LANES), F32),
                   jax.ShapeDtypeStruct((8, LANES), F32)),
        grid=(n // tm,),
        in_specs=[pl.BlockSpec((tm, d), lambda i: (i, 0)),
                  pl.BlockSpec((1, d), lambda i: (0, 0)),
                  pl.BlockSpec((d, LANES), lambda i: (0, 0)),
                  pl.BlockSpec((d, LANES), lambda i: (0, 0)),
                  pl.BlockSpec((1, LANES), lambda i: (0, 0)),
                  pl.BlockSpec((tm, tm), lambda i: (0, 0))],
        out_specs=(pl.BlockSpec((tm * SLAB, LANES), lambda i: (i, 0)),
                   pl.BlockSpec((8, tm), lambda i: (0, i)),
                   pl.BlockSpec((tm, LANES), lambda i: (i, 0)),
                   pl.BlockSpec((8, LANES), lambda i: (0, 0))),
        scratch_shapes=[pltpu.VMEM((1, LANES), F32)],
        compiler_params=_cparams(1),
        name="moe_route",
    )(h, g.reshape(1, d), whi, wlo, br, tri)


def _dispatch_kernel(pad_start_ref, pad_len_ref, nused_ref, dest_ref, hn_ref, xs_ref, zbuf, sem, zsem, *, t):
    sizes = [1 << s for s in reversed(range(MOE_ROWS.bit_length() - 1))]

    @pl.when(pl.program_id(0) == 0)
    def _():
        zbuf[...] = jnp.zeros_like(zbuf)

        def zero_copy(off, size):
            return pltpu.make_async_copy(zbuf.at[pl.ds(0, size * SLAB)], _slab_run(xs_ref, off, size), zsem)

        def gaps(e, wait):
            off = pad_start_ref[e]
            n = pad_len_ref[e]
            for size in sizes:
                hit = (n & size) != 0

                @pl.when(hit)
                def _():
                    c = zero_copy(off, size)
                    c.wait() if wait else c.start()

                off = off + (n & size)

        def start_gaps(e, c):
            gaps(e, False)
            return c

        def wait_gaps(e, c):
            gaps(e, True)
            return c

        lax.fori_loop(0, MOE_EXPERTS, start_gaps, 0)
        lax.fori_loop(0, MOE_EXPERTS, wait_gaps, 0)

        zrows = zbuf.shape[0] // SLAB
        n_blocks = xs_ref.shape[0] // (MOE_ROWS * SLAB)

        def tail(wait):
            def body(b, c):
                for part in range(MOE_ROWS // zrows):
                    cp = zero_copy(b * MOE_ROWS + part * zrows, zrows)
                    cp.wait() if wait else cp.start()
                return c
            lax.fori_loop(nused_ref[0], n_blocks, body, 0)

        tail(False)
        tail(True)

    def row_copy(r, k):
        return pltpu.make_async_copy(_slab(hn_ref, r), _slab(xs_ref, dest_ref[0, k * t + r]), sem)

    def start(r, c):
        row_copy(r, 0).start()
        row_copy(r, 1).start()
        return c

    lax.fori_loop(0, t, start, 0, unroll=8)
    for _ in range(2):
        pltpu.make_async_copy(hn_ref, xs_ref.at[pl.ds(0, t * SLAB)], sem).wait()


def _dispatch(hn, dest, pad_start, pad_len, n_used, n_rows, t=MOE_TOK_TILE):
    n = hn.shape[0] // SLAB
    grid_spec = pltpu.PrefetchScalarGridSpec(
        num_scalar_prefetch=3,
        grid=(n // t,),
        in_specs=[pl.BlockSpec((None, 1, 2 * t), lambda i, *_: (i, 0, 0), memory_space=pltpu.SMEM),
                  pl.BlockSpec((t * SLAB, LANES), lambda i, *_: (i, 0))],
        out_specs=pl.BlockSpec(memory_space=pl.ANY),
        scratch_shapes=[pltpu.VMEM((MOE_ROWS // 2 * SLAB, LANES), hn.dtype),
                        pltpu.SemaphoreType.DMA(()), pltpu.SemaphoreType.DMA(())],
    )
    return pl.pallas_call(
        functools.partial(_dispatch_kernel, t=t),
        out_shape=jax.ShapeDtypeStruct((n_rows * SLAB, LANES), hn.dtype),
        grid_spec=grid_spec,
        compiler_params=_cparams(1),
        name="moe_dispatch",
    )(pad_start, pad_len, n_used, dest, hn)


def _expert_kernel(bexp_ref, first_ref, slot_ref, next_ref, nused_ref, xs_ref, wg_hbm, wu_hbm, wd_hbm, o_ref,
                   wg_raw, wu_raw, wd_raw, wg_bf, wu_bf, wd_bf, sems, *, layer):
    b = pl.program_id(0)
    active = b < nused_ref[0]

    def fetch(e, slot):
        return [pltpu.make_async_copy(hbm.at[layer, e], raw.at[slot], sems.at[slot, k])
                for k, (hbm, raw) in enumerate(((wg_hbm, wg_raw), (wu_hbm, wu_raw), (wd_hbm, wd_raw)))]

    @pl.when(b == 0)
    def _():
        for c in fetch(bexp_ref[0], 0):
            c.start()

    @pl.when(active & (first_ref[b] == 1))
    def _():
        slot = slot_ref[b]
        nxt = next_ref[b]

        @pl.when(nxt >= 0)
        def _():
            for c in fetch(nxt, 1 - slot):
                c.start()

        for c in fetch(bexp_ref[b], slot):
            c.wait()
        wg_bf[...] = wg_raw[slot].astype(BF16)
        wu_bf[...] = wu_raw[slot].astype(BF16)
        wd_bf[...] = wd_raw[slot].astype(BF16)

    @pl.when(active)
    def _():
        rows = xs_ref.shape[0] // SLAB
        halves = [_unpack_rows(u) for u in _load_token_slabs(xs_ref, rows)]
        x = jnp.concatenate([lo.astype(BF16) for lo, _ in halves] + [hi.astype(BF16) for _, hi in halves], axis=1)
        gte = _dot(x, wg_bf[...])
        up = _dot(x, wu_bf[...])
        hb = gte * _sigmoid(gte) * up
        _store_token_slabs(o_ref, _pack_rows(_dot(hb.astype(BF16), wd_bf[...])))

    @pl.when(jnp.logical_not(active))
    def _():
        o_ref[...] = jnp.zeros_like(o_ref)


def _experts(xs, tables, gate_w, up_w, down_w, layer, rows=MOE_ROWS):
    n_rows = xs.shape[0] // SLAB
    d = gate_w.shape[-2]
    hid = gate_w.shape[-1]
    assert d == 2 * SLAB * LANES
    x_map = lambda b, be, fi, sl, nx, nu: (jnp.minimum(b, nu[0] - 1), 0)
    grid_spec = pltpu.PrefetchScalarGridSpec(
        num_scalar_prefetch=5,
        grid=(n_rows // rows,),
        in_specs=[pl.BlockSpec((rows * SLAB, LANES), x_map),
                  pl.BlockSpec(memory_space=pl.ANY),
                  pl.BlockSpec(memory_space=pl.ANY),
                  pl.BlockSpec(memory_space=pl.ANY)],
        out_specs=pl.BlockSpec((rows * SLAB, LANES), lambda b, *_: (b, 0)),
        scratch_shapes=[pltpu.VMEM((2, d, hid), F32), pltpu.VMEM((2, d, hid), F32), pltpu.VMEM((2, hid, d), F32),
                        pltpu.VMEM((d, hid), BF16), pltpu.VMEM((d, hid), BF16), pltpu.VMEM((hid, d), BF16),
                        pltpu.SemaphoreType.DMA((2, 3))],
    )
    return pl.pallas_call(
        functools.partial(_expert_kernel, layer=layer),
        out_shape=jax.ShapeDtypeStruct((n_rows * SLAB, LANES), U32),
        grid_spec=grid_spec,
        compiler_params=_cparams(1),
        name="moe_experts",
    )(*tables, xs, gate_w, up_w, down_w)


def _combine_kernel(*refs, t, mode):
    dest_ref, dest_next_ref, h_ref, rg_ref, ys_ref = refs[:5]
    rest = list(refs[5:])
    g_ref = rest.pop(0) if mode != "plain" else None
    o_ref = rest.pop(0)
    hn_ref = rest.pop(0) if mode == "norm_bf16" else None
    buf00, buf01, buf10, buf11, sems = rest
    bufs = ((buf00, buf01), (buf10, buf11))
    i = pl.program_id(0)
    n = pl.num_programs(0)
    half = SLAB * LANES

    def row_copy(dref, slot, r, k):
        return pltpu.make_async_copy(_slab(ys_ref, dref[0, k * t + r]), _slab(bufs[slot][k], r), sems.at[slot])

    def issue(dref, slot):
        def start(r, c):
            row_copy(dref, slot, r, 0).start()
            row_copy(dref, slot, r, 1).start()
            return c
        lax.fori_loop(0, t, start, 0, unroll=8)

    def drain(slot):
        for k in range(2):
            pltpu.make_async_copy(ys_ref.at[pl.ds(0, t * SLAB)], bufs[slot][k], sems.at[slot]).wait()

    @pl.when(i == 0)
    def _():
        issue(dest_ref, 0)

    for slot in (0, 1):
        @pl.when((i % 2 == slot) & (i + 1 < n))
        def _():
            issue(dest_next_ref, 1 - slot)

        @pl.when(i % 2 == slot)
        def _():
            drain(slot)
            rg = rg_ref[...]
            g0 = rg[:, 0:1]
            g1 = rg[:, 1:2]
            for c, (ua, ub) in enumerate(zip(_load_token_slabs(bufs[slot][0], t), _load_token_slabs(bufs[slot][1], t))):
                a_lo, a_hi = _unpack_rows(ua)
                b_lo, b_hi = _unpack_rows(ub)
                lo_cols = slice(c * LANES, (c + 1) * LANES)
                hi_cols = slice(half + c * LANES, half + (c + 1) * LANES)
                o_ref[:, lo_cols] = h_ref[:, lo_cols] + (g0 * a_lo + g1 * b_lo)
                o_ref[:, hi_cols] = h_ref[:, hi_cols] + (g0 * a_hi + g1 * b_hi)

    if mode != "plain":
        h_new = o_ref[...]
        normed = _rms(h_new, g_ref[...])
        if mode == "norm_bf16":
            hn_ref[...] = normed.astype(BF16)
        else:
            o_ref[...] = normed


def _combine(h, rg, ys, dest, g_next, mode, t=MOE_TOK_TILE):
    n, d = h.shape
    assert d == 2 * SLAB * LANES
    last = n // t - 1
    in_specs = [pl.BlockSpec((None, 1, 2 * t), lambda i: (i, 0, 0), memory_space=pltpu.SMEM),
                pl.BlockSpec((None, 1, 2 * t), lambda i: (jnp.minimum(i + 1, last), 0, 0),
                             memory_space=pltpu.SMEM),
                pl.BlockSpec((t, d), lambda i: (i, 0)),
                pl.BlockSpec((t, LANES), lambda i: (i, 0)),
                pl.BlockSpec(memory_space=pl.ANY)]
    args = [dest, dest, h, rg, ys]
    tile = pl.BlockSpec((t, d), lambda i: (i, 0))
    if mode != "plain":
        in_specs.append(pl.BlockSpec((1, d), lambda i: (0, 0)))
        args.append(g_next.reshape(1, d))
    if mode == "norm_bf16":
        out_shape = (jax.ShapeDtypeStruct((n, d), F32), jax.ShapeDtypeStruct((n, d), BF16))
        out_specs = (tile, tile)
    else:
        out_shape = jax.ShapeDtypeStruct((n, d), F32)
        out_specs = tile
    return pl.pallas_call(
        functools.partial(_combine_kernel, t=t, mode=mode),
        out_shape=out_shape,
        grid=(n // t,),
        in_specs=in_specs,
        out_specs=out_specs,
        scratch_shapes=[pltpu.VMEM((t * SLAB, LANES), U32)] * 4 + [pltpu.SemaphoreType.DMA((2,))],
        compiler_params=_cparams(1),
        name="moe_combine",
    )(*args)


def _moe_layer(h, layer, norm_ffn, group_w, group_b, router_w, router_b, gate_w, up_w, down_w, g_next, mode):
    n, d = h.shape
    pad = LANES - MOE_GROUPS - MOE_EXPERTS
    wr = jnp.concatenate([group_w[layer], router_w[layer], jnp.zeros((d, pad), F32)], axis=1)
    br = jnp.concatenate([group_b[layer], router_b[layer], jnp.zeros((pad,), F32)]).reshape(1, LANES)
    hn, ri, rg, cnt = _route(h, norm_ffn[layer], wr, br)

    t = MOE_TOK_TILE
    counts = cnt[0, :MOE_EXPERTS].astype(I32)
    nblk = (counts + MOE_ROWS - 1) // MOE_ROWS
    blk_end = jnp.cumsum(nblk)
    row_start = (blk_end - nblk) * MOE_ROWS
    n_rows = -(-(n * 2 + MOE_EXPERTS * (MOE_ROWS - 1)) // MOE_ROWS) * MOE_ROWS
    n_blocks = n_rows // MOE_ROWS
    experts = ri[0:2, :]
    ranks = ri[2:4, :]
    onehot = experts[:, :, None] == jnp.arange(MOE_EXPERTS, dtype=I32)
    dest = jnp.sum(jnp.where(onehot, row_start, 0), axis=-1) + ranks
    dest = dest.reshape(2, n // t, t).transpose(1, 0, 2).reshape(n // t, 1, 2 * t).astype(I32)
    blocks = jnp.arange(n_blocks, dtype=I32)
    block_exp = jnp.minimum(jnp.sum(blk_end[None, :] <= blocks[:, None], axis=1), MOE_EXPERTS - 1).astype(I32)
    n_used = blk_end[-1:].astype(I32)
    eids = jnp.arange(MOE_EXPERTS, dtype=I32)
    nonempty = nblk > 0
    ordinal = jnp.cumsum(nonempty.astype(I32)) - 1
    at_or_after = lax.cummin(jnp.where(nonempty, eids, MOE_EXPERTS)[::-1])[::-1]
    after = jnp.concatenate([at_or_after[1:], jnp.full((1,), MOE_EXPERTS, I32)])
    next_nonempty = jnp.where(after < MOE_EXPERTS, after, -1)
    is_first = ((blocks == (blk_end - nblk)[block_exp]) & (blocks < n_used[0])).astype(I32)
    tables = (block_exp, is_first, (ordinal[block_exp] % 2).astype(I32), next_nonempty[block_exp].astype(I32), n_used)
    pad_start = (row_start + counts).astype(I32)
    pad_len = (nblk * MOE_ROWS - counts).astype(I32)

    xs = _dispatch(hn, dest, pad_start, pad_len, n_used, n_rows)
    ys = _experts(xs, tables, gate_w, up_w, down_w, layer)
    return _combine(h, rg, ys, dest, g_next, mode)


def _ssd_kernel(z_ref, x_ref, b_ref, c_ref, dt_ref, cwx_ref, cwb_ref, cwc_ref, cbx_ref, cbb_ref, cbc_ref,
                dtb_ref, a_ref, dexp_ref, nw_ref, exp_ref, tri_ref, o_ref,
                xbuf, bbuf, cbuf, xc_s, xdt_s, xw_s, b_s, c_s, eacs_s, acs_s, acst_s, y_s, st_s,
                *, L, hp, n_state, groups):
    ci = pl.program_id(0)
    d_inner = x_ref.shape[1]
    gw = d_inner // groups
    kh = gw // hp
    H0 = CONV_HALO

    @pl.when(ci == 0)
    def _():
        xbuf[0:H0, :] = jnp.zeros((H0, xbuf.shape[1]), F32)
        bbuf[0:H0, :] = jnp.zeros((H0, bbuf.shape[1]), F32)
        cbuf[0:H0, :] = jnp.zeros((H0, cbuf.shape[1]), F32)
        st_s[...] = jnp.zeros_like(st_s)

    def conv_silu(in_ref, buf, cw_ref, cb_ref):
        buf[H0:H0 + L, :] = in_ref[...].astype(F32)
        acc = cb_ref[...] + cw_ref[SSD_CONV - 1:SSD_CONV, :] * buf[H0:H0 + L, :]
        for j in range(SSD_CONV - 1):
            off = H0 - (SSD_CONV - 1) + j
            acc = acc + cw_ref[j:j + 1, :] * buf[off:off + L, :]
        buf[0:H0, :] = buf[L:L + H0, :]
        return acc * _sigmoid(acc)

    xc_s[...] = conv_silu(x_ref, xbuf, cwx_ref, cbx_ref)
    b_s[...] = conv_silu(b_ref, bbuf, cwb_ref, cbb_ref).astype(BF16)
    c_s[...] = conv_silu(c_ref, cbuf, cwc_ref, cbc_ref).astype(BF16)

    dt = _softplus(dt_ref[...] + dtb_ref[...])
    d_a = dt * a_ref[...]
    acs = _dot3_lhs_exact(tri_ref[...], d_a)
    acs_s[...] = acs
    acst_s[...] = acs.T
    emat = exp_ref[...]
    acs_e = _dot3_rhs_exact(acs, emat)
    dt_e = _dot3_rhs_exact(dt, emat)
    acs_last = acs_e[L - 1:L, :]
    xdt = xc_s[...] * dt_e
    xdt_s[...] = xdt.astype(BF16)
    xw_s[...] = (xdt * jnp.exp(acs_last - acs_e)).astype(BF16)
    eacs_s[...] = jnp.exp(acs_e)
    cdecay = jnp.exp(acs_last)

    rr = lax.broadcasted_iota(I32, (L, L), 0)
    cc = lax.broadcasted_iota(I32, (L, L), 1)
    causal = rr >= cc
    lane2 = lax.broadcasted_iota(I32, (L, 2 * hp), 1)

    for g in range(groups):
        bg = b_s[:, g * n_state:(g + 1) * n_state]
        cg = c_s[:, g * n_state:(g + 1) * n_state]
        cb = _dot_nt(cg, bg)
        st_prev = st_s[g]
        y_off = _dot(cg, st_prev.astype(BF16)) * eacs_s[:, g * gw:(g + 1) * gw]
        for kp in range(kh // 2):
            c0 = g * gw + kp * 2 * hp
            slab = xdt_s[:, c0:c0 + 2 * hp]
            y_pair = None
            for half in range(2):
                hd = g * kh + kp * 2 + half
                seg = acs_s[:, hd:hd + 1] - acst_s[hd:hd + 1, :]
                decay = jnp.exp(jnp.where(causal, seg, -jnp.inf))
                m = (cb * decay).astype(BF16)
                keep = (lane2 < hp) if half == 0 else (lane2 >= hp)
                part = _dot(m, jnp.where(keep, slab, jnp.zeros_like(slab)))
                y_pair = part if y_pair is None else y_pair + part
            y_s[:, c0:c0 + 2 * hp] = y_pair + y_off[:, kp * 2 * hp:(kp + 1) * 2 * hp]
        bgt = bg.astype(F32).T.astype(BF16)
        s_new = _dot(bgt, xw_s[:, g * gw:(g + 1) * gw])
        st_s[g] = cdecay[:, g * gw:(g + 1) * gw] * st_prev + s_new

    y = y_s[...] + dexp_ref[...] * xc_s[...]
    z = z_ref[...].astype(F32)
    y = y * (z * _sigmoid(z))
    for g in range(groups):
        yg = y[:, g * gw:(g + 1) * gw]
        yg = yg * lax.rsqrt(jnp.mean(yg * yg, axis=-1, keepdims=True) + RMS_EPS)
        o_ref[:, g * gw:(g + 1) * gw] = (yg * nw_ref[:, g * gw:(g + 1) * gw]).astype(o_ref.dtype)


def _ssd_scan(zx, dt_raw, conv_w, conv_b, dt_bias, a_log, d_skip, norm_w, d_inner):
    s = zx.shape[0]
    L = SSD_CHUNK
    hp, n_state, groups = SSD_HEAD_DIM, SSD_STATE, SSD_GROUPS
    heads = d_inner // hp
    gn = groups * n_state
    gw = d_inner // groups
    assert d_inner % gn == 0 and heads <= LANES
    pad = LANES - heads
    dtb = jnp.pad(dt_bias, (0, pad)).reshape(1, LANES)
    a_neg = jnp.pad(-jnp.exp(a_log), (0, pad)).reshape(1, LANES)
    dexp = jnp.repeat(d_skip, hp).reshape(1, d_inner)
    emat = (jnp.arange(LANES)[:, None] == (jnp.arange(d_inner)[None, :] // hp)).astype(BF16)
    tri = jnp.tril(jnp.ones((L, L), BF16))
    cb2 = conv_b.reshape(1, -1)
    xb = d_inner // gn
    row = lambda c: (c, 0)
    const = lambda c: (0, 0)
    in_specs = [
        pl.BlockSpec((L, d_inner), row),
        pl.BlockSpec((L, d_inner), lambda c: (c, 1)),
        pl.BlockSpec((L, gn), lambda c: (c, 2 * xb)),
        pl.BlockSpec((L, gn), lambda c: (c, 2 * xb + 1)),
        pl.BlockSpec((L, LANES), row),
        pl.BlockSpec((SSD_CONV, d_inner), const),
        pl.BlockSpec((SSD_CONV, gn), lambda c: (0, xb)),
        pl.BlockSpec((SSD_CONV, gn), lambda c: (0, xb + 1)),
        pl.BlockSpec((1, d_inner), const),
        pl.BlockSpec((1, gn), lambda c: (0, xb)),
        pl.BlockSpec((1, gn), lambda c: (0, xb + 1)),
        pl.BlockSpec((1, LANES), const),
        pl.BlockSpec((1, LANES), const),
        pl.BlockSpec((1, d_inner), const),
        pl.BlockSpec((1, d_inner), const),
        pl.BlockSpec((LANES, d_inner), const),
        pl.BlockSpec((L, L), const),
    ]
    scratch = [
        pltpu.VMEM((CONV_HALO + L, d_inner), F32), pltpu.VMEM((CONV_HALO + L, gn), F32),
        pltpu.VMEM((CONV_HALO + L, gn), F32),
        pltpu.VMEM((L, d_inner), F32),
        pltpu.VMEM((L, d_inner), BF16),
        pltpu.VMEM((L, d_inner), BF16),
        pltpu.VMEM((L, gn), BF16), pltpu.VMEM((L, gn), BF16),
        pltpu.VMEM((L, d_inner), F32),
        pltpu.VMEM((L, LANES), F32), pltpu.VMEM((LANES, L), F32),
        pltpu.VMEM((L, d_inner), F32),
        pltpu.VMEM((groups, n_state, gw), F32),
    ]
    return pl.pallas_call(
        functools.partial(_ssd_kernel, L=L, hp=hp, n_state=n_state, groups=groups),
        out_shape=jax.ShapeDtypeStruct((s, d_inner), BF16),
        grid=(s // L,),
        in_specs=in_specs,
        out_specs=pl.BlockSpec((L, d_inner), row),
        scratch_shapes=scratch,
        compiler_params=_cparams(1),
        name="ssd_scan",
    )(zx, zx, zx, zx, dt_raw, conv_w, conv_w, conv_w, cb2, cb2, cb2, dtb, a_neg, dexp,
      norm_w.reshape(1, d_inner), emat, tri)


def _ssd_mixer(h, hn, j, in_w, conv_w, conv_b, dt_bias, a_log, d_skip, norm_w, out_w):
    d_inner = out_w.shape[1]
    heads = d_inner // SSD_HEAD_DIM
    wide = in_w.shape[2] - heads
    wt = jnp.swapaxes(in_w, 1, 2)
    zx = _matmul_t(hn, wt, layer=j, row0=0, nrows=wide, tn=MM_TN, tm=MM_TM, out_dtype=BF16, name="ssd_in_proj")
    dt_raw = _matmul_t(hn, wt, layer=j, row0=wide, nrows=heads, tn=heads, tm=MM_TM, name="ssd_dt_proj")
    y = _ssd_scan(zx, dt_raw, conv_w, conv_b, dt_bias, a_log, d_skip, norm_w, d_inner)
    return _matmul(y, out_w, layer=j, tn=MM_TN // 2, tm=MM_TM, residual=h, name="ssd_out_proj")


def _fox_cum_kernel(f_ref, b_ref, tri_ref, o_ref, carry):
    @pl.when(pl.program_id(0) == 0)
    def _():
        carry[...] = jnp.zeros_like(carry)

    x = f_ref[...] + b_ref[...]
    log_f = jnp.minimum(x, 0.0) - jnp.log1p(jnp.exp(-jnp.abs(x)))
    cs = _dot3_rhs_exact(log_f, tri_ref[...]) + carry[...]
    o_ref[...] = cs
    carry[...] = cs[:, cs.shape[1] - 1:]


def _fox_cum(f_t, bias, tk=512):
    nh, s = f_t.shape
    triu = jnp.triu(jnp.ones((tk, tk), BF16))
    return pl.pallas_call(
        _fox_cum_kernel,
        out_shape=jax.ShapeDtypeStruct((nh, s), F32),
        grid=(s // tk,),
        in_specs=[pl.BlockSpec((nh, tk), lambda i: (0, i)),
                  pl.BlockSpec((nh, 1), lambda i: (0, 0)),
                  pl.BlockSpec((tk, tk), lambda i: (0, 0))],
        out_specs=pl.BlockSpec((nh, tk), lambda i: (0, i)),
        scratch_shapes=[pltpu.VMEM((nh, 1), F32)],
        compiler_params=_cparams(1),
        name="fox_cum",
    )(f_t, bias.reshape(nh, 1), triu)


def _fox_attn_kernel(q_ref, k_ref, v_ref, cum_ref, o_ref, m_s, acc_s, vext, *, tq, tk, sr):
    qi = pl.program_id(1)
    dh = q_ref.shape[1]
    n_sub = tq // sr

    @pl.when(qi == 0)
    def _():
        vext[:, 0:dh] = v_ref[...]
        vext[:, dh:2 * dh] = jnp.ones((vext.shape[0], dh), BF16)

    q0 = pl.multiple_of(qi * tq, tq)
    c_ref = cum_ref[:, pl.ds(q0, LANES)][:, 0:1]
    m_s[...] = jnp.full_like(m_s, -jnp.inf)
    acc_s[...] = jnp.zeros_like(acc_s)

    def sub_step(r, k0, bias, col_off):
        rows = slice(r * sr, (r + 1) * sr)
        s = _dot_nt(q_ref[rows, :], k_ref[pl.ds(k0, tk), :]) + bias
        if col_off is not None:
            q_pos = r * sr + lax.broadcasted_iota(I32, (sr, tk), 0)
            k_pos = col_off + lax.broadcasted_iota(I32, (sr, tk), 1)
            s = jnp.where(q_pos >= k_pos, s, -jnp.inf)
        m_prev = m_s[rows, :]
        m_new = jnp.maximum(m_prev, jnp.max(s, axis=-1, keepdims=True))
        alpha = jnp.exp2(m_prev - m_new)
        p = jnp.exp2(s - jnp.concatenate([m_new] * (tk // dh), axis=1)).astype(BF16)
        pv = _dot(p, vext[pl.ds(k0, tk), :])
        acc_s[rows, :] = jnp.concatenate([alpha, alpha], axis=1) * acc_s[rows, :] + pv
        m_s[rows, :] = m_new

    def chunk_bias(k0):
        return (c_ref - cum_ref[:, pl.ds(k0, tk)]) * LOG2E

    def body(j, c):
        k0 = pl.multiple_of(j * tk, tk)
        bias = chunk_bias(k0)
        for r in range(n_sub):
            sub_step(r, k0, bias, None)
        return c

    lax.fori_loop(0, qi * (tq // tk), body, 0)
    for c in range(tq // tk):
        k0 = pl.multiple_of(q0 + c * tk, tk)
        bias = chunk_bias(k0)
        for r in range(n_sub):
            if (r + 1) * sr <= c * tk:
                continue
            straddles = r * sr < (c + 1) * tk - 1
            sub_step(r, k0, bias, c * tk if straddles else None)
    acc = acc_s[...]
    o_ref[...] = (acc[:, 0:dh] / acc[:, dh:2 * dh]).astype(o_ref.dtype)


def _fox_attention(qkv, cum, nh, tq=FOX_TQ, tk=FOX_TK, sr=FOX_SUB):
    s = qkv.shape[0]
    dh = FOX_HEAD_DIM
    assert tq % tk == 0 and tq % sr == 0 and tk % dh == 0 and dh == LANES
    cum3 = cum.reshape(nh, 1, s)
    return pl.pallas_call(
        functools.partial(_fox_attn_kernel, tq=tq, tk=tk, sr=sr),
        out_shape=jax.ShapeDtypeStruct((s, nh * dh), BF16),
        grid=(nh, s // tq),
        in_specs=[pl.BlockSpec((tq, dh), lambda h, i: (i, h)),
                  pl.BlockSpec((s, dh), lambda h, i: (0, nh + h)),
                  pl.BlockSpec((s, dh), lambda h, i: (0, 2 * nh + h)),
                  pl.BlockSpec((None, 1, s), lambda h, i: (h, 0, 0))],
        out_specs=pl.BlockSpec((tq, dh), lambda h, i: (i, h)),
        scratch_shapes=[pltpu.VMEM((tq, dh), F32), pltpu.VMEM((tq, 2 * dh), F32), pltpu.VMEM((s, 2 * dh), BF16)],
        compiler_params=_cparams(2),
        name="fox_attention",
    )(qkv, qkv, qkv, cum3)


def _fox_mixer(h, hn, j, in_w, f_bias, out_w):
    d = hn.shape[1]
    nh = d // FOX_HEAD_DIM
    q_scale = FOX_HEAD_DIM ** -0.5 * LOG2E
    colscale = jnp.concatenate([jnp.full((1, d), q_scale, F32), jnp.ones((1, 2 * d), F32)], axis=1)
    wt = jnp.swapaxes(in_w, 1, 2)
    qkv = _matmul_t(hn, wt, layer=j, row0=0, nrows=3 * d, tn=MM_TN, tm=MM_TM, out_dtype=BF16, colscale=colscale,
                    name="fox_in_proj")
    f_raw = _matmul_t(hn, wt, layer=j, row0=3 * d, nrows=nh, tn=nh, tm=MM_TM, name="fox_f_proj")
    cum = _fox_cum(f_raw[:, :nh].T, f_bias)
    o = _fox_attention(qkv, cum, nh)
    return _matmul(o, out_w, layer=j, tn=MM_TN, tm=MM_TM, residual=h, name="fox_out_proj")


def _norm_kernel(h_ref, g_ref, o_ref):
    o_ref[...] = _rms(h_ref[...], g_ref[...]).astype(o_ref.dtype)


def _norm(h, g, out_dtype, tm=512):
    n, d = h.shape
    return pl.pallas_call(
        _norm_kernel,
        out_shape=jax.ShapeDtypeStruct((n, d), out_dtype),
        grid=(n // tm,),
        in_specs=[pl.BlockSpec((tm, d), lambda i: (i, 0)), pl.BlockSpec((1, d), lambda i: (0, 0))],
        out_specs=pl.BlockSpec((tm, d), lambda i: (i, 0)),
        compiler_params=_cparams(1),
        name="rmsnorm",
    )(h, g.reshape(1, d))


def kernel(x, norm_mix, norm_ffn, final_norm, pool_w, pool_scale, ssd_in_w, ssd_conv_w, ssd_conv_b,
           ssd_dt_bias, ssd_a_log, ssd_d_skip, ssd_norm, ssd_out_w, fox_in_w, fox_f_bias, fox_out_w,
           moe_group_w, moe_group_b, moe_router_w, moe_router_b, moe_gate_w, moe_up_w, moe_down_w):
    bsz, s, d = x.shape
    assert bsz == 1
    depth = norm_mix.shape[0]
    h = x.reshape(s, d)
    hn = None
    for i in range(depth):
        kind, j = i % 3, i // 3
        if kind == 0:
            h = _pool_layer(h, norm_mix[i], pool_w, pool_scale, j)
        else:
            if hn is None:
                hn = _norm(h, norm_mix[i], BF16)
            if kind == 1:
                h = _ssd_mixer(h, hn, j, ssd_in_w, ssd_conv_w[j], ssd_conv_b[j], ssd_dt_bias[j], ssd_a_log[j],
                               ssd_d_skip[j], ssd_norm[j], ssd_out_w)
            else:
                h = _fox_mixer(h, hn, j, fox_in_w, fox_f_bias[j], fox_out_w)
        last = i == depth - 1
        next_kind = (i + 1) % 3
        if last:
            g_next, mode = final_norm, "final"
        elif next_kind == 0:
            g_next, mode = None, "plain"
        else:
            g_next, mode = norm_mix[i + 1], "norm_bf16"
        out = _moe_layer(h, i, norm_ffn, moe_group_w, moe_group_b, moe_router_w, moe_router_b,
                         moe_gate_w, moe_up_w, moe_down_w, g_next, mode)
        if mode == "norm_bf16":
            h, hn = out
        else:
            h, hn = out, None
    return h.reshape(bsz, s, d)
```

```python
import functools

import jax
import jax.numpy as jnp
from jax import lax
from jax.experimental import pallas as pl
from jax.experimental.pallas import tpu as pltpu

F32 = jnp.float32
BF16 = jnp.bfloat16
I32 = jnp.int32
U32 = jnp.uint32

RMS_EPS = 1e-6
LOG2E = 1.4426950408889634
LANES = 128
SLAB = 8
VMEM_LIMIT = 56 * 1024 * 1024

MM_TM = 1024
MM_TN = 1024

POOL_WINDOWS = (2, 4, 8, 16)
POOL_HALO = 16

SSD_HEAD_DIM = 64
SSD_STATE = 128
SSD_GROUPS = 8
SSD_CONV = 4
SSD_CHUNK = 128
CONV_HALO = 8

FOX_HEAD_DIM = 128
FOX_TQ = 2048
FOX_TK = 1024
FOX_SUB = 256

MOE_GROUPS = 4
MOE_EPG = 8
MOE_EXPERTS = MOE_GROUPS * MOE_EPG
MOE_ROWS = 256
MOE_TOK_TILE = 256
ROUTE_TILE = 512


def _cparams(n_axes, vmem=VMEM_LIMIT):
    return pltpu.CompilerParams(dimension_semantics=("arbitrary",) * n_axes, vmem_limit_bytes=vmem)


def _rms(x, g):
    ms = jnp.mean(x * x, axis=-1, keepdims=True)
    return x * lax.rsqrt(ms + RMS_EPS) * g


def _split3(x):
    hi = x.astype(BF16)
    r = x - hi.astype(F32)
    mid = r.astype(BF16)
    lo = (r - mid.astype(F32)).astype(BF16)
    return hi, mid, lo


def _dot(a, b):
    return jnp.dot(a, b, preferred_element_type=F32)


def _dot_nt(a, b):
    return lax.dot_general(a, b, (((1,), (1,)), ((), ())), preferred_element_type=F32)


def _dot3_rhs_exact(x, m):
    hi, mid, lo = _split3(x)
    return _dot(hi, m) + _dot(mid, m) + _dot(lo, m)


def _dot3_lhs_exact(m, x):
    hi, mid, lo = _split3(x)
    return _dot(m, hi) + _dot(m, mid) + _dot(m, lo)


def _dot2_rhs_exact(x, m):
    hi = x.astype(BF16)
    lo = (x - hi.astype(F32)).astype(BF16)
    return _dot(hi, m) + _dot(lo, m)


def _silu(x):
    h = 0.5 * x
    return h + h * jnp.tanh(h)


def _softplus(x):
    return jnp.maximum(x, 0.0) + jnp.log1p(jnp.exp(-jnp.abs(x)))


def _pack_rows(x):
    half = x.shape[1] // 2
    lo = lax.bitcast_convert_type(x[:, :half].astype(BF16).astype(F32), U32)
    hi = lax.bitcast_convert_type(x[:, half:].astype(BF16).astype(F32), U32)
    return hi | (lo >> 16)


def _unpack_rows(u):
    lo = lax.bitcast_convert_type(u << 16, F32)
    hi = lax.bitcast_convert_type(u & jnp.uint32(0xFFFF0000), F32)
    return lo, hi


def _load_token_slabs(ref, rows):
    return [ref[pl.ds(c, rows, stride=SLAB), :] for c in range(SLAB)]


def _store_token_slabs(ref, packed):
    rows = packed.shape[0]
    for c in range(SLAB):
        ref[pl.ds(c, rows, stride=SLAB), :] = packed[:, c * LANES:(c + 1) * LANES]


def _slab(ref, token):
    return ref.at[pl.ds(pl.multiple_of(token * SLAB, SLAB), SLAB)]


def _slab_run(ref, token, count):
    return ref.at[pl.ds(pl.multiple_of(token * SLAB, SLAB), count * SLAB)]


def _mm_kernel(*refs, has_res, has_scale, transposed):
    x_ref, w_ref = refs[:2]
    rest = list(refs[2:])
    r_ref = rest.pop(0) if has_res else None
    s_ref = rest.pop(0) if has_scale else None
    o_ref, wbf = rest

    @pl.when(pl.program_id(1) == 0)
    def _():
        if transposed and wbf.shape[0] != w_ref.shape[0]:
            wbf[...] = jnp.zeros_like(wbf)
            wbf[0:w_ref.shape[0], :] = w_ref[...].astype(BF16)
        else:
            wbf[...] = w_ref[...].astype(BF16)

    acc = _dot_nt(x_ref[...], wbf[...]) if transposed else _dot(x_ref[...], wbf[...])
    if s_ref is not None:
        acc = acc * s_ref[...]
    if r_ref is not None:
        acc = acc + r_ref[...]
    o_ref[...] = acc.astype(o_ref.dtype)


def _matmul(x, w, *, layer=0, col0=0, ncols=None, tn=512, tm=512, residual=None, colscale=None,
            out_dtype=F32, name="mm"):
    m, k = x.shape
    ncols = w.shape[-1] - col0 if ncols is None else ncols
    assert ncols % tn == 0 and col0 % tn == 0 and m % tm == 0
    jb = col0 // tn
    if w.ndim == 3:
        w_spec = pl.BlockSpec((None, k, tn), lambda j, i: (layer, 0, jb + j))
    else:
        w_spec = pl.BlockSpec((k, tn), lambda j, i: (0, jb + j))
    in_specs = [pl.BlockSpec((tm, k), lambda j, i: (i, 0)), w_spec]
    args = [x, w]
    if residual is not None:
        in_specs.append(pl.BlockSpec((tm, tn), lambda j, i: (i, j)))
        args.append(residual)
    if colscale is not None:
        in_specs.append(pl.BlockSpec((1, tn), lambda j, i: (0, j)))
        args.append(colscale)
    return pl.pallas_call(
        functools.partial(_mm_kernel, has_res=residual is not None, has_scale=colscale is not None,
                          transposed=False),
        out_shape=jax.ShapeDtypeStruct((m, ncols), out_dtype),
        grid=(ncols // tn, m // tm),
        in_specs=in_specs,
        out_specs=pl.BlockSpec((tm, tn), lambda j, i: (i, j)),
        scratch_shapes=[pltpu.VMEM((k, tn), BF16)],
        compiler_params=_cparams(2),
        name=name,
    )(*args)


def _matmul_t(x, wt, *, layer, row0=0, nrows=None, tn=512, tm=512, colscale=None, out_dtype=F32, name="mm_t"):
    m, k = x.shape
    nrows = wt.shape[1] - row0 if nrows is None else nrows
    assert nrows % tn == 0 and row0 % tn == 0 and m % tm == 0 and tn % 8 == 0
    assert tn % LANES == 0 or nrows == tn
    tn_out = max(tn, LANES)
    jb = row0 // tn
    in_specs = [pl.BlockSpec((tm, k), lambda j, i: (i, 0)),
                pl.BlockSpec((None, tn, k), lambda j, i: (layer, jb + j, 0))]
    args = [x, wt]
    if colscale is not None:
        in_specs.append(pl.BlockSpec((1, tn_out), lambda j, i: (0, j)))
        args.append(colscale)
    return pl.pallas_call(
        functools.partial(_mm_kernel, has_res=False, has_scale=colscale is not None, transposed=True),
        out_shape=jax.ShapeDtypeStruct((m, (nrows // tn) * tn_out), out_dtype),
        grid=(nrows // tn, m // tm),
        in_specs=in_specs,
        out_specs=pl.BlockSpec((tm, tn_out), lambda j, i: (i, j)),
        scratch_shapes=[pltpu.VMEM((tn_out, k), BF16)],
        compiler_params=_cparams(2),
        name=name,
    )(*args)


def _pool_kernel(h_ref, g_ref, w_ref, scale_ref, o_ref, xbuf, wbf, *, tm, pg):
    i = pl.program_id(0)
    d = h_ref.shape[1]

    @pl.when(i == 0)
    def _():
        xbuf[0:POOL_HALO, :] = jnp.zeros((POOL_HALO, d), F32)
        wbf[...] = w_ref[...].astype(BF16)

    h = h_ref[...]
    hn = _rms(h, g_ref[...])
    xbuf[POOL_HALO:POOL_HALO + tm, :] = hn
    row = i * tm + lax.broadcasted_iota(I32, (tm, 1), 0)
    for gi, w in enumerate(POOL_WINDOWS):
        c0 = gi * pg
        cur = hn[:, c0:c0 + pg]
        acc = cur
        for j in range(1, w):
            acc = acc + xbuf[POOL_HALO - j:POOL_HALO - j + tm, c0:c0 + pg]
        cnt = jnp.minimum(row + 1, w).astype(F32)
        pooled = acc / cnt - cur
        mix = _dot(pooled.astype(BF16), wbf[gi]) * scale_ref[:, c0:c0 + pg]
        o_ref[:, c0:c0 + pg] = h[:, c0:c0 + pg] + mix
    xbuf[0:POOL_HALO, :] = xbuf[tm:tm + POOL_HALO, :]


def _pool_layer(h, g, pool_w, pool_scale, j, tm=256):
    n, d = h.shape
    nw, pg, _ = pool_w.shape[1:]
    return pl.pallas_call(
        functools.partial(_pool_kernel, tm=tm, pg=pg),
        out_shape=jax.ShapeDtypeStruct((n, d), F32),
        grid=(n // tm,),
        in_specs=[pl.BlockSpec((tm, d), lambda i: (i, 0)),
                  pl.BlockSpec((1, d), lambda i: (0, 0)),
                  pl.BlockSpec((None, nw, pg, pg), lambda i: (j, 0, 0, 0)),
                  pl.BlockSpec((1, d), lambda i: (0, 0))],
        out_specs=pl.BlockSpec((tm, d), lambda i: (i, 0)),
        scratch_shapes=[pltpu.VMEM((POOL_HALO + tm, d), F32), pltpu.VMEM((nw, pg, pg), BF16)],
        compiler_params=_cparams(1),
        name="pool_layer",
    )(h, g.reshape(1, d), pool_w, pool_scale[j].reshape(1, d))


def _route_kernel(h_ref, g_ref, whi_ref, wlo_ref, b_ref, tri_ref, hn_ref, ri_ref, rg_ref, cnt_ref,
                  run_ref, *, tm):
    i = pl.program_id(0)

    @pl.when(i == 0)
    def _():
        run_ref[...] = jnp.zeros_like(run_ref)

    hn = _rms(h_ref[...], g_ref[...])
    hn_bf = hn.astype(BF16)
    _store_token_slabs(hn_ref, _pack_rows(hn))
    hn_lo = (hn - hn_bf.astype(F32)).astype(BF16)
    whi = whi_ref[...]
    logits = _dot(hn_bf, whi) + _dot(hn_bf, wlo_ref[...]) + _dot(hn_lo, whi) + b_ref[...]

    lane = lax.broadcasted_iota(I32, (tm, LANES), 1)
    lane_f = lane.astype(F32)
    neg = -jnp.inf
    big = float(LANES)
    gl = jnp.where(lane < MOE_GROUPS, logits, neg)
    gmax = jnp.max(gl, axis=-1, keepdims=True)
    gsel = jnp.min(jnp.where(gl == gmax, lane_f, big), axis=-1, keepdims=True).astype(I32)
    p_grp = 1.0 / jnp.sum(jnp.exp(gl - gmax), axis=-1, keepdims=True)
    lo_lane = MOE_GROUPS + MOE_EPG * gsel
    el = jnp.where((lane >= lo_lane) & (lane < lo_lane + MOE_EPG), logits, neg)
    v0 = jnp.max(el, axis=-1, keepdims=True)
    i0 = jnp.min(jnp.where(el == v0, lane_f, big), axis=-1, keepdims=True).astype(I32)
    el2 = jnp.where(lane == i0, neg, el)
    v1 = jnp.max(el2, axis=-1, keepdims=True)
    i1 = jnp.min(jnp.where(el2 == v1, lane_f, big), axis=-1, keepdims=True).astype(I32)
    t = jnp.exp(v1 - v0)
    gate0 = p_grp / (1.0 + t)
    gate1 = p_grp * t / (1.0 + t)
    e0 = i0 - MOE_GROUPS
    e1 = i1 - MOE_GROUPS

    onehot = ((lane == e0) | (lane == e1))
    c_bf = jnp.where(onehot, 1.0, 0.0).astype(BF16)
    prefix = _dot(tri_ref[...], c_bf) + run_ref[...]
    rank0 = jnp.sum(jnp.where(lane == e0, prefix, 0.0), axis=-1, keepdims=True).astype(I32)
    rank1 = jnp.sum(jnp.where(lane == e1, prefix, 0.0), axis=-1, keepdims=True).astype(I32)
    run = run_ref[...] + jnp.sum(jnp.where(onehot, 1.0, 0.0), axis=0, keepdims=True)
    run_ref[...] = run
    cnt_ref[...] = jnp.broadcast_to(run, cnt_ref.shape)

    ri = jnp.where(lane == 0, e0, jnp.where(lane == 1, e1, jnp.where(lane == 2, rank0, rank1)))
    ri_ref[...] = ri.T[0:8, :]
    rg_ref[...] = jnp.where(lane == 0, gate0, gate1)


def _route(h, g, wr, br, tm=ROUTE_TILE):
    n, d = h.shape
    whi = wr.astype(BF16)
    wlo = (wr - whi.astype(F32)).astype(BF16)
    tri = jnp.tril(jnp.ones((tm, tm), BF16), -1)
    return pl.pallas_call(
        functools.partial(_route_kernel, tm=tm),
        out_shape=(jax.ShapeDtypeStruct((n * SLAB, LANES), U32),
                   jax.ShapeDtypeStruct((8, n), I32),
                   jax.ShapeDtypeStruct((n, LANES), F32),
                   jax.ShapeDtypeStruct((8, LANES), F32)),
        grid=(n // tm,),
        in_specs=[pl.BlockSpec((tm, d), lambda i: (i, 0)),
                  pl.BlockSpec((1, d), lambda i: (0, 0)),
                  pl.BlockSpec((d, LANES), lambda i: (0, 0)),
                  pl.BlockSpec((d, LANES), lambda i: (0, 0)),
                  pl.BlockSpec((1, LANES), lambda i: (0, 0)),
                  pl.BlockSpec((tm, tm), lambda i: (0, 0))],
        out_specs=(pl.BlockSpec((tm * SLAB, LANES), lambda i: (i, 0)),
                   pl.BlockSpec((8, tm), lambda i: (0, i)),
                   pl.BlockSpec((tm, LANES), lambda i: (i, 0)),
                   pl.BlockSpec((8, LANES), lambda i: (0, 0))),
        scratch_shapes=[pltpu.VMEM((1, LANES), F32)],
        compiler_params=_cparams(1),
        name="moe_route",
    )(h, g.reshape(1, d), whi, wlo, br, tri)


def _dispatch_kernel(pad_start_ref, pad_len_ref, nused_ref, dest_ref, hn_ref, xs_ref, zbuf, sem, zsem, *, t):
    sizes = [1 << s for s in reversed(range(MOE_ROWS.bit_length() - 1))]

    @pl.when(pl.program_id(0) == 0)
    def _():
        zbuf[...] = jnp.zeros_like(zbuf)

        def zero_copy(off, size):
            return pltpu.make_async_copy(zbuf.at[pl.ds(0, size * SLAB)], _slab_run(xs_ref, off, size), zsem)

        def gaps(e, wait):
            off = pad_start_ref[e]
            n = pad_len_ref[e]
            for size in sizes:
                hit = (n & size) != 0

                @pl.when(hit)
                def _():
                    c = zero_copy(off, size)
                    c.wait() if wait else c.start()

                off = off + (n & size)

        def start_gaps(e, c):
            gaps(e, False)
            return c

        def wait_gaps(e, c):
            gaps(e, True)
            return c

        lax.fori_loop(0, MOE_EXPERTS, start_gaps, 0)
        lax.fori_loop(0, MOE_EXPERTS, wait_gaps, 0)

        zrows = zbuf.shape[0] // SLAB
        n_blocks = xs_ref.shape[0] // (MOE_ROWS * SLAB)

        def tail(wait):
            def body(b, c):
                for part in range(MOE_ROWS // zrows):
                    cp = zero_copy(b * MOE_ROWS + part * zrows, zrows)
                    cp.wait() if wait else cp.start()
                return c
            lax.fori_loop(nused_ref[0], n_blocks, body, 0)

        tail(False)
        tail(True)

    def row_copy(r, k):
        return pltpu.make_async_copy(_slab(hn_ref, r), _slab(xs_ref, dest_ref[0, k * t + r]), sem)

    def start(r, c):
        row_copy(r, 0).start(priority=0)
        row_copy(r, 1).start(priority=1)
        return c

    lax.fori_loop(0, t, start, 0, unroll=8)
    for _ in range(2):
        pltpu.make_async_copy(hn_ref, xs_ref.at[pl.ds(0, t * SLAB)], sem).wait()


def _dispatch(hn, dest, pad_start, pad_len, n_used, n_rows, t=MOE_TOK_TILE):
    n = hn.shape[0] // SLAB
    grid_spec = pltpu.PrefetchScalarGridSpec(
        num_scalar_prefetch=3,
        grid=(n // t,),
        in_specs=[pl.BlockSpec((None, 1, 2 * t), lambda i, *_: (i, 0, 0), memory_space=pltpu.SMEM),
                  pl.BlockSpec((t * SLAB, LANES), lambda i, *_: (i, 0))],
        out_specs=pl.BlockSpec(memory_space=pl.ANY),
        scratch_shapes=[pltpu.VMEM((MOE_ROWS // 2 * SLAB, LANES), hn.dtype),
                        pltpu.SemaphoreType.DMA(()), pltpu.SemaphoreType.DMA(())],
    )
    return pl.pallas_call(
        functools.partial(_dispatch_kernel, t=t),
        out_shape=jax.ShapeDtypeStruct((n_rows * SLAB, LANES), hn.dtype),
        grid_spec=grid_spec,
        compiler_params=_cparams(1),
        name="moe_dispatch",
    )(pad_start, pad_len, n_used, dest, hn)


def _expert_kernel(bexp_ref, first_ref, slot_ref, next_ref, nused_ref, xs_ref, wg_hbm, wu_hbm, wd_hbm, o_ref,
                   wg_raw, wu_raw, wd_raw, wg_bf, wu_bf, wd_bf, sems, *, layer):
    b = pl.program_id(0)
    active = b < nused_ref[0]

    def fetch(e, slot):
        return [pltpu.make_async_copy(hbm.at[layer, e], raw.at[slot], sems.at[slot, k])
                for k, (hbm, raw) in enumerate(((wg_hbm, wg_raw), (wu_hbm, wu_raw), (wd_hbm, wd_raw)))]

    @pl.when(b == 0)
    def _():
        for c in fetch(bexp_ref[0], 0):
            c.start()

    @pl.when(active & (first_ref[b] == 1))
    def _():
        slot = slot_ref[b]
        nxt = next_ref[b]

        @pl.when(nxt >= 0)
        def _():
            for c in fetch(nxt, 1 - slot):
                c.start()

        for c in fetch(bexp_ref[b], slot):
            c.wait()
        wg_bf[...] = wg_raw[slot].astype(BF16)
        wu_bf[...] = wu_raw[slot].astype(BF16)
        wd_bf[...] = wd_raw[slot].astype(BF16)

    @pl.when(active)
    def _():
        rows = xs_ref.shape[0] // SLAB
        halves = [_unpack_rows(u) for u in _load_token_slabs(xs_ref, rows)]
        x = jnp.concatenate([lo.astype(BF16) for lo, _ in halves] + [hi.astype(BF16) for _, hi in halves], axis=1)
        gte = _dot(x, wg_bf[...])
        up = _dot(x, wu_bf[...])
        hb = _silu(gte) * up
        _store_token_slabs(o_ref, _pack_rows(_dot(hb.astype(BF16), wd_bf[...])))

    @pl.when(jnp.logical_not(active))
    def _():
        o_ref[...] = jnp.zeros_like(o_ref)


def _experts(xs, tables, gate_w, up_w, down_w, layer, rows=MOE_ROWS):
    n_rows = xs.shape[0] // SLAB
    d = gate_w.shape[-2]
    hid = gate_w.shape[-1]
    assert d == 2 * SLAB * LANES
    x_map = lambda b, be, fi, sl, nx, nu: (jnp.minimum(b, nu[0] - 1), 0)
    grid_spec = pltpu.PrefetchScalarGridSpec(
        num_scalar_prefetch=5,
        grid=(n_rows // rows,),
        in_specs=[pl.BlockSpec((rows * SLAB, LANES), x_map),
                  pl.BlockSpec(memory_space=pl.ANY),
                  pl.BlockSpec(memory_space=pl.ANY),
                  pl.BlockSpec(memory_space=pl.ANY)],
        out_specs=pl.BlockSpec((rows * SLAB, LANES), lambda b, *_: (b, 0)),
        scratch_shapes=[pltpu.VMEM((2, d, hid), F32), pltpu.VMEM((2, d, hid), F32), pltpu.VMEM((2, hid, d), F32),
                        pltpu.VMEM((d, hid), BF16), pltpu.VMEM((d, hid), BF16), pltpu.VMEM((hid, d), BF16),
                        pltpu.SemaphoreType.DMA((2, 3))],
    )
    return pl.pallas_call(
        functools.partial(_expert_kernel, layer=layer),
        out_shape=jax.ShapeDtypeStruct((n_rows * SLAB, LANES), U32),
        grid_spec=grid_spec,
        compiler_params=_cparams(1),
        name="moe_experts",
    )(*tables, xs, gate_w, up_w, down_w)


def _combine_kernel(*refs, t, mode):
    dest_ref, dest_next_ref, h_ref, rg_ref, ys_ref = refs[:5]
    rest = list(refs[5:])
    g_ref = rest.pop(0) if mode != "plain" else None
    o_ref = rest.pop(0)
    hn_ref = rest.pop(0) if mode == "norm_bf16" else None
    buf00, buf01, buf10, buf11, sems = rest
    bufs = ((buf00, buf01), (buf10, buf11))
    i = pl.program_id(0)
    n = pl.num_programs(0)
    half = SLAB * LANES

    def row_copy(dref, slot, r, k):
        return pltpu.make_async_copy(_slab(ys_ref, dref[0, k * t + r]), _slab(bufs[slot][k], r), sems.at[slot])

    def issue(dref, slot):
        def start(r, c):
            row_copy(dref, slot, r, 0).start(priority=0)
            row_copy(dref, slot, r, 1).start(priority=1)
            return c
        lax.fori_loop(0, t, start, 0, unroll=8)

    def drain(slot):
        for k in range(2):
            pltpu.make_async_copy(ys_ref.at[pl.ds(0, t * SLAB)], bufs[slot][k], sems.at[slot]).wait()

    @pl.when(i == 0)
    def _():
        issue(dest_ref, 0)

    for slot in (0, 1):
        @pl.when((i % 2 == slot) & (i + 1 < n))
        def _():
            issue(dest_next_ref, 1 - slot)

        @pl.when(i % 2 == slot)
        def _():
            drain(slot)
            rg = rg_ref[...]
            g0 = rg[:, 0:1]
            g1 = rg[:, 1:2]
            for c, (ua, ub) in enumerate(zip(_load_token_slabs(bufs[slot][0], t), _load_token_slabs(bufs[slot][1], t))):
                a_lo, a_hi = _unpack_rows(ua)
                b_lo, b_hi = _unpack_rows(ub)
                lo_cols = slice(c * LANES, (c + 1) * LANES)
                hi_cols = slice(half + c * LANES, half + (c + 1) * LANES)
                o_ref[:, lo_cols] = h_ref[:, lo_cols] + (g0 * a_lo + g1 * b_lo)
                o_ref[:, hi_cols] = h_ref[:, hi_cols] + (g0 * a_hi + g1 * b_hi)

    if mode != "plain":
        h_new = o_ref[...]
        normed = _rms(h_new, g_ref[...])
        if mode == "norm_bf16":
            hn_ref[...] = normed.astype(BF16)
        else:
            o_ref[...] = normed


def _combine(h, rg, ys, dest, g_next, mode, t=MOE_TOK_TILE):
    n, d = h.shape
    assert d == 2 * SLAB * LANES
    last = n // t - 1
    in_specs = [pl.BlockSpec((None, 1, 2 * t), lambda i: (i, 0, 0), memory_space=pltpu.SMEM),
                pl.BlockSpec((None, 1, 2 * t), lambda i: (jnp.minimum(i + 1, last), 0, 0),
                             memory_space=pltpu.SMEM),
                pl.BlockSpec((t, d), lambda i: (i, 0)),
                pl.BlockSpec((t, LANES), lambda i: (i, 0)),
                pl.BlockSpec(memory_space=pl.ANY)]
    args = [dest, dest, h, rg, ys]
    tile = pl.BlockSpec((t, d), lambda i: (i, 0))
    if mode != "plain":
        in_specs.append(pl.BlockSpec((1, d), lambda i: (0, 0)))
        args.append(g_next.reshape(1, d))
    if mode == "norm_bf16":
        out_shape = (jax.ShapeDtypeStruct((n, d), F32), jax.ShapeDtypeStruct((n, d), BF16))
        out_specs = (tile, tile)
    else:
        out_shape = jax.ShapeDtypeStruct((n, d), F32)
        out_specs = tile
    return pl.pallas_call(
        functools.partial(_combine_kernel, t=t, mode=mode),
        out_shape=out_shape,
        grid=(n // t,),
        in_specs=in_specs,
        out_specs=out_specs,
        scratch_shapes=[pltpu.VMEM((t * SLAB, LANES), U32)] * 4 + [pltpu.SemaphoreType.DMA((2,))],
        compiler_params=_cparams(1),
        name="moe_combine",
    )(*args)


def _moe_layer(h, layer, norm_ffn, group_w, group_b, router_w, router_b, gate_w, up_w, down_w, g_next, mode):
    n, d = h.shape
    pad = LANES - MOE_GROUPS - MOE_EXPERTS
    wr = jnp.concatenate([group_w[layer], router_w[layer], jnp.zeros((d, pad), F32)], axis=1)
    br = jnp.concatenate([group_b[layer], router_b[layer], jnp.zeros((pad,), F32)]).reshape(1, LANES)
    hn, ri, rg, cnt = _route(h, norm_ffn[layer], wr, br)

    t = MOE_TOK_TILE
    counts = cnt[0, :MOE_EXPERTS].astype(I32)
    nblk = (counts + MOE_ROWS - 1) // MOE_ROWS
    blk_end = jnp.cumsum(nblk)
    row_start = (blk_end - nblk) * MOE_ROWS
    n_rows = -(-(n * 2 + MOE_EXPERTS * (MOE_ROWS - 1)) // MOE_ROWS) * MOE_ROWS
    n_blocks = n_rows // MOE_ROWS
    experts = ri[0:2, :]
    ranks = ri[2:4, :]
    onehot = experts[:, :, None] == jnp.arange(MOE_EXPERTS, dtype=I32)
    dest = jnp.sum(jnp.where(onehot, row_start, 0), axis=-1) + ranks
    dest = dest.reshape(2, n // t, t).transpose(1, 0, 2).reshape(n // t, 1, 2 * t).astype(I32)
    blocks = jnp.arange(n_blocks, dtype=I32)
    block_exp = jnp.minimum(jnp.sum(blk_end[None, :] <= blocks[:, None], axis=1), MOE_EXPERTS - 1).astype(I32)
    n_used = blk_end[-1:].astype(I32)
    eids = jnp.arange(MOE_EXPERTS, dtype=I32)
    nonempty = nblk > 0
    ordinal = jnp.cumsum(nonempty.astype(I32)) - 1
    at_or_after = lax.cummin(jnp.where(nonempty, eids, MOE_EXPERTS)[::-1])[::-1]
    after = jnp.concatenate([at_or_after[1:], jnp.full((1,), MOE_EXPERTS, I32)])
    next_nonempty = jnp.where(after < MOE_EXPERTS, after, -1)
    is_first = ((blocks == (blk_end - nblk)[block_exp]) & (blocks < n_used[0])).astype(I32)
    tables = (block_exp, is_first, (ordinal[block_exp] % 2).astype(I32), next_nonempty[block_exp].astype(I32), n_used)
    pad_start = (row_start + counts).astype(I32)
    pad_len = (nblk * MOE_ROWS - counts).astype(I32)

    xs = _dispatch(hn, dest, pad_start, pad_len, n_used, n_rows)
    ys = _experts(xs, tables, gate_w, up_w, down_w, layer)
    return _combine(h, rg, ys, dest, g_next, mode)


def _ssd_kernel(z_ref, x_ref, b_ref, c_ref, dt_ref, cwx_ref, cwb_ref, cwc_ref, cbx_ref, cbb_ref, cbc_ref,
                dtb_ref, a_ref, dexp_ref, nw_ref, exp_ref, tri_ref, o_ref,
                xbuf, bbuf, cbuf, xc_s, xdt_s, xw_s, b_s, c_s, eacs_s, acs_s, acst_s, y_s, st_s,
                *, L, hp, n_state, groups):
    ci = pl.program_id(0)
    d_inner = x_ref.shape[1]
    gw = d_inner // groups
    kh = gw // hp
    H0 = CONV_HALO

    @pl.when(ci == 0)
    def _():
        xbuf[0:H0, :] = jnp.zeros((H0, xbuf.shape[1]), F32)
        bbuf[0:H0, :] = jnp.zeros((H0, bbuf.shape[1]), F32)
        cbuf[0:H0, :] = jnp.zeros((H0, cbuf.shape[1]), F32)
        st_s[...] = jnp.zeros_like(st_s)

    def conv_silu(in_ref, buf, cw_ref, cb_ref):
        buf[H0:H0 + L, :] = in_ref[...].astype(F32)
        acc = cb_ref[...] + cw_ref[SSD_CONV - 1:SSD_CONV, :] * buf[H0:H0 + L, :]
        for j in range(SSD_CONV - 1):
            off = H0 - (SSD_CONV - 1) + j
            acc = acc + cw_ref[j:j + 1, :] * buf[off:off + L, :]
        buf[0:H0, :] = buf[L:L + H0, :]
        return _silu(acc)

    xc_s[...] = conv_silu(x_ref, xbuf, cwx_ref, cbx_ref)
    b_s[...] = conv_silu(b_ref, bbuf, cwb_ref, cbb_ref).astype(BF16)
    c_s[...] = conv_silu(c_ref, cbuf, cwc_ref, cbc_ref).astype(BF16)

    dt = _softplus(dt_ref[...] + dtb_ref[...])
    d_a = dt * a_ref[...]
    acs = _dot3_lhs_exact(tri_ref[...], d_a)
    acs_s[...] = acs
    acst_s[...] = acs.T
    emat = exp_ref[...]
    acs_last = acs[L - 1:L, :]
    to_end = dt * jnp.exp(acs_last - acs)
    xc = xc_s[...]
    xdt_s[...] = (xc * _dot2_rhs_exact(dt, emat)).astype(BF16)
    xw_s[...] = (xc * _dot2_rhs_exact(to_end, emat)).astype(BF16)
    eacs_s[...] = _dot2_rhs_exact(jnp.exp(acs), emat)
    cdecay = _dot2_rhs_exact(jnp.broadcast_to(jnp.exp(acs_last), (SLAB, LANES)), emat)[0:1, :]

    rr = lax.broadcasted_iota(I32, (L, L), 0)
    cc = lax.broadcasted_iota(I32, (L, L), 1)
    causal = rr >= cc
    lane2 = lax.broadcasted_iota(I32, (L, 2 * hp), 1)

    for g in range(groups):
        bg = b_s[:, g * n_state:(g + 1) * n_state]
        cg = c_s[:, g * n_state:(g + 1) * n_state]
        cb = _dot_nt(cg, bg)
        st_prev = st_s[g]
        y_off = _dot(cg, st_prev.astype(BF16)) * eacs_s[:, g * gw:(g + 1) * gw]
        for kp in range(kh // 2):
            c0 = g * gw + kp * 2 * hp
            slab = xdt_s[:, c0:c0 + 2 * hp]
            y_pair = None
            for half in range(2):
                hd = g * kh + kp * 2 + half
                seg = acs_s[:, hd:hd + 1] - acst_s[hd:hd + 1, :]
                decay = jnp.exp(jnp.where(causal, seg, -jnp.inf))
                m = (cb * decay).astype(BF16)
                keep = (lane2 < hp) if half == 0 else (lane2 >= hp)
                part = _dot(m, jnp.where(keep, slab, jnp.zeros_like(slab)))
                y_pair = part if y_pair is None else y_pair + part
            y_s[:, c0:c0 + 2 * hp] = y_pair + y_off[:, kp * 2 * hp:(kp + 1) * 2 * hp]
        bgt = bg.astype(F32).T.astype(BF16)
        s_new = _dot(bgt, xw_s[:, g * gw:(g + 1) * gw])
        st_s[g] = cdecay[:, g * gw:(g + 1) * gw] * st_prev + s_new

    y = y_s[...] + dexp_ref[...] * xc_s[...]
    z = z_ref[...].astype(F32)
    y = y * _silu(z)
    for g in range(groups):
        yg = y[:, g * gw:(g + 1) * gw]
        yg = yg * lax.rsqrt(jnp.mean(yg * yg, axis=-1, keepdims=True) + RMS_EPS)
        o_ref[:, g * gw:(g + 1) * gw] = (yg * nw_ref[:, g * gw:(g + 1) * gw]).astype(o_ref.dtype)


def _ssd_scan(zx, dt_raw, conv_w, conv_b, dt_bias, a_log, d_skip, norm_w, d_inner):
    s = zx.shape[0]
    L = SSD_CHUNK
    hp, n_state, groups = SSD_HEAD_DIM, SSD_STATE, SSD_GROUPS
    heads = d_inner // hp
    gn = groups * n_state
    gw = d_inner // groups
    assert d_inner % gn == 0 and heads <= LANES
    pad = LANES - heads
    dtb = jnp.pad(dt_bias, (0, pad)).reshape(1, LANES)
    a_neg = jnp.pad(-jnp.exp(a_log), (0, pad)).reshape(1, LANES)
    dexp = jnp.repeat(d_skip, hp).reshape(1, d_inner)
    emat = (jnp.arange(LANES)[:, None] == (jnp.arange(d_inner)[None, :] // hp)).astype(BF16)
    tri = jnp.tril(jnp.ones((L, L), BF16))
    cb2 = conv_b.reshape(1, -1)
    xb = d_inner // gn
    row = lambda c: (c, 0)
    const = lambda c: (0, 0)
    in_specs = [
        pl.BlockSpec((L, d_inner), row),
        pl.BlockSpec((L, d_inner), lambda c: (c, 1)),
        pl.BlockSpec((L, gn), lambda c: (c, 2 * xb)),
        pl.BlockSpec((L, gn), lambda c: (c, 2 * xb + 1)),
        pl.BlockSpec((L, LANES), row),
        pl.BlockSpec((SSD_CONV, d_inner), const),
        pl.BlockSpec((SSD_CONV, gn), lambda c: (0, xb)),
        pl.BlockSpec((SSD_CONV, gn), lambda c: (0, xb + 1)),
        pl.BlockSpec((1, d_inner), const),
        pl.BlockSpec((1, gn), lambda c: (0, xb)),
        pl.BlockSpec((1, gn), lambda c: (0, xb + 1)),
        pl.BlockSpec((1, LANES), const),
        pl.BlockSpec((1, LANES), const),
        pl.BlockSpec((1, d_inner), const),
        pl.BlockSpec((1, d_inner), const),
        pl.BlockSpec((LANES, d_inner), const),
        pl.BlockSpec((L, L), const),
    ]
    scratch = [
        pltpu.VMEM((CONV_HALO + L, d_inner), F32), pltpu.VMEM((CONV_HALO + L, gn), F32),
        pltpu.VMEM((CONV_HALO + L, gn), F32),
        pltpu.VMEM((L, d_inner), F32),
        pltpu.VMEM((L, d_inner), BF16),
        pltpu.VMEM((L, d_inner), BF16),
        pltpu.VMEM((L, gn), BF16), pltpu.VMEM((L, gn), BF16),
        pltpu.VMEM((L, d_inner), F32),
        pltpu.VMEM((L, LANES), F32), pltpu.VMEM((LANES, L), F32),
        pltpu.VMEM((L, d_inner), F32),
        pltpu.VMEM((groups, n_state, gw), F32),
    ]
    return pl.pallas_call(
        functools.partial(_ssd_kernel, L=L, hp=hp, n_state=n_state, groups=groups),
        out_shape=jax.ShapeDtypeStruct((s, d_inner), BF16),
        grid=(s // L,),
        in_specs=in_specs,
        out_specs=pl.BlockSpec((L, d_inner), row),
        scratch_shapes=scratch,
        compiler_params=_cparams(1),
        name="ssd_scan",
    )(zx, zx, zx, zx, dt_raw, conv_w, conv_w, conv_w, cb2, cb2, cb2, dtb, a_neg, dexp,
      norm_w.reshape(1, d_inner), emat, tri)


def _ssd_mixer(h, hn, j, in_w, conv_w, conv_b, dt_bias, a_log, d_skip, norm_w, out_w):
    d_inner = out_w.shape[1]
    heads = d_inner // SSD_HEAD_DIM
    wide = in_w.shape[2] - heads
    wt = jnp.swapaxes(in_w, 1, 2)
    zx = _matmul_t(hn, wt, layer=j, row0=0, nrows=wide, tn=MM_TN, tm=MM_TM, out_dtype=BF16, name="ssd_in_proj")
    dt_raw = _matmul_t(hn, wt, layer=j, row0=wide, nrows=heads, tn=heads, tm=MM_TM, name="ssd_dt_proj")
    y = _ssd_scan(zx, dt_raw, conv_w, conv_b, dt_bias, a_log, d_skip, norm_w, d_inner)
    return _matmul(y, out_w, layer=j, tn=MM_TN // 2, tm=MM_TM, residual=h, name="ssd_out_proj")


def _fox_cum_kernel(f_ref, b_ref, tri_ref, o_ref, carry):
    @pl.when(pl.program_id(0) == 0)
    def _():
        carry[...] = jnp.zeros_like(carry)

    x = f_ref[...] + b_ref[...]
    log_f = jnp.minimum(x, 0.0) - jnp.log1p(jnp.exp(-jnp.abs(x)))
    cs = _dot3_rhs_exact(log_f, tri_ref[...]) + carry[...]
    o_ref[...] = cs
    carry[...] = cs[:, cs.shape[1] - 1:]


def _fox_cum(f_t, bias, tk=512):
    nh, s = f_t.shape
    triu = jnp.triu(jnp.ones((tk, tk), BF16))
    return pl.pallas_call(
        _fox_cum_kernel,
        out_shape=jax.ShapeDtypeStruct((nh, s), F32),
        grid=(s // tk,),
        in_specs=[pl.BlockSpec((nh, tk), lambda i: (0, i)),
                  pl.BlockSpec((nh, 1), lambda i: (0, 0)),
                  pl.BlockSpec((tk, tk), lambda i: (0, 0))],
        out_specs=pl.BlockSpec((nh, tk), lambda i: (0, i)),
        scratch_shapes=[pltpu.VMEM((nh, 1), F32)],
        compiler_params=_cparams(1),
        name="fox_cum",
    )(f_t, bias.reshape(nh, 1), triu)


def _fox_attn_kernel(q_ref, k_ref, v_ref, cum_ref, o_ref, m_s, acc_s, vext, *, tq, tk, sr):
    qi = pl.program_id(1)
    dh = q_ref.shape[1]
    n_sub = tq // sr

    @pl.when(qi == 0)
    def _():
        vext[:, 0:dh] = v_ref[...]
        vext[:, dh:2 * dh] = jnp.ones((vext.shape[0], dh), BF16)

    q0 = pl.multiple_of(qi * tq, tq)
    c_ref = cum_ref[:, pl.ds(q0, LANES)][:, 0:1]
    m_s[...] = jnp.full_like(m_s, -jnp.inf)
    acc_s[...] = jnp.zeros_like(acc_s)

    def sub_step(r, k0, width, bias, diagonal):
        rows = slice(r * sr, (r + 1) * sr)
        s = _dot_nt(q_ref[rows, :], k_ref[pl.ds(k0, width), :]) + bias
        if diagonal:
            keep = lax.broadcasted_iota(I32, (sr, width), 0) >= lax.broadcasted_iota(I32, (sr, width), 1)
            s = jnp.where(keep, s, -jnp.inf)
        m_prev = m_s[rows, :]
        m_new = jnp.maximum(m_prev, jnp.max(s, axis=-1, keepdims=True))
        alpha = jnp.exp2(m_prev - m_new)
        p = jnp.exp2(s - jnp.concatenate([m_new] * (width // dh), axis=1)).astype(BF16)
        pv = _dot(p, vext[pl.ds(k0, width), :])
        acc_s[rows, :] = jnp.concatenate([alpha, alpha], axis=1) * acc_s[rows, :] + pv
        m_s[rows, :] = m_new

    def key_bias(k0, width):
        return (c_ref - cum_ref[:, pl.ds(k0, width)]) * LOG2E

    def body(j, c):
        k0 = pl.multiple_of(j * tk, tk)
        bias = key_bias(k0, tk)
        for r in range(n_sub):
            sub_step(r, k0, tk, bias, False)
        return c

    lax.fori_loop(0, qi * (tq // tk), body, 0)
    for r in range(n_sub):
        if r > 0:
            sub_step(r, q0, r * sr, key_bias(q0, r * sr), False)
        k0 = pl.multiple_of(q0 + r * sr, sr)
        sub_step(r, k0, sr, key_bias(k0, sr), True)
    acc = acc_s[...]
    o_ref[...] = (acc[:, 0:dh] / acc[:, dh:2 * dh]).astype(o_ref.dtype)


def _fox_attention(qkv, cum, nh, tq=FOX_TQ, tk=FOX_TK, sr=FOX_SUB):
    s = qkv.shape[0]
    dh = FOX_HEAD_DIM
    assert tq % tk == 0 and tq % sr == 0 and tk % dh == 0 and dh == LANES
    cum3 = cum.reshape(nh, 1, s)
    return pl.pallas_call(
        functools.partial(_fox_attn_kernel, tq=tq, tk=tk, sr=sr),
        out_shape=jax.ShapeDtypeStruct((s, nh * dh), BF16),
        grid=(nh, s // tq),
        in_specs=[pl.BlockSpec((tq, dh), lambda h, i: (i, h)),
                  pl.BlockSpec((s, dh), lambda h, i: (0, nh + h)),
                  pl.BlockSpec((s, dh), lambda h, i: (0, 2 * nh + h)),
                  pl.BlockSpec((None, 1, s), lambda h, i: (h, 0, 0))],
        out_specs=pl.BlockSpec((tq, dh), lambda h, i: (i, h)),
        scratch_shapes=[pltpu.VMEM((tq, dh), F32), pltpu.VMEM((tq, 2 * dh), F32), pltpu.VMEM((s, 2 * dh), BF16)],
        compiler_params=_cparams(2),
        name="fox_attention",
    )(qkv, qkv, qkv, cum3)


def _fox_mixer(h, hn, j, in_w, f_bias, out_w):
    d = hn.shape[1]
    nh = d // FOX_HEAD_DIM
    q_scale = FOX_HEAD_DIM ** -0.5 * LOG2E
    colscale = jnp.concatenate([jnp.full((1, d), q_scale, F32), jnp.ones((1, 2 * d), F32)], axis=1)
    wt = jnp.swapaxes(in_w, 1, 2)
    qkv = _matmul_t(hn, wt, layer=j, row0=0, nrows=3 * d, tn=MM_TN, tm=MM_TM, out_dtype=BF16, colscale=colscale,
                    name="fox_in_proj")
    f_raw = _matmul_t(hn, wt, layer=j, row0=3 * d, nrows=nh, tn=nh, tm=MM_TM, name="fox_f_proj")
    cum = _fox_cum(f_raw[:, :nh].T, f_bias)
    o = _fox_attention(qkv, cum, nh)
    return _matmul(o, out_w, layer=j, tn=MM_TN, tm=MM_TM, residual=h, name="fox_out_proj")


def _norm_kernel(h_ref, g_ref, o_ref):
    o_ref[...] = _rms(h_ref[...], g_ref[...]).astype(o_ref.dtype)


def _norm(h, g, out_dtype, tm=512):
    n, d = h.shape
    return pl.pallas_call(
        _norm_kernel,
        out_shape=jax.ShapeDtypeStruct((n, d), out_dtype),
        grid=(n // tm,),
        in_specs=[pl.BlockSpec((tm, d), lambda i: (i, 0)), pl.BlockSpec((1, d), lambda i: (0, 0))],
        out_specs=pl.BlockSpec((tm, d), lambda i: (i, 0)),
        compiler_params=_cparams(1),
        name="rmsnorm",
    )(h, g.reshape(1, d))


def kernel(x, norm_mix, norm_ffn, final_norm, pool_w, pool_scale, ssd_in_w, ssd_conv_w, ssd_conv_b,
           ssd_dt_bias, ssd_a_log, ssd_d_skip, ssd_norm, ssd_out_w, fox_in_w, fox_f_bias, fox_out_w,
           moe_group_w, moe_group_b, moe_router_w, moe_router_b, moe_gate_w, moe_up_w, moe_down_w):
    bsz, s, d = x.shape
    assert bsz == 1
    depth = norm_mix.shape[0]
    h = x.reshape(s, d)
    hn = None
    for i in range(depth):
        kind, j = i % 3, i // 3
        if kind == 0:
            h = _pool_layer(h, norm_mix[i], pool_w, pool_scale, j)
        else:
            if hn is None:
                hn = _norm(h, norm_mix[i], BF16)
            if kind == 1:
                h = _ssd_mixer(h, hn, j, ssd_in_w, ssd_conv_w[j], ssd_conv_b[j], ssd_dt_bias[j], ssd_a_log[j],
                               ssd_d_skip[j], ssd_norm[j], ssd_out_w)
            else:
                h = _fox_mixer(h, hn, j, fox_in_w, fox_f_bias[j], fox_out_w)
        last = i == depth - 1
        next_kind = (i + 1) % 3
        if last:
            g_next, mode = final_norm, "final"
        elif next_kind == 0:
            g_next, mode = None, "plain"
        else:
            g_next, mode = norm_mix[i + 1], "norm_bf16"
        out = _moe_layer(h, i, norm_ffn, moe_group_w, moe_group_b, moe_router_w, moe_router_b,
                         moe_gate_w, moe_up_w, moe_down_w, g_next, mode)
        if mode == "norm_bf16":
            h, hn = out
        else:
            h, hn = out, None
    return h.reshape(bsz, s, d)
```

```python
import functools

import jax
import jax.numpy as jnp
from jax import lax
from jax.experimental import pallas as pl
from jax.experimental.pallas import tpu as pltpu

F32 = jnp.float32
BF16 = jnp.bfloat16
I32 = jnp.int32
U32 = jnp.uint32

RMS_EPS = 1e-6
LOG2E = 1.4426950408889634
LANES = 128
SLAB = 8
VMEM_LIMIT = 56 * 1024 * 1024

MM_TM = 1024
MM_TN = 1024

POOL_WINDOWS = (2, 4, 8, 16)
POOL_HALO = 16
POOL_PAD = 8

SSD_HEAD_DIM = 64
SSD_STATE = 128
SSD_GROUPS = 8
SSD_CONV = 4
SSD_CHUNK = 128
CONV_HALO = 8

FOX_HEAD_DIM = 128
FOX_TQ = 2048
FOX_TK = 1024
FOX_SUB = 256

MOE_GROUPS = 4
MOE_EPG = 8
MOE_EXPERTS = MOE_GROUPS * MOE_EPG
MOE_ROWS = 256
MOE_WSLOTS = 3
MOE_TOK_TILE = 256
ROUTE_TILE = 512


def _cparams(n_axes, vmem=VMEM_LIMIT):
    return pltpu.CompilerParams(dimension_semantics=("arbitrary",) * n_axes, vmem_limit_bytes=vmem)


def _rms(x, g):
    ms = jnp.mean(x * x, axis=-1, keepdims=True)
    return x * lax.rsqrt(ms + RMS_EPS) * g


def _split3(x):
    hi = x.astype(BF16)
    r = x - hi.astype(F32)
    mid = r.astype(BF16)
    lo = (r - mid.astype(F32)).astype(BF16)
    return hi, mid, lo


def _dot(a, b):
    return jnp.dot(a, b, preferred_element_type=F32)


def _dot_nt(a, b):
    return lax.dot_general(a, b, (((1,), (1,)), ((), ())), preferred_element_type=F32)


def _dot3_rhs_exact(x, m):
    hi, mid, lo = _split3(x)
    return _dot(hi, m) + _dot(mid, m) + _dot(lo, m)


def _dot3_lhs_exact(m, x):
    hi, mid, lo = _split3(x)
    return _dot(m, hi) + _dot(m, mid) + _dot(m, lo)


def _dot2_rhs_exact(x, m):
    hi = x.astype(BF16)
    lo = (x - hi.astype(F32)).astype(BF16)
    return _dot(hi, m) + _dot(lo, m)


def _silu(x):
    h = 0.5 * x
    return h + h * jnp.tanh(h)


def _softplus(x):
    return jnp.maximum(x, 0.0) + jnp.log1p(jnp.exp(-jnp.abs(x)))


def _pack_rows(x):
    half = x.shape[1] // 2
    lo = lax.bitcast_convert_type(x[:, :half].astype(BF16).astype(F32), U32)
    hi = lax.bitcast_convert_type(x[:, half:].astype(BF16).astype(F32), U32)
    return hi | (lo >> 16)


def _unpack_rows(u):
    lo = lax.bitcast_convert_type(u << 16, F32)
    hi = lax.bitcast_convert_type(u & jnp.uint32(0xFFFF0000), F32)
    return lo, hi


def _load_token_slabs(ref, rows):
    return [ref[pl.ds(c, rows, stride=SLAB), :] for c in range(SLAB)]


def _store_token_slabs(ref, packed):
    rows = packed.shape[0]
    for c in range(SLAB):
        ref[pl.ds(c, rows, stride=SLAB), :] = packed[:, c * LANES:(c + 1) * LANES]


def _slab(ref, token):
    return ref.at[pl.ds(pl.multiple_of(token * SLAB, SLAB), SLAB)]


def _slab_run(ref, token, count):
    return ref.at[pl.ds(pl.multiple_of(token * SLAB, SLAB), count * SLAB)]


def _mm_kernel(*refs, has_res, has_scale, transposed):
    x_ref, w_ref = refs[:2]
    rest = list(refs[2:])
    r_ref = rest.pop(0) if has_res else None
    s_ref = rest.pop(0) if has_scale else None
    o_ref, wbf = rest

    @pl.when(pl.program_id(1) == 0)
    def _():
        if transposed and wbf.shape[0] != w_ref.shape[0]:
            wbf[...] = jnp.zeros_like(wbf)
            wbf[0:w_ref.shape[0], :] = w_ref[...].astype(BF16)
        else:
            wbf[...] = w_ref[...].astype(BF16)

    acc = _dot_nt(x_ref[...], wbf[...]) if transposed else _dot(x_ref[...], wbf[...])
    if s_ref is not None:
        acc = acc * s_ref[...]
    if r_ref is not None:
        acc = acc + r_ref[...]
    o_ref[...] = acc.astype(o_ref.dtype)


def _matmul(x, w, *, layer=0, col0=0, ncols=None, tn=512, tm=512, residual=None, colscale=None,
            out_dtype=F32, name="mm"):
    m, k = x.shape
    ncols = w.shape[-1] - col0 if ncols is None else ncols
    assert ncols % tn == 0 and col0 % tn == 0 and m % tm == 0
    jb = col0 // tn
    if w.ndim == 3:
        w_spec = pl.BlockSpec((None, k, tn), lambda j, i: (layer, 0, jb + j))
    else:
        w_spec = pl.BlockSpec((k, tn), lambda j, i: (0, jb + j))
    in_specs = [pl.BlockSpec((tm, k), lambda j, i: (i, 0)), w_spec]
    args = [x, w]
    if residual is not None:
        in_specs.append(pl.BlockSpec((tm, tn), lambda j, i: (i, j)))
        args.append(residual)
    if colscale is not None:
        in_specs.append(pl.BlockSpec((1, tn), lambda j, i: (0, j)))
        args.append(colscale)
    return pl.pallas_call(
        functools.partial(_mm_kernel, has_res=residual is not None, has_scale=colscale is not None,
                          transposed=False),
        out_shape=jax.ShapeDtypeStruct((m, ncols), out_dtype),
        grid=(ncols // tn, m // tm),
        in_specs=in_specs,
        out_specs=pl.BlockSpec((tm, tn), lambda j, i: (i, j)),
        scratch_shapes=[pltpu.VMEM((k, tn), BF16)],
        compiler_params=_cparams(2),
        name=name,
    )(*args)


def _matmul_t(x, wt, *, layer, row0=0, nrows=None, tn=512, tm=512, colscale=None, out_dtype=F32, name="mm_t"):
    m, k = x.shape
    nrows = wt.shape[1] - row0 if nrows is None else nrows
    assert nrows % tn == 0 and row0 % tn == 0 and m % tm == 0 and tn % 8 == 0
    assert tn % LANES == 0 or nrows == tn
    tn_out = max(tn, LANES)
    jb = row0 // tn
    in_specs = [pl.BlockSpec((tm, k), lambda j, i: (i, 0)),
                pl.BlockSpec((None, tn, k), lambda j, i: (layer, jb + j, 0))]
    args = [x, wt]
    if colscale is not None:
        in_specs.append(pl.BlockSpec((1, tn_out), lambda j, i: (0, j)))
        args.append(colscale)
    return pl.pallas_call(
        functools.partial(_mm_kernel, has_res=False, has_scale=colscale is not None, transposed=True),
        out_shape=jax.ShapeDtypeStruct((m, (nrows // tn) * tn_out), out_dtype),
        grid=(nrows // tn, m // tm),
        in_specs=in_specs,
        out_specs=pl.BlockSpec((tm, tn_out), lambda j, i: (i, j)),
        scratch_shapes=[pltpu.VMEM((tn_out, k), BF16)],
        compiler_params=_cparams(2),
        name=name,
    )(*args)


def _pool_kernel(h_ref, g_ref, w_ref, scale_ref, o_ref, xbuf, sa, sb, wbf, *, tm, pg):
    i = pl.program_id(0)
    d = h_ref.shape[1]
    lo = POOL_PAD
    cur0 = POOL_PAD + POOL_HALO
    end = cur0 + tm

    @pl.when(i == 0)
    def _():
        xbuf[0:cur0, :] = jnp.zeros((cur0, d), F32)
        sa[0:lo, :] = jnp.zeros((lo, pg), F32)
        sb[0:lo, :] = jnp.zeros((lo, pg), F32)
        wbf[...] = w_ref[...].astype(BF16)

    h = h_ref[...]
    hn = _rms(h, g_ref[...])
    xbuf[cur0:end, :] = hn
    row = i * tm + lax.broadcasted_iota(I32, (tm, 1), 0)
    for gi, w in enumerate(POOL_WINDOWS):
        c0 = gi * pg
        cols = slice(c0, c0 + pg)
        src, shift, spare = None, 1, [sa, sb]
        while True:
            first = lo if 2 * shift < w else cur0
            if src is None:
                summed = xbuf[first:end, cols] + xbuf[first - shift:end - shift, cols]
            else:
                summed = src[first:end, :] + src[first - shift:end - shift, :]
            shift *= 2
            if shift == w:
                break
            dst = spare[0] if src is not spare[0] else spare[1]
            dst[first:end, :] = summed
            src = dst
        cur = hn[:, cols]
        cnt = jnp.minimum(row + 1, w).astype(F32)
        pooled = summed / cnt - cur
        mix = _dot(pooled.astype(BF16), wbf[gi]) * scale_ref[:, cols]
        o_ref[:, cols] = h[:, cols] + mix
    xbuf[lo:cur0, :] = xbuf[end - POOL_HALO:end, :]


def _pool_layer(h, g, pool_w, pool_scale, j, tm=256):
    n, d = h.shape
    nw, pg, _ = pool_w.shape[1:]
    return pl.pallas_call(
        functools.partial(_pool_kernel, tm=tm, pg=pg),
        out_shape=jax.ShapeDtypeStruct((n, d), F32),
        grid=(n // tm,),
        in_specs=[pl.BlockSpec((tm, d), lambda i: (i, 0)),
                  pl.BlockSpec((1, d), lambda i: (0, 0)),
                  pl.BlockSpec((None, nw, pg, pg), lambda i: (j, 0, 0, 0)),
                  pl.BlockSpec((1, d), lambda i: (0, 0))],
        out_specs=pl.BlockSpec((tm, d), lambda i: (i, 0)),
        scratch_shapes=[pltpu.VMEM((POOL_PAD + POOL_HALO + tm, d), F32),
                        pltpu.VMEM((POOL_PAD + POOL_HALO + tm, pg), F32),
                        pltpu.VMEM((POOL_PAD + POOL_HALO + tm, pg), F32),
                        pltpu.VMEM((nw, pg, pg), BF16)],
        compiler_params=_cparams(1),
        name="pool_layer",
    )(h, g.reshape(1, d), pool_w, pool_scale[j].reshape(1, d))


def _route_kernel(h_ref, g_ref, whl_ref, b_ref, tri_ref, hn_ref, ri_ref, rg_ref, cnt_ref, run_ref, *, tm):
    i = pl.program_id(0)

    @pl.when(i == 0)
    def _():
        run_ref[...] = jnp.zeros_like(run_ref)

    hn = _rms(h_ref[...], g_ref[...])
    hn_bf = hn.astype(BF16)
    _store_token_slabs(hn_ref, _pack_rows(hn))
    hn_lo = (hn - hn_bf.astype(F32)).astype(BF16)
    both = _dot(hn_bf, whl_ref[...])
    logits = both[:, 0:LANES] + both[:, LANES:2 * LANES] + _dot(hn_lo, whl_ref[:, 0:LANES]) + b_ref[...]

    lane = lax.broadcasted_iota(I32, (tm, LANES), 1)
    lane_f = lane.astype(F32)
    neg = -jnp.inf
    big = float(LANES)
    gl = jnp.where(lane < MOE_GROUPS, logits, neg)
    gmax = jnp.max(gl, axis=-1, keepdims=True)
    gsel = jnp.min(jnp.where(gl == gmax, lane_f, big), axis=-1, keepdims=True).astype(I32)
    p_grp = 1.0 / jnp.sum(jnp.exp(gl - gmax), axis=-1, keepdims=True)
    lo_lane = MOE_GROUPS + MOE_EPG * gsel
    el = jnp.where((lane >= lo_lane) & (lane < lo_lane + MOE_EPG), logits, neg)
    v0 = jnp.max(el, axis=-1, keepdims=True)
    i0 = jnp.min(jnp.where(el == v0, lane_f, big), axis=-1, keepdims=True).astype(I32)
    el2 = jnp.where(lane == i0, neg, el)
    v1 = jnp.max(el2, axis=-1, keepdims=True)
    i1 = jnp.min(jnp.where(el2 == v1, lane_f, big), axis=-1, keepdims=True).astype(I32)
    t = jnp.exp(v1 - v0)
    gate0 = p_grp / (1.0 + t)
    gate1 = p_grp * t / (1.0 + t)
    e0 = i0 - MOE_GROUPS
    e1 = i1 - MOE_GROUPS

    onehot = ((lane == e0) | (lane == e1))
    c_bf = jnp.where(onehot, 1.0, 0.0).astype(BF16)
    prefix = _dot(tri_ref[...], c_bf) + run_ref[...]
    rank0 = jnp.sum(jnp.where(lane == e0, prefix, 0.0), axis=-1, keepdims=True).astype(I32)
    rank1 = jnp.sum(jnp.where(lane == e1, prefix, 0.0), axis=-1, keepdims=True).astype(I32)
    run = run_ref[...] + jnp.sum(jnp.where(onehot, 1.0, 0.0), axis=0, keepdims=True)
    run_ref[...] = run
    cnt_ref[...] = jnp.broadcast_to(run, cnt_ref.shape)

    ri = jnp.where(lane == 0, e0, jnp.where(lane == 1, e1, jnp.where(lane == 2, rank0, rank1)))
    ri_ref[...] = ri.T[0:8, :]
    rg_ref[...] = jnp.where(lane == 0, gate0, gate1)


def _route(h, g, wr, br, tm=ROUTE_TILE):
    n, d = h.shape
    whi = wr.astype(BF16)
    wlo = (wr - whi.astype(F32)).astype(BF16)
    whl = jnp.concatenate([whi, wlo], axis=1)
    tri = jnp.tril(jnp.ones((tm, tm), BF16), -1)
    return pl.pallas_call(
        functools.partial(_route_kernel, tm=tm),
        out_shape=(jax.ShapeDtypeStruct((n * SLAB, LANES), U32),
                   jax.ShapeDtypeStruct((8, n), I32),
                   jax.ShapeDtypeStruct((n, LANES), F32),
                   jax.ShapeDtypeStruct((8, LANES), F32)),
        grid=(n // tm,),
        in_specs=[pl.BlockSpec((tm, d), lambda i: (i, 0)),
                  pl.BlockSpec((1, d), lambda i: (0, 0)),
                  pl.BlockSpec((d, 2 * LANES), lambda i: (0, 0)),
                  pl.BlockSpec((1, LANES), lambda i: (0, 0)),
                  pl.BlockSpec((tm, tm), lambda i: (0, 0))],
        out_specs=(pl.BlockSpec((tm * SLAB, LANES), lambda i: (i, 0)),
                   pl.BlockSpec((8, tm), lambda i: (0, i)),
                   pl.BlockSpec((tm, LANES), lambda i: (i, 0)),
                   pl.BlockSpec((8, LANES), lambda i: (0, 0))),
        scratch_shapes=[pltpu.VMEM((1, LANES), F32)],
        compiler_params=_cparams(1),
        name="moe_route",
    )(h, g.reshape(1, d), whl, br, tri)


def _dispatch_kernel(pad_start_ref, pad_len_ref, nused_ref, dest_ref, hn_ref, xs_ref, zbuf, sem, zsem, *, t):
    sizes = [1 << s for s in reversed(range(MOE_ROWS.bit_length() - 1))]

    @pl.when(pl.program_id(0) == 0)
    def _():
        zbuf[...] = jnp.zeros_like(zbuf)

        def zero_copy(off, size):
            return pltpu.make_async_copy(zbuf.at[pl.ds(0, size * SLAB)], _slab_run(xs_ref, off, size), zsem)

        def gaps(e, wait):
            off = pad_start_ref[e]
            n = pad_len_ref[e]
            for size in sizes:
                hit = (n & size) != 0

                @pl.when(hit)
                def _():
                    c = zero_copy(off, size)
                    c.wait() if wait else c.start()

                off = off + (n & size)

        def start_gaps(e, c):
            gaps(e, False)
            return c

        def wait_gaps(e, c):
            gaps(e, True)
            return c

        lax.fori_loop(0, MOE_EXPERTS, start_gaps, 0)
        lax.fori_loop(0, MOE_EXPERTS, wait_gaps, 0)

        zrows = zbuf.shape[0] // SLAB
        n_blocks = xs_ref.shape[0] // (MOE_ROWS * SLAB)

        def tail(wait):
            def body(b, c):
                for part in range(MOE_ROWS // zrows):
                    cp = zero_copy(b * MOE_ROWS + part * zrows, zrows)
                    cp.wait() if wait else cp.start()
                return c
            lax.fori_loop(nused_ref[0], n_blocks, body, 0)

        tail(False)
        tail(True)

    def row_copy(r, k):
        return pltpu.make_async_copy(_slab(hn_ref, r), _slab(xs_ref, dest_ref[0, k * t + r]), sem)

    def start(r, c):
        row_copy(r, 0).start(priority=0)
        row_copy(r, 1).start(priority=1)
        return c

    lax.fori_loop(0, t, start, 0, unroll=8)
    for _ in range(2):
        pltpu.make_async_copy(hn_ref, xs_ref.at[pl.ds(0, t * SLAB)], sem).wait()


def _dispatch(hn, dest, pad_start, pad_len, n_used, n_rows, t=MOE_TOK_TILE):
    n = hn.shape[0] // SLAB
    grid_spec = pltpu.PrefetchScalarGridSpec(
        num_scalar_prefetch=3,
        grid=(n // t,),
        in_specs=[pl.BlockSpec((None, 1, 2 * t), lambda i, *_: (i, 0, 0), memory_space=pltpu.SMEM),
                  pl.BlockSpec((t * SLAB, LANES), lambda i, *_: (i, 0))],
        out_specs=pl.BlockSpec(memory_space=pl.ANY),
        scratch_shapes=[pltpu.VMEM((MOE_ROWS // 2 * SLAB, LANES), hn.dtype),
                        pltpu.SemaphoreType.DMA(()), pltpu.SemaphoreType.DMA(())],
    )
    return pl.pallas_call(
        functools.partial(_dispatch_kernel, t=t),
        out_shape=jax.ShapeDtypeStruct((n_rows * SLAB, LANES), hn.dtype),
        grid_spec=grid_spec,
        compiler_params=_cparams(1),
        name="moe_dispatch",
    )(pad_start, pad_len, n_used, dest, hn)


def _expert_kernel(bexp_ref, first_ref, slot_ref, ahead_ref, second_ref, nused_ref, xs_ref, wg_hbm, wu_hbm, wd_hbm,
                   o_ref, wg_raw, wu_raw, wd_raw, wg_bf, wu_bf, wd_bf, sems, *, layer):
    b = pl.program_id(0)
    active = b < nused_ref[0]

    def fetch(e, slot):
        return [pltpu.make_async_copy(hbm.at[layer, e], raw.at[slot], sems.at[slot, k])
                for k, (hbm, raw) in enumerate(((wg_hbm, wg_raw), (wu_hbm, wu_raw), (wd_hbm, wd_raw)))]

    @pl.when(b == 0)
    def _():
        for c in fetch(bexp_ref[0], 0):
            c.start()

        @pl.when(second_ref[0] >= 0)
        def _():
            for c in fetch(second_ref[0], 1):
                c.start()

    @pl.when(active & (first_ref[b] == 1))
    def _():
        slot = slot_ref[b]
        ahead = ahead_ref[b]

        @pl.when(ahead >= 0)
        def _():
            for c in fetch(ahead, (slot + MOE_WSLOTS - 1) % MOE_WSLOTS):
                c.start()

        for c in fetch(bexp_ref[b], slot):
            c.wait()
        wg_bf[...] = wg_raw[slot].astype(BF16)
        wu_bf[...] = wu_raw[slot].astype(BF16)
        wd_bf[...] = wd_raw[slot].astype(BF16)

    @pl.when(active)
    def _():
        rows = xs_ref.shape[0] // SLAB
        halves = [_unpack_rows(u) for u in _load_token_slabs(xs_ref, rows)]
        x = jnp.concatenate([lo.astype(BF16) for lo, _ in halves] + [hi.astype(BF16) for _, hi in halves], axis=1)
        gte = _dot(x, wg_bf[...])
        up = _dot(x, wu_bf[...])
        hb = _silu(gte) * up
        _store_token_slabs(o_ref, _pack_rows(_dot(hb.astype(BF16), wd_bf[...])))

    @pl.when(jnp.logical_not(active))
    def _():
        o_ref[...] = jnp.zeros_like(o_ref)


def _experts(xs, tables, gate_w, up_w, down_w, layer, rows=MOE_ROWS):
    n_rows = xs.shape[0] // SLAB
    d = gate_w.shape[-2]
    hid = gate_w.shape[-1]
    assert d == 2 * SLAB * LANES and MOE_WSLOTS == 3
    x_map = lambda b, be, fi, sl, ah, se, nu: (jnp.minimum(b, nu[0] - 1), 0)
    grid_spec = pltpu.PrefetchScalarGridSpec(
        num_scalar_prefetch=6,
        grid=(n_rows // rows,),
        in_specs=[pl.BlockSpec((rows * SLAB, LANES), x_map),
                  pl.BlockSpec(memory_space=pl.ANY),
                  pl.BlockSpec(memory_space=pl.ANY),
                  pl.BlockSpec(memory_space=pl.ANY)],
        out_specs=pl.BlockSpec((rows * SLAB, LANES), lambda b, *_: (b, 0)),
        scratch_shapes=[pltpu.VMEM((MOE_WSLOTS, d, hid), F32), pltpu.VMEM((MOE_WSLOTS, d, hid), F32),
                        pltpu.VMEM((MOE_WSLOTS, hid, d), F32),
                        pltpu.VMEM((d, hid), BF16), pltpu.VMEM((d, hid), BF16), pltpu.VMEM((hid, d), BF16),
                        pltpu.SemaphoreType.DMA((MOE_WSLOTS, 3))],
    )
    return pl.pallas_call(
        functools.partial(_expert_kernel, layer=layer),
        out_shape=jax.ShapeDtypeStruct((n_rows * SLAB, LANES), U32),
        grid_spec=grid_spec,
        compiler_params=_cparams(1),
        name="moe_experts",
    )(*tables, xs, gate_w, up_w, down_w)


def _combine_kernel(*refs, t, mode):
    dest_ref, dest_next_ref, h_ref, rg_ref, ys_ref = refs[:5]
    rest = list(refs[5:])
    g_ref = rest.pop(0) if mode != "plain" else None
    o_ref = rest.pop(0)
    hn_ref = rest.pop(0) if mode == "norm_bf16" else None
    buf00, buf01, buf10, buf11, sems = rest
    bufs = ((buf00, buf01), (buf10, buf11))
    i = pl.program_id(0)
    n = pl.num_programs(0)
    half = SLAB * LANES

    def row_copy(dref, slot, r, k):
        return pltpu.make_async_copy(_slab(ys_ref, dref[0, k * t + r]), _slab(bufs[slot][k], r), sems.at[slot])

    def issue(dref, slot):
        def start(r, c):
            row_copy(dref, slot, r, 0).start(priority=0)
            row_copy(dref, slot, r, 1).start(priority=1)
            return c
        lax.fori_loop(0, t, start, 0, unroll=8)

    def drain(slot):
        for k in range(2):
            pltpu.make_async_copy(ys_ref.at[pl.ds(0, t * SLAB)], bufs[slot][k], sems.at[slot]).wait()

    @pl.when(i == 0)
    def _():
        issue(dest_ref, 0)

    for slot in (0, 1):
        @pl.when((i % 2 == slot) & (i + 1 < n))
        def _():
            issue(dest_next_ref, 1 - slot)

        @pl.when(i % 2 == slot)
        def _():
            drain(slot)
            rg = rg_ref[...]
            g0 = rg[:, 0:1]
            g1 = rg[:, 1:2]
            for c, (ua, ub) in enumerate(zip(_load_token_slabs(bufs[slot][0], t), _load_token_slabs(bufs[slot][1], t))):
                a_lo, a_hi = _unpack_rows(ua)
                b_lo, b_hi = _unpack_rows(ub)
                lo_cols = slice(c * LANES, (c + 1) * LANES)
                hi_cols = slice(half + c * LANES, half + (c + 1) * LANES)
                o_ref[:, lo_cols] = h_ref[:, lo_cols] + (g0 * a_lo + g1 * b_lo)
                o_ref[:, hi_cols] = h_ref[:, hi_cols] + (g0 * a_hi + g1 * b_hi)

    if mode != "plain":
        h_new = o_ref[...]
        normed = _rms(h_new, g_ref[...])
        if mode == "norm_bf16":
            hn_ref[...] = normed.astype(BF16)
        else:
            o_ref[...] = normed


def _combine(h, rg, ys, dest, g_next, mode, t=MOE_TOK_TILE):
    n, d = h.shape
    assert d == 2 * SLAB * LANES
    last = n // t - 1
    in_specs = [pl.BlockSpec((None, 1, 2 * t), lambda i: (i, 0, 0), memory_space=pltpu.SMEM),
                pl.BlockSpec((None, 1, 2 * t), lambda i: (jnp.minimum(i + 1, last), 0, 0),
                             memory_space=pltpu.SMEM),
                pl.BlockSpec((t, d), lambda i: (i, 0)),
                pl.BlockSpec((t, LANES), lambda i: (i, 0)),
                pl.BlockSpec(memory_space=pl.ANY)]
    args = [dest, dest, h, rg, ys]
    tile = pl.BlockSpec((t, d), lambda i: (i, 0))
    if mode != "plain":
        in_specs.append(pl.BlockSpec((1, d), lambda i: (0, 0)))
        args.append(g_next.reshape(1, d))
    if mode == "norm_bf16":
        out_shape = (jax.ShapeDtypeStruct((n, d), F32), jax.ShapeDtypeStruct((n, d), BF16))
        out_specs = (tile, tile)
    else:
        out_shape = jax.ShapeDtypeStruct((n, d), F32)
        out_specs = tile
    return pl.pallas_call(
        functools.partial(_combine_kernel, t=t, mode=mode),
        out_shape=out_shape,
        grid=(n // t,),
        in_specs=in_specs,
        out_specs=out_specs,
        scratch_shapes=[pltpu.VMEM((t * SLAB, LANES), U32)] * 4 + [pltpu.SemaphoreType.DMA((2,))],
        compiler_params=_cparams(1),
        name="moe_combine",
    )(*args)


def _moe_layer(h, layer, norm_ffn, group_w, group_b, router_w, router_b, gate_w, up_w, down_w, g_next, mode):
    n, d = h.shape
    pad = LANES - MOE_GROUPS - MOE_EXPERTS
    wr = jnp.concatenate([group_w[layer], router_w[layer], jnp.zeros((d, pad), F32)], axis=1)
    br = jnp.concatenate([group_b[layer], router_b[layer], jnp.zeros((pad,), F32)]).reshape(1, LANES)
    hn, ri, rg, cnt = _route(h, norm_ffn[layer], wr, br)

    t = MOE_TOK_TILE
    counts = cnt[0, :MOE_EXPERTS].astype(I32)
    nblk = (counts + MOE_ROWS - 1) // MOE_ROWS
    blk_end = jnp.cumsum(nblk)
    row_start = (blk_end - nblk) * MOE_ROWS
    n_rows = -(-(n * 2 + MOE_EXPERTS * (MOE_ROWS - 1)) // MOE_ROWS) * MOE_ROWS
    n_blocks = n_rows // MOE_ROWS
    experts = ri[0:2, :]
    ranks = ri[2:4, :]
    onehot = experts[:, :, None] == jnp.arange(MOE_EXPERTS, dtype=I32)
    dest = jnp.sum(jnp.where(onehot, row_start, 0), axis=-1) + ranks
    dest = dest.reshape(2, n // t, t).transpose(1, 0, 2).reshape(n // t, 1, 2 * t).astype(I32)
    blocks = jnp.arange(n_blocks, dtype=I32)
    block_exp = jnp.minimum(jnp.sum(blk_end[None, :] <= blocks[:, None], axis=1), MOE_EXPERTS - 1).astype(I32)
    n_used = blk_end[-1:].astype(I32)
    eids = jnp.arange(MOE_EXPERTS, dtype=I32)
    nonempty = nblk > 0
    ordinal = jnp.cumsum(nonempty.astype(I32)) - 1
    holds = nonempty[None, :] & (ordinal[None, :] == eids[:, None])
    expert_at = jnp.where(jnp.any(holds, axis=1), jnp.sum(jnp.where(holds, eids[None, :], 0), axis=1), -1)
    expert_at = jnp.concatenate([expert_at, jnp.full((MOE_WSLOTS,), -1, I32)]).astype(I32)
    is_first = ((blocks == (blk_end - nblk)[block_exp]) & (blocks < n_used[0])).astype(I32)
    pos = ordinal[block_exp]
    tables = (block_exp, is_first, (pos % MOE_WSLOTS).astype(I32), expert_at[pos + MOE_WSLOTS - 1],
              expert_at[1:2], n_used)
    pad_start = (row_start + counts).astype(I32)
    pad_len = (nblk * MOE_ROWS - counts).astype(I32)

    xs = _dispatch(hn, dest, pad_start, pad_len, n_used, n_rows)
    ys = _experts(xs, tables, gate_w, up_w, down_w, layer)
    return _combine(h, rg, ys, dest, g_next, mode)


def _ssd_kernel(z_ref, x_ref, b_ref, c_ref, dt_ref, cwx_ref, cwb_ref, cwc_ref, cbx_ref, cbb_ref, cbc_ref,
                dtb_ref, a_ref, dexp_ref, nw_ref, exp_ref, tri_ref, o_ref,
                xbuf, bbuf, cbuf, xc_s, xdt_s, xw_s, b_s, c_s, eacs_s, acs_s, acst_s, y_s, st_s,
                *, L, hp, n_state, groups):
    ci = pl.program_id(0)
    d_inner = x_ref.shape[1]
    gw = d_inner // groups
    kh = gw // hp
    H0 = CONV_HALO

    @pl.when(ci == 0)
    def _():
        xbuf[0:H0, :] = jnp.zeros((H0, xbuf.shape[1]), F32)
        bbuf[0:H0, :] = jnp.zeros((H0, bbuf.shape[1]), F32)
        cbuf[0:H0, :] = jnp.zeros((H0, cbuf.shape[1]), F32)
        st_s[...] = jnp.zeros_like(st_s)

    def conv_silu(in_ref, buf, cw_ref, cb_ref):
        buf[H0:H0 + L, :] = in_ref[...].astype(F32)
        acc = cb_ref[...] + cw_ref[SSD_CONV - 1:SSD_CONV, :] * buf[H0:H0 + L, :]
        for j in range(SSD_CONV - 1):
            off = H0 - (SSD_CONV - 1) + j
            acc = acc + cw_ref[j:j + 1, :] * buf[off:off + L, :]
        buf[0:H0, :] = buf[L:L + H0, :]
        return _silu(acc)

    xc_s[...] = conv_silu(x_ref, xbuf, cwx_ref, cbx_ref)
    b_s[...] = conv_silu(b_ref, bbuf, cwb_ref, cbb_ref).astype(BF16)
    c_s[...] = conv_silu(c_ref, cbuf, cwc_ref, cbc_ref).astype(BF16)

    dt = _softplus(dt_ref[...] + dtb_ref[...])
    d_a = dt * a_ref[...]
    acs = _dot3_lhs_exact(tri_ref[...], d_a)
    acs_s[...] = acs
    acst_s[...] = acs.T
    emat = exp_ref[...]
    acs_last = acs[L - 1:L, :]
    to_end = dt * jnp.exp(acs_last - acs)
    xc = xc_s[...]
    xdt_s[...] = (xc * _dot2_rhs_exact(dt, emat)).astype(BF16)
    xw_s[...] = (xc * _dot2_rhs_exact(to_end, emat)).astype(BF16)
    eacs_s[...] = _dot2_rhs_exact(jnp.exp(acs), emat)
    cdecay = _dot2_rhs_exact(jnp.broadcast_to(jnp.exp(acs_last), (SLAB, LANES)), emat)[0:1, :]

    rr = lax.broadcasted_iota(I32, (L, L), 0)
    cc = lax.broadcasted_iota(I32, (L, L), 1)
    causal = rr >= cc
    lane2 = lax.broadcasted_iota(I32, (L, 2 * hp), 1)

    for g in range(groups):
        bg = b_s[:, g * n_state:(g + 1) * n_state]
        cg = c_s[:, g * n_state:(g + 1) * n_state]
        cb = _dot_nt(cg, bg)
        st_prev = st_s[g]
        y_off = _dot(cg, st_prev.astype(BF16)) * eacs_s[:, g * gw:(g + 1) * gw]
        for kp in range(kh // 2):
            c0 = g * gw + kp * 2 * hp
            slab = xdt_s[:, c0:c0 + 2 * hp]
            ms, xs = [], []
            for half in range(2):
                hd = g * kh + kp * 2 + half
                seg = acs_s[:, hd:hd + 1] - acst_s[hd:hd + 1, :]
                decay = jnp.exp(jnp.where(causal, seg, -jnp.inf))
                ms.append((cb * decay).astype(BF16))
                keep = (lane2 < hp) if half == 0 else (lane2 >= hp)
                xs.append(jnp.where(keep, slab, jnp.zeros_like(slab)))
            y_pair = _dot(jnp.concatenate(ms, axis=1), jnp.concatenate(xs, axis=0))
            y_s[:, c0:c0 + 2 * hp] = y_pair + y_off[:, kp * 2 * hp:(kp + 1) * 2 * hp]
        bgt = bg.astype(F32).T.astype(BF16)
        s_new = _dot(bgt, xw_s[:, g * gw:(g + 1) * gw])
        st_s[g] = cdecay[:, g * gw:(g + 1) * gw] * st_prev + s_new

    y = y_s[...] + dexp_ref[...] * xc_s[...]
    z = z_ref[...].astype(F32)
    y = y * _silu(z)
    for g in range(groups):
        yg = y[:, g * gw:(g + 1) * gw]
        yg = yg * lax.rsqrt(jnp.mean(yg * yg, axis=-1, keepdims=True) + RMS_EPS)
        o_ref[:, g * gw:(g + 1) * gw] = (yg * nw_ref[:, g * gw:(g + 1) * gw]).astype(o_ref.dtype)


def _ssd_scan(zx, dt_raw, conv_w, conv_b, dt_bias, a_log, d_skip, norm_w, d_inner):
    s = zx.shape[0]
    L = SSD_CHUNK
    hp, n_state, groups = SSD_HEAD_DIM, SSD_STATE, SSD_GROUPS
    heads = d_inner // hp
    gn = groups * n_state
    gw = d_inner // groups
    assert d_inner % gn == 0 and heads <= LANES
    pad = LANES - heads
    dtb = jnp.pad(dt_bias, (0, pad)).reshape(1, LANES)
    a_neg = jnp.pad(-jnp.exp(a_log), (0, pad)).reshape(1, LANES)
    dexp = jnp.repeat(d_skip, hp).reshape(1, d_inner)
    emat = (jnp.arange(LANES)[:, None] == (jnp.arange(d_inner)[None, :] // hp)).astype(BF16)
    tri = jnp.tril(jnp.ones((L, L), BF16))
    cb2 = conv_b.reshape(1, -1)
    xb = d_inner // gn
    row = lambda c: (c, 0)
    const = lambda c: (0, 0)
    in_specs = [
        pl.BlockSpec((L, d_inner), row),
        pl.BlockSpec((L, d_inner), lambda c: (c, 1)),
        pl.BlockSpec((L, gn), lambda c: (c, 2 * xb)),
        pl.BlockSpec((L, gn), lambda c: (c, 2 * xb + 1)),
        pl.BlockSpec((L, LANES), row),
        pl.BlockSpec((SSD_CONV, d_inner), const),
        pl.BlockSpec((SSD_CONV, gn), lambda c: (0, xb)),
        pl.BlockSpec((SSD_CONV, gn), lambda c: (0, xb + 1)),
        pl.BlockSpec((1, d_inner), const),
        pl.BlockSpec((1, gn), lambda c: (0, xb)),
        pl.BlockSpec((1, gn), lambda c: (0, xb + 1)),
        pl.BlockSpec((1, LANES), const),
        pl.BlockSpec((1, LANES), const),
        pl.BlockSpec((1, d_inner), const),
        pl.BlockSpec((1, d_inner), const),
        pl.BlockSpec((LANES, d_inner), const),
        pl.BlockSpec((L, L), const),
    ]
    scratch = [
        pltpu.VMEM((CONV_HALO + L, d_inner), F32), pltpu.VMEM((CONV_HALO + L, gn), F32),
        pltpu.VMEM((CONV_HALO + L, gn), F32),
        pltpu.VMEM((L, d_inner), F32),
        pltpu.VMEM((L, d_inner), BF16),
        pltpu.VMEM((L, d_inner), BF16),
        pltpu.VMEM((L, gn), BF16), pltpu.VMEM((L, gn), BF16),
        pltpu.VMEM((L, d_inner), F32),
        pltpu.VMEM((L, LANES), F32), pltpu.VMEM((LANES, L), F32),
        pltpu.VMEM((L, d_inner), F32),
        pltpu.VMEM((groups, n_state, gw), F32),
    ]
    return pl.pallas_call(
        functools.partial(_ssd_kernel, L=L, hp=hp, n_state=n_state, groups=groups),
        out_shape=jax.ShapeDtypeStruct((s, d_inner), BF16),
        grid=(s // L,),
        in_specs=in_specs,
        out_specs=pl.BlockSpec((L, d_inner), row),
        scratch_shapes=scratch,
        compiler_params=_cparams(1),
        name="ssd_scan",
    )(zx, zx, zx, zx, dt_raw, conv_w, conv_w, conv_w, cb2, cb2, cb2, dtb, a_neg, dexp,
      norm_w.reshape(1, d_inner), emat, tri)


def _ssd_mixer(h, hn, j, in_w, conv_w, conv_b, dt_bias, a_log, d_skip, norm_w, out_w):
    d_inner = out_w.shape[1]
    heads = d_inner // SSD_HEAD_DIM
    wide = in_w.shape[2] - heads
    wt = jnp.swapaxes(in_w, 1, 2)
    zx = _matmul_t(hn, wt, layer=j, row0=0, nrows=wide, tn=MM_TN, tm=MM_TM, out_dtype=BF16, name="ssd_in_proj")
    dt_raw = _matmul_t(hn, wt, layer=j, row0=wide, nrows=heads, tn=heads, tm=MM_TM, name="ssd_dt_proj")
    y = _ssd_scan(zx, dt_raw, conv_w, conv_b, dt_bias, a_log, d_skip, norm_w, d_inner)
    return _matmul(y, out_w, layer=j, tn=MM_TN // 2, tm=MM_TM, residual=h, name="ssd_out_proj")


def _fox_cum_kernel(f_ref, b_ref, tri_ref, o_ref, carry):
    @pl.when(pl.program_id(0) == 0)
    def _():
        carry[...] = jnp.zeros_like(carry)

    x = f_ref[...] + b_ref[...]
    log_f = jnp.minimum(x, 0.0) - jnp.log1p(jnp.exp(-jnp.abs(x)))
    cs = _dot3_rhs_exact(log_f, tri_ref[...]) + carry[...]
    o_ref[...] = cs
    carry[...] = cs[:, cs.shape[1] - 1:]


def _fox_cum(f_t, bias, tk=512):
    nh, s = f_t.shape
    triu = jnp.triu(jnp.ones((tk, tk), BF16))
    return pl.pallas_call(
        _fox_cum_kernel,
        out_shape=jax.ShapeDtypeStruct((nh, s), F32),
        grid=(s // tk,),
        in_specs=[pl.BlockSpec((nh, tk), lambda i: (0, i)),
                  pl.BlockSpec((nh, 1), lambda i: (0, 0)),
                  pl.BlockSpec((tk, tk), lambda i: (0, 0))],
        out_specs=pl.BlockSpec((nh, tk), lambda i: (0, i)),
        scratch_shapes=[pltpu.VMEM((nh, 1), F32)],
        compiler_params=_cparams(1),
        name="fox_cum",
    )(f_t, bias.reshape(nh, 1), triu)


def _fox_attn_kernel(q_ref, k_ref, v_ref, cum_ref, o_ref, m_s, acc_s, vext, *, tq, tk, sr):
    qi = pl.program_id(1)
    dh = q_ref.shape[1]
    n_sub = tq // sr

    @pl.when(qi == 0)
    def _():
        vext[:, 0:dh] = v_ref[...]
        vext[:, dh:2 * dh] = jnp.ones((vext.shape[0], dh), BF16)

    q0 = pl.multiple_of(qi * tq, tq)
    c_ref = cum_ref[:, pl.ds(q0, LANES)][:, 0:1]
    m_s[...] = jnp.full_like(m_s, -jnp.inf)
    acc_s[...] = jnp.zeros_like(acc_s)

    def sub_step(r, k0, width, bias, diagonal):
        rows = slice(r * sr, (r + 1) * sr)
        s = _dot_nt(q_ref[rows, :], k_ref[pl.ds(k0, width), :]) + bias
        if diagonal:
            keep = lax.broadcasted_iota(I32, (sr, width), 0) >= lax.broadcasted_iota(I32, (sr, width), 1)
            s = jnp.where(keep, s, -jnp.inf)
        m_prev = m_s[rows, :]
        m_new = jnp.maximum(m_prev, jnp.max(s, axis=-1, keepdims=True))
        alpha = jnp.exp2(m_prev - m_new)
        p = jnp.exp2(s - jnp.concatenate([m_new] * (width // dh), axis=1)).astype(BF16)
        pv = _dot(p, vext[pl.ds(k0, width), :])
        acc_s[rows, :] = jnp.concatenate([alpha, alpha], axis=1) * acc_s[rows, :] + pv
        m_s[rows, :] = m_new

    def key_bias(k0, width):
        return (c_ref - cum_ref[:, pl.ds(k0, width)]) * LOG2E

    def body(j, c):
        k0 = pl.multiple_of(j * tk, tk)
        bias = key_bias(k0, tk)
        for r in range(n_sub):
            sub_step(r, k0, tk, bias, False)
        return c

    lax.fori_loop(0, qi * (tq // tk), body, 0)
    for r in range(n_sub):
        if r > 0:
            sub_step(r, q0, r * sr, key_bias(q0, r * sr), False)
        k0 = pl.multiple_of(q0 + r * sr, sr)
        sub_step(r, k0, sr, key_bias(k0, sr), True)
    acc = acc_s[...]
    o_ref[...] = (acc[:, 0:dh] / acc[:, dh:2 * dh]).astype(o_ref.dtype)


def _fox_attention(qkv, cum, nh, tq=FOX_TQ, tk=FOX_TK, sr=FOX_SUB):
    s = qkv.shape[0]
    dh = FOX_HEAD_DIM
    assert tq % tk == 0 and tq % sr == 0 and tk % dh == 0 and dh == LANES
    cum3 = cum.reshape(nh, 1, s)
    return pl.pallas_call(
        functools.partial(_fox_attn_kernel, tq=tq, tk=tk, sr=sr),
        out_shape=jax.ShapeDtypeStruct((s, nh * dh), BF16),
        grid=(nh, s // tq),
        in_specs=[pl.BlockSpec((tq, dh), lambda h, i: (i, h)),
                  pl.BlockSpec((s, dh), lambda h, i: (0, nh + h)),
                  pl.BlockSpec((s, dh), lambda h, i: (0, 2 * nh + h)),
                  pl.BlockSpec((None, 1, s), lambda h, i: (h, 0, 0))],
        out_specs=pl.BlockSpec((tq, dh), lambda h, i: (i, h)),
        scratch_shapes=[pltpu.VMEM((tq, dh), F32), pltpu.VMEM((tq, 2 * dh), F32), pltpu.VMEM((s, 2 * dh), BF16)],
        compiler_params=_cparams(2),
        name="fox_attention",
    )(qkv, qkv, qkv, cum3)


def _fox_mixer(h, hn, j, in_w, f_bias, out_w):
    d = hn.shape[1]
    nh = d // FOX_HEAD_DIM
    q_scale = FOX_HEAD_DIM ** -0.5 * LOG2E
    colscale = jnp.concatenate([jnp.full((1, d), q_scale, F32), jnp.ones((1, 2 * d), F32)], axis=1)
    wt = jnp.swapaxes(in_w, 1, 2)
    qkv = _matmul_t(hn, wt, layer=j, row0=0, nrows=3 * d, tn=MM_TN, tm=MM_TM, out_dtype=BF16, colscale=colscale,
                    name="fox_in_proj")
    f_raw = _matmul_t(hn, wt, layer=j, row0=3 * d, nrows=nh, tn=nh, tm=MM_TM, name="fox_f_proj")
    cum = _fox_cum(f_raw[:, :nh].T, f_bias)
    o = _fox_attention(qkv, cum, nh)
    return _matmul(o, out_w, layer=j, tn=MM_TN, tm=MM_TM, residual=h, name="fox_out_proj")


def _norm_kernel(h_ref, g_ref, o_ref):
    o_ref[...] = _rms(h_ref[...], g_ref[...]).astype(o_ref.dtype)


def _norm(h, g, out_dtype, tm=512):
    n, d = h.shape
    return pl.pallas_call(
        _norm_kernel,
        out_shape=jax.ShapeDtypeStruct((n, d), out_dtype),
        grid=(n // tm,),
        in_specs=[pl.BlockSpec((tm, d), lambda i: (i, 0)), pl.BlockSpec((1, d), lambda i: (0, 0))],
        out_specs=pl.BlockSpec((tm, d), lambda i: (i, 0)),
        compiler_params=_cparams(1),
        name="rmsnorm",
    )(h, g.reshape(1, d))


def kernel(x, norm_mix, norm_ffn, final_norm, pool_w, pool_scale, ssd_in_w, ssd_conv_w, ssd_conv_b,
           ssd_dt_bias, ssd_a_log, ssd_d_skip, ssd_norm, ssd_out_w, fox_in_w, fox_f_bias, fox_out_w,
           moe_group_w, moe_group_b, moe_router_w, moe_router_b, moe_gate_w, moe_up_w, moe_down_w):
    bsz, s, d = x.shape
    assert bsz == 1
    depth = norm_mix.shape[0]
    h = x.reshape(s, d)
    hn = None
    for i in range(depth):
        kind, j = i % 3, i // 3
        if kind == 0:
            h = _pool_layer(h, norm_mix[i], pool_w, pool_scale, j)
        else:
            if hn is None:
                hn = _norm(h, norm_mix[i], BF16)
            if kind == 1:
                h = _ssd_mixer(h, hn, j, ssd_in_w, ssd_conv_w[j], ssd_conv_b[j], ssd_dt_bias[j], ssd_a_log[j],
                               ssd_d_skip[j], ssd_norm[j], ssd_out_w)
            else:
                h = _fox_mixer(h, hn, j, fox_in_w, fox_f_bias[j], fox_out_w)
        last = i == depth - 1
        next_kind = (i + 1) % 3
        if last:
            g_next, mode = final_norm, "final"
        elif next_kind == 0:
            g_next, mode = None, "plain"
        else:
            g_next, mode = norm_mix[i + 1], "norm_bf16"
        out = _moe_layer(h, i, norm_ffn, moe_group_w, moe_group_b, moe_router_w, moe_router_b,
                         moe_gate_w, moe_up_w, moe_down_w, g_next, mode)
        if mode == "norm_bf16":
            h, hn = out
        else:
            h, hn = out, None
    return h.reshape(bsz, s, d)
```

```python
import functools

import jax
import jax.numpy as jnp
from jax import lax
from jax.experimental import pallas as pl
from jax.experimental.pallas import tpu as pltpu

F32 = jnp.float32
BF16 = jnp.bfloat16
I32 = jnp.int32
U32 = jnp.uint32

RMS_EPS = 1e-6
LOG2E = 1.4426950408889634
LANES = 128
SLAB = 8
VMEM_LIMIT = 56 * 1024 * 1024

MM_TM = 1024
MM_TN = 1024

POOL_WINDOWS = (2, 4, 8, 16)
POOL_HALO = 16
POOL_PAD = 8

SSD_HEAD_DIM = 64
SSD_STATE = 128
SSD_GROUPS = 8
SSD_CONV = 4
SSD_CHUNK = 128
CONV_HALO = 8

FOX_HEAD_DIM = 128
FOX_TQ = 2048
FOX_TK = 1024
FOX_SUB = 256

MOE_GROUPS = 4
MOE_EPG = 8
MOE_EXPERTS = MOE_GROUPS * MOE_EPG
MOE_ROWS = 256
MOE_WSLOTS = 3
MOE_TOK_TILE = 256
ROUTE_TILE = 512


def _cparams(n_axes, vmem=VMEM_LIMIT):
    return pltpu.CompilerParams(dimension_semantics=("arbitrary",) * n_axes, vmem_limit_bytes=vmem)


def _rms(x, g):
    ms = jnp.mean(x * x, axis=-1, keepdims=True)
    return x * lax.rsqrt(ms + RMS_EPS) * g


def _split3(x):
    hi = x.astype(BF16)
    r = x - hi.astype(F32)
    mid = r.astype(BF16)
    lo = (r - mid.astype(F32)).astype(BF16)
    return hi, mid, lo


def _dot(a, b):
    return jnp.dot(a, b, preferred_element_type=F32)


def _dot_nt(a, b):
    return lax.dot_general(a, b, (((1,), (1,)), ((), ())), preferred_element_type=F32)


def _dot3_rhs_exact(x, m):
    hi, mid, lo = _split3(x)
    return _dot(hi, m) + _dot(mid, m) + _dot(lo, m)


def _dot3_lhs_exact(m, x):
    hi, mid, lo = _split3(x)
    return _dot(m, hi) + _dot(m, mid) + _dot(m, lo)


def _dot2_rhs_exact(x, m):
    hi = x.astype(BF16)
    lo = (x - hi.astype(F32)).astype(BF16)
    return _dot(hi, m) + _dot(lo, m)


def _silu(x):
    h = 0.5 * x
    return h + h * jnp.tanh(h)


def _softplus(x):
    return jnp.maximum(x, 0.0) + jnp.log1p(jnp.exp(-jnp.abs(x)))


def _pack_rows(x):
    half = x.shape[1] // 2
    lo = lax.bitcast_convert_type(x[:, :half].astype(BF16).astype(F32), U32)
    hi = lax.bitcast_convert_type(x[:, half:].astype(BF16).astype(F32), U32)
    return hi | (lo >> 16)


def _unpack_rows(u):
    lo = lax.bitcast_convert_type(u << 16, F32)
    hi = lax.bitcast_convert_type(u & jnp.uint32(0xFFFF0000), F32)
    return lo, hi


def _load_token_slabs(ref, rows):
    return [ref[pl.ds(c, rows, stride=SLAB), :] for c in range(SLAB)]


def _store_token_slabs(ref, packed):
    rows = packed.shape[0]
    for c in range(SLAB):
        ref[pl.ds(c, rows, stride=SLAB), :] = packed[:, c * LANES:(c + 1) * LANES]


def _slab(ref, token):
    return ref.at[pl.ds(pl.multiple_of(token * SLAB, SLAB), SLAB)]


def _slab_run(ref, token, count):
    return ref.at[pl.ds(pl.multiple_of(token * SLAB, SLAB), count * SLAB)]


def _mm_kernel(*refs, has_res, has_scale, transposed):
    x_ref, w_ref = refs[:2]
    rest = list(refs[2:])
    r_ref = rest.pop(0) if has_res else None
    s_ref = rest.pop(0) if has_scale else None
    o_ref, wbf = rest

    @pl.when(pl.program_id(1) == 0)
    def _():
        if transposed and wbf.shape[0] != w_ref.shape[0]:
            wbf[...] = jnp.zeros_like(wbf)
            wbf[0:w_ref.shape[0], :] = w_ref[...].astype(BF16)
        else:
            wbf[...] = w_ref[...].astype(BF16)

    acc = _dot_nt(x_ref[...], wbf[...]) if transposed else _dot(x_ref[...], wbf[...])
    if s_ref is not None:
        acc = acc * s_ref[...]
    if r_ref is not None:
        acc = acc + r_ref[...]
    o_ref[...] = acc.astype(o_ref.dtype)


def _matmul(x, w, *, layer=0, col0=0, ncols=None, tn=512, tm=512, residual=None, colscale=None,
            out_dtype=F32, name="mm"):
    m, k = x.shape
    ncols = w.shape[-1] - col0 if ncols is None else ncols
    assert ncols % tn == 0 and col0 % tn == 0 and m % tm == 0
    jb = col0 // tn
    if w.ndim == 3:
        w_spec = pl.BlockSpec((None, k, tn), lambda j, i: (layer, 0, jb + j))
    else:
        w_spec = pl.BlockSpec((k, tn), lambda j, i: (0, jb + j))
    in_specs = [pl.BlockSpec((tm, k), lambda j, i: (i, 0)), w_spec]
    args = [x, w]
    if residual is not None:
        in_specs.append(pl.BlockSpec((tm, tn), lambda j, i: (i, j)))
        args.append(residual)
    if colscale is not None:
        in_specs.append(pl.BlockSpec((1, tn), lambda j, i: (0, j)))
        args.append(colscale)
    return pl.pallas_call(
        functools.partial(_mm_kernel, has_res=residual is not None, has_scale=colscale is not None,
                          transposed=False),
        out_shape=jax.ShapeDtypeStruct((m, ncols), out_dtype),
        grid=(ncols // tn, m // tm),
        in_specs=in_specs,
        out_specs=pl.BlockSpec((tm, tn), lambda j, i: (i, j)),
        scratch_shapes=[pltpu.VMEM((k, tn), BF16)],
        compiler_params=_cparams(2),
        name=name,
    )(*args)


def _matmul_t(x, wt, *, layer, row0=0, nrows=None, tn=512, tm=512, colscale=None, out_dtype=F32, name="mm_t"):
    m, k = x.shape
    nrows = wt.shape[1] - row0 if nrows is None else nrows
    assert nrows % tn == 0 and row0 % tn == 0 and m % tm == 0 and tn % 8 == 0
    assert tn % LANES == 0 or nrows == tn
    tn_out = max(tn, LANES)
    jb = row0 // tn
    in_specs = [pl.BlockSpec((tm, k), lambda j, i: (i, 0)),
                pl.BlockSpec((None, tn, k), lambda j, i: (layer, jb + j, 0))]
    args = [x, wt]
    if colscale is not None:
        in_specs.append(pl.BlockSpec((1, tn_out), lambda j, i: (0, j)))
        args.append(colscale)
    return pl.pallas_call(
        functools.partial(_mm_kernel, has_res=False, has_scale=colscale is not None, transposed=True),
        out_shape=jax.ShapeDtypeStruct((m, (nrows // tn) * tn_out), out_dtype),
        grid=(nrows // tn, m // tm),
        in_specs=in_specs,
        out_specs=pl.BlockSpec((tm, tn_out), lambda j, i: (i, j)),
        scratch_shapes=[pltpu.VMEM((tn_out, k), BF16)],
        compiler_params=_cparams(2),
        name=name,
    )(*args)


def _pool_init(w_ref, xbuf, sa, sb, wbf):
    cur0 = POOL_PAD + POOL_HALO
    xbuf[0:cur0, :] = jnp.zeros((cur0, xbuf.shape[1]), F32)
    sa[0:POOL_PAD, :] = jnp.zeros((POOL_PAD, sa.shape[1]), F32)
    sb[0:POOL_PAD, :] = jnp.zeros((POOL_PAD, sb.shape[1]), F32)
    wbf[...] = w_ref[...].astype(BF16)


def _pool_tile(h, i, g_ref, scale_ref, o_ref, xbuf, sa, sb, wbf, tm, pg):
    lo = POOL_PAD
    cur0 = POOL_PAD + POOL_HALO
    end = cur0 + tm
    hn = _rms(h, g_ref[...])
    xbuf[cur0:end, :] = hn
    row = i * tm + lax.broadcasted_iota(I32, (tm, 1), 0)
    for gi, w in enumerate(POOL_WINDOWS):
        c0 = gi * pg
        cols = slice(c0, c0 + pg)
        src, shift, spare = None, 1, [sa, sb]
        while True:
            first = lo if 2 * shift < w else cur0
            if src is None:
                summed = xbuf[first:end, cols] + xbuf[first - shift:end - shift, cols]
            else:
                summed = src[first:end, :] + src[first - shift:end - shift, :]
            shift *= 2
            if shift == w:
                break
            dst = spare[0] if src is not spare[0] else spare[1]
            dst[first:end, :] = summed
            src = dst
        cur = hn[:, cols]
        cnt = jnp.minimum(row + 1, w).astype(F32)
        pooled = summed / cnt - cur
        mix = _dot(pooled.astype(BF16), wbf[gi]) * scale_ref[:, cols]
        o_ref[:, cols] = h[:, cols] + mix
    xbuf[lo:cur0, :] = xbuf[end - POOL_HALO:end, :]


def _route_kernel(h_ref, g_ref, whl_ref, b_ref, tri_ref, hn_ref, ri_ref, rg_ref, cnt_ref, run_ref, *, tm):
    @pl.when(pl.program_id(0) == 0)
    def _():
        run_ref[...] = jnp.zeros_like(run_ref)

    _route_tile(h_ref[...], g_ref, whl_ref, b_ref, tri_ref, hn_ref, ri_ref, rg_ref, cnt_ref, run_ref, tm)


def _route_tile(h, g_ref, whl_ref, b_ref, tri_ref, hn_ref, ri_ref, rg_ref, cnt_ref, run_ref, tm):
    hn = _rms(h, g_ref[...])
    hn_bf = hn.astype(BF16)
    _store_token_slabs(hn_ref, _pack_rows(hn))
    hn_lo = (hn - hn_bf.astype(F32)).astype(BF16)
    both = _dot(hn_bf, whl_ref[...])
    logits = both[:, 0:LANES] + both[:, LANES:2 * LANES] + _dot(hn_lo, whl_ref[:, 0:LANES]) + b_ref[...]

    lane = lax.broadcasted_iota(I32, (tm, LANES), 1)
    lane_f = lane.astype(F32)
    neg = -jnp.inf
    big = float(LANES)
    gl = jnp.where(lane < MOE_GROUPS, logits, neg)
    gmax = jnp.max(gl, axis=-1, keepdims=True)
    gsel = jnp.min(jnp.where(gl == gmax, lane_f, big), axis=-1, keepdims=True).astype(I32)
    p_grp = 1.0 / jnp.sum(jnp.exp(gl - gmax), axis=-1, keepdims=True)
    lo_lane = MOE_GROUPS + MOE_EPG * gsel
    el = jnp.where((lane >= lo_lane) & (lane < lo_lane + MOE_EPG), logits, neg)
    v0 = jnp.max(el, axis=-1, keepdims=True)
    i0 = jnp.min(jnp.where(el == v0, lane_f, big), axis=-1, keepdims=True).astype(I32)
    el2 = jnp.where(lane == i0, neg, el)
    v1 = jnp.max(el2, axis=-1, keepdims=True)
    i1 = jnp.min(jnp.where(el2 == v1, lane_f, big), axis=-1, keepdims=True).astype(I32)
    t = jnp.exp(v1 - v0)
    gate0 = p_grp / (1.0 + t)
    gate1 = p_grp * t / (1.0 + t)
    e0 = i0 - MOE_GROUPS
    e1 = i1 - MOE_GROUPS

    onehot = ((lane == e0) | (lane == e1))
    c_bf = jnp.where(onehot, 1.0, 0.0).astype(BF16)
    prefix = _dot(tri_ref[...], c_bf) + run_ref[...]
    rank0 = jnp.sum(jnp.where(lane == e0, prefix, 0.0), axis=-1, keepdims=True).astype(I32)
    rank1 = jnp.sum(jnp.where(lane == e1, prefix, 0.0), axis=-1, keepdims=True).astype(I32)
    run = run_ref[...] + jnp.sum(jnp.where(onehot, 1.0, 0.0), axis=0, keepdims=True)
    run_ref[...] = run
    cnt_ref[...] = jnp.broadcast_to(run, cnt_ref.shape)

    ri = jnp.where(lane == 0, e0, jnp.where(lane == 1, e1, jnp.where(lane == 2, rank0, rank1)))
    ri_ref[...] = ri.T[0:8, :]
    rg_ref[...] = jnp.where(lane == 0, gate0, gate1)


def _route_operands(layer, group_w, group_b, router_w, router_b, tm):
    d = group_w.shape[1]
    pad = LANES - MOE_GROUPS - MOE_EXPERTS
    wr = jnp.concatenate([group_w[layer], router_w[layer], jnp.zeros((d, pad), F32)], axis=1)
    br = jnp.concatenate([group_b[layer], router_b[layer], jnp.zeros((pad,), F32)]).reshape(1, LANES)
    whi = wr.astype(BF16)
    wlo = (wr - whi.astype(F32)).astype(BF16)
    whl = jnp.concatenate([whi, wlo], axis=1)
    tri = jnp.tril(jnp.ones((tm, tm), BF16), -1)
    return whl, br, tri


def _route_out(n, tm):
    shapes = (jax.ShapeDtypeStruct((n * SLAB, LANES), U32), jax.ShapeDtypeStruct((8, n), I32),
              jax.ShapeDtypeStruct((n, LANES), F32), jax.ShapeDtypeStruct((8, LANES), F32))
    specs = (pl.BlockSpec((tm * SLAB, LANES), lambda i: (i, 0)), pl.BlockSpec((8, tm), lambda i: (0, i)),
             pl.BlockSpec((tm, LANES), lambda i: (i, 0)), pl.BlockSpec((8, LANES), lambda i: (0, 0)))
    return shapes, specs


def _pool_route_kernel(*refs, with_combine, t, pg):
    refs = list(refs)
    if with_combine:
        dest_ref, dest_next_ref, rgp_ref, ys_ref = refs[:4]
        refs = refs[4:]
    h_ref, gmix_ref, pw_ref, scale_ref, gffn_ref, whl_ref, b_ref, tri_ref = refs[:8]
    o_ref, hn_ref, ri_ref, rg_ref, cnt_ref = refs[8:13]
    refs = refs[13:]
    if with_combine:
        buf00, buf01, buf10, buf11, sems, hbuf = refs[:6]
        refs = refs[6:]
    xbuf, sa, sb, wbf, run_ref = refs
    i = pl.program_id(0)

    @pl.when(i == 0)
    def _():
        _pool_init(pw_ref, xbuf, sa, sb, wbf)
        run_ref[...] = jnp.zeros_like(run_ref)

    if with_combine:
        _combine_tile(dest_ref, dest_next_ref, h_ref, rgp_ref, ys_ref, hbuf, ((buf00, buf01), (buf10, buf11)), sems, t)
        h = hbuf[...]
    else:
        h = h_ref[...]
    _pool_tile(h, i, gmix_ref, scale_ref, o_ref, xbuf, sa, sb, wbf, t, pg)
    _route_tile(o_ref[...], gffn_ref, whl_ref, b_ref, tri_ref, hn_ref, ri_ref, rg_ref, cnt_ref, run_ref, t)


def _pool_route(h, pending, g_mix, pool_w, pool_scale, j, g_ffn, route_ops, t=MOE_TOK_TILE):
    n, d = h.shape
    nw, pg, _ = pool_w.shape[1:]
    whl, br, tri = route_ops
    assert d == 2 * SLAB * LANES
    last = n // t - 1
    row = lambda i: (i, 0)
    const = lambda i: (0, 0)
    in_specs, args, scratch = [], [], []
    if pending is not None:
        rg_prev, ys, dest = pending
        in_specs += [pl.BlockSpec((None, 1, 2 * t), lambda i: (i, 0, 0), memory_space=pltpu.SMEM),
                     pl.BlockSpec((None, 1, 2 * t), lambda i: (jnp.minimum(i + 1, last), 0, 0),
                                  memory_space=pltpu.SMEM),
                     pl.BlockSpec((t, LANES), row),
                     pl.BlockSpec(memory_space=pl.ANY)]
        args += [dest, dest, rg_prev, ys]
        scratch += [pltpu.VMEM((t * SLAB, LANES), U32)] * 4 + [pltpu.SemaphoreType.DMA((2,)), pltpu.VMEM((t, d), F32)]
    in_specs += [pl.BlockSpec((t, d), row), pl.BlockSpec((1, d), const),
                 pl.BlockSpec((None, nw, pg, pg), lambda i: (j, 0, 0, 0)), pl.BlockSpec((1, d), const),
                 pl.BlockSpec((1, d), const), pl.BlockSpec((d, 2 * LANES), const),
                 pl.BlockSpec((1, LANES), const), pl.BlockSpec((t, t), const)]
    args += [h, g_mix.reshape(1, d), pool_w, pool_scale[j].reshape(1, d), g_ffn.reshape(1, d), whl, br, tri]
    r_shapes, r_specs = _route_out(n, t)
    scratch += [pltpu.VMEM((POOL_PAD + POOL_HALO + t, d), F32),
                pltpu.VMEM((POOL_PAD + POOL_HALO + t, pg), F32),
                pltpu.VMEM((POOL_PAD + POOL_HALO + t, pg), F32),
                pltpu.VMEM((nw, pg, pg), BF16),
                pltpu.VMEM((1, LANES), F32)]
    return pl.pallas_call(
        functools.partial(_pool_route_kernel, with_combine=pending is not None, t=t, pg=pg),
        out_shape=(jax.ShapeDtypeStruct((n, d), F32),) + r_shapes,
        grid=(n // t,),
        in_specs=in_specs,
        out_specs=(pl.BlockSpec((t, d), row),) + r_specs,
        scratch_shapes=scratch,
        compiler_params=_cparams(1),
        name="pool_route",
    )(*args)


def _route(h, g, route_ops, tm=ROUTE_TILE):
    n, d = h.shape
    whl, br, tri = route_ops
    return pl.pallas_call(
        functools.partial(_route_kernel, tm=tm),
        out_shape=(jax.ShapeDtypeStruct((n * SLAB, LANES), U32),
                   jax.ShapeDtypeStruct((8, n), I32),
                   jax.ShapeDtypeStruct((n, LANES), F32),
                   jax.ShapeDtypeStruct((8, LANES), F32)),
        grid=(n // tm,),
        in_specs=[pl.BlockSpec((tm, d), lambda i: (i, 0)),
                  pl.BlockSpec((1, d), lambda i: (0, 0)),
                  pl.BlockSpec((d, 2 * LANES), lambda i: (0, 0)),
                  pl.BlockSpec((1, LANES), lambda i: (0, 0)),
                  pl.BlockSpec((tm, tm), lambda i: (0, 0))],
        out_specs=(pl.BlockSpec((tm * SLAB, LANES), lambda i: (i, 0)),
                   pl.BlockSpec((8, tm), lambda i: (0, i)),
                   pl.BlockSpec((tm, LANES), lambda i: (i, 0)),
                   pl.BlockSpec((8, LANES), lambda i: (0, 0))),
        scratch_shapes=[pltpu.VMEM((1, LANES), F32)],
        compiler_params=_cparams(1),
        name="moe_route",
    )(h, g.reshape(1, d), whl, br, tri)


def _dispatch_kernel(pad_start_ref, pad_len_ref, nused_ref, dest_ref, hn_ref, xs_ref, zbuf, sem, zsem, *, t):
    sizes = [1 << s for s in reversed(range(MOE_ROWS.bit_length() - 1))]

    @pl.when(pl.program_id(0) == 0)
    def _():
        zbuf[...] = jnp.zeros_like(zbuf)

        def zero_copy(off, size):
            return pltpu.make_async_copy(zbuf.at[pl.ds(0, size * SLAB)], _slab_run(xs_ref, off, size), zsem)

        def gaps(e, wait):
            off = pad_start_ref[e]
            n = pad_len_ref[e]
            for size in sizes:
                hit = (n & size) != 0

                @pl.when(hit)
                def _():
                    c = zero_copy(off, size)
                    c.wait() if wait else c.start()

                off = off + (n & size)

        def start_gaps(e, c):
            gaps(e, False)
            return c

        def wait_gaps(e, c):
            gaps(e, True)
            return c

        lax.fori_loop(0, MOE_EXPERTS, start_gaps, 0)
        lax.fori_loop(0, MOE_EXPERTS, wait_gaps, 0)

        zrows = zbuf.shape[0] // SLAB
        n_blocks = xs_ref.shape[0] // (MOE_ROWS * SLAB)

        def tail(wait):
            def body(b, c):
                for part in range(MOE_ROWS // zrows):
                    cp = zero_copy(b * MOE_ROWS + part * zrows, zrows)
                    cp.wait() if wait else cp.start()
                return c
            lax.fori_loop(nused_ref[0], n_blocks, body, 0)

        tail(False)
        tail(True)

    def row_copy(r, k):
        return pltpu.make_async_copy(_slab(hn_ref, r), _slab(xs_ref, dest_ref[0, k * t + r]), sem)

    def start(r, c):
        row_copy(r, 0).start(priority=0)
        row_copy(r, 1).start(priority=1)
        return c

    lax.fori_loop(0, t, start, 0, unroll=8)
    for _ in range(2):
        pltpu.make_async_copy(hn_ref, xs_ref.at[pl.ds(0, t * SLAB)], sem).wait()


def _dispatch(hn, dest, pad_start, pad_len, n_used, n_rows, t=MOE_TOK_TILE):
    n = hn.shape[0] // SLAB
    grid_spec = pltpu.PrefetchScalarGridSpec(
        num_scalar_prefetch=3,
        grid=(n // t,),
        in_specs=[pl.BlockSpec((None, 1, 2 * t), lambda i, *_: (i, 0, 0), memory_space=pltpu.SMEM),
                  pl.BlockSpec((t * SLAB, LANES), lambda i, *_: (i, 0))],
        out_specs=pl.BlockSpec(memory_space=pl.ANY),
        scratch_shapes=[pltpu.VMEM((MOE_ROWS // 2 * SLAB, LANES), hn.dtype),
                        pltpu.SemaphoreType.DMA(()), pltpu.SemaphoreType.DMA(())],
    )
    return pl.pallas_call(
        functools.partial(_dispatch_kernel, t=t),
        out_shape=jax.ShapeDtypeStruct((n_rows * SLAB, LANES), hn.dtype),
        grid_spec=grid_spec,
        compiler_params=_cparams(1),
        name="moe_dispatch",
    )(pad_start, pad_len, n_used, dest, hn)


def _expert_kernel(bexp_ref, first_ref, slot_ref, ahead_ref, second_ref, nused_ref, xs_ref, wg_hbm, wu_hbm, wd_hbm,
                   o_ref, wg_raw, wu_raw, wd_raw, wg_bf, wu_bf, wd_bf, sems, *, layer):
    b = pl.program_id(0)
    active = b < nused_ref[0]

    def fetch(e, slot):
        return [pltpu.make_async_copy(hbm.at[layer, e], raw.at[slot], sems.at[slot, k])
                for k, (hbm, raw) in enumerate(((wg_hbm, wg_raw), (wu_hbm, wu_raw), (wd_hbm, wd_raw)))]

    @pl.when(b == 0)
    def _():
        for c in fetch(bexp_ref[0], 0):
            c.start(priority=1)

        @pl.when(second_ref[0] >= 0)
        def _():
            for c in fetch(second_ref[0], 1):
                c.start(priority=1)

    @pl.when(active & (first_ref[b] == 1))
    def _():
        slot = slot_ref[b]
        ahead = ahead_ref[b]

        @pl.when(ahead >= 0)
        def _():
            for c in fetch(ahead, (slot + MOE_WSLOTS - 1) % MOE_WSLOTS):
                c.start(priority=1)

        for c in fetch(bexp_ref[b], slot):
            c.wait()
        wg_bf[...] = wg_raw[slot].astype(BF16)
        wu_bf[...] = wu_raw[slot].astype(BF16)
        wd_bf[...] = wd_raw[slot].astype(BF16)

    @pl.when(active)
    def _():
        rows = xs_ref.shape[0] // SLAB
        halves = [_unpack_rows(u) for u in _load_token_slabs(xs_ref, rows)]
        x = jnp.concatenate([lo.astype(BF16) for lo, _ in halves] + [hi.astype(BF16) for _, hi in halves], axis=1)
        gte = _dot(x, wg_bf[...])
        up = _dot(x, wu_bf[...])
        hb = _silu(gte) * up
        _store_token_slabs(o_ref, _pack_rows(_dot(hb.astype(BF16), wd_bf[...])))

    @pl.when(jnp.logical_not(active))
    def _():
        o_ref[...] = jnp.zeros_like(o_ref)


def _experts(xs, tables, gate_w, up_w, down_w, layer, rows=MOE_ROWS):
    n_rows = xs.shape[0] // SLAB
    d = gate_w.shape[-2]
    hid = gate_w.shape[-1]
    assert d == 2 * SLAB * LANES and MOE_WSLOTS == 3
    x_map = lambda b, be, fi, sl, ah, se, nu: (jnp.minimum(b, nu[0] - 1), 0)
    grid_spec = pltpu.PrefetchScalarGridSpec(
        num_scalar_prefetch=6,
        grid=(n_rows // rows,),
        in_specs=[pl.BlockSpec((rows * SLAB, LANES), x_map),
                  pl.BlockSpec(memory_space=pl.ANY),
                  pl.BlockSpec(memory_space=pl.ANY),
                  pl.BlockSpec(memory_space=pl.ANY)],
        out_specs=pl.BlockSpec((rows * SLAB, LANES), lambda b, *_: (b, 0)),
        scratch_shapes=[pltpu.VMEM((MOE_WSLOTS, d, hid), F32), pltpu.VMEM((MOE_WSLOTS, d, hid), F32),
                        pltpu.VMEM((MOE_WSLOTS, hid, d), F32),
                        pltpu.VMEM((d, hid), BF16), pltpu.VMEM((d, hid), BF16), pltpu.VMEM((hid, d), BF16),
                        pltpu.SemaphoreType.DMA((MOE_WSLOTS, 3))],
    )
    return pl.pallas_call(
        functools.partial(_expert_kernel, layer=layer),
        out_shape=jax.ShapeDtypeStruct((n_rows * SLAB, LANES), U32),
        grid_spec=grid_spec,
        compiler_params=_cparams(1),
        name="moe_experts",
    )(*tables, xs, gate_w, up_w, down_w)


def _combine_kernel(*refs, t, mode):
    dest_ref, dest_next_ref, h_ref, rg_ref, ys_ref = refs[:5]
    rest = list(refs[5:])
    g_ref = rest.pop(0) if mode != "plain" else None
    o_ref = rest.pop(0)
    hn_ref = rest.pop(0) if mode == "norm_bf16" else None
    buf00, buf01, buf10, buf11, sems = rest
    _combine_tile(dest_ref, dest_next_ref, h_ref, rg_ref, ys_ref, o_ref, ((buf00, buf01), (buf10, buf11)), sems, t)
    if mode != "plain":
        h_new = o_ref[...]
        normed = _rms(h_new, g_ref[...])
        if mode == "norm_bf16":
            hn_ref[...] = normed.astype(BF16)
        else:
            o_ref[...] = normed


def _combine_tile(dest_ref, dest_next_ref, h_ref, rg_ref, ys_ref, o_ref, bufs, sems, t):
    i = pl.program_id(0)
    n = pl.num_programs(0)
    half = SLAB * LANES

    def row_copy(dref, slot, r, k):
        return pltpu.make_async_copy(_slab(ys_ref, dref[0, k * t + r]), _slab(bufs[slot][k], r), sems.at[slot])

    def issue(dref, slot):
        def start(r, c):
            row_copy(dref, slot, r, 0).start(priority=0)
            row_copy(dref, slot, r, 1).start(priority=1)
            return c
        lax.fori_loop(0, t, start, 0, unroll=8)

    def drain(slot):
        for k in range(2):
            pltpu.make_async_copy(ys_ref.at[pl.ds(0, t * SLAB)], bufs[slot][k], sems.at[slot]).wait()

    @pl.when(i == 0)
    def _():
        issue(dest_ref, 0)

    for slot in (0, 1):
        @pl.when((i % 2 == slot) & (i + 1 < n))
        def _():
            issue(dest_next_ref, 1 - slot)

        @pl.when(i % 2 == slot)
        def _():
            drain(slot)
            rg = rg_ref[...]
            g0 = rg[:, 0:1]
            g1 = rg[:, 1:2]
            for c, (ua, ub) in enumerate(zip(_load_token_slabs(bufs[slot][0], t), _load_token_slabs(bufs[slot][1], t))):
                a_lo, a_hi = _unpack_rows(ua)
                b_lo, b_hi = _unpack_rows(ub)
                lo_cols = slice(c * LANES, (c + 1) * LANES)
                hi_cols = slice(half + c * LANES, half + (c + 1) * LANES)
                o_ref[:, lo_cols] = h_ref[:, lo_cols] + (g0 * a_lo + g1 * b_lo)
                o_ref[:, hi_cols] = h_ref[:, hi_cols] + (g0 * a_hi + g1 * b_hi)


def _combine(h, rg, ys, dest, g_next, mode, t=MOE_TOK_TILE):
    n, d = h.shape
    assert d == 2 * SLAB * LANES
    last = n // t - 1
    in_specs = [pl.BlockSpec((None, 1, 2 * t), lambda i: (i, 0, 0), memory_space=pltpu.SMEM),
                pl.BlockSpec((None, 1, 2 * t), lambda i: (jnp.minimum(i + 1, last), 0, 0),
                             memory_space=pltpu.SMEM),
                pl.BlockSpec((t, d), lambda i: (i, 0)),
                pl.BlockSpec((t, LANES), lambda i: (i, 0)),
                pl.BlockSpec(memory_space=pl.ANY)]
    args = [dest, dest, h, rg, ys]
    tile = pl.BlockSpec((t, d), lambda i: (i, 0))
    if mode != "plain":
        in_specs.append(pl.BlockSpec((1, d), lambda i: (0, 0)))
        args.append(g_next.reshape(1, d))
    if mode == "norm_bf16":
        out_shape = (jax.ShapeDtypeStruct((n, d), F32), jax.ShapeDtypeStruct((n, d), BF16))
        out_specs = (tile, tile)
    else:
        out_shape = jax.ShapeDtypeStruct((n, d), F32)
        out_specs = tile
    return pl.pallas_call(
        functools.partial(_combine_kernel, t=t, mode=mode),
        out_shape=out_shape,
        grid=(n // t,),
        in_specs=in_specs,
        out_specs=out_specs,
        scratch_shapes=[pltpu.VMEM((t * SLAB, LANES), U32)] * 4 + [pltpu.SemaphoreType.DMA((2,))],
        compiler_params=_cparams(1),
        name="moe_combine",
    )(*args)


def _moe_experts_path(hn, ri, cnt, layer, gate_w, up_w, down_w):
    n = hn.shape[0] // SLAB
    t = MOE_TOK_TILE
    counts = cnt[0, :MOE_EXPERTS].astype(I32)
    nblk = (counts + MOE_ROWS - 1) // MOE_ROWS
    blk_end = jnp.cumsum(nblk)
    row_start = (blk_end - nblk) * MOE_ROWS
    n_rows = -(-(n * 2 + MOE_EXPERTS * (MOE_ROWS - 1)) // MOE_ROWS) * MOE_ROWS
    n_blocks = n_rows // MOE_ROWS
    experts = ri[0:2, :]
    ranks = ri[2:4, :]
    onehot = experts[:, :, None] == jnp.arange(MOE_EXPERTS, dtype=I32)
    dest = jnp.sum(jnp.where(onehot, row_start, 0), axis=-1) + ranks
    dest = dest.reshape(2, n // t, t).transpose(1, 0, 2).reshape(n // t, 1, 2 * t).astype(I32)
    blocks = jnp.arange(n_blocks, dtype=I32)
    block_exp = jnp.minimum(jnp.sum(blk_end[None, :] <= blocks[:, None], axis=1), MOE_EXPERTS - 1).astype(I32)
    n_used = blk_end[-1:].astype(I32)
    eids = jnp.arange(MOE_EXPERTS, dtype=I32)
    nonempty = nblk > 0
    ordinal = jnp.cumsum(nonempty.astype(I32)) - 1
    holds = nonempty[None, :] & (ordinal[None, :] == eids[:, None])
    expert_at = jnp.where(jnp.any(holds, axis=1), jnp.sum(jnp.where(holds, eids[None, :], 0), axis=1), -1)
    expert_at = jnp.concatenate([expert_at, jnp.full((MOE_WSLOTS,), -1, I32)]).astype(I32)
    is_first = ((blocks == (blk_end - nblk)[block_exp]) & (blocks < n_used[0])).astype(I32)
    pos = ordinal[block_exp]
    tables = (block_exp, is_first, (pos % MOE_WSLOTS).astype(I32), expert_at[pos + MOE_WSLOTS - 1],
              expert_at[1:2], n_used)
    pad_start = (row_start + counts).astype(I32)
    pad_len = (nblk * MOE_ROWS - counts).astype(I32)

    xs = _dispatch(hn, dest, pad_start, pad_len, n_used, n_rows)
    return _experts(xs, tables, gate_w, up_w, down_w, layer), dest


def _ssd_kernel(z_ref, x_ref, b_ref, c_ref, dt_ref, cwx_ref, cwb_ref, cwc_ref, cbx_ref, cbb_ref, cbc_ref,
                dtb_ref, a_ref, dexp_ref, nw_ref, exp_ref, tri_ref, o_ref,
                xbuf, bbuf, cbuf, xc_s, xdt_s, xw_s, b_s, c_s, eacs_s, acs_s, acst_s, y_s, st_s,
                *, L, hp, n_state, groups):
    ci = pl.program_id(0)
    d_inner = x_ref.shape[1]
    gw = d_inner // groups
    kh = gw // hp
    H0 = CONV_HALO

    @pl.when(ci == 0)
    def _():
        xbuf[0:H0, :] = jnp.zeros((H0, xbuf.shape[1]), F32)
        bbuf[0:H0, :] = jnp.zeros((H0, bbuf.shape[1]), F32)
        cbuf[0:H0, :] = jnp.zeros((H0, cbuf.shape[1]), F32)
        st_s[...] = jnp.zeros_like(st_s)

    def conv_silu(in_ref, buf, cw_ref, cb_ref):
        buf[H0:H0 + L, :] = in_ref[...].astype(F32)
        acc = cb_ref[...] + cw_ref[SSD_CONV - 1:SSD_CONV, :] * buf[H0:H0 + L, :]
        for j in range(SSD_CONV - 1):
            off = H0 - (SSD_CONV - 1) + j
            acc = acc + cw_ref[j:j + 1, :] * buf[off:off + L, :]
        buf[0:H0, :] = buf[L:L + H0, :]
        return _silu(acc)

    xc_s[...] = conv_silu(x_ref, xbuf, cwx_ref, cbx_ref)
    b_s[...] = conv_silu(b_ref, bbuf, cwb_ref, cbb_ref).astype(BF16)
    c_s[...] = conv_silu(c_ref, cbuf, cwc_ref, cbc_ref).astype(BF16)

    dt = _softplus(dt_ref[...] + dtb_ref[...])
    d_a = dt * a_ref[...]
    acs = _dot3_lhs_exact(tri_ref[...], d_a)
    acs_s[...] = acs
    acst_s[...] = acs.T
    emat = exp_ref[...]
    acs_last = acs[L - 1:L, :]
    to_end = dt * jnp.exp(acs_last - acs)
    xc = xc_s[...]
    xdt_s[...] = (xc * _dot2_rhs_exact(dt, emat)).astype(BF16)
    xw_s[...] = (xc * _dot2_rhs_exact(to_end, emat)).astype(BF16)
    eacs_s[...] = _dot2_rhs_exact(jnp.exp(acs), emat)
    cdecay = _dot2_rhs_exact(jnp.broadcast_to(jnp.exp(acs_last), (SLAB, LANES)), emat)[0:1, :]

    rr = lax.broadcasted_iota(I32, (L, L), 0)
    cc = lax.broadcasted_iota(I32, (L, L), 1)
    causal = rr >= cc
    lane2 = lax.broadcasted_iota(I32, (L, 2 * hp), 1)

    for g in range(groups):
        bg = b_s[:, g * n_state:(g + 1) * n_state]
        cg = c_s[:, g * n_state:(g + 1) * n_state]
        cb = _dot_nt(cg, bg)
        st_prev = st_s[g]
        y_off = _dot(cg, st_prev.astype(BF16)) * eacs_s[:, g * gw:(g + 1) * gw]
        for kp in range(kh // 2):
            c0 = g * gw + kp * 2 * hp
            slab = xdt_s[:, c0:c0 + 2 * hp]
            ms, xs = [], []
            for half in range(2):
                hd = g * kh + kp * 2 + half
                seg = acs_s[:, hd:hd + 1] - acst_s[hd:hd + 1, :]
                decay = jnp.exp(jnp.where(causal, seg, -jnp.inf))
                ms.append((cb * decay).astype(BF16))
                keep = (lane2 < hp) if half == 0 else (lane2 >= hp)
                xs.append(jnp.where(keep, slab, jnp.zeros_like(slab)))
            y_pair = _dot(jnp.concatenate(ms, axis=1), jnp.concatenate(xs, axis=0))
            y_s[:, c0:c0 + 2 * hp] = y_pair + y_off[:, kp * 2 * hp:(kp + 1) * 2 * hp]
        bgt = bg.astype(F32).T.astype(BF16)
        s_new = _dot(bgt, xw_s[:, g * gw:(g + 1) * gw])
        st_s[g] = cdecay[:, g * gw:(g + 1) * gw] * st_prev + s_new

    y = y_s[...] + dexp_ref[...] * xc_s[...]
    z = z_ref[...].astype(F32)
    y = y * _silu(z)
    for g in range(groups):
        yg = y[:, g * gw:(g + 1) * gw]
        yg = yg * lax.rsqrt(jnp.mean(yg * yg, axis=-1, keepdims=True) + RMS_EPS)
        o_ref[:, g * gw:(g + 1) * gw] = (yg * nw_ref[:, g * gw:(g + 1) * gw]).astype(o_ref.dtype)


def _ssd_scan(zx, dt_raw, conv_w, conv_b, dt_bias, a_log, d_skip, norm_w, d_inner):
    s = zx.shape[0]
    L = SSD_CHUNK
    hp, n_state, groups = SSD_HEAD_DIM, SSD_STATE, SSD_GROUPS
    heads = d_inner // hp
    gn = groups * n_state
    gw = d_inner // groups
    assert d_inner % gn == 0 and heads <= LANES
    pad = LANES - heads
    dtb = jnp.pad(dt_bias, (0, pad)).reshape(1, LANES)
    a_neg = jnp.pad(-jnp.exp(a_log), (0, pad)).reshape(1, LANES)
    dexp = jnp.repeat(d_skip, hp).reshape(1, d_inner)
    emat = (jnp.arange(LANES)[:, None] == (jnp.arange(d_inner)[None, :] // hp)).astype(BF16)
    tri = jnp.tril(jnp.ones((L, L), BF16))
    cb2 = conv_b.reshape(1, -1)
    xb = d_inner // gn
    row = lambda c: (c, 0)
    const = lambda c: (0, 0)
    in_specs = [
        pl.BlockSpec((L, d_inner), row),
        pl.BlockSpec((L, d_inner), lambda c: (c, 1)),
        pl.BlockSpec((L, gn), lambda c: (c, 2 * xb)),
        pl.BlockSpec((L, gn), lambda c: (c, 2 * xb + 1)),
        pl.BlockSpec((L, LANES), row),
        pl.BlockSpec((SSD_CONV, d_inner), const),
        pl.BlockSpec((SSD_CONV, gn), lambda c: (0, xb)),
        pl.BlockSpec((SSD_CONV, gn), lambda c: (0, xb + 1)),
        pl.BlockSpec((1, d_inner), const),
        pl.BlockSpec((1, gn), lambda c: (0, xb)),
        pl.BlockSpec((1, gn), lambda c: (0, xb + 1)),
        pl.BlockSpec((1, LANES), const),
        pl.BlockSpec((1, LANES), const),
        pl.BlockSpec((1, d_inner), const),
        pl.BlockSpec((1, d_inner), const),
        pl.BlockSpec((LANES, d_inner), const),
        pl.BlockSpec((L, L), const),
    ]
    scratch = [
        pltpu.VMEM((CONV_HALO + L, d_inner), F32), pltpu.VMEM((CONV_HALO + L, gn), F32),
        pltpu.VMEM((CONV_HALO + L, gn), F32),
        pltpu.VMEM((L, d_inner), F32),
        pltpu.VMEM((L, d_inner), BF16),
        pltpu.VMEM((L, d_inner), BF16),
        pltpu.VMEM((L, gn), BF16), pltpu.VMEM((L, gn), BF16),
        pltpu.VMEM((L, d_inner), F32),
        pltpu.VMEM((L, LANES), F32), pltpu.VMEM((LANES, L), F32),
        pltpu.VMEM((L, d_inner), F32),
        pltpu.VMEM((groups, n_state, gw), F32),
    ]
    return pl.pallas_call(
        functools.partial(_ssd_kernel, L=L, hp=hp, n_state=n_state, groups=groups),
        out_shape=jax.ShapeDtypeStruct((s, d_inner), BF16),
        grid=(s // L,),
        in_specs=in_specs,
        out_specs=pl.BlockSpec((L, d_inner), row),
        scratch_shapes=scratch,
        compiler_params=_cparams(1),
        name="ssd_scan",
    )(zx, zx, zx, zx, dt_raw, conv_w, conv_w, conv_w, cb2, cb2, cb2, dtb, a_neg, dexp,
      norm_w.reshape(1, d_inner), emat, tri)


def _ssd_mixer(h, hn, j, in_w, conv_w, conv_b, dt_bias, a_log, d_skip, norm_w, out_w):
    d_inner = out_w.shape[1]
    heads = d_inner // SSD_HEAD_DIM
    wide = in_w.shape[2] - heads
    wt = jnp.swapaxes(in_w, 1, 2)
    zx = _matmul_t(hn, wt, layer=j, row0=0, nrows=wide, tn=MM_TN, tm=MM_TM, out_dtype=BF16, name="ssd_in_proj")
    dt_raw = _matmul_t(hn, wt, layer=j, row0=wide, nrows=heads, tn=heads, tm=MM_TM, name="ssd_dt_proj")
    y = _ssd_scan(zx, dt_raw, conv_w, conv_b, dt_bias, a_log, d_skip, norm_w, d_inner)
    return _matmul(y, out_w, layer=j, tn=MM_TN // 2, tm=MM_TM, residual=h, name="ssd_out_proj")


def _fox_cum_kernel(f_ref, b_ref, tri_ref, o_ref, carry):
    @pl.when(pl.program_id(0) == 0)
    def _():
        carry[...] = jnp.zeros_like(carry)

    x = f_ref[...] + b_ref[...]
    log_f = jnp.minimum(x, 0.0) - jnp.log1p(jnp.exp(-jnp.abs(x)))
    cs = _dot3_rhs_exact(log_f, tri_ref[...]) + carry[...]
    o_ref[...] = cs
    carry[...] = cs[:, cs.shape[1] - 1:]


def _fox_cum(f_t, bias, tk=512):
    nh, s = f_t.shape
    triu = jnp.triu(jnp.ones((tk, tk), BF16))
    return pl.pallas_call(
        _fox_cum_kernel,
        out_shape=jax.ShapeDtypeStruct((nh, s), F32),
        grid=(s // tk,),
        in_specs=[pl.BlockSpec((nh, tk), lambda i: (0, i)),
                  pl.BlockSpec((nh, 1), lambda i: (0, 0)),
                  pl.BlockSpec((tk, tk), lambda i: (0, 0))],
        out_specs=pl.BlockSpec((nh, tk), lambda i: (0, i)),
        scratch_shapes=[pltpu.VMEM((nh, 1), F32)],
        compiler_params=_cparams(1),
        name="fox_cum",
    )(f_t, bias.reshape(nh, 1), triu)


def _fox_attn_kernel(q_ref, k_ref, v_ref, cum_ref, o_ref, m_s, acc_s, vext, *, tq, tk, sr):
    qi = pl.program_id(1)
    dh = q_ref.shape[1]
    n_sub = tq // sr

    @pl.when(qi == 0)
    def _():
        vext[:, 0:dh] = v_ref[...]
        vext[:, dh:2 * dh] = jnp.ones((vext.shape[0], dh), BF16)

    q0 = pl.multiple_of(qi * tq, tq)
    c_ref = cum_ref[:, pl.ds(q0, LANES)][:, 0:1]
    m_s[...] = jnp.full_like(m_s, -jnp.inf)
    acc_s[...] = jnp.zeros_like(acc_s)

    def sub_step(r, k0, width, bias, diagonal):
        rows = slice(r * sr, (r + 1) * sr)
        s = _dot_nt(q_ref[rows, :], k_ref[pl.ds(k0, width), :]) + bias
        if diagonal:
            keep = lax.broadcasted_iota(I32, (sr, width), 0) >= lax.broadcasted_iota(I32, (sr, width), 1)
            s = jnp.where(keep, s, -jnp.inf)
        m_prev = m_s[rows, :]
        m_new = jnp.maximum(m_prev, jnp.max(s, axis=-1, keepdims=True))
        alpha = jnp.exp2(m_prev - m_new)
        p = jnp.exp2(s - jnp.concatenate([m_new] * (width // dh), axis=1)).astype(BF16)
        pv = _dot(p, vext[pl.ds(k0, width), :])
        acc_s[rows, :] = jnp.concatenate([alpha, alpha], axis=1) * acc_s[rows, :] + pv
        m_s[rows, :] = m_new

    def key_bias(k0, width):
        return (c_ref - cum_ref[:, pl.ds(k0, width)]) * LOG2E

    def body(j, c):
        k0 = pl.multiple_of(j * tk, tk)
        bias = key_bias(k0, tk)
        for r in range(n_sub):
            sub_step(r, k0, tk, bias, False)
        return c

    lax.fori_loop(0, qi * (tq // tk), body, 0)
    for r in range(n_sub):
        if r > 0:
            sub_step(r, q0, r * sr, key_bias(q0, r * sr), False)
        k0 = pl.multiple_of(q0 + r * sr, sr)
        sub_step(r, k0, sr, key_bias(k0, sr), True)
    acc = acc_s[...]
    o_ref[...] = (acc[:, 0:dh] / acc[:, dh:2 * dh]).astype(o_ref.dtype)


def _fox_attention(qkv, cum, nh, tq=FOX_TQ, tk=FOX_TK, sr=FOX_SUB):
    s = qkv.shape[0]
    dh = FOX_HEAD_DIM
    assert tq % tk == 0 and tq % sr == 0 and tk % dh == 0 and dh == LANES
    cum3 = cum.reshape(nh, 1, s)
    return pl.pallas_call(
        functools.partial(_fox_attn_kernel, tq=tq, tk=tk, sr=sr),
        out_shape=jax.ShapeDtypeStruct((s, nh * dh), BF16),
        grid=(nh, s // tq),
        in_specs=[pl.BlockSpec((tq, dh), lambda h, i: (i, h)),
                  pl.BlockSpec((s, dh), lambda h, i: (0, nh + h)),
                  pl.BlockSpec((s, dh), lambda h, i: (0, 2 * nh + h)),
                  pl.BlockSpec((None, 1, s), lambda h, i: (h, 0, 0))],
        out_specs=pl.BlockSpec((tq, dh), lambda h, i: (i, h)),
        scratch_shapes=[pltpu.VMEM((tq, dh), F32), pltpu.VMEM((tq, 2 * dh), F32), pltpu.VMEM((s, 2 * dh), BF16)],
        compiler_params=_cparams(2),
        name="fox_attention",
    )(qkv, qkv, qkv, cum3)


def _fox_mixer(h, hn, j, in_w, f_bias, out_w):
    d = hn.shape[1]
    nh = d // FOX_HEAD_DIM
    q_scale = FOX_HEAD_DIM ** -0.5 * LOG2E
    colscale = jnp.concatenate([jnp.full((1, d), q_scale, F32), jnp.ones((1, 2 * d), F32)], axis=1)
    wt = jnp.swapaxes(in_w, 1, 2)
    qkv = _matmul_t(hn, wt, layer=j, row0=0, nrows=3 * d, tn=MM_TN, tm=MM_TM, out_dtype=BF16, colscale=colscale,
                    name="fox_in_proj")
    f_raw = _matmul_t(hn, wt, layer=j, row0=3 * d, nrows=nh, tn=nh, tm=MM_TM, name="fox_f_proj")
    cum = _fox_cum(f_raw[:, :nh].T, f_bias)
    o = _fox_attention(qkv, cum, nh)
    return _matmul(o, out_w, layer=j, tn=MM_TN, tm=MM_TM, residual=h, name="fox_out_proj")


def _norm_kernel(h_ref, g_ref, o_ref):
    o_ref[...] = _rms(h_ref[...], g_ref[...]).astype(o_ref.dtype)


def _norm(h, g, out_dtype, tm=512):
    n, d = h.shape
    return pl.pallas_call(
        _norm_kernel,
        out_shape=jax.ShapeDtypeStruct((n, d), out_dtype),
        grid=(n // tm,),
        in_specs=[pl.BlockSpec((tm, d), lambda i: (i, 0)), pl.BlockSpec((1, d), lambda i: (0, 0))],
        out_specs=pl.BlockSpec((tm, d), lambda i: (i, 0)),
        compiler_params=_cparams(1),
        name="rmsnorm",
    )(h, g.reshape(1, d))


def kernel(x, norm_mix, norm_ffn, final_norm, pool_w, pool_scale, ssd_in_w, ssd_conv_w, ssd_conv_b,
           ssd_dt_bias, ssd_a_log, ssd_d_skip, ssd_norm, ssd_out_w, fox_in_w, fox_f_bias, fox_out_w,
           moe_group_w, moe_group_b, moe_router_w, moe_router_b, moe_gate_w, moe_up_w, moe_down_w):
    bsz, s, d = x.shape
    assert bsz == 1
    depth = norm_mix.shape[0]
    h = x.reshape(s, d)
    hn = None
    pending = None
    for i in range(depth):
        kind, j = i % 3, i // 3
        if kind == 0:
            ops = _route_operands(i, moe_group_w, moe_group_b, moe_router_w, moe_router_b, MOE_TOK_TILE)
            h, hn_rows, ri, rg, cnt = _pool_route(h, pending, norm_mix[i], pool_w, pool_scale, j, norm_ffn[i], ops)
            pending = None
        else:
            if hn is None:
                hn = _norm(h, norm_mix[i], BF16)
            if kind == 1:
                h = _ssd_mixer(h, hn, j, ssd_in_w, ssd_conv_w[j], ssd_conv_b[j], ssd_dt_bias[j], ssd_a_log[j],
                               ssd_d_skip[j], ssd_norm[j], ssd_out_w)
            else:
                h = _fox_mixer(h, hn, j, fox_in_w, fox_f_bias[j], fox_out_w)
            ops = _route_operands(i, moe_group_w, moe_group_b, moe_router_w, moe_router_b, ROUTE_TILE)
            hn_rows, ri, rg, cnt = _route(h, norm_ffn[i], ops)
        ys, dest = _moe_experts_path(hn_rows, ri, cnt, i, moe_gate_w, moe_up_w, moe_down_w)
        hn = None
        if i == depth - 1:
            h = _combine(h, rg, ys, dest, final_norm, "final")
        elif (i + 1) % 3 == 0:
            pending = (rg, ys, dest)
        else:
            h, hn = _combine(h, rg, ys, dest, norm_mix[i + 1], "norm_bf16")
    return h.reshape(bsz, s, d)
```

```python
import functools

import jax
import jax.numpy as jnp
from jax import lax
from jax.experimental import pallas as pl
from jax.experimental.pallas import tpu as pltpu

F32 = jnp.float32
BF16 = jnp.bfloat16
I32 = jnp.int32
U32 = jnp.uint32

RMS_EPS = 1e-6
LOG2E = 1.4426950408889634
LANES = 128
SLAB = 8
VMEM_LIMIT = 56 * 1024 * 1024

MM_TM = 1024
MM_TN = 1024

POOL_WINDOWS = (2, 4, 8, 16)
POOL_HALO = 16
POOL_PAD = 8

SSD_HEAD_DIM = 64
SSD_STATE = 128
SSD_GROUPS = 8
SSD_CONV = 4
SSD_CHUNK = 128
CONV_HALO = 8

FOX_HEAD_DIM = 128
FOX_TQ = 4096
FOX_TK = 1024
FOX_SUB = 256

MOE_GROUPS = 4
MOE_EPG = 8
MOE_EXPERTS = MOE_GROUPS * MOE_EPG
MOE_ROWS = 256
MOE_WSLOTS = 3
MOE_TOK_TILE = 256
ROUTE_TILE = 512


def _cparams(n_axes, vmem=VMEM_LIMIT):
    return pltpu.CompilerParams(dimension_semantics=("arbitrary",) * n_axes, vmem_limit_bytes=vmem)


def _rms(x, g):
    ms = jnp.mean(x * x, axis=-1, keepdims=True)
    return x * lax.rsqrt(ms + RMS_EPS) * g


def _split3(x):
    hi = x.astype(BF16)
    r = x - hi.astype(F32)
    mid = r.astype(BF16)
    lo = (r - mid.astype(F32)).astype(BF16)
    return hi, mid, lo


def _dot(a, b):
    return jnp.dot(a, b, preferred_element_type=F32)


def _dot_nt(a, b):
    return lax.dot_general(a, b, (((1,), (1,)), ((), ())), preferred_element_type=F32)


def _dot3_rhs_exact(x, m):
    hi, mid, lo = _split3(x)
    return _dot(hi, m) + _dot(mid, m) + _dot(lo, m)


def _dot3_lhs_exact(m, x):
    hi, mid, lo = _split3(x)
    return _dot(m, hi) + _dot(m, mid) + _dot(m, lo)


def _dot2_rhs_exact(x, m):
    hi = x.astype(BF16)
    lo = (x - hi.astype(F32)).astype(BF16)
    return _dot(hi, m) + _dot(lo, m)


def _silu(x):
    h = 0.5 * x
    return h + h * jnp.tanh(h)


def _softplus(x):
    return jnp.maximum(x, 0.0) + jnp.log1p(jnp.exp(-jnp.abs(x)))


def _pack_rows(x):
    half = x.shape[1] // 2
    lo = lax.bitcast_convert_type(x[:, :half].astype(BF16).astype(F32), U32)
    hi = lax.bitcast_convert_type(x[:, half:].astype(BF16).astype(F32), U32)
    return hi | (lo >> 16)


def _unpack_rows(u):
    lo = lax.bitcast_convert_type(u << 16, F32)
    hi = lax.bitcast_convert_type(u & jnp.uint32(0xFFFF0000), F32)
    return lo, hi


def _load_token_slabs(ref, rows):
    return [ref[pl.ds(c, rows, stride=SLAB), :] for c in range(SLAB)]


def _store_token_slabs(ref, packed):
    rows = packed.shape[0]
    for c in range(SLAB):
        ref[pl.ds(c, rows, stride=SLAB), :] = packed[:, c * LANES:(c + 1) * LANES]


def _slab(ref, token):
    return ref.at[pl.ds(pl.multiple_of(token * SLAB, SLAB), SLAB)]


def _slab_run(ref, token, count):
    return ref.at[pl.ds(pl.multiple_of(token * SLAB, SLAB), count * SLAB)]


def _mm_kernel(*refs, has_res, has_scale, transposed):
    x_ref, w_ref = refs[:2]
    rest = list(refs[2:])
    r_ref = rest.pop(0) if has_res else None
    s_ref = rest.pop(0) if has_scale else None
    o_ref, wbf = rest

    @pl.when(pl.program_id(1) == 0)
    def _():
        if transposed and wbf.shape[0] != w_ref.shape[0]:
            wbf[...] = jnp.zeros_like(wbf)
            wbf[0:w_ref.shape[0], :] = w_ref[...].astype(BF16)
        else:
            wbf[...] = w_ref[...].astype(BF16)

    acc = _dot_nt(x_ref[...], wbf[...]) if transposed else _dot(x_ref[...], wbf[...])
    if s_ref is not None:
        acc = acc * s_ref[...]
    if r_ref is not None:
        acc = acc + r_ref[...]
    o_ref[...] = acc.astype(o_ref.dtype)


def _matmul(x, w, *, layer=0, col0=0, ncols=None, tn=512, tm=512, residual=None, colscale=None,
            out_dtype=F32, name="mm"):
    m, k = x.shape
    ncols = w.shape[-1] - col0 if ncols is None else ncols
    assert ncols % tn == 0 and col0 % tn == 0 and m % tm == 0
    jb = col0 // tn
    if w.ndim == 3:
        w_spec = pl.BlockSpec((None, k, tn), lambda j, i: (layer, 0, jb + j))
    else:
        w_spec = pl.BlockSpec((k, tn), lambda j, i: (0, jb + j))
    in_specs = [pl.BlockSpec((tm, k), lambda j, i: (i, 0)), w_spec]
    args = [x, w]
    if residual is not None:
        in_specs.append(pl.BlockSpec((tm, tn), lambda j, i: (i, j)))
        args.append(residual)
    if colscale is not None:
        in_specs.append(pl.BlockSpec((1, tn), lambda j, i: (0, j)))
        args.append(colscale)
    return pl.pallas_call(
        functools.partial(_mm_kernel, has_res=residual is not None, has_scale=colscale is not None,
                          transposed=False),
        out_shape=jax.ShapeDtypeStruct((m, ncols), out_dtype),
        grid=(ncols // tn, m // tm),
        in_specs=in_specs,
        out_specs=pl.BlockSpec((tm, tn), lambda j, i: (i, j)),
        scratch_shapes=[pltpu.VMEM((k, tn), BF16)],
        compiler_params=_cparams(2),
        name=name,
    )(*args)


def _matmul_t(x, wt, *, layer, row0=0, nrows=None, tn=512, tm=512, colscale=None, out_dtype=F32, name="mm_t"):
    m, k = x.shape
    nrows = wt.shape[1] - row0 if nrows is None else nrows
    assert nrows % tn == 0 and row0 % tn == 0 and m % tm == 0 and tn % 8 == 0
    assert tn % LANES == 0 or nrows == tn
    tn_out = max(tn, LANES)
    jb = row0 // tn
    in_specs = [pl.BlockSpec((tm, k), lambda j, i: (i, 0)),
                pl.BlockSpec((None, tn, k), lambda j, i: (layer, jb + j, 0))]
    args = [x, wt]
    if colscale is not None:
        in_specs.append(pl.BlockSpec((1, tn_out), lambda j, i: (0, j)))
        args.append(colscale)
    return pl.pallas_call(
        functools.partial(_mm_kernel, has_res=False, has_scale=colscale is not None, transposed=True),
        out_shape=jax.ShapeDtypeStruct((m, (nrows // tn) * tn_out), out_dtype),
        grid=(nrows // tn, m // tm),
        in_specs=in_specs,
        out_specs=pl.BlockSpec((tm, tn_out), lambda j, i: (i, j)),
        scratch_shapes=[pltpu.VMEM((tn_out, k), BF16)],
        compiler_params=_cparams(2),
        name=name,
    )(*args)


def _pool_init(w_ref, xbuf, sa, sb, wbf):
    cur0 = POOL_PAD + POOL_HALO
    xbuf[0:cur0, :] = jnp.zeros((cur0, xbuf.shape[1]), F32)
    sa[0:POOL_PAD, :] = jnp.zeros((POOL_PAD, sa.shape[1]), F32)
    sb[0:POOL_PAD, :] = jnp.zeros((POOL_PAD, sb.shape[1]), F32)
    wbf[...] = w_ref[...].astype(BF16)


def _pool_tile(h, i, g_ref, scale_ref, o_ref, xbuf, sa, sb, wbf, tm, pg):
    lo = POOL_PAD
    cur0 = POOL_PAD + POOL_HALO
    end = cur0 + tm
    hn = _rms(h, g_ref[...])
    xbuf[cur0:end, :] = hn
    row = i * tm + lax.broadcasted_iota(I32, (tm, 1), 0)
    for gi, w in enumerate(POOL_WINDOWS):
        c0 = gi * pg
        cols = slice(c0, c0 + pg)
        src, shift, spare = None, 1, [sa, sb]
        while True:
            first = lo if 2 * shift < w else cur0
            if src is None:
                summed = xbuf[first:end, cols] + xbuf[first - shift:end - shift, cols]
            else:
                summed = src[first:end, :] + src[first - shift:end - shift, :]
            shift *= 2
            if shift == w:
                break
            dst = spare[0] if src is not spare[0] else spare[1]
            dst[first:end, :] = summed
            src = dst
        cur = hn[:, cols]
        cnt = jnp.minimum(row + 1, w).astype(F32)
        pooled = summed / cnt - cur
        mix = _dot(pooled.astype(BF16), wbf[gi]) * scale_ref[:, cols]
        o_ref[:, cols] = h[:, cols] + mix
    xbuf[lo:cur0, :] = xbuf[end - POOL_HALO:end, :]


def _route_kernel(h_ref, g_ref, whl_ref, b_ref, tri_ref, hn_ref, ri_ref, rg_ref, cnt_ref, run_ref, *, tm):
    @pl.when(pl.program_id(0) == 0)
    def _():
        run_ref[...] = jnp.zeros_like(run_ref)

    _route_tile(h_ref[...], g_ref, whl_ref, b_ref, tri_ref, hn_ref, ri_ref, rg_ref, cnt_ref, run_ref, tm)


def _route_tile(h, g_ref, whl_ref, b_ref, tri_ref, hn_ref, ri_ref, rg_ref, cnt_ref, run_ref, tm):
    hn = _rms(h, g_ref[...])
    hn_bf = hn.astype(BF16)
    _store_token_slabs(hn_ref, _pack_rows(hn))
    hn_lo = (hn - hn_bf.astype(F32)).astype(BF16)
    both = _dot(hn_bf, whl_ref[...])
    logits = both[:, 0:LANES] + both[:, LANES:2 * LANES] + _dot(hn_lo, whl_ref[:, 0:LANES]) + b_ref[...]

    lane = lax.broadcasted_iota(I32, (tm, LANES), 1)
    lane_f = lane.astype(F32)
    neg = -jnp.inf
    big = float(LANES)
    gl = jnp.where(lane < MOE_GROUPS, logits, neg)
    gmax = jnp.max(gl, axis=-1, keepdims=True)
    gsel = jnp.min(jnp.where(gl == gmax, lane_f, big), axis=-1, keepdims=True).astype(I32)
    p_grp = 1.0 / jnp.sum(jnp.exp(gl - gmax), axis=-1, keepdims=True)
    lo_lane = MOE_GROUPS + MOE_EPG * gsel
    el = jnp.where((lane >= lo_lane) & (lane < lo_lane + MOE_EPG), logits, neg)
    v0 = jnp.max(el, axis=-1, keepdims=True)
    i0 = jnp.min(jnp.where(el == v0, lane_f, big), axis=-1, keepdims=True).astype(I32)
    el2 = jnp.where(lane == i0, neg, el)
    v1 = jnp.max(el2, axis=-1, keepdims=True)
    i1 = jnp.min(jnp.where(el2 == v1, lane_f, big), axis=-1, keepdims=True).astype(I32)
    t = jnp.exp(v1 - v0)
    gate0 = p_grp / (1.0 + t)
    gate1 = p_grp * t / (1.0 + t)
    e0 = i0 - MOE_GROUPS
    e1 = i1 - MOE_GROUPS

    onehot = ((lane == e0) | (lane == e1))
    c_bf = jnp.where(onehot, 1.0, 0.0).astype(BF16)
    prefix = _dot(tri_ref[...], c_bf) + run_ref[...]
    rank0 = jnp.sum(jnp.where(lane == e0, prefix, 0.0), axis=-1, keepdims=True).astype(I32)
    rank1 = jnp.sum(jnp.where(lane == e1, prefix, 0.0), axis=-1, keepdims=True).astype(I32)
    run = run_ref[...] + jnp.sum(jnp.where(onehot, 1.0, 0.0), axis=0, keepdims=True)
    run_ref[...] = run
    cnt_ref[...] = jnp.broadcast_to(run, cnt_ref.shape)

    ri = jnp.where(lane == 0, e0, jnp.where(lane == 1, e1, jnp.where(lane == 2, rank0, rank1)))
    ri_ref[...] = ri.T[0:8, :]
    rg_ref[...] = jnp.where(lane == 0, gate0, gate1)


def _route_operands(layer, group_w, group_b, router_w, router_b, tm):
    d = group_w.shape[1]
    pad = LANES - MOE_GROUPS - MOE_EXPERTS
    wr = jnp.concatenate([group_w[layer], router_w[layer], jnp.zeros((d, pad), F32)], axis=1)
    br = jnp.concatenate([group_b[layer], router_b[layer], jnp.zeros((pad,), F32)]).reshape(1, LANES)
    whi = wr.astype(BF16)
    wlo = (wr - whi.astype(F32)).astype(BF16)
    whl = jnp.concatenate([whi, wlo], axis=1)
    tri = jnp.tril(jnp.ones((tm, tm), BF16), -1)
    return whl, br, tri


def _route_out(n, tm):
    shapes = (jax.ShapeDtypeStruct((n * SLAB, LANES), U32), jax.ShapeDtypeStruct((8, n), I32),
              jax.ShapeDtypeStruct((n, LANES), F32), jax.ShapeDtypeStruct((8, LANES), F32))
    specs = (pl.BlockSpec((tm * SLAB, LANES), lambda i: (i, 0)), pl.BlockSpec((8, tm), lambda i: (0, i)),
             pl.BlockSpec((tm, LANES), lambda i: (i, 0)), pl.BlockSpec((8, LANES), lambda i: (0, 0)))
    return shapes, specs


def _pool_route_kernel(*refs, with_combine, t, pg):
    refs = list(refs)
    if with_combine:
        dest_ref, dest_next_ref, rgp_ref, ys_ref = refs[:4]
        refs = refs[4:]
    h_ref, gmix_ref, pw_ref, scale_ref, gffn_ref, whl_ref, b_ref, tri_ref = refs[:8]
    o_ref, hn_ref, ri_ref, rg_ref, cnt_ref = refs[8:13]
    refs = refs[13:]
    if with_combine:
        buf00, buf01, buf10, buf11, sems, hbuf = refs[:6]
        refs = refs[6:]
    xbuf, sa, sb, wbf, run_ref = refs
    i = pl.program_id(0)

    @pl.when(i == 0)
    def _():
        _pool_init(pw_ref, xbuf, sa, sb, wbf)
        run_ref[...] = jnp.zeros_like(run_ref)

    if with_combine:
        _combine_tile(dest_ref, dest_next_ref, h_ref, rgp_ref, ys_ref, hbuf, ((buf00, buf01), (buf10, buf11)), sems, t)
        h = hbuf[...]
    else:
        h = h_ref[...]
    _pool_tile(h, i, gmix_ref, scale_ref, o_ref, xbuf, sa, sb, wbf, t, pg)
    _route_tile(o_ref[...], gffn_ref, whl_ref, b_ref, tri_ref, hn_ref, ri_ref, rg_ref, cnt_ref, run_ref, t)


def _pool_route(h, pending, g_mix, pool_w, pool_scale, j, g_ffn, route_ops, t=MOE_TOK_TILE):
    n, d = h.shape
    nw, pg, _ = pool_w.shape[1:]
    whl, br, tri = route_ops
    assert d == 2 * SLAB * LANES
    last = n // t - 1
    row = lambda i: (i, 0)
    const = lambda i: (0, 0)
    in_specs, args, scratch = [], [], []
    if pending is not None:
        rg_prev, ys, dest = pending
        in_specs += [pl.BlockSpec((None, 1, 2 * t), lambda i: (i, 0, 0), memory_space=pltpu.SMEM),
                     pl.BlockSpec((None, 1, 2 * t), lambda i: (jnp.minimum(i + 1, last), 0, 0),
                                  memory_space=pltpu.SMEM),
                     pl.BlockSpec((t, LANES), row),
                     pl.BlockSpec(memory_space=pl.ANY)]
        args += [dest, dest, rg_prev, ys]
        scratch += [pltpu.VMEM((t * SLAB, LANES), U32)] * 4 + [pltpu.SemaphoreType.DMA((2,)), pltpu.VMEM((t, d), F32)]
    in_specs += [pl.BlockSpec((t, d), row), pl.BlockSpec((1, d), const),
                 pl.BlockSpec((None, nw, pg, pg), lambda i: (j, 0, 0, 0)), pl.BlockSpec((1, d), const),
                 pl.BlockSpec((1, d), const), pl.BlockSpec((d, 2 * LANES), const),
                 pl.BlockSpec((1, LANES), const), pl.BlockSpec((t, t), const)]
    args += [h, g_mix.reshape(1, d), pool_w, pool_scale[j].reshape(1, d), g_ffn.reshape(1, d), whl, br, tri]
    r_shapes, r_specs = _route_out(n, t)
    scratch += [pltpu.VMEM((POOL_PAD + POOL_HALO + t, d), F32),
                pltpu.VMEM((POOL_PAD + POOL_HALO + t, pg), F32),
                pltpu.VMEM((POOL_PAD + POOL_HALO + t, pg), F32),
                pltpu.VMEM((nw, pg, pg), BF16),
                pltpu.VMEM((1, LANES), F32)]
    return pl.pallas_call(
        functools.partial(_pool_route_kernel, with_combine=pending is not None, t=t, pg=pg),
        out_shape=(jax.ShapeDtypeStruct((n, d), F32),) + r_shapes,
        grid=(n // t,),
        in_specs=in_specs,
        out_specs=(pl.BlockSpec((t, d), row),) + r_specs,
        scratch_shapes=scratch,
        compiler_params=_cparams(1),
        name="pool_route",
    )(*args)


def _route(h, g, route_ops, tm=ROUTE_TILE):
    n, d = h.shape
    whl, br, tri = route_ops
    return pl.pallas_call(
        functools.partial(_route_kernel, tm=tm),
        out_shape=(jax.ShapeDtypeStruct((n * SLAB, LANES), U32),
                   jax.ShapeDtypeStruct((8, n), I32),
                   jax.ShapeDtypeStruct((n, LANES), F32),
                   jax.ShapeDtypeStruct((8, LANES), F32)),
        grid=(n // tm,),
        in_specs=[pl.BlockSpec((tm, d), lambda i: (i, 0)),
                  pl.BlockSpec((1, d), lambda i: (0, 0)),
                  pl.BlockSpec((d, 2 * LANES), lambda i: (0, 0)),
                  pl.BlockSpec((1, LANES), lambda i: (0, 0)),
                  pl.BlockSpec((tm, tm), lambda i: (0, 0))],
        out_specs=(pl.BlockSpec((tm * SLAB, LANES), lambda i: (i, 0)),
                   pl.BlockSpec((8, tm), lambda i: (0, i)),
                   pl.BlockSpec((tm, LANES), lambda i: (i, 0)),
                   pl.BlockSpec((8, LANES), lambda i: (0, 0))),
        scratch_shapes=[pltpu.VMEM((1, LANES), F32)],
        compiler_params=_cparams(1),
        name="moe_route",
    )(h, g.reshape(1, d), whl, br, tri)


def _dispatch_kernel(pad_start_ref, pad_len_ref, nused_ref, dest_ref, hn_ref, xs_ref, zbuf, sem, zsem, *, t):
    sizes = [1 << s for s in reversed(range(MOE_ROWS.bit_length() - 1))]

    @pl.when(pl.program_id(0) == 0)
    def _():
        zbuf[...] = jnp.zeros_like(zbuf)

        def zero_copy(off, size):
            return pltpu.make_async_copy(zbuf.at[pl.ds(0, size * SLAB)], _slab_run(xs_ref, off, size), zsem)

        def gaps(e, wait):
            off = pad_start_ref[e]
            n = pad_len_ref[e]
            for size in sizes:
                hit = (n & size) != 0

                @pl.when(hit)
                def _():
                    c = zero_copy(off, size)
                    c.wait() if wait else c.start()

                off = off + (n & size)

        def start_gaps(e, c):
            gaps(e, False)
            return c

        def wait_gaps(e, c):
            gaps(e, True)
            return c

        lax.fori_loop(0, MOE_EXPERTS, start_gaps, 0)
        lax.fori_loop(0, MOE_EXPERTS, wait_gaps, 0)

        zrows = zbuf.shape[0] // SLAB
        n_blocks = xs_ref.shape[0] // (MOE_ROWS * SLAB)

        def tail(wait):
            def body(b, c):
                for part in range(MOE_ROWS // zrows):
                    cp = zero_copy(b * MOE_ROWS + part * zrows, zrows)
                    cp.wait() if wait else cp.start()
                return c
            lax.fori_loop(nused_ref[0], n_blocks, body, 0)

        tail(False)
        tail(True)

    def row_copy(r, k):
        return pltpu.make_async_copy(_slab(hn_ref, r), _slab(xs_ref, dest_ref[0, k * t + r]), sem)

    def start(r, c):
        row_copy(r, 0).start(priority=0)
        row_copy(r, 1).start(priority=1)
        return c

    lax.fori_loop(0, t, start, 0, unroll=8)
    for _ in range(2):
        pltpu.make_async_copy(hn_ref, xs_ref.at[pl.ds(0, t * SLAB)], sem).wait()


def _dispatch(hn, dest, pad_start, pad_len, n_used, n_rows, t=MOE_TOK_TILE):
    n = hn.shape[0] // SLAB
    grid_spec = pltpu.PrefetchScalarGridSpec(
        num_scalar_prefetch=3,
        grid=(n // t,),
        in_specs=[pl.BlockSpec((None, 1, 2 * t), lambda i, *_: (i, 0, 0), memory_space=pltpu.SMEM),
                  pl.BlockSpec((t * SLAB, LANES), lambda i, *_: (i, 0))],
        out_specs=pl.BlockSpec(memory_space=pl.ANY),
        scratch_shapes=[pltpu.VMEM((MOE_ROWS // 2 * SLAB, LANES), hn.dtype),
                        pltpu.SemaphoreType.DMA(()), pltpu.SemaphoreType.DMA(())],
    )
    return pl.pallas_call(
        functools.partial(_dispatch_kernel, t=t),
        out_shape=jax.ShapeDtypeStruct((n_rows * SLAB, LANES), hn.dtype),
        grid_spec=grid_spec,
        compiler_params=_cparams(1),
        name="moe_dispatch",
    )(pad_start, pad_len, n_used, dest, hn)


def _expert_kernel(bexp_ref, first_ref, slot_ref, ahead_ref, second_ref, nused_ref, xs_ref, wg_hbm, wu_hbm, wd_hbm,
                   o_ref, wg_raw, wu_raw, wd_raw, wg_bf, wu_bf, wd_bf, sems, *, layer):
    b = pl.program_id(0)
    active = b < nused_ref[0]

    def fetch(e, slot):
        return [pltpu.make_async_copy(hbm.at[layer, e], raw.at[slot], sems.at[slot, k])
                for k, (hbm, raw) in enumerate(((wg_hbm, wg_raw), (wu_hbm, wu_raw), (wd_hbm, wd_raw)))]

    @pl.when(b == 0)
    def _():
        for c in fetch(bexp_ref[0], 0):
            c.start(priority=1)

        @pl.when(second_ref[0] >= 0)
        def _():
            for c in fetch(second_ref[0], 1):
                c.start(priority=1)

    @pl.when(active & (first_ref[b] == 1))
    def _():
        slot = slot_ref[b]
        ahead = ahead_ref[b]

        @pl.when(ahead >= 0)
        def _():
            for c in fetch(ahead, (slot + MOE_WSLOTS - 1) % MOE_WSLOTS):
                c.start(priority=1)

        for c in fetch(bexp_ref[b], slot):
            c.wait()
        wg_bf[...] = wg_raw[slot].astype(BF16)
        wu_bf[...] = wu_raw[slot].astype(BF16)
        wd_bf[...] = wd_raw[slot].astype(BF16)

    @pl.when(active)
    def _():
        rows = xs_ref.shape[0] // SLAB
        halves = [_unpack_rows(u) for u in _load_token_slabs(xs_ref, rows)]
        x = jnp.concatenate([lo.astype(BF16) for lo, _ in halves] + [hi.astype(BF16) for _, hi in halves], axis=1)
        gte = _dot(x, wg_bf[...])
        up = _dot(x, wu_bf[...])
        hb = _silu(gte) * up
        _store_token_slabs(o_ref, _pack_rows(_dot(hb.astype(BF16), wd_bf[...])))

    @pl.when(jnp.logical_not(active))
    def _():
        o_ref[...] = jnp.zeros_like(o_ref)


def _experts(xs, tables, gate_w, up_w, down_w, layer, rows=MOE_ROWS):
    n_rows = xs.shape[0] // SLAB
    d = gate_w.shape[-2]
    hid = gate_w.shape[-1]
    assert d == 2 * SLAB * LANES and MOE_WSLOTS == 3
    x_map = lambda b, be, fi, sl, ah, se, nu: (jnp.minimum(b, nu[0] - 1), 0)
    grid_spec = pltpu.PrefetchScalarGridSpec(
        num_scalar_prefetch=6,
        grid=(n_rows // rows,),
        in_specs=[pl.BlockSpec((rows * SLAB, LANES), x_map),
                  pl.BlockSpec(memory_space=pl.ANY),
                  pl.BlockSpec(memory_space=pl.ANY),
                  pl.BlockSpec(memory_space=pl.ANY)],
        out_specs=pl.BlockSpec((rows * SLAB, LANES), lambda b, *_: (b, 0)),
        scratch_shapes=[pltpu.VMEM((MOE_WSLOTS, d, hid), F32), pltpu.VMEM((MOE_WSLOTS, d, hid), F32),
                        pltpu.VMEM((MOE_WSLOTS, hid, d), F32),
                        pltpu.VMEM((d, hid), BF16), pltpu.VMEM((d, hid), BF16), pltpu.VMEM((hid, d), BF16),
                        pltpu.SemaphoreType.DMA((MOE_WSLOTS, 3))],
    )
    return pl.pallas_call(
        functools.partial(_expert_kernel, layer=layer),
        out_shape=jax.ShapeDtypeStruct((n_rows * SLAB, LANES), U32),
        grid_spec=grid_spec,
        compiler_params=_cparams(1),
        name="moe_experts",
    )(*tables, xs, gate_w, up_w, down_w)


def _combine_kernel(*refs, t, mode):
    dest_ref, dest_next_ref, h_ref, rg_ref, ys_ref, g_ref, o_ref = refs[:7]
    rest = list(refs[7:])
    hn_ref = rest.pop(0) if mode == "norm_bf16" else None
    buf00, buf01, buf10, buf11, sems = rest
    _combine_tile(dest_ref, dest_next_ref, h_ref, rg_ref, ys_ref, o_ref, ((buf00, buf01), (buf10, buf11)), sems, t)
    normed = _rms(o_ref[...], g_ref[...])
    if mode == "norm_bf16":
        hn_ref[...] = normed.astype(BF16)
    else:
        o_ref[...] = normed


def _combine_tile(dest_ref, dest_next_ref, h_ref, rg_ref, ys_ref, o_ref, bufs, sems, t):
    i = pl.program_id(0)
    n = pl.num_programs(0)
    half = SLAB * LANES

    def row_copy(dref, slot, r, k):
        return pltpu.make_async_copy(_slab(ys_ref, dref[0, k * t + r]), _slab(bufs[slot][k], r), sems.at[slot])

    def issue(dref, slot):
        def start(r, c):
            row_copy(dref, slot, r, 0).start(priority=0)
            row_copy(dref, slot, r, 1).start(priority=1)
            return c
        lax.fori_loop(0, t, start, 0, unroll=8)

    def drain(slot):
        for k in range(2):
            pltpu.make_async_copy(ys_ref.at[pl.ds(0, t * SLAB)], bufs[slot][k], sems.at[slot]).wait()

    @pl.when(i == 0)
    def _():
        issue(dest_ref, 0)

    for slot in (0, 1):
        @pl.when((i % 2 == slot) & (i + 1 < n))
        def _():
            issue(dest_next_ref, 1 - slot)

        @pl.when(i % 2 == slot)
        def _():
            drain(slot)
            rg = rg_ref[...]
            g0 = rg[:, 0:1]
            g1 = rg[:, 1:2]
            for c, (ua, ub) in enumerate(zip(_load_token_slabs(bufs[slot][0], t), _load_token_slabs(bufs[slot][1], t))):
                a_lo, a_hi = _unpack_rows(ua)
                b_lo, b_hi = _unpack_rows(ub)
                lo_cols = slice(c * LANES, (c + 1) * LANES)
                hi_cols = slice(half + c * LANES, half + (c + 1) * LANES)
                o_ref[:, lo_cols] = h_ref[:, lo_cols] + (g0 * a_lo + g1 * b_lo)
                o_ref[:, hi_cols] = h_ref[:, hi_cols] + (g0 * a_hi + g1 * b_hi)


def _combine(h, rg, ys, dest, g_next, mode, t=MOE_TOK_TILE):
    n, d = h.shape
    assert d == 2 * SLAB * LANES
    last = n // t - 1
    in_specs = [pl.BlockSpec((None, 1, 2 * t), lambda i: (i, 0, 0), memory_space=pltpu.SMEM),
                pl.BlockSpec((None, 1, 2 * t), lambda i: (jnp.minimum(i + 1, last), 0, 0),
                             memory_space=pltpu.SMEM),
                pl.BlockSpec((t, d), lambda i: (i, 0)),
                pl.BlockSpec((t, LANES), lambda i: (i, 0)),
                pl.BlockSpec(memory_space=pl.ANY),
                pl.BlockSpec((1, d), lambda i: (0, 0))]
    args = [dest, dest, h, rg, ys, g_next.reshape(1, d)]
    tile = pl.BlockSpec((t, d), lambda i: (i, 0))
    assert mode in ("norm_bf16", "final")
    if mode == "norm_bf16":
        out_shape = (jax.ShapeDtypeStruct((n, d), F32), jax.ShapeDtypeStruct((n, d), BF16))
        out_specs = (tile, tile)
    else:
        out_shape = jax.ShapeDtypeStruct((n, d), F32)
        out_specs = tile
    return pl.pallas_call(
        functools.partial(_combine_kernel, t=t, mode=mode),
        out_shape=out_shape,
        grid=(n // t,),
        in_specs=in_specs,
        out_specs=out_specs,
        scratch_shapes=[pltpu.VMEM((t * SLAB, LANES), U32)] * 4 + [pltpu.SemaphoreType.DMA((2,))],
        compiler_params=_cparams(1),
        name="moe_combine",
    )(*args)


def _moe_experts_path(hn, ri, cnt, layer, gate_w, up_w, down_w):
    n = hn.shape[0] // SLAB
    t = MOE_TOK_TILE
    counts = cnt[0, :MOE_EXPERTS].astype(I32)
    nblk = (counts + MOE_ROWS - 1) // MOE_ROWS
    blk_end = jnp.cumsum(nblk)
    row_start = (blk_end - nblk) * MOE_ROWS
    n_rows = -(-(n * 2 + MOE_EXPERTS * (MOE_ROWS - 1)) // MOE_ROWS) * MOE_ROWS
    n_blocks = n_rows // MOE_ROWS
    experts = ri[0:2, :]
    ranks = ri[2:4, :]
    onehot = experts[:, :, None] == jnp.arange(MOE_EXPERTS, dtype=I32)
    dest = jnp.sum(jnp.where(onehot, row_start, 0), axis=-1) + ranks
    dest = dest.reshape(2, n // t, t).transpose(1, 0, 2).reshape(n // t, 1, 2 * t).astype(I32)
    blocks = jnp.arange(n_blocks, dtype=I32)
    block_exp = jnp.minimum(jnp.sum(blk_end[None, :] <= blocks[:, None], axis=1), MOE_EXPERTS - 1).astype(I32)
    n_used = blk_end[-1:].astype(I32)
    eids = jnp.arange(MOE_EXPERTS, dtype=I32)
    nonempty = nblk > 0
    ordinal = jnp.cumsum(nonempty.astype(I32)) - 1
    holds = nonempty[None, :] & (ordinal[None, :] == eids[:, None])
    expert_at = jnp.where(jnp.any(holds, axis=1), jnp.sum(jnp.where(holds, eids[None, :], 0), axis=1), -1)
    expert_at = jnp.concatenate([expert_at, jnp.full((MOE_WSLOTS,), -1, I32)]).astype(I32)
    is_first = ((blocks == (blk_end - nblk)[block_exp]) & (blocks < n_used[0])).astype(I32)
    pos = ordinal[block_exp]
    tables = (block_exp, is_first, (pos % MOE_WSLOTS).astype(I32), expert_at[pos + MOE_WSLOTS - 1],
              expert_at[1:2], n_used)
    pad_start = (row_start + counts).astype(I32)
    pad_len = (nblk * MOE_ROWS - counts).astype(I32)

    xs = _dispatch(hn, dest, pad_start, pad_len, n_used, n_rows)
    return _experts(xs, tables, gate_w, up_w, down_w, layer), dest


def _ssd_kernel(z_ref, x_ref, b_ref, c_ref, dt_ref, cwx_ref, cwb_ref, cwc_ref, cbx_ref, cbb_ref, cbc_ref,
                dtb_ref, a_ref, dexp_ref, nw_ref, exp_ref, tri_ref, o_ref,
                xbuf, bbuf, cbuf, xc_s, xdt_s, xw_s, b_s, c_s, eacs_s, acs_s, acst_s, y_s, st_s,
                *, L, hp, n_state, groups):
    ci = pl.program_id(0)
    d_inner = x_ref.shape[1]
    gw = d_inner // groups
    kh = gw // hp
    H0 = CONV_HALO

    @pl.when(ci == 0)
    def _():
        xbuf[0:H0, :] = jnp.zeros((H0, xbuf.shape[1]), F32)
        bbuf[0:H0, :] = jnp.zeros((H0, bbuf.shape[1]), F32)
        cbuf[0:H0, :] = jnp.zeros((H0, cbuf.shape[1]), F32)
        st_s[...] = jnp.zeros_like(st_s)

    def conv_silu(in_ref, buf, cw_ref, cb_ref):
        buf[H0:H0 + L, :] = in_ref[...].astype(F32)
        acc = cb_ref[...] + cw_ref[SSD_CONV - 1:SSD_CONV, :] * buf[H0:H0 + L, :]
        for j in range(SSD_CONV - 1):
            off = H0 - (SSD_CONV - 1) + j
            acc = acc + cw_ref[j:j + 1, :] * buf[off:off + L, :]
        buf[0:H0, :] = buf[L:L + H0, :]
        return _silu(acc)

    xc_s[...] = conv_silu(x_ref, xbuf, cwx_ref, cbx_ref)
    b_s[...] = conv_silu(b_ref, bbuf, cwb_ref, cbb_ref).astype(BF16)
    c_s[...] = conv_silu(c_ref, cbuf, cwc_ref, cbc_ref).astype(BF16)

    dt = _softplus(dt_ref[...] + dtb_ref[...])
    d_a = dt * a_ref[...]
    acs = _dot3_lhs_exact(tri_ref[...], d_a)
    acs_s[...] = acs
    acst_s[...] = acs.T
    emat = exp_ref[...]
    acs_last = acs[L - 1:L, :]
    to_end = dt * jnp.exp(acs_last - acs)
    xc = xc_s[...]
    xdt_s[...] = (xc * _dot2_rhs_exact(dt, emat)).astype(BF16)
    xw_s[...] = (xc * _dot2_rhs_exact(to_end, emat)).astype(BF16)
    eacs_s[...] = _dot2_rhs_exact(jnp.exp(acs), emat)
    cdecay = _dot2_rhs_exact(jnp.broadcast_to(jnp.exp(acs_last), (SLAB, LANES)), emat)[0:1, :]

    rr = lax.broadcasted_iota(I32, (L, L), 0)
    cc = lax.broadcasted_iota(I32, (L, L), 1)
    causal = rr >= cc
    lane2 = lax.broadcasted_iota(I32, (L, 2 * hp), 1)

    for g in range(groups):
        bg = b_s[:, g * n_state:(g + 1) * n_state]
        cg = c_s[:, g * n_state:(g + 1) * n_state]
        cb = _dot_nt(cg, bg)
        st_prev = st_s[g]
        y_off = _dot(cg, st_prev.astype(BF16)) * eacs_s[:, g * gw:(g + 1) * gw]
        for kp in range(kh // 2):
            c0 = g * gw + kp * 2 * hp
            slab = xdt_s[:, c0:c0 + 2 * hp]
            ms, xs = [], []
            for half in range(2):
                hd = g * kh + kp * 2 + half
                seg = acs_s[:, hd:hd + 1] - acst_s[hd:hd + 1, :]
                decay = jnp.exp(jnp.where(causal, seg, -jnp.inf))
                ms.append((cb * decay).astype(BF16))
                keep = (lane2 < hp) if half == 0 else (lane2 >= hp)
                xs.append(jnp.where(keep, slab, jnp.zeros_like(slab)))
            y_pair = _dot(jnp.concatenate(ms, axis=1), jnp.concatenate(xs, axis=0))
            y_s[:, c0:c0 + 2 * hp] = y_pair + y_off[:, kp * 2 * hp:(kp + 1) * 2 * hp]
        bgt = bg.astype(F32).T.astype(BF16)
        s_new = _dot(bgt, xw_s[:, g * gw:(g + 1) * gw])
        st_s[g] = cdecay[:, g * gw:(g + 1) * gw] * st_prev + s_new

    y = y_s[...] + dexp_ref[...] * xc_s[...]
    z = z_ref[...].astype(F32)
    y = y * _silu(z)
    for g in range(groups):
        yg = y[:, g * gw:(g + 1) * gw]
        yg = yg * lax.rsqrt(jnp.mean(yg * yg, axis=-1, keepdims=True) + RMS_EPS)
        o_ref[:, g * gw:(g + 1) * gw] = (yg * nw_ref[:, g * gw:(g + 1) * gw]).astype(o_ref.dtype)


def _ssd_scan(zx, dt_raw, conv_w, conv_b, dt_bias, a_log, d_skip, norm_w, d_inner):
    s = zx.shape[0]
    L = SSD_CHUNK
    hp, n_state, groups = SSD_HEAD_DIM, SSD_STATE, SSD_GROUPS
    heads = d_inner // hp
    gn = groups * n_state
    gw = d_inner // groups
    assert d_inner % gn == 0 and heads <= LANES
    pad = LANES - heads
    dtb = jnp.pad(dt_bias, (0, pad)).reshape(1, LANES)
    a_neg = jnp.pad(-jnp.exp(a_log), (0, pad)).reshape(1, LANES)
    dexp = jnp.repeat(d_skip, hp).reshape(1, d_inner)
    emat = (jnp.arange(LANES)[:, None] == (jnp.arange(d_inner)[None, :] // hp)).astype(BF16)
    tri = jnp.tril(jnp.ones((L, L), BF16))
    cb2 = conv_b.reshape(1, -1)
    xb = d_inner // gn
    row = lambda c: (c, 0)
    const = lambda c: (0, 0)
    in_specs = [
        pl.BlockSpec((L, d_inner), row),
        pl.BlockSpec((L, d_inner), lambda c: (c, 1)),
        pl.BlockSpec((L, gn), lambda c: (c, 2 * xb)),
        pl.BlockSpec((L, gn), lambda c: (c, 2 * xb + 1)),
        pl.BlockSpec((L, LANES), row),
        pl.BlockSpec((SSD_CONV, d_inner), const),
        pl.BlockSpec((SSD_CONV, gn), lambda c: (0, xb)),
        pl.BlockSpec((SSD_CONV, gn), lambda c: (0, xb + 1)),
        pl.BlockSpec((1, d_inner), const),
        pl.BlockSpec((1, gn), lambda c: (0, xb)),
        pl.BlockSpec((1, gn), lambda c: (0, xb + 1)),
        pl.BlockSpec((1, LANES), const),
        pl.BlockSpec((1, LANES), const),
        pl.BlockSpec((1, d_inner), const),
        pl.BlockSpec((1, d_inner), const),
        pl.BlockSpec((LANES, d_inner), const),
        pl.BlockSpec((L, L), const),
    ]
    scratch = [
        pltpu.VMEM((CONV_HALO + L, d_inner), F32), pltpu.VMEM((CONV_HALO + L, gn), F32),
        pltpu.VMEM((CONV_HALO + L, gn), F32),
        pltpu.VMEM((L, d_inner), F32),
        pltpu.VMEM((L, d_inner), BF16),
        pltpu.VMEM((L, d_inner), BF16),
        pltpu.VMEM((L, gn), BF16), pltpu.VMEM((L, gn), BF16),
        pltpu.VMEM((L, d_inner), F32),
        pltpu.VMEM((L, LANES), F32), pltpu.VMEM((LANES, L), F32),
        pltpu.VMEM((L, d_inner), F32),
        pltpu.VMEM((groups, n_state, gw), F32),
    ]
    return pl.pallas_call(
        functools.partial(_ssd_kernel, L=L, hp=hp, n_state=n_state, groups=groups),
        out_shape=jax.ShapeDtypeStruct((s, d_inner), BF16),
        grid=(s // L,),
        in_specs=in_specs,
        out_specs=pl.BlockSpec((L, d_inner), row),
        scratch_shapes=scratch,
        compiler_params=_cparams(1),
        name="ssd_scan",
    )(zx, zx, zx, zx, dt_raw, conv_w, conv_w, conv_w, cb2, cb2, cb2, dtb, a_neg, dexp,
      norm_w.reshape(1, d_inner), emat, tri)


def _ssd_mixer(h, hn, j, in_w, conv_w, conv_b, dt_bias, a_log, d_skip, norm_w, out_w):
    d_inner = out_w.shape[1]
    heads = d_inner // SSD_HEAD_DIM
    wide = in_w.shape[2] - heads
    wt = jnp.swapaxes(in_w, 1, 2)
    zx = _matmul_t(hn, wt, layer=j, row0=0, nrows=wide, tn=MM_TN, tm=MM_TM, out_dtype=BF16, name="ssd_in_proj")
    dt_raw = _matmul_t(hn, wt, layer=j, row0=wide, nrows=heads, tn=heads, tm=MM_TM, name="ssd_dt_proj")
    y = _ssd_scan(zx, dt_raw, conv_w, conv_b, dt_bias, a_log, d_skip, norm_w, d_inner)
    return _matmul(y, out_w, layer=j, tn=MM_TN // 2, tm=MM_TM, residual=h, name="ssd_out_proj")


def _fox_cum_kernel(f_ref, b_ref, tri_ref, o_ref, carry):
    @pl.when(pl.program_id(0) == 0)
    def _():
        carry[...] = jnp.zeros_like(carry)

    x = f_ref[...] + b_ref[...]
    log_f = jnp.minimum(x, 0.0) - jnp.log1p(jnp.exp(-jnp.abs(x)))
    cs = _dot3_rhs_exact(log_f, tri_ref[...]) + carry[...]
    o_ref[...] = cs
    carry[...] = cs[:, cs.shape[1] - 1:]


def _fox_cum(f_t, bias, tk=512):
    nh, s = f_t.shape
    triu = jnp.triu(jnp.ones((tk, tk), BF16))
    return pl.pallas_call(
        _fox_cum_kernel,
        out_shape=jax.ShapeDtypeStruct((nh, s), F32),
        grid=(s // tk,),
        in_specs=[pl.BlockSpec((nh, tk), lambda i: (0, i)),
                  pl.BlockSpec((nh, 1), lambda i: (0, 0)),
                  pl.BlockSpec((tk, tk), lambda i: (0, 0))],
        out_specs=pl.BlockSpec((nh, tk), lambda i: (0, i)),
        scratch_shapes=[pltpu.VMEM((nh, 1), F32)],
        compiler_params=_cparams(1),
        name="fox_cum",
    )(f_t, bias.reshape(nh, 1), triu)


def _fox_attn_kernel(q_ref, k_ref, v_ref, cum_ref, o_ref, m_s, acc_s, vext, *, tq, tk, sr):
    qi = pl.program_id(1)
    dh = q_ref.shape[1]
    n_sub = tq // sr

    @pl.when(qi == 0)
    def _():
        vext[:, 0:dh] = v_ref[...]
        vext[:, dh:2 * dh] = jnp.ones((vext.shape[0], dh), BF16)

    q0 = pl.multiple_of(qi * tq, tq)
    c_ref = cum_ref[:, pl.ds(q0, LANES)][:, 0:1]
    m_s[...] = jnp.full_like(m_s, -jnp.inf)
    acc_s[...] = jnp.zeros_like(acc_s)

    def sub_step(r, k0, width, bias, diagonal):
        rows = slice(r * sr, (r + 1) * sr)
        s = _dot_nt(q_ref[rows, :], k_ref[pl.ds(k0, width), :]) + bias
        if diagonal:
            keep = lax.broadcasted_iota(I32, (sr, width), 0) >= lax.broadcasted_iota(I32, (sr, width), 1)
            s = jnp.where(keep, s, -jnp.inf)
        m_prev = m_s[rows, :]
        m_new = jnp.maximum(m_prev, jnp.max(s, axis=-1, keepdims=True))
        alpha = jnp.exp2(m_prev - m_new)
        p = jnp.exp2(s - jnp.concatenate([m_new] * (width // dh), axis=1)).astype(BF16)
        pv = _dot(p, vext[pl.ds(k0, width), :])
        acc_s[rows, :] = jnp.concatenate([alpha, alpha], axis=1) * acc_s[rows, :] + pv
        m_s[rows, :] = m_new

    def key_bias(k0, width):
        return (c_ref - cum_ref[:, pl.ds(k0, width)]) * LOG2E

    def body(j, c):
        k0 = pl.multiple_of(j * tk, tk)
        bias = key_bias(k0, tk)
        for r in range(n_sub):
            sub_step(r, k0, tk, bias, False)
        return c

    lax.fori_loop(0, qi * (tq // tk), body, 0)
    for r in range(n_sub):
        if r > 0:
            sub_step(r, q0, r * sr, key_bias(q0, r * sr), False)
        k0 = pl.multiple_of(q0 + r * sr, sr)
        sub_step(r, k0, sr, key_bias(k0, sr), True)
    acc = acc_s[...]
    o_ref[...] = (acc[:, 0:dh] / acc[:, dh:2 * dh]).astype(o_ref.dtype)


def _fox_attention(qkv, cum, nh, tq=FOX_TQ, tk=FOX_TK, sr=FOX_SUB):
    s = qkv.shape[0]
    dh = FOX_HEAD_DIM
    assert tq % tk == 0 and tq % sr == 0 and tk % dh == 0 and dh == LANES
    cum3 = cum.reshape(nh, 1, s)
    return pl.pallas_call(
        functools.partial(_fox_attn_kernel, tq=tq, tk=tk, sr=sr),
        out_shape=jax.ShapeDtypeStruct((s, nh * dh), BF16),
        grid=(nh, s // tq),
        in_specs=[pl.BlockSpec((tq, dh), lambda h, i: (i, h)),
                  pl.BlockSpec((s, dh), lambda h, i: (0, nh + h)),
                  pl.BlockSpec((s, dh), lambda h, i: (0, 2 * nh + h)),
                  pl.BlockSpec((None, 1, s), lambda h, i: (h, 0, 0))],
        out_specs=pl.BlockSpec((tq, dh), lambda h, i: (i, h)),
        scratch_shapes=[pltpu.VMEM((tq, dh), F32), pltpu.VMEM((tq, 2 * dh), F32), pltpu.VMEM((s, 2 * dh), BF16)],
        compiler_params=_cparams(2),
        name="fox_attention",
    )(qkv, qkv, qkv, cum3)


def _fox_mixer(h, hn, j, in_w, f_bias, out_w):
    d = hn.shape[1]
    nh = d // FOX_HEAD_DIM
    q_scale = FOX_HEAD_DIM ** -0.5 * LOG2E
    colscale = jnp.concatenate([jnp.full((1, d), q_scale, F32), jnp.ones((1, 2 * d), F32)], axis=1)
    wt = jnp.swapaxes(in_w, 1, 2)
    qkv = _matmul_t(hn, wt, layer=j, row0=0, nrows=3 * d, tn=MM_TN, tm=MM_TM, out_dtype=BF16, colscale=colscale,
                    name="fox_in_proj")
    f_raw = _matmul_t(hn, wt, layer=j, row0=3 * d, nrows=nh, tn=nh, tm=MM_TM, name="fox_f_proj")
    cum = _fox_cum(f_raw[:, :nh].T, f_bias)
    o = _fox_attention(qkv, cum, nh)
    return _matmul(o, out_w, layer=j, tn=MM_TN, tm=MM_TM, residual=h, name="fox_out_proj")


def _norm_kernel(h_ref, g_ref, o_ref):
    o_ref[...] = _rms(h_ref[...], g_ref[...]).astype(o_ref.dtype)


def _norm(h, g, out_dtype, tm=512):
    n, d = h.shape
    return pl.pallas_call(
        _norm_kernel,
        out_shape=jax.ShapeDtypeStruct((n, d), out_dtype),
        grid=(n // tm,),
        in_specs=[pl.BlockSpec((tm, d), lambda i: (i, 0)), pl.BlockSpec((1, d), lambda i: (0, 0))],
        out_specs=pl.BlockSpec((tm, d), lambda i: (i, 0)),
        compiler_params=_cparams(1),
        name="rmsnorm",
    )(h, g.reshape(1, d))


def kernel(x, norm_mix, norm_ffn, final_norm, pool_w, pool_scale, ssd_in_w, ssd_conv_w, ssd_conv_b,
           ssd_dt_bias, ssd_a_log, ssd_d_skip, ssd_norm, ssd_out_w, fox_in_w, fox_f_bias, fox_out_w,
           moe_group_w, moe_group_b, moe_router_w, moe_router_b, moe_gate_w, moe_up_w, moe_down_w):
    bsz, s, d = x.shape
    assert bsz == 1
    depth = norm_mix.shape[0]
    h = x.reshape(s, d)
    hn = None
    pending = None
    for i in range(depth):
        kind, j = i % 3, i // 3
        if kind == 0:
            ops = _route_operands(i, moe_group_w, moe_group_b, moe_router_w, moe_router_b, MOE_TOK_TILE)
            h, hn_rows, ri, rg, cnt = _pool_route(h, pending, norm_mix[i], pool_w, pool_scale, j, norm_ffn[i], ops)
            pending = None
        else:
            if hn is None:
                hn = _norm(h, norm_mix[i], BF16)
            if kind == 1:
                h = _ssd_mixer(h, hn, j, ssd_in_w, ssd_conv_w[j], ssd_conv_b[j], ssd_dt_bias[j], ssd_a_log[j],
                               ssd_d_skip[j], ssd_norm[j], ssd_out_w)
            else:
                h = _fox_mixer(h, hn, j, fox_in_w, fox_f_bias[j], fox_out_w)
            ops = _route_operands(i, moe_group_w, moe_group_b, moe_router_w, moe_router_b, ROUTE_TILE)
            hn_rows, ri, rg, cnt = _route(h, norm_ffn[i], ops)
        ys, dest = _moe_experts_path(hn_rows, ri, cnt, i, moe_gate_w, moe_up_w, moe_down_w)
        hn = None
        if i == depth - 1:
            h = _combine(h, rg, ys, dest, final_norm, "final")
        elif (i + 1) % 3 == 0:
            pending = (rg, ys, dest)
        else:
            h, hn = _combine(h, rg, ys, dest, norm_mix[i + 1], "norm_bf16")
    return h.reshape(bsz, s, d)
```

```python
import functools

import jax
import jax.numpy as jnp
from jax import lax
from jax.experimental import pallas as pl
from jax.experimental.pallas import tpu as pltpu

F32 = jnp.float32
BF16 = jnp.bfloat16
I32 = jnp.int32
U32 = jnp.uint32

RMS_EPS = 1e-6
LOG2E = 1.4426950408889634
LANES = 128
SLAB = 8
VMEM_LIMIT = 56 * 1024 * 1024

MM_TM = 1024
MM_TN = 1024

POOL_WINDOWS = (2, 4, 8, 16)
POOL_HALO = 16
POOL_PAD = 8

SSD_HEAD_DIM = 64
SSD_STATE = 128
SSD_GROUPS = 8
SSD_CONV = 4
SSD_CHUNK = 128
CONV_HALO = 8

FOX_HEAD_DIM = 128
FOX_TQ = 4096
FOX_TK = 1024
FOX_SUB = 256

MOE_GROUPS = 4
MOE_EPG = 8
MOE_EXPERTS = MOE_GROUPS * MOE_EPG
MOE_ROWS = 256
MOE_WSLOTS = 3
MOE_TOK_TILE = 256
ROUTE_TILE = 512


def _cparams(n_axes, vmem=VMEM_LIMIT):
    return pltpu.CompilerParams(dimension_semantics=("arbitrary",) * n_axes, vmem_limit_bytes=vmem)


def _rms(x, g):
    ms = jnp.mean(x * x, axis=-1, keepdims=True)
    return x * lax.rsqrt(ms + RMS_EPS) * g


def _split3(x):
    hi = x.astype(BF16)
    r = x - hi.astype(F32)
    mid = r.astype(BF16)
    lo = (r - mid.astype(F32)).astype(BF16)
    return hi, mid, lo


def _dot(a, b):
    return jnp.dot(a, b, preferred_element_type=F32)


def _dot_nt(a, b):
    return lax.dot_general(a, b, (((1,), (1,)), ((), ())), preferred_element_type=F32)


def _dot3_rhs_exact(x, m):
    hi, mid, lo = _split3(x)
    return _dot(hi, m) + _dot(mid, m) + _dot(lo, m)


def _dot3_lhs_exact(m, x):
    hi, mid, lo = _split3(x)
    return _dot(m, hi) + _dot(m, mid) + _dot(m, lo)


def _dot2_rhs_exact(x, m):
    hi = x.astype(BF16)
    lo = (x - hi.astype(F32)).astype(BF16)
    return _dot(hi, m) + _dot(lo, m)


def _silu(x):
    h = 0.5 * x
    return h + h * jnp.tanh(h)


def _softplus(x):
    return jnp.maximum(x, 0.0) + jnp.log1p(jnp.exp(-jnp.abs(x)))


def _pack_rows(x):
    half = x.shape[1] // 2
    lo = lax.bitcast_convert_type(x[:, :half].astype(BF16).astype(F32), U32)
    hi = lax.bitcast_convert_type(x[:, half:].astype(BF16).astype(F32), U32)
    return hi | (lo >> 16)


def _unpack_rows(u):
    lo = lax.bitcast_convert_type(u << 16, F32)
    hi = lax.bitcast_convert_type(u & jnp.uint32(0xFFFF0000), F32)
    return lo, hi


def _load_token_slabs(ref, rows):
    return [ref[pl.ds(c, rows, stride=SLAB), :] for c in range(SLAB)]


def _store_token_slabs(ref, packed):
    rows = packed.shape[0]
    for c in range(SLAB):
        ref[pl.ds(c, rows, stride=SLAB), :] = packed[:, c * LANES:(c + 1) * LANES]


def _slab(ref, token):
    return ref.at[pl.ds(pl.multiple_of(token * SLAB, SLAB), SLAB)]


def _slab_run(ref, token, count):
    return ref.at[pl.ds(pl.multiple_of(token * SLAB, SLAB), count * SLAB)]


def _mm_kernel(*refs, has_res, has_scale, transposed):
    x_ref, w_ref = refs[:2]
    rest = list(refs[2:])
    r_ref = rest.pop(0) if has_res else None
    s_ref = rest.pop(0) if has_scale else None
    o_ref, wbf = rest

    @pl.when(pl.program_id(1) == 0)
    def _():
        if transposed and wbf.shape[0] != w_ref.shape[0]:
            wbf[...] = jnp.zeros_like(wbf)
            wbf[0:w_ref.shape[0], :] = w_ref[...].astype(BF16)
        else:
            wbf[...] = w_ref[...].astype(BF16)

    acc = _dot_nt(x_ref[...], wbf[...]) if transposed else _dot(x_ref[...], wbf[...])
    if s_ref is not None:
        acc = acc * s_ref[...]
    if r_ref is not None:
        acc = acc + r_ref[...]
    o_ref[...] = acc.astype(o_ref.dtype)


def _matmul(x, w, *, layer=0, col0=0, ncols=None, tn=512, tm=512, residual=None, colscale=None,
            out_dtype=F32, name="mm"):
    m, k = x.shape
    ncols = w.shape[-1] - col0 if ncols is None else ncols
    assert ncols % tn == 0 and col0 % tn == 0 and m % tm == 0
    jb = col0 // tn
    if w.ndim == 3:
        w_spec = pl.BlockSpec((None, k, tn), lambda j, i: (layer, 0, jb + j))
    else:
        w_spec = pl.BlockSpec((k, tn), lambda j, i: (0, jb + j))
    in_specs = [pl.BlockSpec((tm, k), lambda j, i: (i, 0)), w_spec]
    args = [x, w]
    if residual is not None:
        in_specs.append(pl.BlockSpec((tm, tn), lambda j, i: (i, j)))
        args.append(residual)
    if colscale is not None:
        in_specs.append(pl.BlockSpec((1, tn), lambda j, i: (0, j)))
        args.append(colscale)
    return pl.pallas_call(
        functools.partial(_mm_kernel, has_res=residual is not None, has_scale=colscale is not None,
                          transposed=False),
        out_shape=jax.ShapeDtypeStruct((m, ncols), out_dtype),
        grid=(ncols // tn, m // tm),
        in_specs=in_specs,
        out_specs=pl.BlockSpec((tm, tn), lambda j, i: (i, j)),
        scratch_shapes=[pltpu.VMEM((k, tn), BF16)],
        compiler_params=_cparams(2),
        name=name,
    )(*args)


def _matmul_t(x, wt, *, layer, row0=0, nrows=None, tn=512, tm=512, colscale=None, out_dtype=F32, name="mm_t"):
    m, k = x.shape
    nrows = wt.shape[1] - row0 if nrows is None else nrows
    assert nrows % tn == 0 and row0 % tn == 0 and m % tm == 0 and tn % 8 == 0
    assert tn % LANES == 0 or nrows == tn
    tn_out = max(tn, LANES)
    jb = row0 // tn
    in_specs = [pl.BlockSpec((tm, k), lambda j, i: (i, 0)),
                pl.BlockSpec((None, tn, k), lambda j, i: (layer, jb + j, 0))]
    args = [x, wt]
    if colscale is not None:
        in_specs.append(pl.BlockSpec((1, tn_out), lambda j, i: (0, j)))
        args.append(colscale)
    return pl.pallas_call(
        functools.partial(_mm_kernel, has_res=False, has_scale=colscale is not None, transposed=True),
        out_shape=jax.ShapeDtypeStruct((m, (nrows // tn) * tn_out), out_dtype),
        grid=(nrows // tn, m // tm),
        in_specs=in_specs,
        out_specs=pl.BlockSpec((tm, tn_out), lambda j, i: (i, j)),
        scratch_shapes=[pltpu.VMEM((tn_out, k), BF16)],
        compiler_params=_cparams(2),
        name=name,
    )(*args)


def _pool_init(w_ref, xbuf, sa, sb, wbf):
    cur0 = POOL_PAD + POOL_HALO
    xbuf[0:cur0, :] = jnp.zeros((cur0, xbuf.shape[1]), F32)
    sa[0:POOL_PAD, :] = jnp.zeros((POOL_PAD, sa.shape[1]), F32)
    sb[0:POOL_PAD, :] = jnp.zeros((POOL_PAD, sb.shape[1]), F32)
    wbf[...] = w_ref[...].astype(BF16)


def _pool_tile(h, i, g_ref, scale_ref, o_ref, xbuf, sa, sb, wbf, tm, pg):
    lo = POOL_PAD
    cur0 = POOL_PAD + POOL_HALO
    end = cur0 + tm
    hn = _rms(h, g_ref[...])
    xbuf[cur0:end, :] = hn
    row = i * tm + lax.broadcasted_iota(I32, (tm, 1), 0)
    for gi, w in enumerate(POOL_WINDOWS):
        c0 = gi * pg
        cols = slice(c0, c0 + pg)
        src, shift, spare = None, 1, [sa, sb]
        while True:
            first = lo if 2 * shift < w else cur0
            if src is None:
                summed = xbuf[first:end, cols] + xbuf[first - shift:end - shift, cols]
            else:
                summed = src[first:end, :] + src[first - shift:end - shift, :]
            shift *= 2
            if shift == w:
                break
            dst = spare[0] if src is not spare[0] else spare[1]
            dst[first:end, :] = summed
            src = dst
        cur = hn[:, cols]
        cnt = jnp.minimum(row + 1, w).astype(F32)
        pooled = summed / cnt - cur
        mix = _dot(pooled.astype(BF16), wbf[gi]) * scale_ref[:, cols]
        o_ref[:, cols] = h[:, cols] + mix
    xbuf[lo:cur0, :] = xbuf[end - POOL_HALO:end, :]


def _route_kernel(h_ref, g_ref, whl_ref, b_ref, tri_ref, hn_ref, ri_ref, rg_ref, cnt_ref, run_ref, *, tm):
    @pl.when(pl.program_id(0) == 0)
    def _():
        run_ref[...] = jnp.zeros_like(run_ref)

    _route_tile(h_ref[...], g_ref, whl_ref, b_ref, tri_ref, hn_ref, ri_ref, rg_ref, cnt_ref, run_ref, tm)


def _route_tile(h, g_ref, whl_ref, b_ref, tri_ref, hn_ref, ri_ref, rg_ref, cnt_ref, run_ref, tm):
    hn = _rms(h, g_ref[...])
    hn_bf = hn.astype(BF16)
    _store_token_slabs(hn_ref, _pack_rows(hn))
    hn_lo = (hn - hn_bf.astype(F32)).astype(BF16)
    both = _dot(hn_bf, whl_ref[...])
    logits = both[:, 0:LANES] + both[:, LANES:2 * LANES] + _dot(hn_lo, whl_ref[:, 0:LANES]) + b_ref[...]

    lane = lax.broadcasted_iota(I32, (tm, LANES), 1)
    lane_f = lane.astype(F32)
    neg = -jnp.inf
    big = float(LANES)
    gl = jnp.where(lane < MOE_GROUPS, logits, neg)
    gmax = jnp.max(gl, axis=-1, keepdims=True)
    gsel = jnp.min(jnp.where(gl == gmax, lane_f, big), axis=-1, keepdims=True).astype(I32)
    p_grp = 1.0 / jnp.sum(jnp.exp(gl - gmax), axis=-1, keepdims=True)
    lo_lane = MOE_GROUPS + MOE_EPG * gsel
    el = jnp.where((lane >= lo_lane) & (lane < lo_lane + MOE_EPG), logits, neg)
    v0 = jnp.max(el, axis=-1, keepdims=True)
    i0 = jnp.min(jnp.where(el == v0, lane_f, big), axis=-1, keepdims=True).astype(I32)
    el2 = jnp.where(lane == i0, neg, el)
    v1 = jnp.max(el2, axis=-1, keepdims=True)
    i1 = jnp.min(jnp.where(el2 == v1, lane_f, big), axis=-1, keepdims=True).astype(I32)
    t = jnp.exp(v1 - v0)
    gate0 = p_grp / (1.0 + t)
    gate1 = p_grp * t / (1.0 + t)
    e0 = i0 - MOE_GROUPS
    e1 = i1 - MOE_GROUPS

    onehot = ((lane == e0) | (lane == e1))
    c_bf = jnp.where(onehot, 1.0, 0.0).astype(BF16)
    prefix = _dot(tri_ref[...], c_bf) + run_ref[...]
    rank0 = jnp.sum(jnp.where(lane == e0, prefix, 0.0), axis=-1, keepdims=True).astype(I32)
    rank1 = jnp.sum(jnp.where(lane == e1, prefix, 0.0), axis=-1, keepdims=True).astype(I32)
    run = run_ref[...] + jnp.sum(jnp.where(onehot, 1.0, 0.0), axis=0, keepdims=True)
    run_ref[...] = run
    cnt_ref[...] = jnp.broadcast_to(run, cnt_ref.shape)

    ri = jnp.where(lane == 0, e0, jnp.where(lane == 1, e1, jnp.where(lane == 2, rank0, rank1)))
    ri_ref[...] = ri.T[0:8, :]
    rg_ref[...] = jnp.where(lane == 0, gate0, gate1)


def _route_operands(group_w, group_b, router_w, router_b):
    depth, d, _ = group_w.shape
    pad = LANES - MOE_GROUPS - MOE_EXPERTS
    wr = jnp.concatenate([group_w, router_w, jnp.zeros((depth, d, pad), F32)], axis=2)
    br = jnp.concatenate([group_b, router_b, jnp.zeros((depth, pad), F32)], axis=1).reshape(depth, 1, LANES)
    whi = wr.astype(BF16)
    wlo = (wr - whi.astype(F32)).astype(BF16)
    return jnp.concatenate([whi, wlo], axis=2), br


def _prefix_matrix(tm):
    return jnp.tril(jnp.ones((tm, tm), BF16), -1)


def _route_out(n, tm):
    shapes = (jax.ShapeDtypeStruct((n * SLAB, LANES), U32), jax.ShapeDtypeStruct((8, n), I32),
              jax.ShapeDtypeStruct((n, LANES), F32), jax.ShapeDtypeStruct((8, LANES), F32))
    specs = (pl.BlockSpec((tm * SLAB, LANES), lambda i: (i, 0)), pl.BlockSpec((8, tm), lambda i: (0, i)),
             pl.BlockSpec((tm, LANES), lambda i: (i, 0)), pl.BlockSpec((8, LANES), lambda i: (0, 0)))
    return shapes, specs


def _pool_route_kernel(*refs, with_combine, t, pg):
    refs = list(refs)
    if with_combine:
        dest_ref, dest_next_ref, rgp_ref, ys_ref = refs[:4]
        refs = refs[4:]
    h_ref, gmix_ref, pw_ref, scale_ref, gffn_ref, whl_ref, b_ref, tri_ref = refs[:8]
    o_ref, hn_ref, ri_ref, rg_ref, cnt_ref = refs[8:13]
    refs = refs[13:]
    if with_combine:
        buf00, buf01, buf10, buf11, sems, hbuf = refs[:6]
        refs = refs[6:]
    xbuf, sa, sb, wbf, run_ref = refs
    i = pl.program_id(0)

    @pl.when(i == 0)
    def _():
        _pool_init(pw_ref, xbuf, sa, sb, wbf)
        run_ref[...] = jnp.zeros_like(run_ref)

    if with_combine:
        _combine_tile(dest_ref, dest_next_ref, h_ref, rgp_ref, ys_ref, hbuf, ((buf00, buf01), (buf10, buf11)), sems, t)
        h = hbuf[...]
    else:
        h = h_ref[...]
    _pool_tile(h, i, gmix_ref, scale_ref, o_ref, xbuf, sa, sb, wbf, t, pg)
    _route_tile(o_ref[...], gffn_ref, whl_ref, b_ref, tri_ref, hn_ref, ri_ref, rg_ref, cnt_ref, run_ref, t)


def _pool_route(h, pending, g_mix, pool_w, pool_scale, j, g_ffn, route_ops, layer, t=MOE_TOK_TILE):
    n, d = h.shape
    nw, pg, _ = pool_w.shape[1:]
    whl, br = route_ops
    tri = _prefix_matrix(t)
    assert d == 2 * SLAB * LANES
    last = n // t - 1
    row = lambda i: (i, 0)
    const = lambda i: (0, 0)
    in_specs, args, scratch = [], [], []
    if pending is not None:
        rg_prev, ys, dest = pending
        in_specs += [pl.BlockSpec((None, 1, 2 * t), lambda i: (i, 0, 0), memory_space=pltpu.SMEM),
                     pl.BlockSpec((None, 1, 2 * t), lambda i: (jnp.minimum(i + 1, last), 0, 0),
                                  memory_space=pltpu.SMEM),
                     pl.BlockSpec((t, LANES), row),
                     pl.BlockSpec(memory_space=pl.ANY)]
        args += [dest, dest, rg_prev, ys]
        scratch += [pltpu.VMEM((t * SLAB, LANES), U32)] * 4 + [pltpu.SemaphoreType.DMA((2,)), pltpu.VMEM((t, d), F32)]
    in_specs += [pl.BlockSpec((t, d), row), pl.BlockSpec((1, d), const),
                 pl.BlockSpec((None, nw, pg, pg), lambda i: (j, 0, 0, 0)), pl.BlockSpec((1, d), const),
                 pl.BlockSpec((1, d), const), pl.BlockSpec((None, d, 2 * LANES), lambda i: (layer, 0, 0)),
                 pl.BlockSpec((None, 1, LANES), lambda i: (layer, 0, 0)), pl.BlockSpec((t, t), const)]
    args += [h, g_mix.reshape(1, d), pool_w, pool_scale[j].reshape(1, d), g_ffn.reshape(1, d), whl, br, tri]
    r_shapes, r_specs = _route_out(n, t)
    scratch += [pltpu.VMEM((POOL_PAD + POOL_HALO + t, d), F32),
                pltpu.VMEM((POOL_PAD + POOL_HALO + t, pg), F32),
                pltpu.VMEM((POOL_PAD + POOL_HALO + t, pg), F32),
                pltpu.VMEM((nw, pg, pg), BF16),
                pltpu.VMEM((1, LANES), F32)]
    return pl.pallas_call(
        functools.partial(_pool_route_kernel, with_combine=pending is not None, t=t, pg=pg),
        out_shape=(jax.ShapeDtypeStruct((n, d), F32),) + r_shapes,
        grid=(n // t,),
        in_specs=in_specs,
        out_specs=(pl.BlockSpec((t, d), row),) + r_specs,
        scratch_shapes=scratch,
        compiler_params=_cparams(1),
        name="pool_route",
    )(*args)


def _route(h, g, route_ops, layer, tm=ROUTE_TILE):
    n, d = h.shape
    whl, br = route_ops
    tri = _prefix_matrix(tm)
    return pl.pallas_call(
        functools.partial(_route_kernel, tm=tm),
        out_shape=(jax.ShapeDtypeStruct((n * SLAB, LANES), U32),
                   jax.ShapeDtypeStruct((8, n), I32),
                   jax.ShapeDtypeStruct((n, LANES), F32),
                   jax.ShapeDtypeStruct((8, LANES), F32)),
        grid=(n // tm,),
        in_specs=[pl.BlockSpec((tm, d), lambda i: (i, 0)),
                  pl.BlockSpec((1, d), lambda i: (0, 0)),
                  pl.BlockSpec((None, d, 2 * LANES), lambda i: (layer, 0, 0)),
                  pl.BlockSpec((None, 1, LANES), lambda i: (layer, 0, 0)),
                  pl.BlockSpec((tm, tm), lambda i: (0, 0))],
        out_specs=(pl.BlockSpec((tm * SLAB, LANES), lambda i: (i, 0)),
                   pl.BlockSpec((8, tm), lambda i: (0, i)),
                   pl.BlockSpec((tm, LANES), lambda i: (i, 0)),
                   pl.BlockSpec((8, LANES), lambda i: (0, 0))),
        scratch_shapes=[pltpu.VMEM((1, LANES), F32)],
        compiler_params=_cparams(1),
        name="moe_route",
    )(h, g.reshape(1, d), whl, br, tri)


def _dispatch_kernel(plan_ref, dest_ref, hn_ref, xs_ref, zbuf, sem, zsem, *, t):
    sizes = [1 << s for s in reversed(range(MOE_ROWS.bit_length() - 1))]

    @pl.when(pl.program_id(0) == 0)
    def _():
        zbuf[...] = jnp.zeros_like(zbuf)

        def zero_copy(off, size):
            return pltpu.make_async_copy(zbuf.at[pl.ds(0, size * SLAB)], _slab_run(xs_ref, off, size), zsem)

        def gaps(e, wait):
            off = plan_ref[PLAN_PAD_START, e]
            n = plan_ref[PLAN_PAD_LEN, e]
            for size in sizes:
                hit = (n & size) != 0

                @pl.when(hit)
                def _():
                    c = zero_copy(off, size)
                    c.wait() if wait else c.start()

                off = off + (n & size)

        def start_gaps(e, c):
            gaps(e, False)
            return c

        def wait_gaps(e, c):
            gaps(e, True)
            return c

        lax.fori_loop(0, MOE_EXPERTS, start_gaps, 0)
        lax.fori_loop(0, MOE_EXPERTS, wait_gaps, 0)

        zrows = zbuf.shape[0] // SLAB
        n_blocks = xs_ref.shape[0] // (MOE_ROWS * SLAB)

        def tail(wait):
            def body(b, c):
                for part in range(MOE_ROWS // zrows):
                    cp = zero_copy(b * MOE_ROWS + part * zrows, zrows)
                    cp.wait() if wait else cp.start()
                return c
            lax.fori_loop(plan_ref[PLAN_NUSED, 0], n_blocks, body, 0)

        tail(False)
        tail(True)

    def row_copy(r, k):
        return pltpu.make_async_copy(_slab(hn_ref, r), _slab(xs_ref, dest_ref[0, k * t + r]), sem)

    def start(r, c):
        row_copy(r, 0).start(priority=0)
        row_copy(r, 1).start(priority=1)
        return c

    lax.fori_loop(0, t, start, 0, unroll=8)
    for _ in range(2):
        pltpu.make_async_copy(hn_ref, xs_ref.at[pl.ds(0, t * SLAB)], sem).wait()


def _dispatch(hn, dest, plan, n_rows, t=MOE_TOK_TILE):
    n = hn.shape[0] // SLAB
    grid_spec = pltpu.PrefetchScalarGridSpec(
        num_scalar_prefetch=1,
        grid=(n // t,),
        in_specs=[pl.BlockSpec((None, 1, 2 * t), lambda i, *_: (i, 0, 0), memory_space=pltpu.SMEM),
                  pl.BlockSpec((t * SLAB, LANES), lambda i, *_: (i, 0))],
        out_specs=pl.BlockSpec(memory_space=pl.ANY),
        scratch_shapes=[pltpu.VMEM((MOE_ROWS // 2 * SLAB, LANES), hn.dtype),
                        pltpu.SemaphoreType.DMA(()), pltpu.SemaphoreType.DMA(())],
    )
    return pl.pallas_call(
        functools.partial(_dispatch_kernel, t=t),
        out_shape=jax.ShapeDtypeStruct((n_rows * SLAB, LANES), hn.dtype),
        grid_spec=grid_spec,
        compiler_params=_cparams(1),
        name="moe_dispatch",
    )(plan, dest, hn)


def _expert_kernel(plan_ref, xs_ref, wg_hbm, wu_hbm, wd_hbm,
                   o_ref, wg_raw, wu_raw, wd_raw, wg_bf, wu_bf, wd_bf, sems, *, layer):
    b = pl.program_id(0)
    active = b < plan_ref[PLAN_NUSED, 0]
    expert = plan_ref[PLAN_EXPERT, b]
    second = plan_ref[PLAN_SECOND, 0]

    def fetch(e, slot):
        return [pltpu.make_async_copy(hbm.at[layer, e], raw.at[slot], sems.at[slot, k])
                for k, (hbm, raw) in enumerate(((wg_hbm, wg_raw), (wu_hbm, wu_raw), (wd_hbm, wd_raw)))]

    @pl.when(b == 0)
    def _():
        for c in fetch(expert, 0):
            c.start(priority=1)

        @pl.when(second >= 0)
        def _():
            for c in fetch(second, 1):
                c.start(priority=1)

    @pl.when(active & (plan_ref[PLAN_FIRST, b] == 1))
    def _():
        slot = plan_ref[PLAN_SLOT, b]
        ahead = plan_ref[PLAN_AHEAD, b]

        @pl.when(ahead >= 0)
        def _():
            for c in fetch(ahead, (slot + MOE_WSLOTS - 1) % MOE_WSLOTS):
                c.start(priority=1)

        for c in fetch(expert, slot):
            c.wait()
        wg_bf[...] = wg_raw[slot].astype(BF16)
        wu_bf[...] = wu_raw[slot].astype(BF16)
        wd_bf[...] = wd_raw[slot].astype(BF16)

    @pl.when(active)
    def _():
        rows = xs_ref.shape[0] // SLAB
        halves = [_unpack_rows(u) for u in _load_token_slabs(xs_ref, rows)]
        x = jnp.concatenate([lo.astype(BF16) for lo, _ in halves] + [hi.astype(BF16) for _, hi in halves], axis=1)
        gte = _dot(x, wg_bf[...])
        up = _dot(x, wu_bf[...])
        hb = _silu(gte) * up
        _store_token_slabs(o_ref, _pack_rows(_dot(hb.astype(BF16), wd_bf[...])))

    @pl.when(jnp.logical_not(active))
    def _():
        o_ref[...] = jnp.zeros_like(o_ref)


def _experts(xs, plan, gate_w, up_w, down_w, layer, rows=MOE_ROWS):
    n_rows = xs.shape[0] // SLAB
    d = gate_w.shape[-2]
    hid = gate_w.shape[-1]
    assert d == 2 * SLAB * LANES and MOE_WSLOTS == 3 and n_rows // rows <= LANES
    x_map = lambda b, plan_ref: (jnp.minimum(b, plan_ref[PLAN_NUSED, 0] - 1), 0)
    grid_spec = pltpu.PrefetchScalarGridSpec(
        num_scalar_prefetch=1,
        grid=(n_rows // rows,),
        in_specs=[pl.BlockSpec((rows * SLAB, LANES), x_map),
                  pl.BlockSpec(memory_space=pl.ANY),
                  pl.BlockSpec(memory_space=pl.ANY),
                  pl.BlockSpec(memory_space=pl.ANY)],
        out_specs=pl.BlockSpec((rows * SLAB, LANES), lambda b, *_: (b, 0)),
        scratch_shapes=[pltpu.VMEM((MOE_WSLOTS, d, hid), F32), pltpu.VMEM((MOE_WSLOTS, d, hid), F32),
                        pltpu.VMEM((MOE_WSLOTS, hid, d), F32),
                        pltpu.VMEM((d, hid), BF16), pltpu.VMEM((d, hid), BF16), pltpu.VMEM((hid, d), BF16),
                        pltpu.SemaphoreType.DMA((MOE_WSLOTS, 3))],
    )
    return pl.pallas_call(
        functools.partial(_expert_kernel, layer=layer),
        out_shape=jax.ShapeDtypeStruct((n_rows * SLAB, LANES), U32),
        grid_spec=grid_spec,
        compiler_params=_cparams(1),
        name="moe_experts",
    )(plan, xs, gate_w, up_w, down_w)


def _combine_kernel(*refs, t, mode):
    dest_ref, dest_next_ref, h_ref, rg_ref, ys_ref, g_ref, o_ref = refs[:7]
    rest = list(refs[7:])
    hn_ref = rest.pop(0) if mode == "norm_bf16" else None
    buf00, buf01, buf10, buf11, sems = rest
    _combine_tile(dest_ref, dest_next_ref, h_ref, rg_ref, ys_ref, o_ref, ((buf00, buf01), (buf10, buf11)), sems, t)
    normed = _rms(o_ref[...], g_ref[...])
    if mode == "norm_bf16":
        hn_ref[...] = normed.astype(BF16)
    else:
        o_ref[...] = normed


def _combine_tile(dest_ref, dest_next_ref, h_ref, rg_ref, ys_ref, o_ref, bufs, sems, t):
    i = pl.program_id(0)
    n = pl.num_programs(0)
    half = SLAB * LANES

    def row_copy(dref, slot, r, k):
        return pltpu.make_async_copy(_slab(ys_ref, dref[0, k * t + r]), _slab(bufs[slot][k], r), sems.at[slot])

    def issue(dref, slot):
        def start(r, c):
            row_copy(dref, slot, r, 0).start(priority=0)
            row_copy(dref, slot, r, 1).start(priority=1)
            return c
        lax.fori_loop(0, t, start, 0, unroll=8)

    def drain(slot):
        for k in range(2):
            pltpu.make_async_copy(ys_ref.at[pl.ds(0, t * SLAB)], bufs[slot][k], sems.at[slot]).wait()

    @pl.when(i == 0)
    def _():
        issue(dest_ref, 0)

    for slot in (0, 1):
        @pl.when((i % 2 == slot) & (i + 1 < n))
        def _():
            issue(dest_next_ref, 1 - slot)

        @pl.when(i % 2 == slot)
        def _():
            drain(slot)
            rg = rg_ref[...]
            g0 = rg[:, 0:1]
            g1 = rg[:, 1:2]
            for c, (ua, ub) in enumerate(zip(_load_token_slabs(bufs[slot][0], t), _load_token_slabs(bufs[slot][1], t))):
                a_lo, a_hi = _unpack_rows(ua)
                b_lo, b_hi = _unpack_rows(ub)
                lo_cols = slice(c * LANES, (c + 1) * LANES)
                hi_cols = slice(half + c * LANES, half + (c + 1) * LANES)
                o_ref[:, lo_cols] = h_ref[:, lo_cols] + (g0 * a_lo + g1 * b_lo)
                o_ref[:, hi_cols] = h_ref[:, hi_cols] + (g0 * a_hi + g1 * b_hi)


def _combine(h, rg, ys, dest, g_next, mode, t=MOE_TOK_TILE):
    n, d = h.shape
    assert d == 2 * SLAB * LANES
    last = n // t - 1
    in_specs = [pl.BlockSpec((None, 1, 2 * t), lambda i: (i, 0, 0), memory_space=pltpu.SMEM),
                pl.BlockSpec((None, 1, 2 * t), lambda i: (jnp.minimum(i + 1, last), 0, 0),
                             memory_space=pltpu.SMEM),
                pl.BlockSpec((t, d), lambda i: (i, 0)),
                pl.BlockSpec((t, LANES), lambda i: (i, 0)),
                pl.BlockSpec(memory_space=pl.ANY),
                pl.BlockSpec((1, d), lambda i: (0, 0))]
    args = [dest, dest, h, rg, ys, g_next.reshape(1, d)]
    tile = pl.BlockSpec((t, d), lambda i: (i, 0))
    assert mode in ("norm_bf16", "final")
    if mode == "norm_bf16":
        out_shape = (jax.ShapeDtypeStruct((n, d), F32), jax.ShapeDtypeStruct((n, d), BF16))
        out_specs = (tile, tile)
    else:
        out_shape = jax.ShapeDtypeStruct((n, d), F32)
        out_specs = tile
    return pl.pallas_call(
        functools.partial(_combine_kernel, t=t, mode=mode),
        out_shape=out_shape,
        grid=(n // t,),
        in_specs=in_specs,
        out_specs=out_specs,
        scratch_shapes=[pltpu.VMEM((t * SLAB, LANES), U32)] * 4 + [pltpu.SemaphoreType.DMA((2,))],
        compiler_params=_cparams(1),
        name="moe_combine",
    )(*args)


def _moe_experts_path(hn, ri, cnt, layer, gate_w, up_w, down_w):
    n = hn.shape[0] // SLAB
    t = MOE_TOK_TILE
    n_rows = -(-(n * 2 + MOE_EXPERTS * (MOE_ROWS - 1)) // MOE_ROWS) * MOE_ROWS
    dest, plan = _moe_plan(ri, cnt)
    dest = dest[0:2].reshape(2, n // t, t).transpose(1, 0, 2).reshape(n // t, 1, 2 * t)
    xs = _dispatch(hn, dest, plan, n_rows)
    return _experts(xs, plan, gate_w, up_w, down_w, layer), dest


PLAN_EXPERT = 0
PLAN_FIRST = 1
PLAN_SLOT = 2
PLAN_AHEAD = 3
PLAN_SECOND = 4
PLAN_NUSED = 5
PLAN_PAD_START = 6
PLAN_PAD_LEN = 7


def _plan_kernel(ri_ref, cnt_ref, tri_ref, dest_ref, plan_ref, *, chunk):
    e_lane = lax.broadcasted_iota(I32, (LANES, LANES), 1)
    e_sub = lax.broadcasted_iota(I32, (LANES, LANES), 0)
    tri = tri_ref[...]
    rows = float(MOE_ROWS)

    cnt_l = jnp.where(e_lane < MOE_EXPERTS, jnp.broadcast_to(cnt_ref[0:1, :], (LANES, LANES)), 0.0)
    cnt_s = cnt_l.T
    nblk_l = jnp.floor((cnt_l + (rows - 1.0)) * (1.0 / rows))
    nblk_s = jnp.floor((cnt_s + (rows - 1.0)) * (1.0 / rows))
    end_l = _dot_nt(nblk_l.astype(BF16), tri)
    end_s = _dot(tri, nblk_s.astype(BF16))
    start_l = end_l - nblk_l
    start_s = end_s - nblk_s
    row_start_l = start_l * rows
    n_used = end_l[0:1, LANES - 1:LANES]

    row_start_bf = row_start_l[0:SLAB, :].astype(BF16)
    n = ri_ref.shape[1]
    for c in range(n // chunk):
        cols = slice(c * chunk, (c + 1) * chunk)
        parts = []
        for k in range(2):
            onehot = (lax.broadcasted_iota(I32, (LANES, chunk), 0) == ri_ref[k:k + 1, cols])
            base = _dot(row_start_bf, jnp.where(onehot, 1.0, 0.0).astype(BF16))[0:1, :]
            parts.append(base.astype(I32) + ri_ref[2 + k:3 + k, cols])
        dest_ref[:, cols] = jnp.concatenate(parts + [jnp.zeros((SLAB - 2, chunk), I32)], axis=0)

    blk = e_lane.astype(F32)
    valid = e_sub < MOE_EXPERTS
    block_exp = jnp.sum(jnp.where(valid & (end_s <= blk), 1.0, 0.0), axis=0, keepdims=True)
    block_exp = jnp.minimum(block_exp, float(MOE_EXPERTS - 1))
    mine = e_sub.astype(F32) == block_exp
    blk_row = blk[0:1, :]
    first = (blk_row == jnp.sum(jnp.where(mine, start_s, 0.0), axis=0, keepdims=True)) & (blk_row < n_used)
    nonempty_s = nblk_s > 0.0
    ordinal_s = _dot(tri, jnp.where(nonempty_s, 1.0, 0.0).astype(BF16)) - 1.0
    pos = jnp.sum(jnp.where(mine, ordinal_s, 0.0), axis=0, keepdims=True)
    slot = pos - MOE_WSLOTS * jnp.floor((pos + 0.5) * (1.0 / MOE_WSLOTS))

    def expert_at(target):
        hit = nonempty_s & (ordinal_s == target)
        found = jnp.sum(jnp.where(hit, 1.0, 0.0), axis=0, keepdims=True) > 0.0
        return jnp.where(found, jnp.sum(jnp.where(hit, e_sub.astype(F32), 0.0), axis=0, keepdims=True), -1.0)

    ahead = expert_at(pos + (MOE_WSLOTS - 1.0))
    second = expert_at(jnp.ones((1, LANES), F32))
    pad_start = (row_start_l + cnt_l)[0:1, :]
    pad_len = (nblk_l * rows - cnt_l)[0:1, :]
    table = [block_exp, jnp.where(first, 1.0, 0.0), slot, ahead, second,
             jnp.broadcast_to(n_used, (1, LANES)), pad_start, pad_len]
    plan_ref[...] = jnp.concatenate(table, axis=0).astype(I32)


def _moe_plan(ri, cnt, chunk=2048):
    n = ri.shape[1]
    tri = jnp.tril(jnp.ones((LANES, LANES), BF16))
    return pl.pallas_call(
        functools.partial(_plan_kernel, chunk=chunk),
        out_shape=(jax.ShapeDtypeStruct((SLAB, n), I32), jax.ShapeDtypeStruct((SLAB, LANES), I32)),
        grid=(1,),
        in_specs=[pl.BlockSpec((SLAB, n), lambda i: (0, 0)), pl.BlockSpec((SLAB, LANES), lambda i: (0, 0)),
                  pl.BlockSpec((LANES, LANES), lambda i: (0, 0))],
        out_specs=(pl.BlockSpec((SLAB, n), lambda i: (0, 0)), pl.BlockSpec((SLAB, LANES), lambda i: (0, 0))),
        compiler_params=_cparams(1),
        name="moe_plan",
    )(ri, cnt, tri)


def _ssd_kernel(z_ref, x_ref, b_ref, c_ref, dt_ref, cwx_ref, cwb_ref, cwc_ref, cbx_ref, cbb_ref, cbc_ref,
                dtb_ref, a_ref, dexp_ref, nw_ref, exp_ref, tri_ref, o_ref,
                xbuf, bbuf, cbuf, xc_s, xdt_s, xw_s, b_s, c_s, eacs_s, acs_s, acst_s, y_s, st_s,
                *, L, hp, n_state, groups):
    ci = pl.program_id(0)
    d_inner = x_ref.shape[1]
    gw = d_inner // groups
    kh = gw // hp
    H0 = CONV_HALO

    @pl.when(ci == 0)
    def _():
        xbuf[0:H0, :] = jnp.zeros((H0, xbuf.shape[1]), F32)
        bbuf[0:H0, :] = jnp.zeros((H0, bbuf.shape[1]), F32)
        cbuf[0:H0, :] = jnp.zeros((H0, cbuf.shape[1]), F32)
        st_s[...] = jnp.zeros_like(st_s)

    def conv_silu(in_ref, buf, cw_ref, cb_ref):
        buf[H0:H0 + L, :] = in_ref[...].astype(F32)
        acc = cb_ref[...] + cw_ref[SSD_CONV - 1:SSD_CONV, :] * buf[H0:H0 + L, :]
        for j in range(SSD_CONV - 1):
            off = H0 - (SSD_CONV - 1) + j
            acc = acc + cw_ref[j:j + 1, :] * buf[off:off + L, :]
        buf[0:H0, :] = buf[L:L + H0, :]
        return _silu(acc)

    xc_s[...] = conv_silu(x_ref, xbuf, cwx_ref, cbx_ref)
    b_s[...] = conv_silu(b_ref, bbuf, cwb_ref, cbb_ref).astype(BF16)
    c_s[...] = conv_silu(c_ref, cbuf, cwc_ref, cbc_ref).astype(BF16)

    dt = _softplus(dt_ref[...] + dtb_ref[...])
    d_a = dt * a_ref[...]
    acs = _dot3_lhs_exact(tri_ref[...], d_a)
    acs_s[...] = acs
    acst_s[...] = acs.T
    emat = exp_ref[...]
    acs_last = acs[L - 1:L, :]
    to_end = dt * jnp.exp(acs_last - acs)
    xc = xc_s[...]
    xdt_s[...] = (xc * _dot2_rhs_exact(dt, emat)).astype(BF16)
    xw_s[...] = (xc * _dot2_rhs_exact(to_end, emat)).astype(BF16)
    eacs_s[...] = _dot2_rhs_exact(jnp.exp(acs), emat)
    cdecay = _dot2_rhs_exact(jnp.broadcast_to(jnp.exp(acs_last), (SLAB, LANES)), emat)[0:1, :]

    rr = lax.broadcasted_iota(I32, (L, L), 0)
    cc = lax.broadcasted_iota(I32, (L, L), 1)
    causal = rr >= cc
    lane2 = lax.broadcasted_iota(I32, (L, 2 * hp), 1)

    for g in range(groups):
        bg = b_s[:, g * n_state:(g + 1) * n_state]
        cg = c_s[:, g * n_state:(g + 1) * n_state]
        cb = _dot_nt(cg, bg)
        st_prev = st_s[g]
        y_off = _dot(cg, st_prev.astype(BF16)) * eacs_s[:, g * gw:(g + 1) * gw]
        for kp in range(kh // 2):
            c0 = g * gw + kp * 2 * hp
            slab = xdt_s[:, c0:c0 + 2 * hp]
            ms, xs = [], []
            for half in range(2):
                hd = g * kh + kp * 2 + half
                seg = acs_s[:, hd:hd + 1] - acst_s[hd:hd + 1, :]
                decay = jnp.exp(jnp.where(causal, seg, -jnp.inf))
                ms.append((cb * decay).astype(BF16))
                keep = (lane2 < hp) if half == 0 else (lane2 >= hp)
                xs.append(jnp.where(keep, slab, jnp.zeros_like(slab)))
            y_pair = _dot(jnp.concatenate(ms, axis=1), jnp.concatenate(xs, axis=0))
            y_s[:, c0:c0 + 2 * hp] = y_pair + y_off[:, kp * 2 * hp:(kp + 1) * 2 * hp]
        bgt = bg.astype(F32).T.astype(BF16)
        s_new = _dot(bgt, xw_s[:, g * gw:(g + 1) * gw])
        st_s[g] = cdecay[:, g * gw:(g + 1) * gw] * st_prev + s_new

    y = y_s[...] + dexp_ref[...] * xc_s[...]
    z = z_ref[...].astype(F32)
    y = y * _silu(z)
    for g in range(groups):
        yg = y[:, g * gw:(g + 1) * gw]
        yg = yg * lax.rsqrt(jnp.mean(yg * yg, axis=-1, keepdims=True) + RMS_EPS)
        o_ref[:, g * gw:(g + 1) * gw] = (yg * nw_ref[:, g * gw:(g + 1) * gw]).astype(o_ref.dtype)


def _ssd_scan(zx, dt_raw, conv_w, conv_b, dt_bias, a_log, d_skip, norm_w, d_inner):
    s = zx.shape[0]
    L = SSD_CHUNK
    hp, n_state, groups = SSD_HEAD_DIM, SSD_STATE, SSD_GROUPS
    heads = d_inner // hp
    gn = groups * n_state
    gw = d_inner // groups
    assert d_inner % gn == 0 and heads <= LANES
    pad = LANES - heads
    dtb = jnp.pad(dt_bias, (0, pad)).reshape(1, LANES)
    a_neg = jnp.pad(-jnp.exp(a_log), (0, pad)).reshape(1, LANES)
    dexp = jnp.repeat(d_skip, hp).reshape(1, d_inner)
    emat = (jnp.arange(LANES)[:, None] == (jnp.arange(d_inner)[None, :] // hp)).astype(BF16)
    tri = jnp.tril(jnp.ones((L, L), BF16))
    cb2 = conv_b.reshape(1, -1)
    xb = d_inner // gn
    row = lambda c: (c, 0)
    const = lambda c: (0, 0)
    in_specs = [
        pl.BlockSpec((L, d_inner), row),
        pl.BlockSpec((L, d_inner), lambda c: (c, 1)),
        pl.BlockSpec((L, gn), lambda c: (c, 2 * xb)),
        pl.BlockSpec((L, gn), lambda c: (c, 2 * xb + 1)),
        pl.BlockSpec((L, LANES), row),
        pl.BlockSpec((SSD_CONV, d_inner), const),
        pl.BlockSpec((SSD_CONV, gn), lambda c: (0, xb)),
        pl.BlockSpec((SSD_CONV, gn), lambda c: (0, xb + 1)),
        pl.BlockSpec((1, d_inner), const),
        pl.BlockSpec((1, gn), lambda c: (0, xb)),
        pl.BlockSpec((1, gn), lambda c: (0, xb + 1)),
        pl.BlockSpec((1, LANES), const),
        pl.BlockSpec((1, LANES), const),
        pl.BlockSpec((1, d_inner), const),
        pl.BlockSpec((1, d_inner), const),
        pl.BlockSpec((LANES, d_inner), const),
        pl.BlockSpec((L, L), const),
    ]
    scratch = [
        pltpu.VMEM((CONV_HALO + L, d_inner), F32), pltpu.VMEM((CONV_HALO + L, gn), F32),
        pltpu.VMEM((CONV_HALO + L, gn), F32),
        pltpu.VMEM((L, d_inner), F32),
        pltpu.VMEM((L, d_inner), BF16),
        pltpu.VMEM((L, d_inner), BF16),
        pltpu.VMEM((L, gn), BF16), pltpu.VMEM((L, gn), BF16),
        pltpu.VMEM((L, d_inner), F32),
        pltpu.VMEM((L, LANES), F32), pltpu.VMEM((LANES, L), F32),
        pltpu.VMEM((L, d_inner), F32),
        pltpu.VMEM((groups, n_state, gw), F32),
    ]
    return pl.pallas_call(
        functools.partial(_ssd_kernel, L=L, hp=hp, n_state=n_state, groups=groups),
        out_shape=jax.ShapeDtypeStruct((s, d_inner), BF16),
        grid=(s // L,),
        in_specs=in_specs,
        out_specs=pl.BlockSpec((L, d_inner), row),
        scratch_shapes=scratch,
        compiler_params=_cparams(1),
        name="ssd_scan",
    )(zx, zx, zx, zx, dt_raw, conv_w, conv_w, conv_w, cb2, cb2, cb2, dtb, a_neg, dexp,
      norm_w.reshape(1, d_inner), emat, tri)


def _ssd_mixer(h, hn, j, in_w, conv_w, conv_b, dt_bias, a_log, d_skip, norm_w, out_w):
    d_inner = out_w.shape[1]
    heads = d_inner // SSD_HEAD_DIM
    wide = in_w.shape[2] - heads
    wt = jnp.swapaxes(in_w, 1, 2)
    zx = _matmul_t(hn, wt, layer=j, row0=0, nrows=wide, tn=MM_TN, tm=MM_TM, out_dtype=BF16, name="ssd_in_proj")
    dt_raw = _matmul_t(hn, wt, layer=j, row0=wide, nrows=heads, tn=heads, tm=MM_TM, name="ssd_dt_proj")
    y = _ssd_scan(zx, dt_raw, conv_w, conv_b, dt_bias, a_log, d_skip, norm_w, d_inner)
    return _matmul(y, out_w, layer=j, tn=MM_TN // 2, tm=MM_TM, residual=h, name="ssd_out_proj")


def _fox_cum_kernel(f_ref, b_ref, tri_ref, o_ref, carry):
    @pl.when(pl.program_id(0) == 0)
    def _():
        carry[...] = jnp.zeros_like(carry)

    x = f_ref[...] + b_ref[...]
    log_f = jnp.minimum(x, 0.0) - jnp.log1p(jnp.exp(-jnp.abs(x)))
    cs = _dot3_rhs_exact(log_f, tri_ref[...]) + carry[...]
    o_ref[...] = cs
    carry[...] = cs[:, cs.shape[1] - 1:]


def _fox_cum(f_t, bias, tk=512):
    nh, s = f_t.shape
    triu = jnp.triu(jnp.ones((tk, tk), BF16))
    return pl.pallas_call(
        _fox_cum_kernel,
        out_shape=jax.ShapeDtypeStruct((nh, s), F32),
        grid=(s // tk,),
        in_specs=[pl.BlockSpec((nh, tk), lambda i: (0, i)),
                  pl.BlockSpec((nh, 1), lambda i: (0, 0)),
                  pl.BlockSpec((tk, tk), lambda i: (0, 0))],
        out_specs=pl.BlockSpec((nh, tk), lambda i: (0, i)),
        scratch_shapes=[pltpu.VMEM((nh, 1), F32)],
        compiler_params=_cparams(1),
        name="fox_cum",
    )(f_t, bias.reshape(nh, 1), triu)


def _fox_attn_kernel(q_ref, k_ref, v_ref, cum_ref, o_ref, m_s, acc_s, vext, *, tq, tk, sr):
    qi = pl.program_id(1)
    dh = q_ref.shape[1]
    n_sub = tq // sr

    @pl.when(qi == 0)
    def _():
        vext[:, 0:dh] = v_ref[...]
        vext[:, dh:2 * dh] = jnp.ones((vext.shape[0], dh), BF16)

    q0 = pl.multiple_of(qi * tq, tq)
    c_ref = cum_ref[:, pl.ds(q0, LANES)][:, 0:1]
    m_s[...] = jnp.full_like(m_s, -jnp.inf)
    acc_s[...] = jnp.zeros_like(acc_s)

    def sub_step(r, k0, width, bias, diagonal):
        rows = slice(r * sr, (r + 1) * sr)
        s = _dot_nt(q_ref[rows, :], k_ref[pl.ds(k0, width), :]) + bias
        if diagonal:
            keep = lax.broadcasted_iota(I32, (sr, width), 0) >= lax.broadcasted_iota(I32, (sr, width), 1)
            s = jnp.where(keep, s, -jnp.inf)
        m_prev = m_s[rows, :]
        m_new = jnp.maximum(m_prev, jnp.max(s, axis=-1, keepdims=True))
        alpha = jnp.exp2(m_prev - m_new)
        p = jnp.exp2(s - jnp.concatenate([m_new] * (width // dh), axis=1)).astype(BF16)
        pv = _dot(p, vext[pl.ds(k0, width), :])
        acc_s[rows, :] = jnp.concatenate([alpha, alpha], axis=1) * acc_s[rows, :] + pv
        m_s[rows, :] = m_new

    def key_bias(k0, width):
        return (c_ref - cum_ref[:, pl.ds(k0, width)]) * LOG2E

    def body(j, c):
        k0 = pl.multiple_of(j * tk, tk)
        bias = key_bias(k0, tk)
        for r in range(n_sub):
            sub_step(r, k0, tk, bias, False)
        return c

    lax.fori_loop(0, qi * (tq // tk), body, 0)
    for r in range(n_sub):
        if r > 0:
            sub_step(r, q0, r * sr, key_bias(q0, r * sr), False)
        k0 = pl.multiple_of(q0 + r * sr, sr)
        sub_step(r, k0, sr, key_bias(k0, sr), True)
    acc = acc_s[...]
    o_ref[...] = (acc[:, 0:dh] / acc[:, dh:2 * dh]).astype(o_ref.dtype)


def _fox_attention(qkv, cum, nh, tq=FOX_TQ, tk=FOX_TK, sr=FOX_SUB):
    s = qkv.shape[0]
    dh = FOX_HEAD_DIM
    assert s % tq == 0 and tq % tk == 0 and tq % sr == 0 and tk % dh == 0 and dh == LANES
    cum3 = cum.reshape(nh, 1, s)
    return pl.pallas_call(
        functools.partial(_fox_attn_kernel, tq=tq, tk=tk, sr=sr),
        out_shape=jax.ShapeDtypeStruct((s, nh * dh), BF16),
        grid=(nh, s // tq),
        in_specs=[pl.BlockSpec((tq, dh), lambda h, i: (i, h)),
                  pl.BlockSpec((s, dh), lambda h, i: (0, nh + h)),
                  pl.BlockSpec((s, dh), lambda h, i: (0, 2 * nh + h)),
                  pl.BlockSpec((None, 1, s), lambda h, i: (h, 0, 0))],
        out_specs=pl.BlockSpec((tq, dh), lambda h, i: (i, h)),
        scratch_shapes=[pltpu.VMEM((tq, dh), F32), pltpu.VMEM((tq, 2 * dh), F32), pltpu.VMEM((s, 2 * dh), BF16)],
        compiler_params=_cparams(2),
        name="fox_attention",
    )(qkv, qkv, qkv, cum3)


def _fox_mixer(h, hn, j, in_w, f_bias, out_w):
    d = hn.shape[1]
    nh = d // FOX_HEAD_DIM
    q_scale = FOX_HEAD_DIM ** -0.5 * LOG2E
    colscale = jnp.concatenate([jnp.full((1, d), q_scale, F32), jnp.ones((1, 2 * d), F32)], axis=1)
    wt = jnp.swapaxes(in_w, 1, 2)
    qkv = _matmul_t(hn, wt, layer=j, row0=0, nrows=3 * d, tn=MM_TN, tm=MM_TM, out_dtype=BF16, colscale=colscale,
                    name="fox_in_proj")
    f_raw = _matmul_t(hn, wt, layer=j, row0=3 * d, nrows=nh, tn=nh, tm=MM_TM, name="fox_f_proj")
    cum = _fox_cum(f_raw[:, :nh].T, f_bias)
    o = _fox_attention(qkv, cum, nh)
    return _matmul(o, out_w, layer=j, tn=MM_TN, tm=MM_TM, residual=h, name="fox_out_proj")


def _norm_kernel(h_ref, g_ref, o_ref):
    o_ref[...] = _rms(h_ref[...], g_ref[...]).astype(o_ref.dtype)


def _norm(h, g, out_dtype, tm=512):
    n, d = h.shape
    return pl.pallas_call(
        _norm_kernel,
        out_shape=jax.ShapeDtypeStruct((n, d), out_dtype),
        grid=(n // tm,),
        in_specs=[pl.BlockSpec((tm, d), lambda i: (i, 0)), pl.BlockSpec((1, d), lambda i: (0, 0))],
        out_specs=pl.BlockSpec((tm, d), lambda i: (i, 0)),
        compiler_params=_cparams(1),
        name="rmsnorm",
    )(h, g.reshape(1, d))


def kernel(x, norm_mix, norm_ffn, final_norm, pool_w, pool_scale, ssd_in_w, ssd_conv_w, ssd_conv_b,
           ssd_dt_bias, ssd_a_log, ssd_d_skip, ssd_norm, ssd_out_w, fox_in_w, fox_f_bias, fox_out_w,
           moe_group_w, moe_group_b, moe_router_w, moe_router_b, moe_gate_w, moe_up_w, moe_down_w):
    bsz, s, d = x.shape
    assert bsz == 1
    depth = norm_mix.shape[0]
    h = x.reshape(s, d)
    hn = None
    pending = None
    ops = _route_operands(moe_group_w, moe_group_b, moe_router_w, moe_router_b)
    for i in range(depth):
        kind, j = i % 3, i // 3
        if kind == 0:
            h, hn_rows, ri, rg, cnt = _pool_route(h, pending, norm_mix[i], pool_w, pool_scale, j, norm_ffn[i], ops, i)
            pending = None
        else:
            if hn is None:
                hn = _norm(h, norm_mix[i], BF16)
            if kind == 1:
                h = _ssd_mixer(h, hn, j, ssd_in_w, ssd_conv_w[j], ssd_conv_b[j], ssd_dt_bias[j], ssd_a_log[j],
                               ssd_d_skip[j], ssd_norm[j], ssd_out_w)
            else:
                h = _fox_mixer(h, hn, j, fox_in_w, fox_f_bias[j], fox_out_w)
            hn_rows, ri, rg, cnt = _route(h, norm_ffn[i], ops, i)
        ys, dest = _moe_experts_path(hn_rows, ri, cnt, i, moe_gate_w, moe_up_w, moe_down_w)
        hn = None
        if i == depth - 1:
            h = _combine(h, rg, ys, dest, final_norm, "final")
        elif (i + 1) % 3 == 0:
            pending = (rg, ys, dest)
        else:
            h, hn = _combine(h, rg, ys, dest, norm_mix[i + 1], "norm_bf16")
    return h.reshape(bsz, s, d)
```

```python
import functools

import jax
import jax.numpy as jnp
from jax import lax
from jax.experimental import pallas as pl
from jax.experimental.pallas import tpu as pltpu

F32 = jnp.float32
BF16 = jnp.bfloat16
I32 = jnp.int32
U32 = jnp.uint32

RMS_EPS = 1e-6
LOG2E = 1.4426950408889634
LANES = 128
SLAB = 8
VMEM_LIMIT = 56 * 1024 * 1024

MM_TM = 1024
MM_TN = 1024

POOL_WINDOWS = (2, 4, 8, 16)
POOL_HALO = 16
POOL_PAD = 8

SSD_HEAD_DIM = 64
SSD_STATE = 128
SSD_GROUPS = 8
SSD_CONV = 4
SSD_CHUNK = 128
CONV_HALO = 8

FOX_HEAD_DIM = 128
FOX_TQ = 4096
FOX_TK = 1024
FOX_SUB = 256

MOE_GROUPS = 4
MOE_EPG = 8
MOE_EXPERTS = MOE_GROUPS * MOE_EPG
MOE_ROWS = 256
MOE_WSLOTS = 3
MOE_TOK_TILE = 256
ROUTE_TILE = 512


def _cparams(n_axes, vmem=VMEM_LIMIT):
    return pltpu.CompilerParams(dimension_semantics=("arbitrary",) * n_axes, vmem_limit_bytes=vmem)


def _rms(x, g):
    ms = jnp.mean(x * x, axis=-1, keepdims=True)
    return x * lax.rsqrt(ms + RMS_EPS) * g


def _split3(x):
    hi = x.astype(BF16)
    r = x - hi.astype(F32)
    mid = r.astype(BF16)
    lo = (r - mid.astype(F32)).astype(BF16)
    return hi, mid, lo


def _dot(a, b):
    return jnp.dot(a, b, preferred_element_type=F32)


def _dot_nt(a, b):
    return lax.dot_general(a, b, (((1,), (1,)), ((), ())), preferred_element_type=F32)


def _dot3_rhs_exact(x, m):
    hi, mid, lo = _split3(x)
    return _dot(hi, m) + _dot(mid, m) + _dot(lo, m)


def _dot3_lhs_exact(m, x):
    hi, mid, lo = _split3(x)
    return _dot(m, hi) + _dot(m, mid) + _dot(m, lo)


def _dot2_rhs_exact(x, m):
    hi = x.astype(BF16)
    lo = (x - hi.astype(F32)).astype(BF16)
    return _dot(hi, m) + _dot(lo, m)


def _silu(x):
    h = 0.5 * x
    return h + h * jnp.tanh(h)


def _softplus(x):
    return jnp.maximum(x, 0.0) + jnp.log1p(jnp.exp(-jnp.abs(x)))


def _pack_rows(x):
    half = x.shape[1] // 2
    lo = lax.bitcast_convert_type(x[:, :half].astype(BF16).astype(F32), U32)
    hi = lax.bitcast_convert_type(x[:, half:].astype(BF16).astype(F32), U32)
    return hi | (lo >> 16)


def _unpack_rows(u):
    lo = lax.bitcast_convert_type(u << 16, F32)
    hi = lax.bitcast_convert_type(u & jnp.uint32(0xFFFF0000), F32)
    return lo, hi


def _load_token_slabs(ref, rows):
    return [ref[pl.ds(c, rows, stride=SLAB), :] for c in range(SLAB)]


def _store_token_slabs(ref, packed):
    rows = packed.shape[0]
    for c in range(SLAB):
        ref[pl.ds(c, rows, stride=SLAB), :] = packed[:, c * LANES:(c + 1) * LANES]


def _slab(ref, token):
    return ref.at[pl.ds(pl.multiple_of(token * SLAB, SLAB), SLAB)]


def _slab_run(ref, token, count):
    return ref.at[pl.ds(pl.multiple_of(token * SLAB, SLAB), count * SLAB)]


def _mm_kernel(*refs, has_res, has_scale, transposed):
    x_ref, w_ref = refs[:2]
    rest = list(refs[2:])
    r_ref = rest.pop(0) if has_res else None
    s_ref = rest.pop(0) if has_scale else None
    o_ref, wbf = rest

    @pl.when(pl.program_id(1) == 0)
    def _():
        if transposed and wbf.shape[0] != w_ref.shape[0]:
            wbf[...] = jnp.zeros_like(wbf)
            wbf[0:w_ref.shape[0], :] = w_ref[...].astype(BF16)
        else:
            wbf[...] = w_ref[...].astype(BF16)

    acc = _dot_nt(x_ref[...], wbf[...]) if transposed else _dot(x_ref[...], wbf[...])
    if s_ref is not None:
        acc = acc * s_ref[...]
    if r_ref is not None:
        acc = acc + r_ref[...]
    o_ref[...] = acc.astype(o_ref.dtype)


def _matmul(x, w, *, layer=0, col0=0, ncols=None, tn=512, tm=512, residual=None, colscale=None,
            out_dtype=F32, name="mm"):
    m, k = x.shape
    ncols = w.shape[-1] - col0 if ncols is None else ncols
    assert ncols % tn == 0 and col0 % tn == 0 and m % tm == 0
    jb = col0 // tn
    if w.ndim == 3:
        w_spec = pl.BlockSpec((None, k, tn), lambda j, i: (layer, 0, jb + j))
    else:
        w_spec = pl.BlockSpec((k, tn), lambda j, i: (0, jb + j))
    in_specs = [pl.BlockSpec((tm, k), lambda j, i: (i, 0)), w_spec]
    args = [x, w]
    if residual is not None:
        in_specs.append(pl.BlockSpec((tm, tn), lambda j, i: (i, j)))
        args.append(residual)
    if colscale is not None:
        in_specs.append(pl.BlockSpec((1, tn), lambda j, i: (0, j)))
        args.append(colscale)
    return pl.pallas_call(
        functools.partial(_mm_kernel, has_res=residual is not None, has_scale=colscale is not None,
                          transposed=False),
        out_shape=jax.ShapeDtypeStruct((m, ncols), out_dtype),
        grid=(ncols // tn, m // tm),
        in_specs=in_specs,
        out_specs=pl.BlockSpec((tm, tn), lambda j, i: (i, j)),
        scratch_shapes=[pltpu.VMEM((k, tn), BF16)],
        compiler_params=_cparams(2),
        name=name,
    )(*args)


def _matmul_t(x, wt, *, layer, row0=0, nrows=None, tn=512, tm=512, colscale=None, out_dtype=F32, name="mm_t"):
    m, k = x.shape
    nrows = wt.shape[1] - row0 if nrows is None else nrows
    assert nrows % tn == 0 and row0 % tn == 0 and m % tm == 0 and tn % 8 == 0
    assert tn % LANES == 0 or nrows == tn
    tn_out = max(tn, LANES)
    jb = row0 // tn
    in_specs = [pl.BlockSpec((tm, k), lambda j, i: (i, 0)),
                pl.BlockSpec((None, tn, k), lambda j, i: (layer, jb + j, 0))]
    args = [x, wt]
    if colscale is not None:
        in_specs.append(pl.BlockSpec((1, tn_out), lambda j, i: (0, j)))
        args.append(colscale)
    return pl.pallas_call(
        functools.partial(_mm_kernel, has_res=False, has_scale=colscale is not None, transposed=True),
        out_shape=jax.ShapeDtypeStruct((m, (nrows // tn) * tn_out), out_dtype),
        grid=(nrows // tn, m // tm),
        in_specs=in_specs,
        out_specs=pl.BlockSpec((tm, tn_out), lambda j, i: (i, j)),
        scratch_shapes=[pltpu.VMEM((tn_out, k), BF16)],
        compiler_params=_cparams(2),
        name=name,
    )(*args)


def _pool_init(w_ref, xbuf, sa, sb, wbf):
    cur0 = POOL_PAD + POOL_HALO
    xbuf[0:cur0, :] = jnp.zeros((cur0, xbuf.shape[1]), F32)
    sa[0:POOL_PAD, :] = jnp.zeros((POOL_PAD, sa.shape[1]), F32)
    sb[0:POOL_PAD, :] = jnp.zeros((POOL_PAD, sb.shape[1]), F32)
    wbf[...] = w_ref[...].astype(BF16)


def _pool_tile(h, i, g_ref, scale_ref, o_ref, xbuf, sa, sb, wbf, tm, pg):
    lo = POOL_PAD
    cur0 = POOL_PAD + POOL_HALO
    end = cur0 + tm
    hn = _rms(h, g_ref[...])
    xbuf[cur0:end, :] = hn
    row = i * tm + lax.broadcasted_iota(I32, (tm, 1), 0)
    for gi, w in enumerate(POOL_WINDOWS):
        c0 = gi * pg
        cols = slice(c0, c0 + pg)
        src, shift, spare = None, 1, [sa, sb]
        while True:
            first = lo if 2 * shift < w else cur0
            if src is None:
                summed = xbuf[first:end, cols] + xbuf[first - shift:end - shift, cols]
            else:
                summed = src[first:end, :] + src[first - shift:end - shift, :]
            shift *= 2
            if shift == w:
                break
            dst = spare[0] if src is not spare[0] else spare[1]
            dst[first:end, :] = summed
            src = dst
        cur = hn[:, cols]
        cnt = jnp.minimum(row + 1, w).astype(F32)
        pooled = summed / cnt - cur
        mix = _dot(pooled.astype(BF16), wbf[gi]) * scale_ref[:, cols]
        o_ref[:, cols] = h[:, cols] + mix
    xbuf[lo:cur0, :] = xbuf[end - POOL_HALO:end, :]


def _route_kernel(h_ref, g_ref, whl_ref, b_ref, tri_ref, hn_ref, ri_ref, rg_ref, cnt_ref, run_ref, *, tm):
    @pl.when(pl.program_id(0) == 0)
    def _():
        run_ref[...] = jnp.zeros_like(run_ref)

    _route_tile(h_ref[...], g_ref, whl_ref, b_ref, tri_ref, hn_ref, ri_ref, rg_ref, cnt_ref, run_ref, tm)


def _route_tile(h, g_ref, whl_ref, b_ref, tri_ref, hn_ref, ri_ref, rg_ref, cnt_ref, run_ref, tm):
    hn = _rms(h, g_ref[...])
    hn_bf = hn.astype(BF16)
    _store_token_slabs(hn_ref, _pack_rows(hn))
    hn_lo = (hn - hn_bf.astype(F32)).astype(BF16)
    both = _dot(hn_bf, whl_ref[...])
    logits = both[:, 0:LANES] + both[:, LANES:2 * LANES] + _dot(hn_lo, whl_ref[:, 0:LANES]) + b_ref[...]

    lane = lax.broadcasted_iota(I32, (tm, LANES), 1)
    lane_f = lane.astype(F32)
    neg = -jnp.inf
    big = float(LANES)
    gl = jnp.where(lane < MOE_GROUPS, logits, neg)
    gmax = jnp.max(gl, axis=-1, keepdims=True)
    gsel = jnp.min(jnp.where(gl == gmax, lane_f, big), axis=-1, keepdims=True).astype(I32)
    p_grp = 1.0 / jnp.sum(jnp.exp(gl - gmax), axis=-1, keepdims=True)
    lo_lane = MOE_GROUPS + MOE_EPG * gsel
    el = jnp.where((lane >= lo_lane) & (lane < lo_lane + MOE_EPG), logits, neg)
    v0 = jnp.max(el, axis=-1, keepdims=True)
    i0 = jnp.min(jnp.where(el == v0, lane_f, big), axis=-1, keepdims=True).astype(I32)
    el2 = jnp.where(lane == i0, neg, el)
    v1 = jnp.max(el2, axis=-1, keepdims=True)
    i1 = jnp.min(jnp.where(el2 == v1, lane_f, big), axis=-1, keepdims=True).astype(I32)
    t = jnp.exp(v1 - v0)
    gate0 = p_grp / (1.0 + t)
    gate1 = p_grp * t / (1.0 + t)
    e0 = i0 - MOE_GROUPS
    e1 = i1 - MOE_GROUPS

    onehot = ((lane == e0) | (lane == e1))
    c_bf = jnp.where(onehot, 1.0, 0.0).astype(BF16)
    prefix = _dot(tri_ref[...], c_bf) + run_ref[...]
    rank0 = jnp.sum(jnp.where(lane == e0, prefix, 0.0), axis=-1, keepdims=True).astype(I32)
    rank1 = jnp.sum(jnp.where(lane == e1, prefix, 0.0), axis=-1, keepdims=True).astype(I32)
    run = run_ref[...] + jnp.sum(jnp.where(onehot, 1.0, 0.0), axis=0, keepdims=True)
    run_ref[...] = run
    cnt_ref[...] = jnp.broadcast_to(run, cnt_ref.shape)

    ri = jnp.where(lane == 0, e0, jnp.where(lane == 1, e1, jnp.where(lane == 2, rank0, rank1)))
    ri_ref[...] = ri.T[0:8, :]
    rg_ref[...] = jnp.where(lane == 0, gate0, gate1)


def _route_operands(group_w, group_b, router_w, router_b):
    depth, d, _ = group_w.shape
    pad = LANES - MOE_GROUPS - MOE_EXPERTS
    wr = jnp.concatenate([group_w, router_w, jnp.zeros((depth, d, pad), F32)], axis=2)
    br = jnp.concatenate([group_b, router_b, jnp.zeros((depth, pad), F32)], axis=1).reshape(depth, 1, LANES)
    whi = wr.astype(BF16)
    wlo = (wr - whi.astype(F32)).astype(BF16)
    return jnp.concatenate([whi, wlo], axis=2), br


def _prefix_matrix(tm):
    return jnp.tril(jnp.ones((tm, tm), BF16), -1)


def _route_out(n, tm):
    shapes = (jax.ShapeDtypeStruct((n * SLAB, LANES), U32), jax.ShapeDtypeStruct((8, n), I32),
              jax.ShapeDtypeStruct((n, LANES), F32), jax.ShapeDtypeStruct((8, LANES), F32))
    specs = (pl.BlockSpec((tm * SLAB, LANES), lambda i: (i, 0)), pl.BlockSpec((8, tm), lambda i: (0, i)),
             pl.BlockSpec((tm, LANES), lambda i: (i, 0)), pl.BlockSpec((8, LANES), lambda i: (0, 0)))
    return shapes, specs


def _pool_route_kernel(*refs, with_combine, t, pg):
    refs = list(refs)
    if with_combine:
        dest_ref, dest_next_ref, rgp_ref, ys_ref = refs[:4]
        refs = refs[4:]
    h_ref, gmix_ref, pw_ref, scale_ref, gffn_ref, whl_ref, b_ref, tri_ref = refs[:8]
    o_ref, hn_ref, ri_ref, rg_ref, cnt_ref = refs[8:13]
    refs = refs[13:]
    if with_combine:
        buf00, buf01, buf10, buf11, sems, hbuf = refs[:6]
        refs = refs[6:]
    xbuf, sa, sb, wbf, run_ref = refs
    i = pl.program_id(0)

    @pl.when(i == 0)
    def _():
        _pool_init(pw_ref, xbuf, sa, sb, wbf)
        run_ref[...] = jnp.zeros_like(run_ref)

    if with_combine:
        _combine_tile(dest_ref, dest_next_ref, h_ref, rgp_ref, ys_ref, hbuf, ((buf00, buf01), (buf10, buf11)), sems, t)
        h = hbuf[...]
    else:
        h = h_ref[...]
    _pool_tile(h, i, gmix_ref, scale_ref, o_ref, xbuf, sa, sb, wbf, t, pg)
    _route_tile(o_ref[...], gffn_ref, whl_ref, b_ref, tri_ref, hn_ref, ri_ref, rg_ref, cnt_ref, run_ref, t)


def _pool_route(h, pending, g_mix, pool_w, pool_scale, j, g_ffn, route_ops, layer, t=MOE_TOK_TILE):
    n, d = h.shape
    nw, pg, _ = pool_w.shape[1:]
    whl, br = route_ops
    tri = _prefix_matrix(t)
    assert d == 2 * SLAB * LANES
    last = n // t - 1
    row = lambda i: (i, 0)
    const = lambda i: (0, 0)
    in_specs, args, scratch = [], [], []
    if pending is not None:
        rg_prev, ys, dest = pending
        in_specs += [pl.BlockSpec((None, 1, 2 * t), lambda i: (i, 0, 0), memory_space=pltpu.SMEM),
                     pl.BlockSpec((None, 1, 2 * t), lambda i: (jnp.minimum(i + 1, last), 0, 0),
                                  memory_space=pltpu.SMEM),
                     pl.BlockSpec((t, LANES), row),
                     pl.BlockSpec(memory_space=pl.ANY)]
        args += [dest, dest, rg_prev, ys]
        scratch += [pltpu.VMEM((t * SLAB, LANES), U32)] * 4 + [pltpu.SemaphoreType.DMA((2,)), pltpu.VMEM((t, d), F32)]
    in_specs += [pl.BlockSpec((t, d), row), pl.BlockSpec((1, d), const),
                 pl.BlockSpec((None, nw, pg, pg), lambda i: (j, 0, 0, 0)), pl.BlockSpec((1, d), const),
                 pl.BlockSpec((1, d), const), pl.BlockSpec((None, d, 2 * LANES), lambda i: (layer, 0, 0)),
                 pl.BlockSpec((None, 1, LANES), lambda i: (layer, 0, 0)), pl.BlockSpec((t, t), const)]
    args += [h, g_mix.reshape(1, d), pool_w, pool_scale[j].reshape(1, d), g_ffn.reshape(1, d), whl, br, tri]
    r_shapes, r_specs = _route_out(n, t)
    scratch += [pltpu.VMEM((POOL_PAD + POOL_HALO + t, d), F32),
                pltpu.VMEM((POOL_PAD + POOL_HALO + t, pg), F32),
                pltpu.VMEM((POOL_PAD + POOL_HALO + t, pg), F32),
                pltpu.VMEM((nw, pg, pg), BF16),
                pltpu.VMEM((1, LANES), F32)]
    return pl.pallas_call(
        functools.partial(_pool_route_kernel, with_combine=pending is not None, t=t, pg=pg),
        out_shape=(jax.ShapeDtypeStruct((n, d), F32),) + r_shapes,
        grid=(n // t,),
        in_specs=in_specs,
        out_specs=(pl.BlockSpec((t, d), row),) + r_specs,
        scratch_shapes=scratch,
        compiler_params=_cparams(1),
        name="pool_route",
    )(*args)


def _route(h, g, route_ops, layer, tm=ROUTE_TILE):
    n, d = h.shape
    whl, br = route_ops
    tri = _prefix_matrix(tm)
    return pl.pallas_call(
        functools.partial(_route_kernel, tm=tm),
        out_shape=(jax.ShapeDtypeStruct((n * SLAB, LANES), U32),
                   jax.ShapeDtypeStruct((8, n), I32),
                   jax.ShapeDtypeStruct((n, LANES), F32),
                   jax.ShapeDtypeStruct((8, LANES), F32)),
        grid=(n // tm,),
        in_specs=[pl.BlockSpec((tm, d), lambda i: (i, 0)),
                  pl.BlockSpec((1, d), lambda i: (0, 0)),
                  pl.BlockSpec((None, d, 2 * LANES), lambda i: (layer, 0, 0)),
                  pl.BlockSpec((None, 1, LANES), lambda i: (layer, 0, 0)),
                  pl.BlockSpec((tm, tm), lambda i: (0, 0))],
        out_specs=(pl.BlockSpec((tm * SLAB, LANES), lambda i: (i, 0)),
                   pl.BlockSpec((8, tm), lambda i: (0, i)),
                   pl.BlockSpec((tm, LANES), lambda i: (i, 0)),
                   pl.BlockSpec((8, LANES), lambda i: (0, 0))),
        scratch_shapes=[pltpu.VMEM((1, LANES), F32)],
        compiler_params=_cparams(1),
        name="moe_route",
    )(h, g.reshape(1, d), whl, br, tri)


def _dispatch_kernel(plan_ref, dest_ref, hn_ref, xs_ref, zbuf, sem, zsem, *, t):
    sizes = [1 << s for s in reversed(range(MOE_ROWS.bit_length() - 1))]
    zrows = zbuf.shape[0] // SLAB
    n_blocks = xs_ref.shape[0] // (MOE_ROWS * SLAB)

    def zero_copy(off, size):
        return pltpu.make_async_copy(zbuf.at[pl.ds(0, size * SLAB)], _slab_run(xs_ref, off, size), zsem)

    def zero_fill(wait):
        def gaps(e, c):
            off = plan_ref[PLAN_PAD_START, e]
            n = plan_ref[PLAN_PAD_LEN, e]
            for size in sizes:
                @pl.when((n & size) != 0)
                def _():
                    cp = zero_copy(off, size)
                    cp.wait() if wait else cp.start()

                off = off + (n & size)
            return c

        def tail(b, c):
            for part in range(MOE_ROWS // zrows):
                cp = zero_copy(b * MOE_ROWS + part * zrows, zrows)
                cp.wait() if wait else cp.start()
            return c

        lax.fori_loop(0, MOE_EXPERTS, gaps, 0)
        lax.fori_loop(plan_ref[PLAN_NUSED, 0], n_blocks, tail, 0)

    @pl.when(pl.program_id(0) == 0)
    def _():
        zbuf[...] = jnp.zeros_like(zbuf)
        zero_fill(False)

    def row_copy(r, k):
        return pltpu.make_async_copy(_slab(hn_ref, r), _slab(xs_ref, dest_ref[0, k * t + r]), sem)

    def start(r, c):
        row_copy(r, 0).start(priority=0)
        row_copy(r, 1).start(priority=1)
        return c

    lax.fori_loop(0, t, start, 0, unroll=8)
    for _ in range(2):
        pltpu.make_async_copy(hn_ref, xs_ref.at[pl.ds(0, t * SLAB)], sem).wait()

    @pl.when(pl.program_id(0) == pl.num_programs(0) - 1)
    def _():
        zero_fill(True)


def _dispatch(hn, dest, plan, n_rows, t=MOE_TOK_TILE):
    n = hn.shape[0] // SLAB
    grid_spec = pltpu.PrefetchScalarGridSpec(
        num_scalar_prefetch=1,
        grid=(n // t,),
        in_specs=[pl.BlockSpec((None, 1, 2 * t), lambda i, *_: (i, 0, 0), memory_space=pltpu.SMEM),
                  pl.BlockSpec((t * SLAB, LANES), lambda i, *_: (i, 0))],
        out_specs=pl.BlockSpec(memory_space=pl.ANY),
        scratch_shapes=[pltpu.VMEM((MOE_ROWS // 2 * SLAB, LANES), hn.dtype),
                        pltpu.SemaphoreType.DMA(()), pltpu.SemaphoreType.DMA(())],
    )
    return pl.pallas_call(
        functools.partial(_dispatch_kernel, t=t),
        out_shape=jax.ShapeDtypeStruct((n_rows * SLAB, LANES), hn.dtype),
        grid_spec=grid_spec,
        compiler_params=_cparams(1),
        name="moe_dispatch",
    )(plan, dest, hn)


def _expert_kernel(plan_ref, xs_ref, wg_hbm, wu_hbm, wd_hbm,
                   o_ref, wg_raw, wu_raw, wd_raw, wg_bf, wu_bf, wd_bf, sems, *, layer):
    b = pl.program_id(0)
    active = b < plan_ref[PLAN_NUSED, 0]
    expert = plan_ref[PLAN_EXPERT, b]
    second = plan_ref[PLAN_SECOND, 0]

    def fetch(e, slot):
        return [pltpu.make_async_copy(hbm.at[layer, e], raw.at[slot], sems.at[slot, k])
                for k, (hbm, raw) in enumerate(((wg_hbm, wg_raw), (wu_hbm, wu_raw), (wd_hbm, wd_raw)))]

    @pl.when(b == 0)
    def _():
        for c in fetch(expert, 0):
            c.start(priority=1)

        @pl.when(second >= 0)
        def _():
            for c in fetch(second, 1):
                c.start(priority=1)

    @pl.when(active & (plan_ref[PLAN_FIRST, b] == 1))
    def _():
        slot = plan_ref[PLAN_SLOT, b]
        ahead = plan_ref[PLAN_AHEAD, b]

        @pl.when(ahead >= 0)
        def _():
            for c in fetch(ahead, (slot + MOE_WSLOTS - 1) % MOE_WSLOTS):
                c.start(priority=1)

        for c in fetch(expert, slot):
            c.wait()
        wg_bf[...] = wg_raw[slot].astype(BF16)
        wu_bf[...] = wu_raw[slot].astype(BF16)
        wd_bf[...] = wd_raw[slot].astype(BF16)

    @pl.when(active)
    def _():
        rows = xs_ref.shape[0] // SLAB
        halves = [_unpack_rows(u) for u in _load_token_slabs(xs_ref, rows)]
        x = jnp.concatenate([lo.astype(BF16) for lo, _ in halves] + [hi.astype(BF16) for _, hi in halves], axis=1)
        gte = _dot(x, wg_bf[...])
        up = _dot(x, wu_bf[...])
        hb = _silu(gte) * up
        _store_token_slabs(o_ref, _pack_rows(_dot(hb.astype(BF16), wd_bf[...])))

    @pl.when(jnp.logical_not(active))
    def _():
        o_ref[...] = jnp.zeros_like(o_ref)


def _experts(xs, plan, gate_w, up_w, down_w, layer, rows=MOE_ROWS):
    n_rows = xs.shape[0] // SLAB
    d = gate_w.shape[-2]
    hid = gate_w.shape[-1]
    assert d == 2 * SLAB * LANES and MOE_WSLOTS == 3 and n_rows // rows <= LANES
    x_map = lambda b, plan_ref: (jnp.minimum(b, plan_ref[PLAN_NUSED, 0] - 1), 0)
    grid_spec = pltpu.PrefetchScalarGridSpec(
        num_scalar_prefetch=1,
        grid=(n_rows // rows,),
        in_specs=[pl.BlockSpec((rows * SLAB, LANES), x_map),
                  pl.BlockSpec(memory_space=pl.ANY),
                  pl.BlockSpec(memory_space=pl.ANY),
                  pl.BlockSpec(memory_space=pl.ANY)],
        out_specs=pl.BlockSpec((rows * SLAB, LANES), lambda b, *_: (b, 0)),
        scratch_shapes=[pltpu.VMEM((MOE_WSLOTS, d, hid), F32), pltpu.VMEM((MOE_WSLOTS, d, hid), F32),
                        pltpu.VMEM((MOE_WSLOTS, hid, d), F32),
                        pltpu.VMEM((d, hid), BF16), pltpu.VMEM((d, hid), BF16), pltpu.VMEM((hid, d), BF16),
                        pltpu.SemaphoreType.DMA((MOE_WSLOTS, 3))],
    )
    return pl.pallas_call(
        functools.partial(_expert_kernel, layer=layer),
        out_shape=jax.ShapeDtypeStruct((n_rows * SLAB, LANES), U32),
        grid_spec=grid_spec,
        compiler_params=_cparams(1),
        name="moe_experts",
    )(plan, xs, gate_w, up_w, down_w)


def _combine_kernel(*refs, t, mode):
    dest_ref, dest_next_ref, h_ref, rg_ref, ys_ref, g_ref, o_ref = refs[:7]
    rest = list(refs[7:])
    hn_ref = rest.pop(0) if mode == "norm_bf16" else None
    buf00, buf01, buf10, buf11, sems = rest
    _combine_tile(dest_ref, dest_next_ref, h_ref, rg_ref, ys_ref, o_ref, ((buf00, buf01), (buf10, buf11)), sems, t)
    normed = _rms(o_ref[...], g_ref[...])
    if mode == "norm_bf16":
        hn_ref[...] = normed.astype(BF16)
    else:
        o_ref[...] = normed


def _combine_tile(dest_ref, dest_next_ref, h_ref, rg_ref, ys_ref, o_ref, bufs, sems, t):
    i = pl.program_id(0)
    n = pl.num_programs(0)
    half = SLAB * LANES

    def row_copy(dref, slot, r, k):
        return pltpu.make_async_copy(_slab(ys_ref, dref[0, k * t + r]), _slab(bufs[slot][k], r), sems.at[slot])

    def issue(dref, slot):
        def start(r, c):
            row_copy(dref, slot, r, 0).start(priority=0)
            row_copy(dref, slot, r, 1).start(priority=1)
            return c
        lax.fori_loop(0, t, start, 0, unroll=8)

    def drain(slot):
        for k in range(2):
            pltpu.make_async_copy(ys_ref.at[pl.ds(0, t * SLAB)], bufs[slot][k], sems.at[slot]).wait()

    @pl.when(i == 0)
    def _():
        issue(dest_ref, 0)

    for slot in (0, 1):
        @pl.when((i % 2 == slot) & (i + 1 < n))
        def _():
            issue(dest_next_ref, 1 - slot)

        @pl.when(i % 2 == slot)
        def _():
            drain(slot)
            rg = rg_ref[...]
            g0 = rg[:, 0:1]
            g1 = rg[:, 1:2]
            for c, (ua, ub) in enumerate(zip(_load_token_slabs(bufs[slot][0], t), _load_token_slabs(bufs[slot][1], t))):
                a_lo, a_hi = _unpack_rows(ua)
                b_lo, b_hi = _unpack_rows(ub)
                lo_cols = slice(c * LANES, (c + 1) * LANES)
                hi_cols = slice(half + c * LANES, half + (c + 1) * LANES)
                o_ref[:, lo_cols] = h_ref[:, lo_cols] + (g0 * a_lo + g1 * b_lo)
                o_ref[:, hi_cols] = h_ref[:, hi_cols] + (g0 * a_hi + g1 * b_hi)


def _combine(h, rg, ys, dest, g_next, mode, t=MOE_TOK_TILE):
    n, d = h.shape
    assert d == 2 * SLAB * LANES
    last = n // t - 1
    in_specs = [pl.BlockSpec((None, 1, 2 * t), lambda i: (i, 0, 0), memory_space=pltpu.SMEM),
                pl.BlockSpec((None, 1, 2 * t), lambda i: (jnp.minimum(i + 1, last), 0, 0),
                             memory_space=pltpu.SMEM),
                pl.BlockSpec((t, d), lambda i: (i, 0)),
                pl.BlockSpec((t, LANES), lambda i: (i, 0)),
                pl.BlockSpec(memory_space=pl.ANY),
                pl.BlockSpec((1, d), lambda i: (0, 0))]
    args = [dest, dest, h, rg, ys, g_next.reshape(1, d)]
    tile = pl.BlockSpec((t, d), lambda i: (i, 0))
    assert mode in ("norm_bf16", "final")
    if mode == "norm_bf16":
        out_shape = (jax.ShapeDtypeStruct((n, d), F32), jax.ShapeDtypeStruct((n, d), BF16))
        out_specs = (tile, tile)
    else:
        out_shape = jax.ShapeDtypeStruct((n, d), F32)
        out_specs = tile
    return pl.pallas_call(
        functools.partial(_combine_kernel, t=t, mode=mode),
        out_shape=out_shape,
        grid=(n // t,),
        in_specs=in_specs,
        out_specs=out_specs,
        scratch_shapes=[pltpu.VMEM((t * SLAB, LANES), U32)] * 4 + [pltpu.SemaphoreType.DMA((2,))],
        compiler_params=_cparams(1),
        name="moe_combine",
    )(*args)


def _moe_experts_path(hn, ri, cnt, layer, gate_w, up_w, down_w):
    n = hn.shape[0] // SLAB
    t = MOE_TOK_TILE
    n_rows = -(-(n * 2 + MOE_EXPERTS * (MOE_ROWS - 1)) // MOE_ROWS) * MOE_ROWS
    dest, plan = _moe_plan(ri, cnt)
    dest = dest[0:2].reshape(2, n // t, t).transpose(1, 0, 2).reshape(n // t, 1, 2 * t)
    xs = _dispatch(hn, dest, plan, n_rows)
    return _experts(xs, plan, gate_w, up_w, down_w, layer), dest


PLAN_EXPERT = 0
PLAN_FIRST = 1
PLAN_SLOT = 2
PLAN_AHEAD = 3
PLAN_SECOND = 4
PLAN_NUSED = 5
PLAN_PAD_START = 6
PLAN_PAD_LEN = 7


def _plan_kernel(ri_ref, cnt_ref, tri_ref, dest_ref, plan_ref, *, chunk):
    e_lane = lax.broadcasted_iota(I32, (LANES, LANES), 1)
    e_sub = lax.broadcasted_iota(I32, (LANES, LANES), 0)
    tri = tri_ref[...]
    rows = float(MOE_ROWS)

    cnt_l = jnp.where(e_lane < MOE_EXPERTS, jnp.broadcast_to(cnt_ref[0:1, :], (LANES, LANES)), 0.0)
    cnt_s = cnt_l.T
    nblk_l = jnp.floor((cnt_l + (rows - 1.0)) * (1.0 / rows))
    nblk_s = jnp.floor((cnt_s + (rows - 1.0)) * (1.0 / rows))
    end_l = _dot_nt(nblk_l.astype(BF16), tri)
    end_s = _dot(tri, nblk_s.astype(BF16))
    start_l = end_l - nblk_l
    start_s = end_s - nblk_s
    row_start_l = start_l * rows
    n_used = end_l[0:1, LANES - 1:LANES]

    row_start_bf = row_start_l[0:SLAB, :].astype(BF16)
    n = ri_ref.shape[1]
    for c in range(n // chunk):
        cols = slice(c * chunk, (c + 1) * chunk)
        parts = []
        for k in range(2):
            onehot = (lax.broadcasted_iota(I32, (LANES, chunk), 0) == ri_ref[k:k + 1, cols])
            base = _dot(row_start_bf, jnp.where(onehot, 1.0, 0.0).astype(BF16))[0:1, :]
            parts.append(base.astype(I32) + ri_ref[2 + k:3 + k, cols])
        dest_ref[:, cols] = jnp.concatenate(parts + [jnp.zeros((SLAB - 2, chunk), I32)], axis=0)

    blk = e_lane.astype(F32)
    valid = e_sub < MOE_EXPERTS
    block_exp = jnp.sum(jnp.where(valid & (end_s <= blk), 1.0, 0.0), axis=0, keepdims=True)
    block_exp = jnp.minimum(block_exp, float(MOE_EXPERTS - 1))
    mine = e_sub.astype(F32) == block_exp
    blk_row = blk[0:1, :]
    first = (blk_row == jnp.sum(jnp.where(mine, start_s, 0.0), axis=0, keepdims=True)) & (blk_row < n_used)
    nonempty_s = nblk_s > 0.0
    ordinal_s = _dot(tri, jnp.where(nonempty_s, 1.0, 0.0).astype(BF16)) - 1.0
    pos = jnp.sum(jnp.where(mine, ordinal_s, 0.0), axis=0, keepdims=True)
    slot = pos - MOE_WSLOTS * jnp.floor((pos + 0.5) * (1.0 / MOE_WSLOTS))

    def expert_at(target):
        hit = nonempty_s & (ordinal_s == target)
        found = jnp.sum(jnp.where(hit, 1.0, 0.0), axis=0, keepdims=True) > 0.0
        return jnp.where(found, jnp.sum(jnp.where(hit, e_sub.astype(F32), 0.0), axis=0, keepdims=True), -1.0)

    ahead = expert_at(pos + (MOE_WSLOTS - 1.0))
    second = expert_at(jnp.ones((1, LANES), F32))
    pad_start = (row_start_l + cnt_l)[0:1, :]
    pad_len = (nblk_l * rows - cnt_l)[0:1, :]
    table = [block_exp, jnp.where(first, 1.0, 0.0), slot, ahead, second,
             jnp.broadcast_to(n_used, (1, LANES)), pad_start, pad_len]
    plan_ref[...] = jnp.concatenate(table, axis=0).astype(I32)


def _moe_plan(ri, cnt, chunk=2048):
    n = ri.shape[1]
    tri = jnp.tril(jnp.ones((LANES, LANES), BF16))
    return pl.pallas_call(
        functools.partial(_plan_kernel, chunk=chunk),
        out_shape=(jax.ShapeDtypeStruct((SLAB, n), I32), jax.ShapeDtypeStruct((SLAB, LANES), I32)),
        grid=(1,),
        in_specs=[pl.BlockSpec((SLAB, n), lambda i: (0, 0)), pl.BlockSpec((SLAB, LANES), lambda i: (0, 0)),
                  pl.BlockSpec((LANES, LANES), lambda i: (0, 0))],
        out_specs=(pl.BlockSpec((SLAB, n), lambda i: (0, 0)), pl.BlockSpec((SLAB, LANES), lambda i: (0, 0))),
        compiler_params=_cparams(1),
        name="moe_plan",
    )(ri, cnt, tri)


def _ssd_kernel(z_ref, x_ref, b_ref, c_ref, dt_ref, cwx_ref, cwb_ref, cwc_ref, cbx_ref, cbb_ref, cbc_ref,
                dtb_ref, a_ref, dexp_ref, nw_ref, exp_ref, tri_ref, o_ref,
                xbuf, bbuf, cbuf, xc_s, xdt_s, xw_s, b_s, c_s, eacs_s, acs_s, acst_s, y_s, st_s,
                *, L, hp, n_state, groups):
    ci = pl.program_id(0)
    d_inner = x_ref.shape[1]
    gw = d_inner // groups
    kh = gw // hp
    H0 = CONV_HALO

    @pl.when(ci == 0)
    def _():
        xbuf[0:H0, :] = jnp.zeros((H0, xbuf.shape[1]), F32)
        bbuf[0:H0, :] = jnp.zeros((H0, bbuf.shape[1]), F32)
        cbuf[0:H0, :] = jnp.zeros((H0, cbuf.shape[1]), F32)
        st_s[...] = jnp.zeros_like(st_s)

    def conv_silu(in_ref, buf, cw_ref, cb_ref):
        buf[H0:H0 + L, :] = in_ref[...].astype(F32)
        acc = cb_ref[...] + cw_ref[SSD_CONV - 1:SSD_CONV, :] * buf[H0:H0 + L, :]
        for j in range(SSD_CONV - 1):
            off = H0 - (SSD_CONV - 1) + j
            acc = acc + cw_ref[j:j + 1, :] * buf[off:off + L, :]
        buf[0:H0, :] = buf[L:L + H0, :]
        return _silu(acc)

    xc_s[...] = conv_silu(x_ref, xbuf, cwx_ref, cbx_ref)
    b_s[...] = conv_silu(b_ref, bbuf, cwb_ref, cbb_ref).astype(BF16)
    c_s[...] = conv_silu(c_ref, cbuf, cwc_ref, cbc_ref).astype(BF16)

    dt = _softplus(dt_ref[...] + dtb_ref[...])
    d_a = dt * a_ref[...]
    acs = _dot3_lhs_exact(tri_ref[...], d_a)
    acs_s[...] = acs
    acst_s[...] = acs.T
    emat = exp_ref[...]
    acs_last = acs[L - 1:L, :]
    to_end = dt * jnp.exp(acs_last - acs)
    xc = xc_s[...]
    xdt_s[...] = (xc * _dot2_rhs_exact(dt, emat)).astype(BF16)
    xw_s[...] = (xc * _dot2_rhs_exact(to_end, emat)).astype(BF16)
    eacs_s[...] = _dot2_rhs_exact(jnp.exp(acs), emat)
    cdecay = _dot2_rhs_exact(jnp.broadcast_to(jnp.exp(acs_last), (SLAB, LANES)), emat)[0:1, :]

    rr = lax.broadcasted_iota(I32, (L, L), 0)
    cc = lax.broadcasted_iota(I32, (L, L), 1)
    causal = rr >= cc
    lane2 = lax.broadcasted_iota(I32, (L, 2 * hp), 1)

    for g in range(groups):
        bg = b_s[:, g * n_state:(g + 1) * n_state]
        cg = c_s[:, g * n_state:(g + 1) * n_state]
        cb = _dot_nt(cg, bg)
        st_prev = st_s[g]
        y_off = _dot(cg, st_prev.astype(BF16)) * eacs_s[:, g * gw:(g + 1) * gw]
        for kp in range(kh // 2):
            c0 = g * gw + kp * 2 * hp
            slab = xdt_s[:, c0:c0 + 2 * hp]
            ms, xs = [], []
            for half in range(2):
                hd = g * kh + kp * 2 + half
                seg = acs_s[:, hd:hd + 1] - acst_s[hd:hd + 1, :]
                decay = jnp.exp(jnp.where(causal, seg, -jnp.inf))
                ms.append((cb * decay).astype(BF16))
                keep = (lane2 < hp) if half == 0 else (lane2 >= hp)
                xs.append(jnp.where(keep, slab, jnp.zeros_like(slab)))
            y_pair = _dot(jnp.concatenate(ms, axis=1), jnp.concatenate(xs, axis=0))
            y_s[:, c0:c0 + 2 * hp] = y_pair + y_off[:, kp * 2 * hp:(kp + 1) * 2 * hp]
        bgt = bg.astype(F32).T.astype(BF16)
        s_new = _dot(bgt, xw_s[:, g * gw:(g + 1) * gw])
        st_s[g] = cdecay[:, g * gw:(g + 1) * gw] * st_prev + s_new

    y = y_s[...] + dexp_ref[...] * xc_s[...]
    z = z_ref[...].astype(F32)
    y = y * _silu(z)
    for g in range(groups):
        yg = y[:, g * gw:(g + 1) * gw]
        yg = yg * lax.rsqrt(jnp.mean(yg * yg, axis=-1, keepdims=True) + RMS_EPS)
        o_ref[:, g * gw:(g + 1) * gw] = (yg * nw_ref[:, g * gw:(g + 1) * gw]).astype(o_ref.dtype)


def _ssd_scan(zx, dt_raw, conv_w, conv_b, dt_bias, a_log, d_skip, norm_w, d_inner):
    s = zx.shape[0]
    L = SSD_CHUNK
    hp, n_state, groups = SSD_HEAD_DIM, SSD_STATE, SSD_GROUPS
    heads = d_inner // hp
    gn = groups * n_state
    gw = d_inner // groups
    assert d_inner % gn == 0 and heads <= LANES
    pad = LANES - heads
    dtb = jnp.pad(dt_bias, (0, pad)).reshape(1, LANES)
    a_neg = jnp.pad(-jnp.exp(a_log), (0, pad)).reshape(1, LANES)
    dexp = jnp.repeat(d_skip, hp).reshape(1, d_inner)
    emat = (jnp.arange(LANES)[:, None] == (jnp.arange(d_inner)[None, :] // hp)).astype(BF16)
    tri = jnp.tril(jnp.ones((L, L), BF16))
    cb2 = conv_b.reshape(1, -1)
    xb = d_inner // gn
    row = lambda c: (c, 0)
    const = lambda c: (0, 0)
    in_specs = [
        pl.BlockSpec((L, d_inner), row),
        pl.BlockSpec((L, d_inner), lambda c: (c, 1)),
        pl.BlockSpec((L, gn), lambda c: (c, 2 * xb)),
        pl.BlockSpec((L, gn), lambda c: (c, 2 * xb + 1)),
        pl.BlockSpec((L, LANES), row),
        pl.BlockSpec((SSD_CONV, d_inner), const),
        pl.BlockSpec((SSD_CONV, gn), lambda c: (0, xb)),
        pl.BlockSpec((SSD_CONV, gn), lambda c: (0, xb + 1)),
        pl.BlockSpec((1, d_inner), const),
        pl.BlockSpec((1, gn), lambda c: (0, xb)),
        pl.BlockSpec((1, gn), lambda c: (0, xb + 1)),
        pl.BlockSpec((1, LANES), const),
        pl.BlockSpec((1, LANES), const),
        pl.BlockSpec((1, d_inner), const),
        pl.BlockSpec((1, d_inner), const),
        pl.BlockSpec((LANES, d_inner), const),
        pl.BlockSpec((L, L), const),
    ]
    scratch = [
        pltpu.VMEM((CONV_HALO + L, d_inner), F32), pltpu.VMEM((CONV_HALO + L, gn), F32),
        pltpu.VMEM((CONV_HALO + L, gn), F32),
        pltpu.VMEM((L, d_inner), F32),
        pltpu.VMEM((L, d_inner), BF16),
        pltpu.VMEM((L, d_inner), BF16),
        pltpu.VMEM((L, gn), BF16), pltpu.VMEM((L, gn), BF16),
        pltpu.VMEM((L, d_inner), F32),
        pltpu.VMEM((L, LANES), F32), pltpu.VMEM((LANES, L), F32),
        pltpu.VMEM((L, d_inner), F32),
        pltpu.VMEM((groups, n_state, gw), F32),
    ]
    return pl.pallas_call(
        functools.partial(_ssd_kernel, L=L, hp=hp, n_state=n_state, groups=groups),
        out_shape=jax.ShapeDtypeStruct((s, d_inner), BF16),
        grid=(s // L,),
        in_specs=in_specs,
        out_specs=pl.BlockSpec((L, d_inner), row),
        scratch_shapes=scratch,
        compiler_params=_cparams(1),
        name="ssd_scan",
    )(zx, zx, zx, zx, dt_raw, conv_w, conv_w, conv_w, cb2, cb2, cb2, dtb, a_neg, dexp,
      norm_w.reshape(1, d_inner), emat, tri)


def _ssd_mixer(h, hn, j, in_w, conv_w, conv_b, dt_bias, a_log, d_skip, norm_w, out_w):
    d_inner = out_w.shape[1]
    heads = d_inner // SSD_HEAD_DIM
    wide = in_w.shape[2] - heads
    wt = jnp.swapaxes(in_w, 1, 2)
    zx = _matmul_t(hn, wt, layer=j, row0=0, nrows=wide, tn=MM_TN, tm=MM_TM, out_dtype=BF16, name="ssd_in_proj")
    dt_raw = _matmul_t(hn, wt, layer=j, row0=wide, nrows=heads, tn=heads, tm=MM_TM, name="ssd_dt_proj")
    y = _ssd_scan(zx, dt_raw, conv_w, conv_b, dt_bias, a_log, d_skip, norm_w, d_inner)
    return _matmul(y, out_w, layer=j, tn=MM_TN // 2, tm=MM_TM, residual=h, name="ssd_out_proj")


def _fox_cum_kernel(f_ref, b_ref, tri_ref, o_ref, carry):
    @pl.when(pl.program_id(0) == 0)
    def _():
        carry[...] = jnp.zeros_like(carry)

    x = f_ref[...] + b_ref[...]
    log_f = jnp.minimum(x, 0.0) - jnp.log1p(jnp.exp(-jnp.abs(x)))
    cs = _dot3_rhs_exact(log_f, tri_ref[...]) + carry[...]
    o_ref[...] = cs
    carry[...] = cs[:, cs.shape[1] - 1:]


def _fox_cum(f_t, bias, tk=512):
    nh, s = f_t.shape
    triu = jnp.triu(jnp.ones((tk, tk), BF16))
    return pl.pallas_call(
        _fox_cum_kernel,
        out_shape=jax.ShapeDtypeStruct((nh, s), F32),
        grid=(s // tk,),
        in_specs=[pl.BlockSpec((nh, tk), lambda i: (0, i)),
                  pl.BlockSpec((nh, 1), lambda i: (0, 0)),
                  pl.BlockSpec((tk, tk), lambda i: (0, 0))],
        out_specs=pl.BlockSpec((nh, tk), lambda i: (0, i)),
        scratch_shapes=[pltpu.VMEM((nh, 1), F32)],
        compiler_params=_cparams(1),
        name="fox_cum",
    )(f_t, bias.reshape(nh, 1), triu)


def _fox_attn_kernel(q_ref, k_ref, v_ref, cum_ref, o_ref, m_s, acc_s, vext, *, tq, tk, sr):
    qi = pl.program_id(1)
    dh = q_ref.shape[1]
    n_sub = tq // sr

    @pl.when(qi == 0)
    def _():
        vext[:, 0:dh] = v_ref[...]
        vext[:, dh:2 * dh] = jnp.ones((vext.shape[0], dh), BF16)

    q0 = pl.multiple_of(qi * tq, tq)
    c_ref = cum_ref[:, pl.ds(q0, LANES)][:, 0:1]
    m_s[...] = jnp.full_like(m_s, -jnp.inf)
    acc_s[...] = jnp.zeros_like(acc_s)

    def sub_step(r, k0, width, bias, diagonal):
        rows = slice(r * sr, (r + 1) * sr)
        s = _dot_nt(q_ref[rows, :], k_ref[pl.ds(k0, width), :]) + bias
        if diagonal:
            keep = lax.broadcasted_iota(I32, (sr, width), 0) >= lax.broadcasted_iota(I32, (sr, width), 1)
            s = jnp.where(keep, s, -jnp.inf)
        m_prev = m_s[rows, :]
        m_new = jnp.maximum(m_prev, jnp.max(s, axis=-1, keepdims=True))
        alpha = jnp.exp2(m_prev - m_new)
        p = jnp.exp2(s - jnp.concatenate([m_new] * (width // dh), axis=1)).astype(BF16)
        pv = _dot(p, vext[pl.ds(k0, width), :])
        acc_s[rows, :] = jnp.concatenate([alpha, alpha], axis=1) * acc_s[rows, :] + pv
        m_s[rows, :] = m_new

    def key_bias(k0, width):
        return (c_ref - cum_ref[:, pl.ds(k0, width)]) * LOG2E

    def body(j, c):
        k0 = pl.multiple_of(j * tk, tk)
        bias = key_bias(k0, tk)
        for r in range(n_sub):
            sub_step(r, k0, tk, bias, False)
        return c

    lax.fori_loop(0, qi * (tq // tk), body, 0)
    per = tk // sr
    for c in range(tq // tk):
        kc = pl.multiple_of(q0 + c * tk, tk)
        bias = key_bias(kc, tk)
        for r in range((c + 1) * per, n_sub):
            sub_step(r, kc, tk, bias, False)
        for r in range(c * per, (c + 1) * per):
            w = (r - c * per) * sr
            if w > 0:
                sub_step(r, kc, w, bias[:, 0:w], False)
            sub_step(r, pl.multiple_of(kc + w, sr), sr, bias[:, w:w + sr], True)
    acc = acc_s[...]
    o_ref[...] = (acc[:, 0:dh] / acc[:, dh:2 * dh]).astype(o_ref.dtype)


def _fox_attention(qkv, cum, nh, tq=FOX_TQ, tk=FOX_TK, sr=FOX_SUB):
    s = qkv.shape[0]
    dh = FOX_HEAD_DIM
    assert s % tq == 0 and tq % tk == 0 and tq % sr == 0 and tk % dh == 0 and dh == LANES
    cum3 = cum.reshape(nh, 1, s)
    return pl.pallas_call(
        functools.partial(_fox_attn_kernel, tq=tq, tk=tk, sr=sr),
        out_shape=jax.ShapeDtypeStruct((s, nh * dh), BF16),
        grid=(nh, s // tq),
        in_specs=[pl.BlockSpec((tq, dh), lambda h, i: (i, h)),
                  pl.BlockSpec((s, dh), lambda h, i: (0, nh + h)),
                  pl.BlockSpec((s, dh), lambda h, i: (0, 2 * nh + h)),
                  pl.BlockSpec((None, 1, s), lambda h, i: (h, 0, 0))],
        out_specs=pl.BlockSpec((tq, dh), lambda h, i: (i, h)),
        scratch_shapes=[pltpu.VMEM((tq, dh), F32), pltpu.VMEM((tq, 2 * dh), F32), pltpu.VMEM((s, 2 * dh), BF16)],
        compiler_params=_cparams(2),
        name="fox_attention",
    )(qkv, qkv, qkv, cum3)


def _fox_mixer(h, hn, j, in_w, f_bias, out_w):
    d = hn.shape[1]
    nh = d // FOX_HEAD_DIM
    q_scale = FOX_HEAD_DIM ** -0.5 * LOG2E
    colscale = jnp.concatenate([jnp.full((1, d), q_scale, F32), jnp.ones((1, 2 * d), F32)], axis=1)
    wt = jnp.swapaxes(in_w, 1, 2)
    qkv = _matmul_t(hn, wt, layer=j, row0=0, nrows=3 * d, tn=MM_TN, tm=MM_TM, out_dtype=BF16, colscale=colscale,
                    name="fox_in_proj")
    f_raw = _matmul_t(hn, wt, layer=j, row0=3 * d, nrows=nh, tn=nh, tm=MM_TM, name="fox_f_proj")
    cum = _fox_cum(f_raw[:, :nh].T, f_bias)
    o = _fox_attention(qkv, cum, nh)
    return _matmul(o, out_w, layer=j, tn=MM_TN, tm=MM_TM, residual=h, name="fox_out_proj")


def _norm_kernel(h_ref, g_ref, o_ref):
    o_ref[...] = _rms(h_ref[...], g_ref[...]).astype(o_ref.dtype)


def _norm(h, g, out_dtype, tm=512):
    n, d = h.shape
    return pl.pallas_call(
        _norm_kernel,
        out_shape=jax.ShapeDtypeStruct((n, d), out_dtype),
        grid=(n // tm,),
        in_specs=[pl.BlockSpec((tm, d), lambda i: (i, 0)), pl.BlockSpec((1, d), lambda i: (0, 0))],
        out_specs=pl.BlockSpec((tm, d), lambda i: (i, 0)),
        compiler_params=_cparams(1),
        name="rmsnorm",
    )(h, g.reshape(1, d))


def kernel(x, norm_mix, norm_ffn, final_norm, pool_w, pool_scale, ssd_in_w, ssd_conv_w, ssd_conv_b,
           ssd_dt_bias, ssd_a_log, ssd_d_skip, ssd_norm, ssd_out_w, fox_in_w, fox_f_bias, fox_out_w,
           moe_group_w, moe_group_b, moe_router_w, moe_router_b, moe_gate_w, moe_up_w, moe_down_w):
    bsz, s, d = x.shape
    assert bsz == 1
    depth = norm_mix.shape[0]
    h = x.reshape(s, d)
    hn = None
    pending = None
    ops = _route_operands(moe_group_w, moe_group_b, moe_router_w, moe_router_b)
    for i in range(depth):
        kind, j = i % 3, i // 3
        if kind == 0:
            h, hn_rows, ri, rg, cnt = _pool_route(h, pending, norm_mix[i], pool_w, pool_scale, j, norm_ffn[i], ops, i)
            pending = None
        else:
            if hn is None:
                hn = _norm(h, norm_mix[i], BF16)
            if kind == 1:
                h = _ssd_mixer(h, hn, j, ssd_in_w, ssd_conv_w[j], ssd_conv_b[j], ssd_dt_bias[j], ssd_a_log[j],
                               ssd_d_skip[j], ssd_norm[j], ssd_out_w)
            else:
                h = _fox_mixer(h, hn, j, fox_in_w, fox_f_bias[j], fox_out_w)
            hn_rows, ri, rg, cnt = _route(h, norm_ffn[i], ops, i)
        ys, dest = _moe_experts_path(hn_rows, ri, cnt, i, moe_gate_w, moe_up_w, moe_down_w)
        hn = None
        if i == depth - 1:
            h = _combine(h, rg, ys, dest, final_norm, "final")
        elif (i + 1) % 3 == 0:
            pending = (rg, ys, dest)
        else:
            h, hn = _combine(h, rg, ys, dest, norm_mix[i + 1], "norm_bf16")
    return h.reshape(bsz, s, d)
```

```python
import functools

import jax
import jax.numpy as jnp
from jax import lax
from jax.experimental import pallas as pl
from jax.experimental.pallas import tpu as pltpu

F32 = jnp.float32
BF16 = jnp.bfloat16
I32 = jnp.int32
U32 = jnp.uint32

RMS_EPS = 1e-6
LOG2E = 1.4426950408889634
LANES = 128
SLAB = 8
VMEM_LIMIT = 56 * 1024 * 1024

MM_TM = 1024
MM_TN = 1024

POOL_WINDOWS = (2, 4, 8, 16)
POOL_HALO = 16
POOL_PAD = 8

SSD_HEAD_DIM = 64
SSD_STATE = 128
SSD_GROUPS = 8
SSD_CONV = 4
SSD_CHUNK = 128
CONV_HALO = 8

FOX_HEAD_DIM = 128
FOX_TQ = 4096
FOX_TK = 1024
FOX_SUB = 256

MOE_GROUPS = 4
MOE_EPG = 8
MOE_EXPERTS = MOE_GROUPS * MOE_EPG
MOE_ROWS = 256
MOE_WSLOTS = 3
MOE_TOK_TILE = 512
ROUTE_TILE = 512


def _cparams(n_axes, vmem=VMEM_LIMIT):
    return pltpu.CompilerParams(dimension_semantics=("arbitrary",) * n_axes, vmem_limit_bytes=vmem)


def _rms(x, g):
    ms = jnp.mean(x * x, axis=-1, keepdims=True)
    return x * lax.rsqrt(ms + RMS_EPS) * g


def _split3(x):
    hi = x.astype(BF16)
    r = x - hi.astype(F32)
    mid = r.astype(BF16)
    lo = (r - mid.astype(F32)).astype(BF16)
    return hi, mid, lo


def _dot(a, b):
    return jnp.dot(a, b, preferred_element_type=F32)


def _dot_nt(a, b):
    return lax.dot_general(a, b, (((1,), (1,)), ((), ())), preferred_element_type=F32)


def _dot3_rhs_exact(x, m):
    hi, mid, lo = _split3(x)
    return _dot(hi, m) + _dot(mid, m) + _dot(lo, m)


def _dot3_lhs_exact(m, x):
    hi, mid, lo = _split3(x)
    return _dot(m, hi) + _dot(m, mid) + _dot(m, lo)


def _dot2_rhs_exact(x, m):
    hi = x.astype(BF16)
    lo = (x - hi.astype(F32)).astype(BF16)
    return _dot(hi, m) + _dot(lo, m)


def _silu(x):
    h = 0.5 * x
    return h + h * jnp.tanh(h)


def _softplus(x):
    return jnp.maximum(x, 0.0) + jnp.log1p(jnp.exp(-jnp.abs(x)))


def _pack_rows(x):
    half = x.shape[1] // 2
    lo = lax.bitcast_convert_type(x[:, :half].astype(BF16).astype(F32), U32)
    hi = lax.bitcast_convert_type(x[:, half:].astype(BF16).astype(F32), U32)
    return hi | (lo >> 16)


def _unpack_rows(u):
    lo = lax.bitcast_convert_type(u << 16, F32)
    hi = lax.bitcast_convert_type(u & jnp.uint32(0xFFFF0000), F32)
    return lo, hi


def _load_token_slabs(ref, rows):
    return [ref[pl.ds(c, rows, stride=SLAB), :] for c in range(SLAB)]


def _store_token_slabs(ref, packed):
    rows = packed.shape[0]
    for c in range(SLAB):
        ref[pl.ds(c, rows, stride=SLAB), :] = packed[:, c * LANES:(c + 1) * LANES]


def _slab(ref, token):
    return ref.at[pl.ds(pl.multiple_of(token * SLAB, SLAB), SLAB)]


def _slab_run(ref, token, count):
    return ref.at[pl.ds(pl.multiple_of(token * SLAB, SLAB), count * SLAB)]


def _mm_kernel(*refs, has_res, has_scale, transposed):
    x_ref, w_ref = refs[:2]
    rest = list(refs[2:])
    r_ref = rest.pop(0) if has_res else None
    s_ref = rest.pop(0) if has_scale else None
    o_ref, wbf = rest

    @pl.when(pl.program_id(1) == 0)
    def _():
        if transposed and wbf.shape[0] != w_ref.shape[0]:
            wbf[...] = jnp.zeros_like(wbf)
            wbf[0:w_ref.shape[0], :] = w_ref[...].astype(BF16)
        else:
            wbf[...] = w_ref[...].astype(BF16)

    acc = _dot_nt(x_ref[...], wbf[...]) if transposed else _dot(x_ref[...], wbf[...])
    if s_ref is not None:
        acc = acc * s_ref[...]
    if r_ref is not None:
        acc = acc + r_ref[...]
    o_ref[...] = acc.astype(o_ref.dtype)


def _matmul(x, w, *, layer=0, col0=0, ncols=None, tn=512, tm=512, residual=None, colscale=None,
            out_dtype=F32, name="mm"):
    m, k = x.shape
    ncols = w.shape[-1] - col0 if ncols is None else ncols
    assert ncols % tn == 0 and col0 % tn == 0 and m % tm == 0
    jb = col0 // tn
    if w.ndim == 3:
        w_spec = pl.BlockSpec((None, k, tn), lambda j, i: (layer, 0, jb + j))
    else:
        w_spec = pl.BlockSpec((k, tn), lambda j, i: (0, jb + j))
    in_specs = [pl.BlockSpec((tm, k), lambda j, i: (i, 0)), w_spec]
    args = [x, w]
    if residual is not None:
        in_specs.append(pl.BlockSpec((tm, tn), lambda j, i: (i, j)))
        args.append(residual)
    if colscale is not None:
        in_specs.append(pl.BlockSpec((1, tn), lambda j, i: (0, j)))
        args.append(colscale)
    return pl.pallas_call(
        functools.partial(_mm_kernel, has_res=residual is not None, has_scale=colscale is not None,
                          transposed=False),
        out_shape=jax.ShapeDtypeStruct((m, ncols), out_dtype),
        grid=(ncols // tn, m // tm),
        in_specs=in_specs,
        out_specs=pl.BlockSpec((tm, tn), lambda j, i: (i, j)),
        scratch_shapes=[pltpu.VMEM((k, tn), BF16)],
        compiler_params=_cparams(2),
        name=name,
    )(*args)


def _matmul_t(x, wt, *, layer, row0=0, nrows=None, tn=512, tm=512, colscale=None, out_dtype=F32, name="mm_t"):
    m, k = x.shape
    nrows = wt.shape[1] - row0 if nrows is None else nrows
    assert nrows % tn == 0 and row0 % tn == 0 and m % tm == 0 and tn % 8 == 0
    assert tn % LANES == 0 or nrows == tn
    tn_out = max(tn, LANES)
    jb = row0 // tn
    in_specs = [pl.BlockSpec((tm, k), lambda j, i: (i, 0)),
                pl.BlockSpec((None, tn, k), lambda j, i: (layer, jb + j, 0))]
    args = [x, wt]
    if colscale is not None:
        in_specs.append(pl.BlockSpec((1, tn_out), lambda j, i: (0, j)))
        args.append(colscale)
    return pl.pallas_call(
        functools.partial(_mm_kernel, has_res=False, has_scale=colscale is not None, transposed=True),
        out_shape=jax.ShapeDtypeStruct((m, (nrows // tn) * tn_out), out_dtype),
        grid=(nrows // tn, m // tm),
        in_specs=in_specs,
        out_specs=pl.BlockSpec((tm, tn_out), lambda j, i: (i, j)),
        scratch_shapes=[pltpu.VMEM((tn_out, k), BF16)],
        compiler_params=_cparams(2),
        name=name,
    )(*args)


def _pool_init(w_ref, xbuf, sa, sb, wbf):
    cur0 = POOL_PAD + POOL_HALO
    xbuf[0:cur0, :] = jnp.zeros((cur0, xbuf.shape[1]), F32)
    sa[0:POOL_PAD, :] = jnp.zeros((POOL_PAD, sa.shape[1]), F32)
    sb[0:POOL_PAD, :] = jnp.zeros((POOL_PAD, sb.shape[1]), F32)
    wbf[...] = w_ref[...].astype(BF16)


def _pool_tile(h, i, g_ref, scale_ref, o_ref, xbuf, sa, sb, wbf, tm, pg):
    lo = POOL_PAD
    cur0 = POOL_PAD + POOL_HALO
    end = cur0 + tm
    hn = _rms(h, g_ref[...])
    xbuf[cur0:end, :] = hn
    row = i * tm + lax.broadcasted_iota(I32, (tm, 1), 0)
    for gi, w in enumerate(POOL_WINDOWS):
        c0 = gi * pg
        cols = slice(c0, c0 + pg)
        src, shift, spare = None, 1, [sa, sb]
        while True:
            first = lo if 2 * shift < w else cur0
            if src is None:
                summed = xbuf[first:end, cols] + xbuf[first - shift:end - shift, cols]
            else:
                summed = src[first:end, :] + src[first - shift:end - shift, :]
            shift *= 2
            if shift == w:
                break
            dst = spare[0] if src is not spare[0] else spare[1]
            dst[first:end, :] = summed
            src = dst
        cur = hn[:, cols]
        cnt = jnp.minimum(row + 1, w).astype(F32)
        pooled = summed / cnt - cur
        mix = _dot(pooled.astype(BF16), wbf[gi]) * scale_ref[:, cols]
        o_ref[:, cols] = h[:, cols] + mix
    xbuf[lo:cur0, :] = xbuf[end - POOL_HALO:end, :]


def _route_kernel(h_ref, g_ref, whl_ref, b_ref, tri_ref, hn_ref, ri_ref, rg_ref, cnt_ref, run_ref, *, tm):
    @pl.when(pl.program_id(0) == 0)
    def _():
        run_ref[...] = jnp.zeros_like(run_ref)

    _route_tile(h_ref[...], g_ref, whl_ref, b_ref, tri_ref, hn_ref, ri_ref, rg_ref, cnt_ref, run_ref, tm)


def _route_tile(h, g_ref, whl_ref, b_ref, tri_ref, hn_ref, ri_ref, rg_ref, cnt_ref, run_ref, tm):
    hn = _rms(h, g_ref[...])
    hn_bf = hn.astype(BF16)
    _store_token_slabs(hn_ref, _pack_rows(hn))
    hn_lo = (hn - hn_bf.astype(F32)).astype(BF16)
    both = _dot(hn_bf, whl_ref[...])
    logits = both[:, 0:LANES] + both[:, LANES:2 * LANES] + _dot(hn_lo, whl_ref[:, 0:LANES]) + b_ref[...]

    lane = lax.broadcasted_iota(I32, (tm, LANES), 1)
    lane_f = lane.astype(F32)
    neg = -jnp.inf
    big = float(LANES)
    gl = jnp.where(lane < MOE_GROUPS, logits, neg)
    gmax = jnp.max(gl, axis=-1, keepdims=True)
    gsel = jnp.min(jnp.where(gl == gmax, lane_f, big), axis=-1, keepdims=True).astype(I32)
    p_grp = 1.0 / jnp.sum(jnp.exp(gl - gmax), axis=-1, keepdims=True)
    lo_lane = MOE_GROUPS + MOE_EPG * gsel
    el = jnp.where((lane >= lo_lane) & (lane < lo_lane + MOE_EPG), logits, neg)
    v0 = jnp.max(el, axis=-1, keepdims=True)
    i0 = jnp.min(jnp.where(el == v0, lane_f, big), axis=-1, keepdims=True).astype(I32)
    el2 = jnp.where(lane == i0, neg, el)
    v1 = jnp.max(el2, axis=-1, keepdims=True)
    i1 = jnp.min(jnp.where(el2 == v1, lane_f, big), axis=-1, keepdims=True).astype(I32)
    t = jnp.exp(v1 - v0)
    gate0 = p_grp / (1.0 + t)
    gate1 = p_grp * t / (1.0 + t)
    e0 = i0 - MOE_GROUPS
    e1 = i1 - MOE_GROUPS

    onehot = ((lane == e0) | (lane == e1))
    c_bf = jnp.where(onehot, 1.0, 0.0).astype(BF16)
    prefix = _dot(tri_ref[...], c_bf) + run_ref[...]
    rank0 = jnp.sum(jnp.where(lane == e0, prefix, 0.0), axis=-1, keepdims=True).astype(I32)
    rank1 = jnp.sum(jnp.where(lane == e1, prefix, 0.0), axis=-1, keepdims=True).astype(I32)
    run = run_ref[...] + jnp.sum(jnp.where(onehot, 1.0, 0.0), axis=0, keepdims=True)
    run_ref[...] = run
    cnt_ref[...] = jnp.broadcast_to(run, cnt_ref.shape)

    ri = jnp.where(lane == 0, e0, jnp.where(lane == 1, e1, jnp.where(lane == 2, rank0, rank1)))
    ri_ref[...] = ri.T[0:8, :]
    rg_ref[...] = jnp.where(lane == 0, gate0, gate1)


def _route_operands(group_w, group_b, router_w, router_b):
    depth, d, _ = group_w.shape
    pad = LANES - MOE_GROUPS - MOE_EXPERTS
    wr = jnp.concatenate([group_w, router_w, jnp.zeros((depth, d, pad), F32)], axis=2)
    br = jnp.concatenate([group_b, router_b, jnp.zeros((depth, pad), F32)], axis=1).reshape(depth, 1, LANES)
    whi = wr.astype(BF16)
    wlo = (wr - whi.astype(F32)).astype(BF16)
    return jnp.concatenate([whi, wlo], axis=2), br


def _prefix_matrix(tm):
    return jnp.tril(jnp.ones((tm, tm), BF16), -1)


def _route_out(n, tm):
    shapes = (jax.ShapeDtypeStruct((n * SLAB, LANES), U32), jax.ShapeDtypeStruct((8, n), I32),
              jax.ShapeDtypeStruct((n, LANES), F32), jax.ShapeDtypeStruct((8, LANES), F32))
    specs = (pl.BlockSpec((tm * SLAB, LANES), lambda i: (i, 0)), pl.BlockSpec((8, tm), lambda i: (0, i)),
             pl.BlockSpec((tm, LANES), lambda i: (i, 0)), pl.BlockSpec((8, LANES), lambda i: (0, 0)))
    return shapes, specs


def _pool_route_kernel(*refs, with_combine, t, pg):
    refs = list(refs)
    if with_combine:
        dest_ref, dest_next_ref, rgp_ref, ys_ref = refs[:4]
        refs = refs[4:]
    h_ref, gmix_ref, pw_ref, scale_ref, gffn_ref, whl_ref, b_ref, tri_ref = refs[:8]
    o_ref, hn_ref, ri_ref, rg_ref, cnt_ref = refs[8:13]
    refs = refs[13:]
    if with_combine:
        buf00, buf01, buf10, buf11, sems, hbuf = refs[:6]
        refs = refs[6:]
    xbuf, sa, sb, wbf, run_ref = refs
    i = pl.program_id(0)

    @pl.when(i == 0)
    def _():
        _pool_init(pw_ref, xbuf, sa, sb, wbf)
        run_ref[...] = jnp.zeros_like(run_ref)

    if with_combine:
        _combine_tile(dest_ref, dest_next_ref, h_ref, rgp_ref, ys_ref, hbuf, ((buf00, buf01), (buf10, buf11)), sems, t)
        h = hbuf[...]
    else:
        h = h_ref[...]
    _pool_tile(h, i, gmix_ref, scale_ref, o_ref, xbuf, sa, sb, wbf, t, pg)
    _route_tile(o_ref[...], gffn_ref, whl_ref, b_ref, tri_ref, hn_ref, ri_ref, rg_ref, cnt_ref, run_ref, t)


def _pool_route(h, pending, g_mix, pool_w, pool_scale, j, g_ffn, route_ops, layer, t=MOE_TOK_TILE):
    n, d = h.shape
    nw, pg, _ = pool_w.shape[1:]
    whl, br = route_ops
    tri = _prefix_matrix(t)
    assert d == 2 * SLAB * LANES
    last = n // t - 1
    row = lambda i: (i, 0)
    const = lambda i: (0, 0)
    in_specs, args, scratch = [], [], []
    if pending is not None:
        rg_prev, ys, dest = pending
        in_specs += [pl.BlockSpec((None, 1, 2 * t), lambda i: (i, 0, 0), memory_space=pltpu.SMEM),
                     pl.BlockSpec((None, 1, 2 * t), lambda i: (jnp.minimum(i + 1, last), 0, 0),
                                  memory_space=pltpu.SMEM),
                     pl.BlockSpec((t, LANES), row),
                     pl.BlockSpec(memory_space=pl.ANY)]
        args += [dest, dest, rg_prev, ys]
        scratch += [pltpu.VMEM((t * SLAB, LANES), U32)] * 4 + [pltpu.SemaphoreType.DMA((2,)), pltpu.VMEM((t, d), F32)]
    in_specs += [pl.BlockSpec((t, d), row), pl.BlockSpec((1, d), const),
                 pl.BlockSpec((None, nw, pg, pg), lambda i: (j, 0, 0, 0)), pl.BlockSpec((1, d), const),
                 pl.BlockSpec((1, d), const), pl.BlockSpec((None, d, 2 * LANES), lambda i: (layer, 0, 0)),
                 pl.BlockSpec((None, 1, LANES), lambda i: (layer, 0, 0)), pl.BlockSpec((t, t), const)]
    args += [h, g_mix.reshape(1, d), pool_w, pool_scale[j].reshape(1, d), g_ffn.reshape(1, d), whl, br, tri]
    r_shapes, r_specs = _route_out(n, t)
    scratch += [pltpu.VMEM((POOL_PAD + POOL_HALO + t, d), F32),
                pltpu.VMEM((POOL_PAD + POOL_HALO + t, pg), F32),
                pltpu.VMEM((POOL_PAD + POOL_HALO + t, pg), F32),
                pltpu.VMEM((nw, pg, pg), BF16),
                pltpu.VMEM((1, LANES), F32)]
    return pl.pallas_call(
        functools.partial(_pool_route_kernel, with_combine=pending is not None, t=t, pg=pg),
        out_shape=(jax.ShapeDtypeStruct((n, d), F32),) + r_shapes,
        grid=(n // t,),
        in_specs=in_specs,
        out_specs=(pl.BlockSpec((t, d), row),) + r_specs,
        scratch_shapes=scratch,
        compiler_params=_cparams(1),
        name="pool_route",
    )(*args)


def _route(h, g, route_ops, layer, tm=ROUTE_TILE):
    n, d = h.shape
    whl, br = route_ops
    tri = _prefix_matrix(tm)
    return pl.pallas_call(
        functools.partial(_route_kernel, tm=tm),
        out_shape=(jax.ShapeDtypeStruct((n * SLAB, LANES), U32),
                   jax.ShapeDtypeStruct((8, n), I32),
                   jax.ShapeDtypeStruct((n, LANES), F32),
                   jax.ShapeDtypeStruct((8, LANES), F32)),
        grid=(n // tm,),
        in_specs=[pl.BlockSpec((tm, d), lambda i: (i, 0)),
                  pl.BlockSpec((1, d), lambda i: (0, 0)),
                  pl.BlockSpec((None, d, 2 * LANES), lambda i: (layer, 0, 0)),
                  pl.BlockSpec((None, 1, LANES), lambda i: (layer, 0, 0)),
                  pl.BlockSpec((tm, tm), lambda i: (0, 0))],
        out_specs=(pl.BlockSpec((tm * SLAB, LANES), lambda i: (i, 0)),
                   pl.BlockSpec((8, tm), lambda i: (0, i)),
                   pl.BlockSpec((tm, LANES), lambda i: (i, 0)),
                   pl.BlockSpec((8, LANES), lambda i: (0, 0))),
        scratch_shapes=[pltpu.VMEM((1, LANES), F32)],
        compiler_params=_cparams(1),
        name="moe_route",
    )(h, g.reshape(1, d), whl, br, tri)


def _dispatch_kernel(plan_ref, dest_ref, hn_ref, xs_ref, zbuf, sem, zsem, *, t):
    sizes = [1 << s for s in reversed(range(MOE_ROWS.bit_length() - 1))]
    zrows = zbuf.shape[0] // SLAB
    n_blocks = xs_ref.shape[0] // (MOE_ROWS * SLAB)

    def zero_copy(off, size):
        return pltpu.make_async_copy(zbuf.at[pl.ds(0, size * SLAB)], _slab_run(xs_ref, off, size), zsem)

    def zero_fill(wait):
        def gaps(e, c):
            off = plan_ref[PLAN_PAD_START, e]
            n = plan_ref[PLAN_PAD_LEN, e]
            for size in sizes:
                @pl.when((n & size) != 0)
                def _():
                    cp = zero_copy(off, size)
                    cp.wait() if wait else cp.start()

                off = off + (n & size)
            return c

        def tail(b, c):
            for part in range(MOE_ROWS // zrows):
                cp = zero_copy(b * MOE_ROWS + part * zrows, zrows)
                cp.wait() if wait else cp.start()
            return c

        lax.fori_loop(0, MOE_EXPERTS, gaps, 0)
        lax.fori_loop(plan_ref[PLAN_NUSED, 0], n_blocks, tail, 0)

    @pl.when(pl.program_id(0) == 0)
    def _():
        zbuf[...] = jnp.zeros_like(zbuf)
        zero_fill(False)

    def row_copy(r, k):
        return pltpu.make_async_copy(_slab(hn_ref, r), _slab(xs_ref, dest_ref[0, k * t + r]), sem)

    def start(r, c):
        row_copy(r, 0).start(priority=0)
        row_copy(r, 1).start(priority=1)
        return c

    lax.fori_loop(0, t, start, 0, unroll=8)
    for _ in range(2):
        pltpu.make_async_copy(hn_ref, xs_ref.at[pl.ds(0, t * SLAB)], sem).wait()

    @pl.when(pl.program_id(0) == pl.num_programs(0) - 1)
    def _():
        zero_fill(True)


def _dispatch(hn, dest, plan, n_rows, t=MOE_TOK_TILE):
    n = hn.shape[0] // SLAB
    grid_spec = pltpu.PrefetchScalarGridSpec(
        num_scalar_prefetch=1,
        grid=(n // t,),
        in_specs=[pl.BlockSpec((None, 1, 2 * t), lambda i, *_: (i, 0, 0), memory_space=pltpu.SMEM),
                  pl.BlockSpec((t * SLAB, LANES), lambda i, *_: (i, 0))],
        out_specs=pl.BlockSpec(memory_space=pl.ANY),
        scratch_shapes=[pltpu.VMEM((MOE_ROWS // 2 * SLAB, LANES), hn.dtype),
                        pltpu.SemaphoreType.DMA(()), pltpu.SemaphoreType.DMA(())],
    )
    return pl.pallas_call(
        functools.partial(_dispatch_kernel, t=t),
        out_shape=jax.ShapeDtypeStruct((n_rows * SLAB, LANES), hn.dtype),
        grid_spec=grid_spec,
        compiler_params=_cparams(1),
        name="moe_dispatch",
    )(plan, dest, hn)


def _expert_kernel(plan_ref, xs_ref, wg_hbm, wu_hbm, wd_hbm,
                   o_ref, wg_raw, wu_raw, wd_raw, wg_bf, wu_bf, wd_bf, sems, *, layer):
    b = pl.program_id(0)
    active = b < plan_ref[PLAN_NUSED, 0]
    expert = plan_ref[PLAN_EXPERT, b]
    second = plan_ref[PLAN_SECOND, 0]

    def fetch(e, slot):
        return [pltpu.make_async_copy(hbm.at[layer, e], raw.at[slot], sems.at[slot, k])
                for k, (hbm, raw) in enumerate(((wg_hbm, wg_raw), (wu_hbm, wu_raw), (wd_hbm, wd_raw)))]

    @pl.when(b == 0)
    def _():
        for c in fetch(expert, 0):
            c.start(priority=1)

        @pl.when(second >= 0)
        def _():
            for c in fetch(second, 1):
                c.start(priority=1)

    @pl.when(active & (plan_ref[PLAN_FIRST, b] == 1))
    def _():
        slot = plan_ref[PLAN_SLOT, b]
        ahead = plan_ref[PLAN_AHEAD, b]

        @pl.when(ahead >= 0)
        def _():
            for c in fetch(ahead, (slot + MOE_WSLOTS - 1) % MOE_WSLOTS):
                c.start(priority=1)

        for c in fetch(expert, slot):
            c.wait()
        wg_bf[...] = wg_raw[slot].astype(BF16)
        wu_bf[...] = wu_raw[slot].astype(BF16)
        wd_bf[...] = wd_raw[slot].astype(BF16)

    @pl.when(active)
    def _():
        rows = xs_ref.shape[0] // SLAB
        halves = [_unpack_rows(u) for u in _load_token_slabs(xs_ref, rows)]
        x = jnp.concatenate([lo.astype(BF16) for lo, _ in halves] + [hi.astype(BF16) for _, hi in halves], axis=1)
        gte = _dot(x, wg_bf[...])
        up = _dot(x, wu_bf[...])
        hb = _silu(gte) * up
        _store_token_slabs(o_ref, _pack_rows(_dot(hb.astype(BF16), wd_bf[...])))

    @pl.when(jnp.logical_not(active))
    def _():
        o_ref[...] = jnp.zeros_like(o_ref)


def _experts(xs, plan, gate_w, up_w, down_w, layer, rows=MOE_ROWS):
    n_rows = xs.shape[0] // SLAB
    d = gate_w.shape[-2]
    hid = gate_w.shape[-1]
    assert d == 2 * SLAB * LANES and MOE_WSLOTS == 3 and n_rows // rows <= LANES
    x_map = lambda b, plan_ref: (jnp.minimum(b, plan_ref[PLAN_NUSED, 0] - 1), 0)
    grid_spec = pltpu.PrefetchScalarGridSpec(
        num_scalar_prefetch=1,
        grid=(n_rows // rows,),
        in_specs=[pl.BlockSpec((rows * SLAB, LANES), x_map),
                  pl.BlockSpec(memory_space=pl.ANY),
                  pl.BlockSpec(memory_space=pl.ANY),
                  pl.BlockSpec(memory_space=pl.ANY)],
        out_specs=pl.BlockSpec((rows * SLAB, LANES), lambda b, *_: (b, 0)),
        scratch_shapes=[pltpu.VMEM((MOE_WSLOTS, d, hid), F32), pltpu.VMEM((MOE_WSLOTS, d, hid), F32),
                        pltpu.VMEM((MOE_WSLOTS, hid, d), F32),
                        pltpu.VMEM((d, hid), BF16), pltpu.VMEM((d, hid), BF16), pltpu.VMEM((hid, d), BF16),
                        pltpu.SemaphoreType.DMA((MOE_WSLOTS, 3))],
    )
    return pl.pallas_call(
        functools.partial(_expert_kernel, layer=layer),
        out_shape=jax.ShapeDtypeStruct((n_rows * SLAB, LANES), U32),
        grid_spec=grid_spec,
        compiler_params=_cparams(1),
        name="moe_experts",
    )(plan, xs, gate_w, up_w, down_w)


def _combine_kernel(*refs, t, mode):
    dest_ref, dest_next_ref, h_ref, rg_ref, ys_ref, g_ref, o_ref = refs[:7]
    rest = list(refs[7:])
    hn_ref = rest.pop(0) if mode == "norm_bf16" else None
    buf00, buf01, buf10, buf11, sems = rest
    _combine_tile(dest_ref, dest_next_ref, h_ref, rg_ref, ys_ref, o_ref, ((buf00, buf01), (buf10, buf11)), sems, t)
    normed = _rms(o_ref[...], g_ref[...])
    if mode == "norm_bf16":
        hn_ref[...] = normed.astype(BF16)
    else:
        o_ref[...] = normed


def _combine_tile(dest_ref, dest_next_ref, h_ref, rg_ref, ys_ref, o_ref, bufs, sems, t):
    i = pl.program_id(0)
    n = pl.num_programs(0)
    half = SLAB * LANES

    def row_copy(dref, slot, r, k):
        return pltpu.make_async_copy(_slab(ys_ref, dref[0, k * t + r]), _slab(bufs[slot][k], r), sems.at[slot])

    def issue(dref, slot):
        def start(r, c):
            row_copy(dref, slot, r, 0).start(priority=0)
            row_copy(dref, slot, r, 1).start(priority=1)
            return c
        lax.fori_loop(0, t, start, 0, unroll=8)

    def drain(slot):
        for k in range(2):
            pltpu.make_async_copy(ys_ref.at[pl.ds(0, t * SLAB)], bufs[slot][k], sems.at[slot]).wait()

    @pl.when(i == 0)
    def _():
        issue(dest_ref, 0)

    for slot in (0, 1):
        @pl.when((i % 2 == slot) & (i + 1 < n))
        def _():
            issue(dest_next_ref, 1 - slot)

        @pl.when(i % 2 == slot)
        def _():
            drain(slot)
            rg = rg_ref[...]
            g0 = rg[:, 0:1]
            g1 = rg[:, 1:2]
            for c, (ua, ub) in enumerate(zip(_load_token_slabs(bufs[slot][0], t), _load_token_slabs(bufs[slot][1], t))):
                a_lo, a_hi = _unpack_rows(ua)
                b_lo, b_hi = _unpack_rows(ub)
                lo_cols = slice(c * LANES, (c + 1) * LANES)
                hi_cols = slice(half + c * LANES, half + (c + 1) * LANES)
                o_ref[:, lo_cols] = h_ref[:, lo_cols] + (g0 * a_lo + g1 * b_lo)
                o_ref[:, hi_cols] = h_ref[:, hi_cols] + (g0 * a_hi + g1 * b_hi)


def _combine(h, rg, ys, dest, g_next, mode, t=MOE_TOK_TILE):
    n, d = h.shape
    assert d == 2 * SLAB * LANES
    last = n // t - 1
    in_specs = [pl.BlockSpec((None, 1, 2 * t), lambda i: (i, 0, 0), memory_space=pltpu.SMEM),
                pl.BlockSpec((None, 1, 2 * t), lambda i: (jnp.minimum(i + 1, last), 0, 0),
                             memory_space=pltpu.SMEM),
                pl.BlockSpec((t, d), lambda i: (i, 0)),
                pl.BlockSpec((t, LANES), lambda i: (i, 0)),
                pl.BlockSpec(memory_space=pl.ANY),
                pl.BlockSpec((1, d), lambda i: (0, 0))]
    args = [dest, dest, h, rg, ys, g_next.reshape(1, d)]
    tile = pl.BlockSpec((t, d), lambda i: (i, 0))
    assert mode in ("norm_bf16", "final")
    if mode == "norm_bf16":
        out_shape = (jax.ShapeDtypeStruct((n, d), F32), jax.ShapeDtypeStruct((n, d), BF16))
        out_specs = (tile, tile)
    else:
        out_shape = jax.ShapeDtypeStruct((n, d), F32)
        out_specs = tile
    return pl.pallas_call(
        functools.partial(_combine_kernel, t=t, mode=mode),
        out_shape=out_shape,
        grid=(n // t,),
        in_specs=in_specs,
        out_specs=out_specs,
        scratch_shapes=[pltpu.VMEM((t * SLAB, LANES), U32)] * 4 + [pltpu.SemaphoreType.DMA((2,))],
        compiler_params=_cparams(1),
        name="moe_combine",
    )(*args)


def _moe_experts_path(hn, ri, cnt, layer, gate_w, up_w, down_w):
    n = hn.shape[0] // SLAB
    t = MOE_TOK_TILE
    n_rows = -(-(n * 2 + MOE_EXPERTS * (MOE_ROWS - 1)) // MOE_ROWS) * MOE_ROWS
    dest, plan = _moe_plan(ri, cnt)
    dest = dest[0:2].reshape(2, n // t, t).transpose(1, 0, 2).reshape(n // t, 1, 2 * t)
    xs = _dispatch(hn, dest, plan, n_rows)
    return _experts(xs, plan, gate_w, up_w, down_w, layer), dest


PLAN_EXPERT = 0
PLAN_FIRST = 1
PLAN_SLOT = 2
PLAN_AHEAD = 3
PLAN_SECOND = 4
PLAN_NUSED = 5
PLAN_PAD_START = 6
PLAN_PAD_LEN = 7


def _plan_kernel(ri_ref, cnt_ref, tri_ref, dest_ref, plan_ref, *, chunk):
    e_lane = lax.broadcasted_iota(I32, (LANES, LANES), 1)
    e_sub = lax.broadcasted_iota(I32, (LANES, LANES), 0)
    tri = tri_ref[...]
    rows = float(MOE_ROWS)

    cnt_l = jnp.where(e_lane < MOE_EXPERTS, jnp.broadcast_to(cnt_ref[0:1, :], (LANES, LANES)), 0.0)
    cnt_s = cnt_l.T
    nblk_l = jnp.floor((cnt_l + (rows - 1.0)) * (1.0 / rows))
    nblk_s = jnp.floor((cnt_s + (rows - 1.0)) * (1.0 / rows))
    end_l = _dot_nt(nblk_l.astype(BF16), tri)
    end_s = _dot(tri, nblk_s.astype(BF16))
    start_l = end_l - nblk_l
    start_s = end_s - nblk_s
    row_start_l = start_l * rows
    n_used = end_l[0:1, LANES - 1:LANES]

    row_start_bf = row_start_l[0:SLAB, :].astype(BF16)
    n = ri_ref.shape[1]
    for c in range(n // chunk):
        cols = slice(c * chunk, (c + 1) * chunk)
        parts = []
        for k in range(2):
            onehot = (lax.broadcasted_iota(I32, (LANES, chunk), 0) == ri_ref[k:k + 1, cols])
            base = _dot(row_start_bf, jnp.where(onehot, 1.0, 0.0).astype(BF16))[0:1, :]
            parts.append(base.astype(I32) + ri_ref[2 + k:3 + k, cols])
        dest_ref[:, cols] = jnp.concatenate(parts + [jnp.zeros((SLAB - 2, chunk), I32)], axis=0)

    blk = e_lane.astype(F32)
    valid = e_sub < MOE_EXPERTS
    block_exp = jnp.sum(jnp.where(valid & (end_s <= blk), 1.0, 0.0), axis=0, keepdims=True)
    block_exp = jnp.minimum(block_exp, float(MOE_EXPERTS - 1))
    mine = e_sub.astype(F32) == block_exp
    blk_row = blk[0:1, :]
    first = (blk_row == jnp.sum(jnp.where(mine, start_s, 0.0), axis=0, keepdims=True)) & (blk_row < n_used)
    nonempty_s = nblk_s > 0.0
    ordinal_s = _dot(tri, jnp.where(nonempty_s, 1.0, 0.0).astype(BF16)) - 1.0
    pos = jnp.sum(jnp.where(mine, ordinal_s, 0.0), axis=0, keepdims=True)
    slot = pos - MOE_WSLOTS * jnp.floor((pos + 0.5) * (1.0 / MOE_WSLOTS))

    def expert_at(target):
        hit = nonempty_s & (ordinal_s == target)
        found = jnp.sum(jnp.where(hit, 1.0, 0.0), axis=0, keepdims=True) > 0.0
        return jnp.where(found, jnp.sum(jnp.where(hit, e_sub.astype(F32), 0.0), axis=0, keepdims=True), -1.0)

    ahead = expert_at(pos + (MOE_WSLOTS - 1.0))
    second = expert_at(jnp.ones((1, LANES), F32))
    pad_start = (row_start_l + cnt_l)[0:1, :]
    pad_len = (nblk_l * rows - cnt_l)[0:1, :]
    table = [block_exp, jnp.where(first, 1.0, 0.0), slot, ahead, second,
             jnp.broadcast_to(n_used, (1, LANES)), pad_start, pad_len]
    plan_ref[...] = jnp.concatenate(table, axis=0).astype(I32)


def _moe_plan(ri, cnt, chunk=2048):
    n = ri.shape[1]
    tri = jnp.tril(jnp.ones((LANES, LANES), BF16))
    return pl.pallas_call(
        functools.partial(_plan_kernel, chunk=chunk),
        out_shape=(jax.ShapeDtypeStruct((SLAB, n), I32), jax.ShapeDtypeStruct((SLAB, LANES), I32)),
        grid=(1,),
        in_specs=[pl.BlockSpec((SLAB, n), lambda i: (0, 0)), pl.BlockSpec((SLAB, LANES), lambda i: (0, 0)),
                  pl.BlockSpec((LANES, LANES), lambda i: (0, 0))],
        out_specs=(pl.BlockSpec((SLAB, n), lambda i: (0, 0)), pl.BlockSpec((SLAB, LANES), lambda i: (0, 0))),
        compiler_params=_cparams(1),
        name="moe_plan",
    )(ri, cnt, tri)


def _ssd_kernel(z_ref, x_ref, b_ref, c_ref, dt_ref, cwx_ref, cwb_ref, cwc_ref, cbx_ref, cbb_ref, cbc_ref,
                dtb_ref, a_ref, dexp_ref, nw_ref, exp_ref, tri_ref, o_ref,
                xbuf, bbuf, cbuf, xc_s, xdt_s, xw_s, b_s, c_s, eacs_s, acs_s, acst_s, y_s, st_s,
                *, L, hp, n_state, groups):
    ci = pl.program_id(0)
    d_inner = x_ref.shape[1]
    gw = d_inner // groups
    kh = gw // hp
    H0 = CONV_HALO

    @pl.when(ci == 0)
    def _():
        xbuf[0:H0, :] = jnp.zeros((H0, xbuf.shape[1]), F32)
        bbuf[0:H0, :] = jnp.zeros((H0, bbuf.shape[1]), F32)
        cbuf[0:H0, :] = jnp.zeros((H0, cbuf.shape[1]), F32)
        st_s[...] = jnp.zeros_like(st_s)

    def conv_silu(in_ref, buf, cw_ref, cb_ref):
        buf[H0:H0 + L, :] = in_ref[...].astype(F32)
        acc = cb_ref[...] + cw_ref[SSD_CONV - 1:SSD_CONV, :] * buf[H0:H0 + L, :]
        for j in range(SSD_CONV - 1):
            off = H0 - (SSD_CONV - 1) + j
            acc = acc + cw_ref[j:j + 1, :] * buf[off:off + L, :]
        buf[0:H0, :] = buf[L:L + H0, :]
        return _silu(acc)

    xc_s[...] = conv_silu(x_ref, xbuf, cwx_ref, cbx_ref)
    b_s[...] = conv_silu(b_ref, bbuf, cwb_ref, cbb_ref).astype(BF16)
    c_s[...] = conv_silu(c_ref, cbuf, cwc_ref, cbc_ref).astype(BF16)

    dt = _softplus(dt_ref[...] + dtb_ref[...])
    d_a = dt * a_ref[...]
    acs = _dot3_lhs_exact(tri_ref[...], d_a)
    acs_s[...] = acs
    acst_s[...] = acs.T
    emat = exp_ref[...]
    acs_last = acs[L - 1:L, :]
    to_end = dt * jnp.exp(acs_last - acs)
    xc = xc_s[...]
    xdt_s[...] = (xc * _dot2_rhs_exact(dt, emat)).astype(BF16)
    xw_s[...] = (xc * _dot2_rhs_exact(to_end, emat)).astype(BF16)
    eacs_s[...] = _dot2_rhs_exact(jnp.exp(acs), emat)
    cdecay = _dot2_rhs_exact(jnp.broadcast_to(jnp.exp(acs_last), (SLAB, LANES)), emat)[0:1, :]

    rr = lax.broadcasted_iota(I32, (L, L), 0)
    cc = lax.broadcasted_iota(I32, (L, L), 1)
    causal = rr >= cc
    lane2 = lax.broadcasted_iota(I32, (L, 2 * hp), 1)

    for g in range(groups):
        bg = b_s[:, g * n_state:(g + 1) * n_state]
        cg = c_s[:, g * n_state:(g + 1) * n_state]
        cb = _dot_nt(cg, bg)
        st_prev = st_s[g]
        y_off = _dot(cg, st_prev.astype(BF16)) * eacs_s[:, g * gw:(g + 1) * gw]
        for kp in range(kh // 2):
            c0 = g * gw + kp * 2 * hp
            slab = xdt_s[:, c0:c0 + 2 * hp]
            ms, xs = [], []
            for half in range(2):
                hd = g * kh + kp * 2 + half
                seg = acs_s[:, hd:hd + 1] - acst_s[hd:hd + 1, :]
                decay = jnp.exp(jnp.where(causal, seg, -jnp.inf))
                ms.append((cb * decay).astype(BF16))
                keep = (lane2 < hp) if half == 0 else (lane2 >= hp)
                xs.append(jnp.where(keep, slab, jnp.zeros_like(slab)))
            y_pair = _dot(jnp.concatenate(ms, axis=1), jnp.concatenate(xs, axis=0))
            y_s[:, c0:c0 + 2 * hp] = y_pair + y_off[:, kp * 2 * hp:(kp + 1) * 2 * hp]
        bgt = bg.astype(F32).T.astype(BF16)
        s_new = _dot(bgt, xw_s[:, g * gw:(g + 1) * gw])
        st_s[g] = cdecay[:, g * gw:(g + 1) * gw] * st_prev + s_new

    y = y_s[...] + dexp_ref[...] * xc_s[...]
    z = z_ref[...].astype(F32)
    y = y * _silu(z)
    for g in range(groups):
        yg = y[:, g * gw:(g + 1) * gw]
        yg = yg * lax.rsqrt(jnp.mean(yg * yg, axis=-1, keepdims=True) + RMS_EPS)
        o_ref[:, g * gw:(g + 1) * gw] = (yg * nw_ref[:, g * gw:(g + 1) * gw]).astype(o_ref.dtype)


def _ssd_scan(zx, dt_raw, conv_w, conv_b, dt_bias, a_log, d_skip, norm_w, d_inner):
    s = zx.shape[0]
    L = SSD_CHUNK
    hp, n_state, groups = SSD_HEAD_DIM, SSD_STATE, SSD_GROUPS
    heads = d_inner // hp
    gn = groups * n_state
    gw = d_inner // groups
    assert d_inner % gn == 0 and heads <= LANES
    pad = LANES - heads
    dtb = jnp.pad(dt_bias, (0, pad)).reshape(1, LANES)
    a_neg = jnp.pad(-jnp.exp(a_log), (0, pad)).reshape(1, LANES)
    dexp = jnp.repeat(d_skip, hp).reshape(1, d_inner)
    emat = (jnp.arange(LANES)[:, None] == (jnp.arange(d_inner)[None, :] // hp)).astype(BF16)
    tri = jnp.tril(jnp.ones((L, L), BF16))
    cb2 = conv_b.reshape(1, -1)
    xb = d_inner // gn
    row = lambda c: (c, 0)
    const = lambda c: (0, 0)
    in_specs = [
        pl.BlockSpec((L, d_inner), row),
        pl.BlockSpec((L, d_inner), lambda c: (c, 1)),
        pl.BlockSpec((L, gn), lambda c: (c, 2 * xb)),
        pl.BlockSpec((L, gn), lambda c: (c, 2 * xb + 1)),
        pl.BlockSpec((L, LANES), row),
        pl.BlockSpec((SSD_CONV, d_inner), const),
        pl.BlockSpec((SSD_CONV, gn), lambda c: (0, xb)),
        pl.BlockSpec((SSD_CONV, gn), lambda c: (0, xb + 1)),
        pl.BlockSpec((1, d_inner), const),
        pl.BlockSpec((1, gn), lambda c: (0, xb)),
        pl.BlockSpec((1, gn), lambda c: (0, xb + 1)),
        pl.BlockSpec((1, LANES), const),
        pl.BlockSpec((1, LANES), const),
        pl.BlockSpec((1, d_inner), const),
        pl.BlockSpec((1, d_inner), const),
        pl.BlockSpec((LANES, d_inner), const),
        pl.BlockSpec((L, L), const),
    ]
    scratch = [
        pltpu.VMEM((CONV_HALO + L, d_inner), F32), pltpu.VMEM((CONV_HALO + L, gn), F32),
        pltpu.VMEM((CONV_HALO + L, gn), F32),
        pltpu.VMEM((L, d_inner), F32),
        pltpu.VMEM((L, d_inner), BF16),
        pltpu.VMEM((L, d_inner), BF16),
        pltpu.VMEM((L, gn), BF16), pltpu.VMEM((L, gn), BF16),
        pltpu.VMEM((L, d_inner), F32),
        pltpu.VMEM((L, LANES), F32), pltpu.VMEM((LANES, L), F32),
        pltpu.VMEM((L, d_inner), F32),
        pltpu.VMEM((groups, n_state, gw), F32),
    ]
    return pl.pallas_call(
        functools.partial(_ssd_kernel, L=L, hp=hp, n_state=n_state, groups=groups),
        out_shape=jax.ShapeDtypeStruct((s, d_inner), BF16),
        grid=(s // L,),
        in_specs=in_specs,
        out_specs=pl.BlockSpec((L, d_inner), row),
        scratch_shapes=scratch,
        compiler_params=_cparams(1),
        name="ssd_scan",
    )(zx, zx, zx, zx, dt_raw, conv_w, conv_w, conv_w, cb2, cb2, cb2, dtb, a_neg, dexp,
      norm_w.reshape(1, d_inner), emat, tri)


def _ssd_mixer(h, hn, j, in_w, conv_w, conv_b, dt_bias, a_log, d_skip, norm_w, out_w):
    d_inner = out_w.shape[1]
    heads = d_inner // SSD_HEAD_DIM
    wide = in_w.shape[2] - heads
    wt = jnp.swapaxes(in_w, 1, 2)
    zx = _matmul_t(hn, wt, layer=j, row0=0, nrows=wide, tn=MM_TN, tm=MM_TM, out_dtype=BF16, name="ssd_in_proj")
    dt_raw = _matmul_t(hn, wt, layer=j, row0=wide, nrows=heads, tn=heads, tm=MM_TM, name="ssd_dt_proj")
    y = _ssd_scan(zx, dt_raw, conv_w, conv_b, dt_bias, a_log, d_skip, norm_w, d_inner)
    return _matmul(y, out_w, layer=j, tn=MM_TN // 2, tm=MM_TM, residual=h, name="ssd_out_proj")


def _fox_cum_kernel(f_ref, b_ref, tri_ref, o_ref, carry):
    @pl.when(pl.program_id(0) == 0)
    def _():
        carry[...] = jnp.zeros_like(carry)

    x = f_ref[...] + b_ref[...]
    log_f = jnp.minimum(x, 0.0) - jnp.log1p(jnp.exp(-jnp.abs(x)))
    cs = _dot3_rhs_exact(log_f, tri_ref[...]) + carry[...]
    o_ref[...] = cs
    carry[...] = cs[:, cs.shape[1] - 1:]


def _fox_cum(f_t, bias, tk=512):
    nh, s = f_t.shape
    triu = jnp.triu(jnp.ones((tk, tk), BF16))
    return pl.pallas_call(
        _fox_cum_kernel,
        out_shape=jax.ShapeDtypeStruct((nh, s), F32),
        grid=(s // tk,),
        in_specs=[pl.BlockSpec((nh, tk), lambda i: (0, i)),
                  pl.BlockSpec((nh, 1), lambda i: (0, 0)),
                  pl.BlockSpec((tk, tk), lambda i: (0, 0))],
        out_specs=pl.BlockSpec((nh, tk), lambda i: (0, i)),
        scratch_shapes=[pltpu.VMEM((nh, 1), F32)],
        compiler_params=_cparams(1),
        name="fox_cum",
    )(f_t, bias.reshape(nh, 1), triu)


def _fox_attn_kernel(q_ref, k_ref, v_ref, cum_ref, o_ref, m_s, acc_s, vext, *, tq, tk, sr):
    qi = pl.program_id(1)
    dh = q_ref.shape[1]
    n_sub = tq // sr

    @pl.when(qi == 0)
    def _():
        vext[:, 0:dh] = v_ref[...]
        vext[:, dh:2 * dh] = jnp.ones((vext.shape[0], dh), BF16)

    q0 = pl.multiple_of(qi * tq, tq)
    c_ref = cum_ref[:, pl.ds(q0, LANES)][:, 0:1]
    m_s[...] = jnp.full_like(m_s, -jnp.inf)
    acc_s[...] = jnp.zeros_like(acc_s)

    def sub_step(r, k0, width, bias, diagonal):
        rows = slice(r * sr, (r + 1) * sr)
        s = _dot_nt(q_ref[rows, :], k_ref[pl.ds(k0, width), :]) + bias
        if diagonal:
            keep = lax.broadcasted_iota(I32, (sr, width), 0) >= lax.broadcasted_iota(I32, (sr, width), 1)
            s = jnp.where(keep, s, -jnp.inf)
        m_prev = m_s[rows, :]
        m_new = jnp.maximum(m_prev, jnp.max(s, axis=-1, keepdims=True))
        alpha = jnp.exp2(m_prev - m_new)
        p = jnp.exp2(s - jnp.concatenate([m_new] * (width // dh), axis=1)).astype(BF16)
        pv = _dot(p, vext[pl.ds(k0, width), :])
        acc_s[rows, :] = jnp.concatenate([alpha, alpha], axis=1) * acc_s[rows, :] + pv
        m_s[rows, :] = m_new

    def key_bias(k0, width):
        return (c_ref - cum_ref[:, pl.ds(k0, width)]) * LOG2E

    def body(j, c):
        k0 = pl.multiple_of(j * tk, tk)
        bias = key_bias(k0, tk)
        for r in range(n_sub):
            sub_step(r, k0, tk, bias, False)
        return c

    lax.fori_loop(0, qi * (tq // tk), body, 0)
    per = tk // sr
    for c in range(tq // tk):
        kc = pl.multiple_of(q0 + c * tk, tk)
        bias = key_bias(kc, tk)
        for r in range((c + 1) * per, n_sub):
            sub_step(r, kc, tk, bias, False)
        for r in range(c * per, (c + 1) * per):
            w = (r - c * per) * sr
            if w > 0:
                sub_step(r, kc, w, bias[:, 0:w], False)
            sub_step(r, pl.multiple_of(kc + w, sr), sr, bias[:, w:w + sr], True)
    acc = acc_s[...]
    o_ref[...] = (acc[:, 0:dh] / acc[:, dh:2 * dh]).astype(o_ref.dtype)


def _fox_attention(qkv, cum, nh, tq=FOX_TQ, tk=FOX_TK, sr=FOX_SUB):
    s = qkv.shape[0]
    dh = FOX_HEAD_DIM
    assert s % tq == 0 and tq % tk == 0 and tq % sr == 0 and tk % dh == 0 and dh == LANES
    cum3 = cum.reshape(nh, 1, s)
    return pl.pallas_call(
        functools.partial(_fox_attn_kernel, tq=tq, tk=tk, sr=sr),
        out_shape=jax.ShapeDtypeStruct((s, nh * dh), BF16),
        grid=(nh, s // tq),
        in_specs=[pl.BlockSpec((tq, dh), lambda h, i: (i, h)),
                  pl.BlockSpec((s, dh), lambda h, i: (0, nh + h)),
                  pl.BlockSpec((s, dh), lambda h, i: (0, 2 * nh + h)),
                  pl.BlockSpec((None, 1, s), lambda h, i: (h, 0, 0))],
        out_specs=pl.BlockSpec((tq, dh), lambda h, i: (i, h)),
        scratch_shapes=[pltpu.VMEM((tq, dh), F32), pltpu.VMEM((tq, 2 * dh), F32), pltpu.VMEM((s, 2 * dh), BF16)],
        compiler_params=_cparams(2),
        name="fox_attention",
    )(qkv, qkv, qkv, cum3)


def _fox_mixer(h, hn, j, in_w, f_bias, out_w):
    d = hn.shape[1]
    nh = d // FOX_HEAD_DIM
    q_scale = FOX_HEAD_DIM ** -0.5 * LOG2E
    colscale = jnp.concatenate([jnp.full((1, d), q_scale, F32), jnp.ones((1, 2 * d), F32)], axis=1)
    wt = jnp.swapaxes(in_w, 1, 2)
    qkv = _matmul_t(hn, wt, layer=j, row0=0, nrows=3 * d, tn=MM_TN, tm=MM_TM, out_dtype=BF16, colscale=colscale,
                    name="fox_in_proj")
    f_raw = _matmul_t(hn, wt, layer=j, row0=3 * d, nrows=nh, tn=nh, tm=MM_TM, name="fox_f_proj")
    cum = _fox_cum(f_raw[:, :nh].T, f_bias)
    o = _fox_attention(qkv, cum, nh)
    return _matmul(o, out_w, layer=j, tn=MM_TN, tm=MM_TM, residual=h, name="fox_out_proj")


def _norm_kernel(h_ref, g_ref, o_ref):
    o_ref[...] = _rms(h_ref[...], g_ref[...]).astype(o_ref.dtype)


def _norm(h, g, out_dtype, tm=512):
    n, d = h.shape
    return pl.pallas_call(
        _norm_kernel,
        out_shape=jax.ShapeDtypeStruct((n, d), out_dtype),
        grid=(n // tm,),
        in_specs=[pl.BlockSpec((tm, d), lambda i: (i, 0)), pl.BlockSpec((1, d), lambda i: (0, 0))],
        out_specs=pl.BlockSpec((tm, d), lambda i: (i, 0)),
        compiler_params=_cparams(1),
        name="rmsnorm",
    )(h, g.reshape(1, d))


def kernel(x, norm_mix, norm_ffn, final_norm, pool_w, pool_scale, ssd_in_w, ssd_conv_w, ssd_conv_b,
           ssd_dt_bias, ssd_a_log, ssd_d_skip, ssd_norm, ssd_out_w, fox_in_w, fox_f_bias, fox_out_w,
           moe_group_w, moe_group_b, moe_router_w, moe_router_b, moe_gate_w, moe_up_w, moe_down_w):
    bsz, s, d = x.shape
    assert bsz == 1
    depth = norm_mix.shape[0]
    h = x.reshape(s, d)
    hn = None
    pending = None
    ops = _route_operands(moe_group_w, moe_group_b, moe_router_w, moe_router_b)
    for i in range(depth):
        kind, j = i % 3, i // 3
        if kind == 0:
            h, hn_rows, ri, rg, cnt = _pool_route(h, pending, norm_mix[i], pool_w, pool_scale, j, norm_ffn[i], ops, i)
            pending = None
        else:
            if hn is None:
                hn = _norm(h, norm_mix[i], BF16)
            if kind == 1:
                h = _ssd_mixer(h, hn, j, ssd_in_w, ssd_conv_w[j], ssd_conv_b[j], ssd_dt_bias[j], ssd_a_log[j],
                               ssd_d_skip[j], ssd_norm[j], ssd_out_w)
            else:
                h = _fox_mixer(h, hn, j, fox_in_w, fox_f_bias[j], fox_out_w)
            hn_rows, ri, rg, cnt = _route(h, norm_ffn[i], ops, i)
        ys, dest = _moe_experts_path(hn_rows, ri, cnt, i, moe_gate_w, moe_up_w, moe_down_w)
        hn = None
        if i == depth - 1:
            h = _combine(h, rg, ys, dest, final_norm, "final")
        elif (i + 1) % 3 == 0:
            pending = (rg, ys, dest)
        else:
            h, hn = _combine(h, rg, ys, dest, norm_mix[i + 1], "norm_bf16")
    return h.reshape(bsz, s, d)
```

```python
import functools

import jax
import jax.numpy as jnp
from jax import lax
from jax.experimental import pallas as pl
from jax.experimental.pallas import tpu as pltpu

F32 = jnp.float32
BF16 = jnp.bfloat16
I32 = jnp.int32
U32 = jnp.uint32

RMS_EPS = 1e-6
LOG2E = 1.4426950408889634
LANES = 128
SLAB = 8
VMEM_LIMIT = 56 * 1024 * 1024

MM_TM = 1024
MM_TN = 1024

POOL_WINDOWS = (2, 4, 8, 16)
POOL_HALO = 16
POOL_PAD = 8

SSD_HEAD_DIM = 64
SSD_STATE = 128
SSD_GROUPS = 8
SSD_CONV = 4
SSD_CHUNK = 128
CONV_HALO = 8

FOX_HEAD_DIM = 128
FOX_TQ = 4096
FOX_TK = 1024
FOX_SUB = 256

MOE_GROUPS = 4
MOE_EPG = 8
MOE_EXPERTS = MOE_GROUPS * MOE_EPG
MOE_ROWS = 256
MOE_WSLOTS = 3
DISPATCH_TILE = 512
COMBINE_TILE = 256
ROUTE_TILE = 512


def _cparams(n_axes, vmem=VMEM_LIMIT):
    return pltpu.CompilerParams(dimension_semantics=("arbitrary",) * n_axes, vmem_limit_bytes=vmem)


def _rms(x, g):
    ms = jnp.mean(x * x, axis=-1, keepdims=True)
    return x * lax.rsqrt(ms + RMS_EPS) * g


def _split3(x):
    hi = x.astype(BF16)
    r = x - hi.astype(F32)
    mid = r.astype(BF16)
    lo = (r - mid.astype(F32)).astype(BF16)
    return hi, mid, lo


def _dot(a, b):
    return jnp.dot(a, b, preferred_element_type=F32)


def _dot_nt(a, b):
    return lax.dot_general(a, b, (((1,), (1,)), ((), ())), preferred_element_type=F32)


def _dot3_rhs_exact(x, m):
    hi, mid, lo = _split3(x)
    return _dot(hi, m) + _dot(mid, m) + _dot(lo, m)


def _dot3_lhs_exact(m, x):
    hi, mid, lo = _split3(x)
    return _dot(m, hi) + _dot(m, mid) + _dot(m, lo)


def _dot2_rhs_exact(x, m):
    hi = x.astype(BF16)
    lo = (x - hi.astype(F32)).astype(BF16)
    return _dot(hi, m) + _dot(lo, m)


def _silu(x):
    h = 0.5 * x
    return h + h * jnp.tanh(h)


def _softplus(x):
    return jnp.maximum(x, 0.0) + jnp.log1p(jnp.exp(-jnp.abs(x)))


def _pack_rows(x):
    half = x.shape[1] // 2
    lo = lax.bitcast_convert_type(x[:, :half].astype(BF16).astype(F32), U32)
    hi = lax.bitcast_convert_type(x[:, half:].astype(BF16).astype(F32), U32)
    return hi | (lo >> 16)


def _unpack_rows(u):
    lo = lax.bitcast_convert_type(u << 16, F32)
    hi = lax.bitcast_convert_type(u & jnp.uint32(0xFFFF0000), F32)
    return lo, hi


def _load_token_slabs(ref, rows):
    return [ref[pl.ds(c, rows, stride=SLAB), :] for c in range(SLAB)]


def _store_token_slabs(ref, packed):
    rows = packed.shape[0]
    for c in range(SLAB):
        ref[pl.ds(c, rows, stride=SLAB), :] = packed[:, c * LANES:(c + 1) * LANES]


def _slab(ref, token):
    return ref.at[pl.ds(pl.multiple_of(token * SLAB, SLAB), SLAB)]


def _slab_run(ref, token, count):
    return ref.at[pl.ds(pl.multiple_of(token * SLAB, SLAB), count * SLAB)]


def _mm_kernel(*refs, has_res, has_scale, transposed):
    x_ref, w_ref = refs[:2]
    rest = list(refs[2:])
    r_ref = rest.pop(0) if has_res else None
    s_ref = rest.pop(0) if has_scale else None
    o_ref, wbf = rest

    @pl.when(pl.program_id(1) == 0)
    def _():
        if transposed and wbf.shape[0] != w_ref.shape[0]:
            wbf[...] = jnp.zeros_like(wbf)
            wbf[0:w_ref.shape[0], :] = w_ref[...].astype(BF16)
        else:
            wbf[...] = w_ref[...].astype(BF16)

    acc = _dot_nt(x_ref[...], wbf[...]) if transposed else _dot(x_ref[...], wbf[...])
    if s_ref is not None:
        acc = acc * s_ref[...]
    if r_ref is not None:
        acc = acc + r_ref[...]
    o_ref[...] = acc.astype(o_ref.dtype)


def _matmul(x, w, *, layer=0, col0=0, ncols=None, tn=512, tm=512, residual=None, colscale=None,
            out_dtype=F32, name="mm"):
    m, k = x.shape
    ncols = w.shape[-1] - col0 if ncols is None else ncols
    assert ncols % tn == 0 and col0 % tn == 0 and m % tm == 0
    jb = col0 // tn
    if w.ndim == 3:
        w_spec = pl.BlockSpec((None, k, tn), lambda j, i: (layer, 0, jb + j))
    else:
        w_spec = pl.BlockSpec((k, tn), lambda j, i: (0, jb + j))
    in_specs = [pl.BlockSpec((tm, k), lambda j, i: (i, 0)), w_spec]
    args = [x, w]
    if residual is not None:
        in_specs.append(pl.BlockSpec((tm, tn), lambda j, i: (i, j)))
        args.append(residual)
    if colscale is not None:
        in_specs.append(pl.BlockSpec((1, tn), lambda j, i: (0, j)))
        args.append(colscale)
    return pl.pallas_call(
        functools.partial(_mm_kernel, has_res=residual is not None, has_scale=colscale is not None,
                          transposed=False),
        out_shape=jax.ShapeDtypeStruct((m, ncols), out_dtype),
        grid=(ncols // tn, m // tm),
        in_specs=in_specs,
        out_specs=pl.BlockSpec((tm, tn), lambda j, i: (i, j)),
        scratch_shapes=[pltpu.VMEM((k, tn), BF16)],
        compiler_params=_cparams(2),
        name=name,
    )(*args)


def _matmul_t(x, wt, *, layer, row0=0, nrows=None, tn=512, tm=512, colscale=None, out_dtype=F32, name="mm_t"):
    m, k = x.shape
    nrows = wt.shape[1] - row0 if nrows is None else nrows
    assert nrows % tn == 0 and row0 % tn == 0 and m % tm == 0 and tn % 8 == 0
    assert tn % LANES == 0 or nrows == tn
    tn_out = max(tn, LANES)
    jb = row0 // tn
    in_specs = [pl.BlockSpec((tm, k), lambda j, i: (i, 0)),
                pl.BlockSpec((None, tn, k), lambda j, i: (layer, jb + j, 0))]
    args = [x, wt]
    if colscale is not None:
        in_specs.append(pl.BlockSpec((1, tn_out), lambda j, i: (0, j)))
        args.append(colscale)
    return pl.pallas_call(
        functools.partial(_mm_kernel, has_res=False, has_scale=colscale is not None, transposed=True),
        out_shape=jax.ShapeDtypeStruct((m, (nrows // tn) * tn_out), out_dtype),
        grid=(nrows // tn, m // tm),
        in_specs=in_specs,
        out_specs=pl.BlockSpec((tm, tn_out), lambda j, i: (i, j)),
        scratch_shapes=[pltpu.VMEM((tn_out, k), BF16)],
        compiler_params=_cparams(2),
        name=name,
    )(*args)


def _pool_init(w_ref, xbuf, sa, sb, wbf):
    cur0 = POOL_PAD + POOL_HALO
    xbuf[0:cur0, :] = jnp.zeros((cur0, xbuf.shape[1]), F32)
    sa[0:POOL_PAD, :] = jnp.zeros((POOL_PAD, sa.shape[1]), F32)
    sb[0:POOL_PAD, :] = jnp.zeros((POOL_PAD, sb.shape[1]), F32)
    wbf[...] = w_ref[...].astype(BF16)


def _pool_tile(h, i, g_ref, scale_ref, o_ref, xbuf, sa, sb, wbf, tm, pg):
    lo = POOL_PAD
    cur0 = POOL_PAD + POOL_HALO
    end = cur0 + tm
    hn = _rms(h, g_ref[...])
    xbuf[cur0:end, :] = hn
    row = i * tm + lax.broadcasted_iota(I32, (tm, 1), 0)
    for gi, w in enumerate(POOL_WINDOWS):
        c0 = gi * pg
        cols = slice(c0, c0 + pg)
        src, shift, spare = None, 1, [sa, sb]
        while True:
            first = lo if 2 * shift < w else cur0
            if src is None:
                summed = xbuf[first:end, cols] + xbuf[first - shift:end - shift, cols]
            else:
                summed = src[first:end, :] + src[first - shift:end - shift, :]
            shift *= 2
            if shift == w:
                break
            dst = spare[0] if src is not spare[0] else spare[1]
            dst[first:end, :] = summed
            src = dst
        cur = hn[:, cols]
        cnt = jnp.minimum(row + 1, w).astype(F32)
        pooled = summed / cnt - cur
        mix = _dot(pooled.astype(BF16), wbf[gi]) * scale_ref[:, cols]
        o_ref[:, cols] = h[:, cols] + mix
    xbuf[lo:cur0, :] = xbuf[end - POOL_HALO:end, :]


def _route_kernel(h_ref, g_ref, whl_ref, b_ref, tri_ref, hn_ref, ri_ref, rg_ref, cnt_ref, run_ref, *, tm):
    @pl.when(pl.program_id(0) == 0)
    def _():
        run_ref[...] = jnp.zeros_like(run_ref)

    _route_tile(h_ref[...], g_ref, whl_ref, b_ref, tri_ref, hn_ref, ri_ref, rg_ref, cnt_ref, run_ref, tm)


def _route_tile(h, g_ref, whl_ref, b_ref, tri_ref, hn_ref, ri_ref, rg_ref, cnt_ref, run_ref, tm):
    hn = _rms(h, g_ref[...])
    hn_bf = hn.astype(BF16)
    _store_token_slabs(hn_ref, _pack_rows(hn))
    hn_lo = (hn - hn_bf.astype(F32)).astype(BF16)
    both = _dot(hn_bf, whl_ref[...])
    logits = both[:, 0:LANES] + both[:, LANES:2 * LANES] + _dot(hn_lo, whl_ref[:, 0:LANES]) + b_ref[...]

    lane = lax.broadcasted_iota(I32, (tm, LANES), 1)
    lane_f = lane.astype(F32)
    neg = -jnp.inf
    big = float(LANES)
    gl = jnp.where(lane < MOE_GROUPS, logits, neg)
    gmax = jnp.max(gl, axis=-1, keepdims=True)
    gsel = jnp.min(jnp.where(gl == gmax, lane_f, big), axis=-1, keepdims=True).astype(I32)
    p_grp = 1.0 / jnp.sum(jnp.exp(gl - gmax), axis=-1, keepdims=True)
    lo_lane = MOE_GROUPS + MOE_EPG * gsel
    el = jnp.where((lane >= lo_lane) & (lane < lo_lane + MOE_EPG), logits, neg)
    v0 = jnp.max(el, axis=-1, keepdims=True)
    i0 = jnp.min(jnp.where(el == v0, lane_f, big), axis=-1, keepdims=True).astype(I32)
    el2 = jnp.where(lane == i0, neg, el)
    v1 = jnp.max(el2, axis=-1, keepdims=True)
    i1 = jnp.min(jnp.where(el2 == v1, lane_f, big), axis=-1, keepdims=True).astype(I32)
    t = jnp.exp(v1 - v0)
    gate0 = p_grp / (1.0 + t)
    gate1 = p_grp * t / (1.0 + t)
    e0 = i0 - MOE_GROUPS
    e1 = i1 - MOE_GROUPS

    onehot = ((lane == e0) | (lane == e1))
    c_bf = jnp.where(onehot, 1.0, 0.0).astype(BF16)
    prefix = _dot(tri_ref[...], c_bf) + run_ref[...]
    rank0 = jnp.sum(jnp.where(lane == e0, prefix, 0.0), axis=-1, keepdims=True).astype(I32)
    rank1 = jnp.sum(jnp.where(lane == e1, prefix, 0.0), axis=-1, keepdims=True).astype(I32)
    run = run_ref[...] + jnp.sum(jnp.where(onehot, 1.0, 0.0), axis=0, keepdims=True)
    run_ref[...] = run
    cnt_ref[...] = jnp.broadcast_to(run, cnt_ref.shape)

    ri = jnp.where(lane == 0, e0, jnp.where(lane == 1, e1, jnp.where(lane == 2, rank0, rank1)))
    ri_ref[...] = ri.T[0:8, :]
    rg_ref[...] = jnp.where(lane == 0, gate0, gate1)


def _route_operands(group_w, group_b, router_w, router_b):
    depth, d, _ = group_w.shape
    pad = LANES - MOE_GROUPS - MOE_EXPERTS
    wr = jnp.concatenate([group_w, router_w, jnp.zeros((depth, d, pad), F32)], axis=2)
    br = jnp.concatenate([group_b, router_b, jnp.zeros((depth, pad), F32)], axis=1).reshape(depth, 1, LANES)
    whi = wr.astype(BF16)
    wlo = (wr - whi.astype(F32)).astype(BF16)
    return jnp.concatenate([whi, wlo], axis=2), br


def _prefix_matrix(tm):
    return jnp.tril(jnp.ones((tm, tm), BF16), -1)


def _route_out(n, tm):
    shapes = (jax.ShapeDtypeStruct((n * SLAB, LANES), U32), jax.ShapeDtypeStruct((8, n), I32),
              jax.ShapeDtypeStruct((n, LANES), F32), jax.ShapeDtypeStruct((8, LANES), F32))
    specs = (pl.BlockSpec((tm * SLAB, LANES), lambda i: (i, 0)), pl.BlockSpec((8, tm), lambda i: (0, i)),
             pl.BlockSpec((tm, LANES), lambda i: (i, 0)), pl.BlockSpec((8, LANES), lambda i: (0, 0)))
    return shapes, specs


def _pool_route_kernel(*refs, with_combine, t, pg):
    refs = list(refs)
    if with_combine:
        dest_ref, dest_next_ref, rgp_ref, ys_ref = refs[:4]
        refs = refs[4:]
    h_ref, gmix_ref, pw_ref, scale_ref, gffn_ref, whl_ref, b_ref, tri_ref = refs[:8]
    o_ref, hn_ref, ri_ref, rg_ref, cnt_ref = refs[8:13]
    refs = refs[13:]
    if with_combine:
        buf00, buf01, buf10, buf11, sems, hbuf = refs[:6]
        refs = refs[6:]
    xbuf, sa, sb, wbf, run_ref = refs
    i = pl.program_id(0)

    @pl.when(i == 0)
    def _():
        _pool_init(pw_ref, xbuf, sa, sb, wbf)
        run_ref[...] = jnp.zeros_like(run_ref)

    if with_combine:
        _combine_tile(dest_ref, dest_next_ref, h_ref, rgp_ref, ys_ref, hbuf, ((buf00, buf01), (buf10, buf11)), sems, t)
        h = hbuf[...]
    else:
        h = h_ref[...]
    _pool_tile(h, i, gmix_ref, scale_ref, o_ref, xbuf, sa, sb, wbf, t, pg)
    _route_tile(o_ref[...], gffn_ref, whl_ref, b_ref, tri_ref, hn_ref, ri_ref, rg_ref, cnt_ref, run_ref, t)


def _pool_route(h, pending, g_mix, pool_w, pool_scale, j, g_ffn, route_ops, layer):
    n, d = h.shape
    nw, pg, _ = pool_w.shape[1:]
    whl, br = route_ops
    t = DISPATCH_TILE if pending is None else COMBINE_TILE
    tri = _prefix_matrix(t)
    assert d == 2 * SLAB * LANES
    last = n // t - 1
    row = lambda i: (i, 0)
    const = lambda i: (0, 0)
    in_specs, args, scratch = [], [], []
    if pending is not None:
        rg_prev, ys, dest = pending
        in_specs += [pl.BlockSpec((None, 1, 2 * t), lambda i: (i, 0, 0), memory_space=pltpu.SMEM),
                     pl.BlockSpec((None, 1, 2 * t), lambda i: (jnp.minimum(i + 1, last), 0, 0),
                                  memory_space=pltpu.SMEM),
                     pl.BlockSpec((t, LANES), row),
                     pl.BlockSpec(memory_space=pl.ANY)]
        args += [dest, dest, rg_prev, ys]
        scratch += [pltpu.VMEM((t * SLAB, LANES), U32)] * 4 + [pltpu.SemaphoreType.DMA((2,)), pltpu.VMEM((t, d), F32)]
    in_specs += [pl.BlockSpec((t, d), row), pl.BlockSpec((1, d), const),
                 pl.BlockSpec((None, nw, pg, pg), lambda i: (j, 0, 0, 0)), pl.BlockSpec((1, d), const),
                 pl.BlockSpec((1, d), const), pl.BlockSpec((None, d, 2 * LANES), lambda i: (layer, 0, 0)),
                 pl.BlockSpec((None, 1, LANES), lambda i: (layer, 0, 0)), pl.BlockSpec((t, t), const)]
    args += [h, g_mix.reshape(1, d), pool_w, pool_scale[j].reshape(1, d), g_ffn.reshape(1, d), whl, br, tri]
    r_shapes, r_specs = _route_out(n, t)
    scratch += [pltpu.VMEM((POOL_PAD + POOL_HALO + t, d), F32),
                pltpu.VMEM((POOL_PAD + POOL_HALO + t, pg), F32),
                pltpu.VMEM((POOL_PAD + POOL_HALO + t, pg), F32),
                pltpu.VMEM((nw, pg, pg), BF16),
                pltpu.VMEM((1, LANES), F32)]
    return pl.pallas_call(
        functools.partial(_pool_route_kernel, with_combine=pending is not None, t=t, pg=pg),
        out_shape=(jax.ShapeDtypeStruct((n, d), F32),) + r_shapes,
        grid=(n // t,),
        in_specs=in_specs,
        out_specs=(pl.BlockSpec((t, d), row),) + r_specs,
        scratch_shapes=scratch,
        compiler_params=_cparams(1),
        name="pool_route",
    )(*args)


def _route(h, g, route_ops, layer, tm=ROUTE_TILE):
    n, d = h.shape
    whl, br = route_ops
    tri = _prefix_matrix(tm)
    return pl.pallas_call(
        functools.partial(_route_kernel, tm=tm),
        out_shape=(jax.ShapeDtypeStruct((n * SLAB, LANES), U32),
                   jax.ShapeDtypeStruct((8, n), I32),
                   jax.ShapeDtypeStruct((n, LANES), F32),
                   jax.ShapeDtypeStruct((8, LANES), F32)),
        grid=(n // tm,),
        in_specs=[pl.BlockSpec((tm, d), lambda i: (i, 0)),
                  pl.BlockSpec((1, d), lambda i: (0, 0)),
                  pl.BlockSpec((None, d, 2 * LANES), lambda i: (layer, 0, 0)),
                  pl.BlockSpec((None, 1, LANES), lambda i: (layer, 0, 0)),
                  pl.BlockSpec((tm, tm), lambda i: (0, 0))],
        out_specs=(pl.BlockSpec((tm * SLAB, LANES), lambda i: (i, 0)),
                   pl.BlockSpec((8, tm), lambda i: (0, i)),
                   pl.BlockSpec((tm, LANES), lambda i: (i, 0)),
                   pl.BlockSpec((8, LANES), lambda i: (0, 0))),
        scratch_shapes=[pltpu.VMEM((1, LANES), F32)],
        compiler_params=_cparams(1),
        name="moe_route",
    )(h, g.reshape(1, d), whl, br, tri)


def _dispatch_kernel(plan_ref, dest_ref, hn_ref, xs_ref, zbuf, sem, zsem, *, t):
    sizes = [1 << s for s in reversed(range(MOE_ROWS.bit_length() - 1))]
    zrows = zbuf.shape[0] // SLAB
    n_blocks = xs_ref.shape[0] // (MOE_ROWS * SLAB)

    def zero_copy(off, size):
        return pltpu.make_async_copy(zbuf.at[pl.ds(0, size * SLAB)], _slab_run(xs_ref, off, size), zsem)

    def zero_fill(wait):
        def gaps(e, c):
            off = plan_ref[PLAN_PAD_START, e]
            n = plan_ref[PLAN_PAD_LEN, e]
            for size in sizes:
                @pl.when((n & size) != 0)
                def _():
                    cp = zero_copy(off, size)
                    cp.wait() if wait else cp.start()

                off = off + (n & size)
            return c

        def tail(b, c):
            for part in range(MOE_ROWS // zrows):
                cp = zero_copy(b * MOE_ROWS + part * zrows, zrows)
                cp.wait() if wait else cp.start()
            return c

        lax.fori_loop(0, MOE_EXPERTS, gaps, 0)
        lax.fori_loop(plan_ref[PLAN_NUSED, 0], n_blocks, tail, 0)

    @pl.when(pl.program_id(0) == 0)
    def _():
        zbuf[...] = jnp.zeros_like(zbuf)
        zero_fill(False)

    def row_copy(r, k):
        return pltpu.make_async_copy(_slab(hn_ref, r), _slab(xs_ref, dest_ref[0, k * t + r]), sem)

    def start(r, c):
        row_copy(r, 0).start(priority=0)
        row_copy(r, 1).start(priority=1)
        return c

    lax.fori_loop(0, t, start, 0, unroll=8)
    for _ in range(2):
        pltpu.make_async_copy(hn_ref, xs_ref.at[pl.ds(0, t * SLAB)], sem).wait()

    @pl.when(pl.program_id(0) == pl.num_programs(0) - 1)
    def _():
        zero_fill(True)


def _dispatch(hn, dest, plan, n_rows, t=DISPATCH_TILE):
    n = hn.shape[0] // SLAB
    grid_spec = pltpu.PrefetchScalarGridSpec(
        num_scalar_prefetch=1,
        grid=(n // t,),
        in_specs=[pl.BlockSpec((None, 1, 2 * t), lambda i, *_: (i, 0, 0), memory_space=pltpu.SMEM),
                  pl.BlockSpec((t * SLAB, LANES), lambda i, *_: (i, 0))],
        out_specs=pl.BlockSpec(memory_space=pl.ANY),
        scratch_shapes=[pltpu.VMEM((MOE_ROWS // 2 * SLAB, LANES), hn.dtype),
                        pltpu.SemaphoreType.DMA(()), pltpu.SemaphoreType.DMA(())],
    )
    return pl.pallas_call(
        functools.partial(_dispatch_kernel, t=t),
        out_shape=jax.ShapeDtypeStruct((n_rows * SLAB, LANES), hn.dtype),
        grid_spec=grid_spec,
        compiler_params=_cparams(1),
        name="moe_dispatch",
    )(plan, dest, hn)


def _expert_kernel(plan_ref, xs_ref, wg_hbm, wu_hbm, wd_hbm,
                   o_ref, wg_raw, wu_raw, wd_raw, wg_bf, wu_bf, wd_bf, sems, *, layer):
    b = pl.program_id(0)
    active = b < plan_ref[PLAN_NUSED, 0]
    expert = plan_ref[PLAN_EXPERT, b]
    second = plan_ref[PLAN_SECOND, 0]

    def fetch(e, slot):
        return [pltpu.make_async_copy(hbm.at[layer, e], raw.at[slot], sems.at[slot, k])
                for k, (hbm, raw) in enumerate(((wg_hbm, wg_raw), (wu_hbm, wu_raw), (wd_hbm, wd_raw)))]

    @pl.when(b == 0)
    def _():
        for c in fetch(expert, 0):
            c.start(priority=1)

        @pl.when(second >= 0)
        def _():
            for c in fetch(second, 1):
                c.start(priority=1)

    @pl.when(active & (plan_ref[PLAN_FIRST, b] == 1))
    def _():
        slot = plan_ref[PLAN_SLOT, b]
        ahead = plan_ref[PLAN_AHEAD, b]

        @pl.when(ahead >= 0)
        def _():
            for c in fetch(ahead, (slot + MOE_WSLOTS - 1) % MOE_WSLOTS):
                c.start(priority=1)

        for c in fetch(expert, slot):
            c.wait()
        wg_bf[...] = wg_raw[slot].astype(BF16)
        wu_bf[...] = wu_raw[slot].astype(BF16)
        wd_bf[...] = wd_raw[slot].astype(BF16)

    @pl.when(active)
    def _():
        rows = xs_ref.shape[0] // SLAB
        halves = [_unpack_rows(u) for u in _load_token_slabs(xs_ref, rows)]
        x = jnp.concatenate([lo.astype(BF16) for lo, _ in halves] + [hi.astype(BF16) for _, hi in halves], axis=1)
        gte = _dot(x, wg_bf[...])
        up = _dot(x, wu_bf[...])
        hb = _silu(gte) * up
        _store_token_slabs(o_ref, _pack_rows(_dot(hb.astype(BF16), wd_bf[...])))

    @pl.when(jnp.logical_not(active))
    def _():
        o_ref[...] = jnp.zeros_like(o_ref)


def _experts(xs, plan, gate_w, up_w, down_w, layer, rows=MOE_ROWS):
    n_rows = xs.shape[0] // SLAB
    d = gate_w.shape[-2]
    hid = gate_w.shape[-1]
    assert d == 2 * SLAB * LANES and MOE_WSLOTS == 3 and n_rows // rows <= LANES
    x_map = lambda b, plan_ref: (jnp.minimum(b, plan_ref[PLAN_NUSED, 0] - 1), 0)
    grid_spec = pltpu.PrefetchScalarGridSpec(
        num_scalar_prefetch=1,
        grid=(n_rows // rows,),
        in_specs=[pl.BlockSpec((rows * SLAB, LANES), x_map),
                  pl.BlockSpec(memory_space=pl.ANY),
                  pl.BlockSpec(memory_space=pl.ANY),
                  pl.BlockSpec(memory_space=pl.ANY)],
        out_specs=pl.BlockSpec((rows * SLAB, LANES), lambda b, *_: (b, 0)),
        scratch_shapes=[pltpu.VMEM((MOE_WSLOTS, d, hid), F32), pltpu.VMEM((MOE_WSLOTS, d, hid), F32),
                        pltpu.VMEM((MOE_WSLOTS, hid, d), F32),
                        pltpu.VMEM((d, hid), BF16), pltpu.VMEM((d, hid), BF16), pltpu.VMEM((hid, d), BF16),
                        pltpu.SemaphoreType.DMA((MOE_WSLOTS, 3))],
    )
    return pl.pallas_call(
        functools.partial(_expert_kernel, layer=layer),
        out_shape=jax.ShapeDtypeStruct((n_rows * SLAB, LANES), U32),
        grid_spec=grid_spec,
        compiler_params=_cparams(1),
        name="moe_experts",
    )(plan, xs, gate_w, up_w, down_w)


def _combine_kernel(*refs, t, mode):
    dest_ref, dest_next_ref, h_ref, rg_ref, ys_ref, g_ref, o_ref = refs[:7]
    rest = list(refs[7:])
    hn_ref = rest.pop(0) if mode == "norm_bf16" else None
    buf00, buf01, buf10, buf11, sems = rest
    _combine_tile(dest_ref, dest_next_ref, h_ref, rg_ref, ys_ref, o_ref, ((buf00, buf01), (buf10, buf11)), sems, t)
    normed = _rms(o_ref[...], g_ref[...])
    if mode == "norm_bf16":
        hn_ref[...] = normed.astype(BF16)
    else:
        o_ref[...] = normed


def _combine_tile(dest_ref, dest_next_ref, h_ref, rg_ref, ys_ref, o_ref, bufs, sems, t):
    i = pl.program_id(0)
    n = pl.num_programs(0)
    half = SLAB * LANES

    def row_copy(dref, slot, r, k):
        return pltpu.make_async_copy(_slab(ys_ref, dref[0, k * t + r]), _slab(bufs[slot][k], r), sems.at[slot])

    def issue(dref, slot):
        def start(r, c):
            row_copy(dref, slot, r, 0).start(priority=0)
            row_copy(dref, slot, r, 1).start(priority=1)
            return c
        lax.fori_loop(0, t, start, 0, unroll=8)

    def drain(slot):
        for k in range(2):
            pltpu.make_async_copy(ys_ref.at[pl.ds(0, t * SLAB)], bufs[slot][k], sems.at[slot]).wait()

    @pl.when(i == 0)
    def _():
        issue(dest_ref, 0)

    for slot in (0, 1):
        @pl.when((i % 2 == slot) & (i + 1 < n))
        def _():
            issue(dest_next_ref, 1 - slot)

        @pl.when(i % 2 == slot)
        def _():
            drain(slot)
            rg = rg_ref[...]
            g0 = rg[:, 0:1]
            g1 = rg[:, 1:2]
            for c, (ua, ub) in enumerate(zip(_load_token_slabs(bufs[slot][0], t), _load_token_slabs(bufs[slot][1], t))):
                a_lo, a_hi = _unpack_rows(ua)
                b_lo, b_hi = _unpack_rows(ub)
                lo_cols = slice(c * LANES, (c + 1) * LANES)
                hi_cols = slice(half + c * LANES, half + (c + 1) * LANES)
                o_ref[:, lo_cols] = h_ref[:, lo_cols] + (g0 * a_lo + g1 * b_lo)
                o_ref[:, hi_cols] = h_ref[:, hi_cols] + (g0 * a_hi + g1 * b_hi)


def _combine(h, rg, ys, dest, g_next, mode, t=COMBINE_TILE):
    n, d = h.shape
    assert d == 2 * SLAB * LANES
    last = n // t - 1
    in_specs = [pl.BlockSpec((None, 1, 2 * t), lambda i: (i, 0, 0), memory_space=pltpu.SMEM),
                pl.BlockSpec((None, 1, 2 * t), lambda i: (jnp.minimum(i + 1, last), 0, 0),
                             memory_space=pltpu.SMEM),
                pl.BlockSpec((t, d), lambda i: (i, 0)),
                pl.BlockSpec((t, LANES), lambda i: (i, 0)),
                pl.BlockSpec(memory_space=pl.ANY),
                pl.BlockSpec((1, d), lambda i: (0, 0))]
    args = [dest, dest, h, rg, ys, g_next.reshape(1, d)]
    tile = pl.BlockSpec((t, d), lambda i: (i, 0))
    assert mode in ("norm_bf16", "final")
    if mode == "norm_bf16":
        out_shape = (jax.ShapeDtypeStruct((n, d), F32), jax.ShapeDtypeStruct((n, d), BF16))
        out_specs = (tile, tile)
    else:
        out_shape = jax.ShapeDtypeStruct((n, d), F32)
        out_specs = tile
    return pl.pallas_call(
        functools.partial(_combine_kernel, t=t, mode=mode),
        out_shape=out_shape,
        grid=(n // t,),
        in_specs=in_specs,
        out_specs=out_specs,
        scratch_shapes=[pltpu.VMEM((t * SLAB, LANES), U32)] * 4 + [pltpu.SemaphoreType.DMA((2,))],
        compiler_params=_cparams(1),
        name="moe_combine",
    )(*args)


def _moe_experts_path(hn, ri, cnt, layer, gate_w, up_w, down_w):
    n = hn.shape[0] // SLAB
    n_rows = -(-(n * 2 + MOE_EXPERTS * (MOE_ROWS - 1)) // MOE_ROWS) * MOE_ROWS
    dest, plan = _moe_plan(ri, cnt)
    xs = _dispatch(hn, _dest_tiles(dest, DISPATCH_TILE), plan, n_rows)
    return _experts(xs, plan, gate_w, up_w, down_w, layer), _dest_tiles(dest, COMBINE_TILE)


def _dest_tiles(dest, t):
    n = dest.shape[1]
    return dest[0:2].reshape(2, n // t, t).transpose(1, 0, 2).reshape(n // t, 1, 2 * t)


PLAN_EXPERT = 0
PLAN_FIRST = 1
PLAN_SLOT = 2
PLAN_AHEAD = 3
PLAN_SECOND = 4
PLAN_NUSED = 5
PLAN_PAD_START = 6
PLAN_PAD_LEN = 7


def _plan_kernel(ri_ref, cnt_ref, tri_ref, dest_ref, plan_ref, *, chunk):
    e_lane = lax.broadcasted_iota(I32, (LANES, LANES), 1)
    e_sub = lax.broadcasted_iota(I32, (LANES, LANES), 0)
    tri = tri_ref[...]
    rows = float(MOE_ROWS)

    cnt_l = jnp.where(e_lane < MOE_EXPERTS, jnp.broadcast_to(cnt_ref[0:1, :], (LANES, LANES)), 0.0)
    cnt_s = cnt_l.T
    nblk_l = jnp.floor((cnt_l + (rows - 1.0)) * (1.0 / rows))
    nblk_s = jnp.floor((cnt_s + (rows - 1.0)) * (1.0 / rows))
    end_l = _dot_nt(nblk_l.astype(BF16), tri)
    end_s = _dot(tri, nblk_s.astype(BF16))
    start_l = end_l - nblk_l
    start_s = end_s - nblk_s
    row_start_l = start_l * rows
    n_used = end_l[0:1, LANES - 1:LANES]

    row_start_bf = row_start_l[0:SLAB, :].astype(BF16)
    n = ri_ref.shape[1]
    for c in range(n // chunk):
        cols = slice(c * chunk, (c + 1) * chunk)
        parts = []
        for k in range(2):
            onehot = (lax.broadcasted_iota(I32, (LANES, chunk), 0) == ri_ref[k:k + 1, cols])
            base = _dot(row_start_bf, jnp.where(onehot, 1.0, 0.0).astype(BF16))[0:1, :]
            parts.append(base.astype(I32) + ri_ref[2 + k:3 + k, cols])
        dest_ref[:, cols] = jnp.concatenate(parts + [jnp.zeros((SLAB - 2, chunk), I32)], axis=0)

    blk = e_lane.astype(F32)
    valid = e_sub < MOE_EXPERTS
    block_exp = jnp.sum(jnp.where(valid & (end_s <= blk), 1.0, 0.0), axis=0, keepdims=True)
    block_exp = jnp.minimum(block_exp, float(MOE_EXPERTS - 1))
    mine = e_sub.astype(F32) == block_exp
    blk_row = blk[0:1, :]
    first = (blk_row == jnp.sum(jnp.where(mine, start_s, 0.0), axis=0, keepdims=True)) & (blk_row < n_used)
    nonempty_s = nblk_s > 0.0
    ordinal_s = _dot(tri, jnp.where(nonempty_s, 1.0, 0.0).astype(BF16)) - 1.0
    pos = jnp.sum(jnp.where(mine, ordinal_s, 0.0), axis=0, keepdims=True)
    slot = pos - MOE_WSLOTS * jnp.floor((pos + 0.5) * (1.0 / MOE_WSLOTS))

    def expert_at(target):
        hit = nonempty_s & (ordinal_s == target)
        found = jnp.sum(jnp.where(hit, 1.0, 0.0), axis=0, keepdims=True) > 0.0
        return jnp.where(found, jnp.sum(jnp.where(hit, e_sub.astype(F32), 0.0), axis=0, keepdims=True), -1.0)

    ahead = expert_at(pos + (MOE_WSLOTS - 1.0))
    second = expert_at(jnp.ones((1, LANES), F32))
    pad_start = (row_start_l + cnt_l)[0:1, :]
    pad_len = (nblk_l * rows - cnt_l)[0:1, :]
    table = [block_exp, jnp.where(first, 1.0, 0.0), slot, ahead, second,
             jnp.broadcast_to(n_used, (1, LANES)), pad_start, pad_len]
    plan_ref[...] = jnp.concatenate(table, axis=0).astype(I32)


def _moe_plan(ri, cnt, chunk=2048):
    n = ri.shape[1]
    tri = jnp.tril(jnp.ones((LANES, LANES), BF16))
    return pl.pallas_call(
        functools.partial(_plan_kernel, chunk=chunk),
        out_shape=(jax.ShapeDtypeStruct((SLAB, n), I32), jax.ShapeDtypeStruct((SLAB, LANES), I32)),
        grid=(1,),
        in_specs=[pl.BlockSpec((SLAB, n), lambda i: (0, 0)), pl.BlockSpec((SLAB, LANES), lambda i: (0, 0)),
                  pl.BlockSpec((LANES, LANES), lambda i: (0, 0))],
        out_specs=(pl.BlockSpec((SLAB, n), lambda i: (0, 0)), pl.BlockSpec((SLAB, LANES), lambda i: (0, 0))),
        compiler_params=_cparams(1),
        name="moe_plan",
    )(ri, cnt, tri)


def _ssd_kernel(z_ref, x_ref, b_ref, c_ref, dt_ref, cwx_ref, cwb_ref, cwc_ref, cbx_ref, cbb_ref, cbc_ref,
                dtb_ref, a_ref, dexp_ref, nw_ref, exp_ref, tri_ref, o_ref,
                xbuf, bbuf, cbuf, xc_s, xdt_s, xw_s, b_s, c_s, eacs_s, acs_s, acst_s, y_s, st_s,
                *, L, hp, n_state, groups):
    ci = pl.program_id(0)
    d_inner = x_ref.shape[1]
    gw = d_inner // groups
    kh = gw // hp
    H0 = CONV_HALO

    @pl.when(ci == 0)
    def _():
        xbuf[0:H0, :] = jnp.zeros((H0, xbuf.shape[1]), F32)
        bbuf[0:H0, :] = jnp.zeros((H0, bbuf.shape[1]), F32)
        cbuf[0:H0, :] = jnp.zeros((H0, cbuf.shape[1]), F32)
        st_s[...] = jnp.zeros_like(st_s)

    def conv_silu(in_ref, buf, cw_ref, cb_ref):
        buf[H0:H0 + L, :] = in_ref[...].astype(F32)
        acc = cb_ref[...] + cw_ref[SSD_CONV - 1:SSD_CONV, :] * buf[H0:H0 + L, :]
        for j in range(SSD_CONV - 1):
            off = H0 - (SSD_CONV - 1) + j
            acc = acc + cw_ref[j:j + 1, :] * buf[off:off + L, :]
        buf[0:H0, :] = buf[L:L + H0, :]
        return _silu(acc)

    xc_s[...] = conv_silu(x_ref, xbuf, cwx_ref, cbx_ref)
    b_s[...] = conv_silu(b_ref, bbuf, cwb_ref, cbb_ref).astype(BF16)
    c_s[...] = conv_silu(c_ref, cbuf, cwc_ref, cbc_ref).astype(BF16)

    dt = _softplus(dt_ref[...] + dtb_ref[...])
    d_a = dt * a_ref[...]
    acs = _dot3_lhs_exact(tri_ref[...], d_a)
    acs_s[...] = acs
    acst_s[...] = acs.T
    emat = exp_ref[...]
    acs_last = acs[L - 1:L, :]
    to_end = dt * jnp.exp(acs_last - acs)
    xc = xc_s[...]
    xdt_s[...] = (xc * _dot2_rhs_exact(dt, emat)).astype(BF16)
    xw_s[...] = (xc * _dot2_rhs_exact(to_end, emat)).astype(BF16)
    eacs_s[...] = _dot2_rhs_exact(jnp.exp(acs), emat)
    cdecay = _dot2_rhs_exact(jnp.broadcast_to(jnp.exp(acs_last), (SLAB, LANES)), emat)[0:1, :]

    rr = lax.broadcasted_iota(I32, (L, L), 0)
    cc = lax.broadcasted_iota(I32, (L, L), 1)
    causal = rr >= cc
    lane2 = lax.broadcasted_iota(I32, (L, 2 * hp), 1)

    for g in range(groups):
        bg = b_s[:, g * n_state:(g + 1) * n_state]
        cg = c_s[:, g * n_state:(g + 1) * n_state]
        cb = _dot_nt(cg, bg)
        st_prev = st_s[g]
        y_off = _dot(cg, st_prev.astype(BF16)) * eacs_s[:, g * gw:(g + 1) * gw]
        for kp in range(kh // 2):
            c0 = g * gw + kp * 2 * hp
            slab = xdt_s[:, c0:c0 + 2 * hp]
            ms, xs = [], []
            for half in range(2):
                hd = g * kh + kp * 2 + half
                seg = acs_s[:, hd:hd + 1] - acst_s[hd:hd + 1, :]
                decay = jnp.exp(jnp.where(causal, seg, -jnp.inf))
                ms.append((cb * decay).astype(BF16))
                keep = (lane2 < hp) if half == 0 else (lane2 >= hp)
                xs.append(jnp.where(keep, slab, jnp.zeros_like(slab)))
            y_pair = _dot(jnp.concatenate(ms, axis=1), jnp.concatenate(xs, axis=0))
            y_s[:, c0:c0 + 2 * hp] = y_pair + y_off[:, kp * 2 * hp:(kp + 1) * 2 * hp]
        bgt = bg.astype(F32).T.astype(BF16)
        s_new = _dot(bgt, xw_s[:, g * gw:(g + 1) * gw])
        st_s[g] = cdecay[:, g * gw:(g + 1) * gw] * st_prev + s_new

    y = y_s[...] + dexp_ref[...] * xc_s[...]
    z = z_ref[...].astype(F32)
    y = y * _silu(z)
    for g in range(groups):
        yg = y[:, g * gw:(g + 1) * gw]
        yg = yg * lax.rsqrt(jnp.mean(yg * yg, axis=-1, keepdims=True) + RMS_EPS)
        o_ref[:, g * gw:(g + 1) * gw] = (yg * nw_ref[:, g * gw:(g + 1) * gw]).astype(o_ref.dtype)


def _ssd_scan(zx, dt_raw, conv_w, conv_b, dt_bias, a_log, d_skip, norm_w, d_inner):
    s = zx.shape[0]
    L = SSD_CHUNK
    hp, n_state, groups = SSD_HEAD_DIM, SSD_STATE, SSD_GROUPS
    heads = d_inner // hp
    gn = groups * n_state
    gw = d_inner // groups
    assert d_inner % gn == 0 and heads <= LANES
    pad = LANES - heads
    dtb = jnp.pad(dt_bias, (0, pad)).reshape(1, LANES)
    a_neg = jnp.pad(-jnp.exp(a_log), (0, pad)).reshape(1, LANES)
    dexp = jnp.repeat(d_skip, hp).reshape(1, d_inner)
    emat = (jnp.arange(LANES)[:, None] == (jnp.arange(d_inner)[None, :] // hp)).astype(BF16)
    tri = jnp.tril(jnp.ones((L, L), BF16))
    cb2 = conv_b.reshape(1, -1)
    xb = d_inner // gn
    row = lambda c: (c, 0)
    const = lambda c: (0, 0)
    in_specs = [
        pl.BlockSpec((L, d_inner), row),
        pl.BlockSpec((L, d_inner), lambda c: (c, 1)),
        pl.BlockSpec((L, gn), lambda c: (c, 2 * xb)),
        pl.BlockSpec((L, gn), lambda c: (c, 2 * xb + 1)),
        pl.BlockSpec((L, LANES), row),
        pl.BlockSpec((SSD_CONV, d_inner), const),
        pl.BlockSpec((SSD_CONV, gn), lambda c: (0, xb)),
        pl.BlockSpec((SSD_CONV, gn), lambda c: (0, xb + 1)),
        pl.BlockSpec((1, d_inner), const),
        pl.BlockSpec((1, gn), lambda c: (0, xb)),
        pl.BlockSpec((1, gn), lambda c: (0, xb + 1)),
        pl.BlockSpec((1, LANES), const),
        pl.BlockSpec((1, LANES), const),
        pl.BlockSpec((1, d_inner), const),
        pl.BlockSpec((1, d_inner), const),
        pl.BlockSpec((LANES, d_inner), const),
        pl.BlockSpec((L, L), const),
    ]
    scratch = [
        pltpu.VMEM((CONV_HALO + L, d_inner), F32), pltpu.VMEM((CONV_HALO + L, gn), F32),
        pltpu.VMEM((CONV_HALO + L, gn), F32),
        pltpu.VMEM((L, d_inner), F32),
        pltpu.VMEM((L, d_inner), BF16),
        pltpu.VMEM((L, d_inner), BF16),
        pltpu.VMEM((L, gn), BF16), pltpu.VMEM((L, gn), BF16),
        pltpu.VMEM((L, d_inner), F32),
        pltpu.VMEM((L, LANES), F32), pltpu.VMEM((LANES, L), F32),
        pltpu.VMEM((L, d_inner), F32),
        pltpu.VMEM((groups, n_state, gw), F32),
    ]
    return pl.pallas_call(
        functools.partial(_ssd_kernel, L=L, hp=hp, n_state=n_state, groups=groups),
        out_shape=jax.ShapeDtypeStruct((s, d_inner), BF16),
        grid=(s // L,),
        in_specs=in_specs,
        out_specs=pl.BlockSpec((L, d_inner), row),
        scratch_shapes=scratch,
        compiler_params=_cparams(1),
        name="ssd_scan",
    )(zx, zx, zx, zx, dt_raw, conv_w, conv_w, conv_w, cb2, cb2, cb2, dtb, a_neg, dexp,
      norm_w.reshape(1, d_inner), emat, tri)


def _ssd_mixer(h, hn, j, in_w, conv_w, conv_b, dt_bias, a_log, d_skip, norm_w, out_w):
    d_inner = out_w.shape[1]
    heads = d_inner // SSD_HEAD_DIM
    wide = in_w.shape[2] - heads
    wt = jnp.swapaxes(in_w, 1, 2)
    zx = _matmul_t(hn, wt, layer=j, row0=0, nrows=wide, tn=MM_TN, tm=MM_TM, out_dtype=BF16, name="ssd_in_proj")
    dt_raw = _matmul_t(hn, wt, layer=j, row0=wide, nrows=heads, tn=heads, tm=MM_TM, name="ssd_dt_proj")
    y = _ssd_scan(zx, dt_raw, conv_w, conv_b, dt_bias, a_log, d_skip, norm_w, d_inner)
    return _matmul(y, out_w, layer=j, tn=MM_TN // 2, tm=MM_TM, residual=h, name="ssd_out_proj")


def _fox_cum_kernel(f_ref, b_ref, tri_ref, o_ref, carry):
    @pl.when(pl.program_id(0) == 0)
    def _():
        carry[...] = jnp.zeros_like(carry)

    x = f_ref[...] + b_ref[...]
    log_f = jnp.minimum(x, 0.0) - jnp.log1p(jnp.exp(-jnp.abs(x)))
    cs = _dot3_rhs_exact(log_f, tri_ref[...]) + carry[...]
    o_ref[...] = cs
    carry[...] = cs[:, cs.shape[1] - 1:]


def _fox_cum(f_t, bias, tk=512):
    nh, s = f_t.shape
    triu = jnp.triu(jnp.ones((tk, tk), BF16))
    return pl.pallas_call(
        _fox_cum_kernel,
        out_shape=jax.ShapeDtypeStruct((nh, s), F32),
        grid=(s // tk,),
        in_specs=[pl.BlockSpec((nh, tk), lambda i: (0, i)),
                  pl.BlockSpec((nh, 1), lambda i: (0, 0)),
                  pl.BlockSpec((tk, tk), lambda i: (0, 0))],
        out_specs=pl.BlockSpec((nh, tk), lambda i: (0, i)),
        scratch_shapes=[pltpu.VMEM((nh, 1), F32)],
        compiler_params=_cparams(1),
        name="fox_cum",
    )(f_t, bias.reshape(nh, 1), triu)


def _fox_attn_kernel(q_ref, k_ref, v_ref, cum_ref, o_ref, m_s, acc_s, vext, *, tq, tk, sr):
    qi = pl.program_id(1)
    dh = q_ref.shape[1]
    n_sub = tq // sr

    @pl.when(qi == 0)
    def _():
        vext[:, 0:dh] = v_ref[...]
        vext[:, dh:2 * dh] = jnp.ones((vext.shape[0], dh), BF16)

    q0 = pl.multiple_of(qi * tq, tq)
    c_ref = cum_ref[:, pl.ds(q0, LANES)][:, 0:1]
    m_s[...] = jnp.full_like(m_s, -jnp.inf)
    acc_s[...] = jnp.zeros_like(acc_s)

    def sub_step(r, k0, width, bias, diagonal):
        rows = slice(r * sr, (r + 1) * sr)
        s = _dot_nt(q_ref[rows, :], k_ref[pl.ds(k0, width), :]) + bias
        if diagonal:
            keep = lax.broadcasted_iota(I32, (sr, width), 0) >= lax.broadcasted_iota(I32, (sr, width), 1)
            s = jnp.where(keep, s, -jnp.inf)
        m_prev = m_s[rows, :]
        m_new = jnp.maximum(m_prev, jnp.max(s, axis=-1, keepdims=True))
        alpha = jnp.exp2(m_prev - m_new)
        p = jnp.exp2(s - jnp.concatenate([m_new] * (width // dh), axis=1)).astype(BF16)
        pv = _dot(p, vext[pl.ds(k0, width), :])
        acc_s[rows, :] = jnp.concatenate([alpha, alpha], axis=1) * acc_s[rows, :] + pv
        m_s[rows, :] = m_new

    def key_bias(k0, width):
        return (c_ref - cum_ref[:, pl.ds(k0, width)]) * LOG2E

    def body(j, c):
        k0 = pl.multiple_of(j * tk, tk)
        bias = key_bias(k0, tk)
        for r in range(n_sub):
            sub_step(r, k0, tk, bias, False)
        return c

    lax.fori_loop(0, qi * (tq // tk), body, 0)
    per = tk // sr
    for c in range(tq // tk):
        kc = pl.multiple_of(q0 + c * tk, tk)
        bias = key_bias(kc, tk)
        for r in range((c + 1) * per, n_sub):
            sub_step(r, kc, tk, bias, False)
        for r in range(c * per, (c + 1) * per):
            w = (r - c * per) * sr
            if w > 0:
                sub_step(r, kc, w, bias[:, 0:w], False)
            sub_step(r, pl.multiple_of(kc + w, sr), sr, bias[:, w:w + sr], True)
    acc = acc_s[...]
    o_ref[...] = (acc[:, 0:dh] / acc[:, dh:2 * dh]).astype(o_ref.dtype)


def _fox_attention(qkv, cum, nh, tq=FOX_TQ, tk=FOX_TK, sr=FOX_SUB):
    s = qkv.shape[0]
    dh = FOX_HEAD_DIM
    assert s % tq == 0 and tq % tk == 0 and tq % sr == 0 and tk % dh == 0 and dh == LANES
    cum3 = cum.reshape(nh, 1, s)
    return pl.pallas_call(
        functools.partial(_fox_attn_kernel, tq=tq, tk=tk, sr=sr),
        out_shape=jax.ShapeDtypeStruct((s, nh * dh), BF16),
        grid=(nh, s // tq),
        in_specs=[pl.BlockSpec((tq, dh), lambda h, i: (i, h)),
                  pl.BlockSpec((s, dh), lambda h, i: (0, nh + h)),
                  pl.BlockSpec((s, dh), lambda h, i: (0, 2 * nh + h)),
                  pl.BlockSpec((None, 1, s), lambda h, i: (h, 0, 0))],
        out_specs=pl.BlockSpec((tq, dh), lambda h, i: (i, h)),
        scratch_shapes=[pltpu.VMEM((tq, dh), F32), pltpu.VMEM((tq, 2 * dh), F32), pltpu.VMEM((s, 2 * dh), BF16)],
        compiler_params=_cparams(2),
        name="fox_attention",
    )(qkv, qkv, qkv, cum3)


def _fox_mixer(h, hn, j, in_w, f_bias, out_w):
    d = hn.shape[1]
    nh = d // FOX_HEAD_DIM
    q_scale = FOX_HEAD_DIM ** -0.5 * LOG2E
    colscale = jnp.concatenate([jnp.full((1, d), q_scale, F32), jnp.ones((1, 2 * d), F32)], axis=1)
    wt = jnp.swapaxes(in_w, 1, 2)
    qkv = _matmul_t(hn, wt, layer=j, row0=0, nrows=3 * d, tn=MM_TN, tm=MM_TM, out_dtype=BF16, colscale=colscale,
                    name="fox_in_proj")
    f_raw = _matmul_t(hn, wt, layer=j, row0=3 * d, nrows=nh, tn=nh, tm=MM_TM, name="fox_f_proj")
    cum = _fox_cum(f_raw[:, :nh].T, f_bias)
    o = _fox_attention(qkv, cum, nh)
    return _matmul(o, out_w, layer=j, tn=MM_TN, tm=MM_TM, residual=h, name="fox_out_proj")


def _norm_kernel(h_ref, g_ref, o_ref):
    o_ref[...] = _rms(h_ref[...], g_ref[...]).astype(o_ref.dtype)


def _norm(h, g, out_dtype, tm=512):
    n, d = h.shape
    return pl.pallas_call(
        _norm_kernel,
        out_shape=jax.ShapeDtypeStruct((n, d), out_dtype),
        grid=(n // tm,),
        in_specs=[pl.BlockSpec((tm, d), lambda i: (i, 0)), pl.BlockSpec((1, d), lambda i: (0, 0))],
        out_specs=pl.BlockSpec((tm, d), lambda i: (i, 0)),
        compiler_params=_cparams(1),
        name="rmsnorm",
    )(h, g.reshape(1, d))


def kernel(x, norm_mix, norm_ffn, final_norm, pool_w, pool_scale, ssd_in_w, ssd_conv_w, ssd_conv_b,
           ssd_dt_bias, ssd_a_log, ssd_d_skip, ssd_norm, ssd_out_w, fox_in_w, fox_f_bias, fox_out_w,
           moe_group_w, moe_group_b, moe_router_w, moe_router_b, moe_gate_w, moe_up_w, moe_down_w):
    bsz, s, d = x.shape
    assert bsz == 1
    depth = norm_mix.shape[0]
    h = x.reshape(s, d)
    hn = None
    pending = None
    ops = _route_operands(moe_group_w, moe_group_b, moe_router_w, moe_router_b)
    for i in range(depth):
        kind, j = i % 3, i // 3
        if kind == 0:
            h, hn_rows, ri, rg, cnt = _pool_route(h, pending, norm_mix[i], pool_w, pool_scale, j, norm_ffn[i], ops, i)
            pending = None
        else:
            if hn is None:
                hn = _norm(h, norm_mix[i], BF16)
            if kind == 1:
                h = _ssd_mixer(h, hn, j, ssd_in_w, ssd_conv_w[j], ssd_conv_b[j], ssd_dt_bias[j], ssd_a_log[j],
                               ssd_d_skip[j], ssd_norm[j], ssd_out_w)
            else:
                h = _fox_mixer(h, hn, j, fox_in_w, fox_f_bias[j], fox_out_w)
            hn_rows, ri, rg, cnt = _route(h, norm_ffn[i], ops, i)
        ys, dest = _moe_experts_path(hn_rows, ri, cnt, i, moe_gate_w, moe_up_w, moe_down_w)
        hn = None
        if i == depth - 1:
            h = _combine(h, rg, ys, dest, final_norm, "final")
        elif (i + 1) % 3 == 0:
            pending = (rg, ys, dest)
        else:
            h, hn = _combine(h, rg, ys, dest, norm_mix[i + 1], "norm_bf16")
    return h.reshape(bsz, s, d)
```

```python
import functools

import jax
import jax.numpy as jnp
from jax import lax
from jax.experimental import pallas as pl
from jax.experimental.pallas import tpu as pltpu

F32 = jnp.float32
BF16 = jnp.bfloat16
I32 = jnp.int32
U32 = jnp.uint32

RMS_EPS = 1e-6
LOG2E = 1.4426950408889634
LANES = 128
SLAB = 8
VMEM_LIMIT = 56 * 1024 * 1024

MM_TM = 1024
MM_TN = 1024
MM_TM_IN = 2048

POOL_WINDOWS = (2, 4, 8, 16)
POOL_HALO = 16
POOL_PAD = 8

SSD_HEAD_DIM = 64
SSD_STATE = 128
SSD_GROUPS = 8
SSD_CONV = 4
SSD_CHUNK = 128
CONV_HALO = 8

FOX_HEAD_DIM = 128
FOX_TQ = 4096
FOX_TK = 1024
FOX_SUB = 256
CUM_TILE = 512

MOE_GROUPS = 4
MOE_EPG = 8
MOE_EXPERTS = MOE_GROUPS * MOE_EPG
MOE_ROWS = 256
MOE_WSLOTS = 3
DISPATCH_TILE = 512
COMBINE_TILE = 256
ROUTE_TILE = 1024
PLAN_CHUNK = 2048


def _cparams(n_axes, vmem=VMEM_LIMIT):
    return pltpu.CompilerParams(dimension_semantics=("arbitrary",) * n_axes, vmem_limit_bytes=vmem)


def _rms(x, g):
    ms = jnp.mean(x * x, axis=-1, keepdims=True)
    return x * lax.rsqrt(ms + RMS_EPS) * g


def _split3(x):
    hi = x.astype(BF16)
    r = x - hi.astype(F32)
    mid = r.astype(BF16)
    lo = (r - mid.astype(F32)).astype(BF16)
    return hi, mid, lo


def _dot(a, b):
    return jnp.dot(a, b, preferred_element_type=F32)


def _dot_nt(a, b):
    return lax.dot_general(a, b, (((1,), (1,)), ((), ())), preferred_element_type=F32)


def _dot3_rhs_exact(x, m):
    hi, mid, lo = _split3(x)
    return _dot(hi, m) + _dot(mid, m) + _dot(lo, m)


def _dot3_lhs_exact(m, x):
    hi, mid, lo = _split3(x)
    return _dot(m, hi) + _dot(m, mid) + _dot(m, lo)


def _dot2_rhs_exact(x, m):
    hi = x.astype(BF16)
    lo = (x - hi.astype(F32)).astype(BF16)
    return _dot(hi, m) + _dot(lo, m)


def _silu(x):
    h = 0.5 * x
    return h + h * jnp.tanh(h)


def _softplus(x):
    return jnp.maximum(x, 0.0) + jnp.log1p(jnp.exp(-jnp.abs(x)))


def _pack_rows(x):
    half = x.shape[1] // 2
    lo = lax.bitcast_convert_type(x[:, :half].astype(BF16).astype(F32), U32)
    hi = lax.bitcast_convert_type(x[:, half:].astype(BF16).astype(F32), U32)
    return hi | (lo >> 16)


def _unpack_rows(u):
    lo = lax.bitcast_convert_type(u << 16, F32)
    hi = lax.bitcast_convert_type(u & jnp.uint32(0xFFFF0000), F32)
    return lo, hi


def _load_token_slabs(ref, rows):
    return [ref[pl.ds(c, rows, stride=SLAB), :] for c in range(SLAB)]


def _store_token_slabs(ref, packed):
    rows = packed.shape[0]
    for c in range(SLAB):
        ref[pl.ds(c, rows, stride=SLAB), :] = packed[:, c * LANES:(c + 1) * LANES]


def _slab(ref, token):
    return ref.at[pl.ds(pl.multiple_of(token * SLAB, SLAB), SLAB)]


def _slab_run(ref, token, count):
    return ref.at[pl.ds(pl.multiple_of(token * SLAB, SLAB), count * SLAB)]


def _mm_kernel(*refs, has_res, has_scale, transposed):
    x_ref, w_ref = refs[:2]
    rest = list(refs[2:])
    r_ref = rest.pop(0) if has_res else None
    s_ref = rest.pop(0) if has_scale else None
    o_ref, wbf = rest

    @pl.when(pl.program_id(1) == 0)
    def _():
        if transposed and wbf.shape[0] != w_ref.shape[0]:
            wbf[...] = jnp.zeros_like(wbf)
            wbf[0:w_ref.shape[0], :] = w_ref[...].astype(BF16)
        else:
            wbf[...] = w_ref[...].astype(BF16)

    acc = _dot_nt(x_ref[...], wbf[...]) if transposed else _dot(x_ref[...], wbf[...])
    if s_ref is not None:
        acc = acc * s_ref[...]
    if r_ref is not None:
        acc = acc + r_ref[...]
    o_ref[...] = acc.astype(o_ref.dtype)


def _matmul(x, w, *, layer=0, col0=0, ncols=None, tn=512, tm=512, residual=None, colscale=None,
            out_dtype=F32, name="mm"):
    m, k = x.shape
    ncols = w.shape[-1] - col0 if ncols is None else ncols
    assert ncols % tn == 0 and col0 % tn == 0 and m % tm == 0
    jb = col0 // tn
    if w.ndim == 3:
        w_spec = pl.BlockSpec((None, k, tn), lambda j, i: (layer, 0, jb + j))
    else:
        w_spec = pl.BlockSpec((k, tn), lambda j, i: (0, jb + j))
    in_specs = [pl.BlockSpec((tm, k), lambda j, i: (i, 0)), w_spec]
    args = [x, w]
    if residual is not None:
        in_specs.append(pl.BlockSpec((tm, tn), lambda j, i: (i, j)))
        args.append(residual)
    if colscale is not None:
        in_specs.append(pl.BlockSpec((1, tn), lambda j, i: (0, j)))
        args.append(colscale)
    return pl.pallas_call(
        functools.partial(_mm_kernel, has_res=residual is not None, has_scale=colscale is not None,
                          transposed=False),
        out_shape=jax.ShapeDtypeStruct((m, ncols), out_dtype),
        grid=(ncols // tn, m // tm),
        in_specs=in_specs,
        out_specs=pl.BlockSpec((tm, tn), lambda j, i: (i, j)),
        scratch_shapes=[pltpu.VMEM((k, tn), BF16)],
        compiler_params=_cparams(2),
        name=name,
    )(*args)


def _matmul_t(x, wt, *, layer, row0=0, nrows=None, tn=512, tm=512, colscale=None, out_dtype=F32, name="mm_t"):
    m, k = x.shape
    nrows = wt.shape[1] - row0 if nrows is None else nrows
    assert nrows % tn == 0 and row0 % tn == 0 and m % tm == 0 and tn % 8 == 0
    assert tn % LANES == 0 or nrows == tn
    tn_out = max(tn, LANES)
    jb = row0 // tn
    in_specs = [pl.BlockSpec((tm, k), lambda j, i: (i, 0)),
                pl.BlockSpec((None, tn, k), lambda j, i: (layer, jb + j, 0))]
    args = [x, wt]
    if colscale is not None:
        in_specs.append(pl.BlockSpec((1, tn_out), lambda j, i: (0, j)))
        args.append(colscale)
    return pl.pallas_call(
        functools.partial(_mm_kernel, has_res=False, has_scale=colscale is not None, transposed=True),
        out_shape=jax.ShapeDtypeStruct((m, (nrows // tn) * tn_out), out_dtype),
        grid=(nrows // tn, m // tm),
        in_specs=in_specs,
        out_specs=pl.BlockSpec((tm, tn_out), lambda j, i: (i, j)),
        scratch_shapes=[pltpu.VMEM((tn_out, k), BF16)],
        compiler_params=_cparams(2),
        name=name,
    )(*args)


def _pool_init(w_ref, xbuf, sa, sb, wbf):
    cur0 = POOL_PAD + POOL_HALO
    xbuf[0:cur0, :] = jnp.zeros((cur0, xbuf.shape[1]), F32)
    sa[0:POOL_PAD, :] = jnp.zeros((POOL_PAD, sa.shape[1]), F32)
    sb[0:POOL_PAD, :] = jnp.zeros((POOL_PAD, sb.shape[1]), F32)
    wbf[...] = w_ref[...].astype(BF16)


def _pool_tile(h, i, g_ref, scale_ref, o_ref, xbuf, sa, sb, wbf, tm, pg):
    lo = POOL_PAD
    cur0 = POOL_PAD + POOL_HALO
    end = cur0 + tm
    hn = _rms(h, g_ref[...])
    xbuf[cur0:end, :] = hn
    row = i * tm + lax.broadcasted_iota(I32, (tm, 1), 0)
    for gi, w in enumerate(POOL_WINDOWS):
        c0 = gi * pg
        cols = slice(c0, c0 + pg)
        src, shift, spare = None, 1, [sa, sb]
        while True:
            first = lo if 2 * shift < w else cur0
            if src is None:
                summed = xbuf[first:end, cols] + xbuf[first - shift:end - shift, cols]
            else:
                summed = src[first:end, :] + src[first - shift:end - shift, :]
            shift *= 2
            if shift == w:
                break
            dst = spare[0] if src is not spare[0] else spare[1]
            dst[first:end, :] = summed
            src = dst
        cur = hn[:, cols]
        cnt = jnp.minimum(row + 1, w).astype(F32)
        pooled = summed / cnt - cur
        mix = _dot(pooled.astype(BF16), wbf[gi]) * scale_ref[:, cols]
        o_ref[:, cols] = h[:, cols] + mix
    xbuf[lo:cur0, :] = xbuf[end - POOL_HALO:end, :]


def _route_kernel(h_ref, g_ref, whl_ref, b_ref, tri_ref, hn_ref, ri_ref, rg_ref, cnt_ref, run_ref, *, tm):
    @pl.when(pl.program_id(0) == 0)
    def _():
        run_ref[...] = jnp.zeros_like(run_ref)

    _route_tile(h_ref[...], g_ref, whl_ref, b_ref, tri_ref, hn_ref, ri_ref, rg_ref, cnt_ref, run_ref, tm)


def _route_tile(h, g_ref, whl_ref, b_ref, tri_ref, hn_ref, ri_ref, rg_ref, cnt_ref, run_ref, tm):
    hn = _rms(h, g_ref[...])
    hn_bf = hn.astype(BF16)
    _store_token_slabs(hn_ref, _pack_rows(hn))
    hn_lo = (hn - hn_bf.astype(F32)).astype(BF16)
    both = _dot(hn_bf, whl_ref[...])
    logits = both[:, 0:LANES] + both[:, LANES:2 * LANES] + _dot(hn_lo, whl_ref[:, 0:LANES]) + b_ref[...]

    lane = lax.broadcasted_iota(I32, (tm, LANES), 1)
    lane_f = lane.astype(F32)
    neg = -jnp.inf
    big = float(LANES)
    gl = jnp.where(lane < MOE_GROUPS, logits, neg)
    gmax = jnp.max(gl, axis=-1, keepdims=True)
    gsel = jnp.min(jnp.where(gl == gmax, lane_f, big), axis=-1, keepdims=True).astype(I32)
    p_grp = 1.0 / jnp.sum(jnp.exp(gl - gmax), axis=-1, keepdims=True)
    lo_lane = MOE_GROUPS + MOE_EPG * gsel
    el = jnp.where((lane >= lo_lane) & (lane < lo_lane + MOE_EPG), logits, neg)
    v0 = jnp.max(el, axis=-1, keepdims=True)
    i0 = jnp.min(jnp.where(el == v0, lane_f, big), axis=-1, keepdims=True).astype(I32)
    el2 = jnp.where(lane == i0, neg, el)
    v1 = jnp.max(el2, axis=-1, keepdims=True)
    i1 = jnp.min(jnp.where(el2 == v1, lane_f, big), axis=-1, keepdims=True).astype(I32)
    t = jnp.exp(v1 - v0)
    gate0 = p_grp / (1.0 + t)
    gate1 = p_grp * t / (1.0 + t)
    e0 = i0 - MOE_GROUPS
    e1 = i1 - MOE_GROUPS

    onehot = ((lane == e0) | (lane == e1))
    c_bf = jnp.where(onehot, 1.0, 0.0).astype(BF16)
    prefix = _dot(tri_ref[...], c_bf) + run_ref[...]
    rank0 = jnp.sum(jnp.where(lane == e0, prefix, 0.0), axis=-1, keepdims=True).astype(I32)
    rank1 = jnp.sum(jnp.where(lane == e1, prefix, 0.0), axis=-1, keepdims=True).astype(I32)
    run = run_ref[...] + jnp.sum(jnp.where(onehot, 1.0, 0.0), axis=0, keepdims=True)
    run_ref[...] = run
    cnt_ref[...] = jnp.broadcast_to(run, cnt_ref.shape)

    ri = jnp.where(lane == 0, e0, jnp.where(lane == 1, e1, jnp.where(lane == 2, rank0, rank1)))
    ri_ref[...] = ri.T[0:8, :]
    rg_ref[...] = jnp.where(lane == 0, gate0, gate1)


def _route_operands(group_w, group_b, router_w, router_b):
    depth, d, _ = group_w.shape
    pad = LANES - MOE_GROUPS - MOE_EXPERTS
    wr = jnp.concatenate([group_w, router_w, jnp.zeros((depth, d, pad), F32)], axis=2)
    br = jnp.concatenate([group_b, router_b, jnp.zeros((depth, pad), F32)], axis=1).reshape(depth, 1, LANES)
    whi = wr.astype(BF16)
    wlo = (wr - whi.astype(F32)).astype(BF16)
    return jnp.concatenate([whi, wlo], axis=2), br


def _prefix_matrix(tm):
    return jnp.tril(jnp.ones((tm, tm), BF16), -1)


def _route_out(n, tm):
    shapes = (jax.ShapeDtypeStruct((n * SLAB, LANES), U32), jax.ShapeDtypeStruct((8, n), I32),
              jax.ShapeDtypeStruct((n, LANES), F32), jax.ShapeDtypeStruct((8, LANES), F32))
    specs = (pl.BlockSpec((tm * SLAB, LANES), lambda i: (i, 0)), pl.BlockSpec((8, tm), lambda i: (0, i)),
             pl.BlockSpec((tm, LANES), lambda i: (i, 0)), pl.BlockSpec((8, LANES), lambda i: (0, 0)))
    return shapes, specs


def _pool_route_kernel(*refs, with_combine, t, pg):
    refs = list(refs)
    if with_combine:
        dest_ref, dest_next_ref, rgp_ref, ys_ref = refs[:4]
        refs = refs[4:]
    h_ref, gmix_ref, pw_ref, scale_ref, gffn_ref, whl_ref, b_ref, tri_ref = refs[:8]
    o_ref, hn_ref, ri_ref, rg_ref, cnt_ref = refs[8:13]
    refs = refs[13:]
    if with_combine:
        buf00, buf01, buf10, buf11, sems, hbuf = refs[:6]
        refs = refs[6:]
    xbuf, sa, sb, wbf, run_ref = refs
    i = pl.program_id(0)

    @pl.when(i == 0)
    def _():
        _pool_init(pw_ref, xbuf, sa, sb, wbf)
        run_ref[...] = jnp.zeros_like(run_ref)

    if with_combine:
        _combine_tile(dest_ref, dest_next_ref, h_ref, rgp_ref, ys_ref, hbuf, ((buf00, buf01), (buf10, buf11)), sems, t)
        h = hbuf[...]
    else:
        h = h_ref[...]
    _pool_tile(h, i, gmix_ref, scale_ref, o_ref, xbuf, sa, sb, wbf, t, pg)
    _route_tile(o_ref[...], gffn_ref, whl_ref, b_ref, tri_ref, hn_ref, ri_ref, rg_ref, cnt_ref, run_ref, t)


def _pool_route(h, pending, g_mix, pool_w, pool_scale, j, g_ffn, route_ops, layer):
    n, d = h.shape
    nw, pg, _ = pool_w.shape[1:]
    whl, br = route_ops
    t = DISPATCH_TILE if pending is None else COMBINE_TILE
    tri = _prefix_matrix(t)
    assert d == 2 * SLAB * LANES
    last = n // t - 1
    row = lambda i: (i, 0)
    const = lambda i: (0, 0)
    in_specs, args, scratch = [], [], []
    if pending is not None:
        rg_prev, ys, dest = pending
        in_specs += [pl.BlockSpec((None, 1, 2 * t), lambda i: (i, 0, 0), memory_space=pltpu.SMEM),
                     pl.BlockSpec((None, 1, 2 * t), lambda i: (jnp.minimum(i + 1, last), 0, 0),
                                  memory_space=pltpu.SMEM),
                     pl.BlockSpec((t, LANES), row),
                     pl.BlockSpec(memory_space=pl.ANY)]
        args += [dest, dest, rg_prev, ys]
        scratch += [pltpu.VMEM((t * SLAB, LANES), U32)] * 4 + [pltpu.SemaphoreType.DMA((2,)), pltpu.VMEM((t, d), F32)]
    in_specs += [pl.BlockSpec((t, d), row), pl.BlockSpec((1, d), const),
                 pl.BlockSpec((None, nw, pg, pg), lambda i: (j, 0, 0, 0)), pl.BlockSpec((1, d), const),
                 pl.BlockSpec((1, d), const), pl.BlockSpec((None, d, 2 * LANES), lambda i: (layer, 0, 0)),
                 pl.BlockSpec((None, 1, LANES), lambda i: (layer, 0, 0)), pl.BlockSpec((t, t), const)]
    args += [h, g_mix.reshape(1, d), pool_w, pool_scale[j].reshape(1, d), g_ffn.reshape(1, d), whl, br, tri]
    r_shapes, r_specs = _route_out(n, t)
    scratch += [pltpu.VMEM((POOL_PAD + POOL_HALO + t, d), F32),
                pltpu.VMEM((POOL_PAD + POOL_HALO + t, pg), F32),
                pltpu.VMEM((POOL_PAD + POOL_HALO + t, pg), F32),
                pltpu.VMEM((nw, pg, pg), BF16),
                pltpu.VMEM((1, LANES), F32)]
    return pl.pallas_call(
        functools.partial(_pool_route_kernel, with_combine=pending is not None, t=t, pg=pg),
        out_shape=(jax.ShapeDtypeStruct((n, d), F32),) + r_shapes,
        grid=(n // t,),
        in_specs=in_specs,
        out_specs=(pl.BlockSpec((t, d), row),) + r_specs,
        scratch_shapes=scratch,
        compiler_params=_cparams(1),
        name="pool_route",
    )(*args)


def _route(h, g, route_ops, layer, tm=ROUTE_TILE):
    n, d = h.shape
    whl, br = route_ops
    tri = _prefix_matrix(tm)
    return pl.pallas_call(
        functools.partial(_route_kernel, tm=tm),
        out_shape=(jax.ShapeDtypeStruct((n * SLAB, LANES), U32),
                   jax.ShapeDtypeStruct((8, n), I32),
                   jax.ShapeDtypeStruct((n, LANES), F32),
                   jax.ShapeDtypeStruct((8, LANES), F32)),
        grid=(n // tm,),
        in_specs=[pl.BlockSpec((tm, d), lambda i: (i, 0)),
                  pl.BlockSpec((1, d), lambda i: (0, 0)),
                  pl.BlockSpec((None, d, 2 * LANES), lambda i: (layer, 0, 0)),
                  pl.BlockSpec((None, 1, LANES), lambda i: (layer, 0, 0)),
                  pl.BlockSpec((tm, tm), lambda i: (0, 0))],
        out_specs=(pl.BlockSpec((tm * SLAB, LANES), lambda i: (i, 0)),
                   pl.BlockSpec((8, tm), lambda i: (0, i)),
                   pl.BlockSpec((tm, LANES), lambda i: (i, 0)),
                   pl.BlockSpec((8, LANES), lambda i: (0, 0))),
        scratch_shapes=[pltpu.VMEM((1, LANES), F32)],
        compiler_params=_cparams(1),
        name="moe_route",
    )(h, g.reshape(1, d), whl, br, tri)


def _dispatch_kernel(plan_ref, dest_ref, hn_ref, xs_ref, zbuf, sem, zsem, *, t):
    sizes = [1 << s for s in reversed(range(MOE_ROWS.bit_length() - 1))]
    zrows = zbuf.shape[0] // SLAB
    n_blocks = xs_ref.shape[0] // (MOE_ROWS * SLAB)

    def zero_copy(off, size):
        return pltpu.make_async_copy(zbuf.at[pl.ds(0, size * SLAB)], _slab_run(xs_ref, off, size), zsem)

    def zero_fill(wait):
        def gaps(e, c):
            off = plan_ref[PLAN_PAD_START, e]
            n = plan_ref[PLAN_PAD_LEN, e]
            for size in sizes:
                @pl.when((n & size) != 0)
                def _():
                    cp = zero_copy(off, size)
                    cp.wait() if wait else cp.start()

                off = off + (n & size)
            return c

        def tail(b, c):
            for part in range(MOE_ROWS // zrows):
                cp = zero_copy(b * MOE_ROWS + part * zrows, zrows)
                cp.wait() if wait else cp.start()
            return c

        lax.fori_loop(0, MOE_EXPERTS, gaps, 0)
        lax.fori_loop(plan_ref[PLAN_NUSED, 0], n_blocks, tail, 0)

    @pl.when(pl.program_id(0) == 0)
    def _():
        zbuf[...] = jnp.zeros_like(zbuf)
        zero_fill(False)

    def row_copy(r, k):
        return pltpu.make_async_copy(_slab(hn_ref, r), _slab(xs_ref, dest_ref[0, k * t + r]), sem)

    def start(r, c):
        row_copy(r, 0).start(priority=0)
        row_copy(r, 1).start(priority=1)
        return c

    lax.fori_loop(0, t, start, 0, unroll=8)
    for _ in range(2):
        pltpu.make_async_copy(hn_ref, xs_ref.at[pl.ds(0, t * SLAB)], sem).wait()

    @pl.when(pl.program_id(0) == pl.num_programs(0) - 1)
    def _():
        zero_fill(True)


def _dispatch(hn, dest, plan, n_rows, t=DISPATCH_TILE):
    n = hn.shape[0] // SLAB
    grid_spec = pltpu.PrefetchScalarGridSpec(
        num_scalar_prefetch=1,
        grid=(n // t,),
        in_specs=[pl.BlockSpec((None, 1, 2 * t), lambda i, *_: (i, 0, 0), memory_space=pltpu.SMEM),
                  pl.BlockSpec((t * SLAB, LANES), lambda i, *_: (i, 0))],
        out_specs=pl.BlockSpec(memory_space=pl.ANY),
        scratch_shapes=[pltpu.VMEM((MOE_ROWS // 2 * SLAB, LANES), hn.dtype),
                        pltpu.SemaphoreType.DMA(()), pltpu.SemaphoreType.DMA(())],
    )
    return pl.pallas_call(
        functools.partial(_dispatch_kernel, t=t),
        out_shape=jax.ShapeDtypeStruct((n_rows * SLAB, LANES), hn.dtype),
        grid_spec=grid_spec,
        compiler_params=_cparams(1),
        name="moe_dispatch",
    )(plan, dest, hn)


def _expert_kernel(plan_ref, xs_ref, wg_hbm, wu_hbm, wd_hbm,
                   o_ref, wg_raw, wu_raw, wd_raw, wg_bf, wu_bf, wd_bf, sems, *, layer):
    b = pl.program_id(0)
    active = b < plan_ref[PLAN_NUSED, 0]
    expert = plan_ref[PLAN_EXPERT, b]
    second = plan_ref[PLAN_SECOND, 0]

    def fetch(e, slot):
        return [pltpu.make_async_copy(hbm.at[layer, e], raw.at[slot], sems.at[slot, k])
                for k, (hbm, raw) in enumerate(((wg_hbm, wg_raw), (wu_hbm, wu_raw), (wd_hbm, wd_raw)))]

    @pl.when(b == 0)
    def _():
        for c in fetch(expert, 0):
            c.start(priority=1)

        @pl.when(second >= 0)
        def _():
            for c in fetch(second, 1):
                c.start(priority=1)

    @pl.when(active & (plan_ref[PLAN_FIRST, b] == 1))
    def _():
        slot = plan_ref[PLAN_SLOT, b]
        ahead = plan_ref[PLAN_AHEAD, b]

        @pl.when(ahead >= 0)
        def _():
            for c in fetch(ahead, (slot + MOE_WSLOTS - 1) % MOE_WSLOTS):
                c.start(priority=1)

        for c in fetch(expert, slot):
            c.wait()
        wg_bf[...] = wg_raw[slot].astype(BF16)
        wu_bf[...] = wu_raw[slot].astype(BF16)
        wd_bf[...] = wd_raw[slot].astype(BF16)

    @pl.when(active)
    def _():
        rows = xs_ref.shape[0] // SLAB
        halves = [_unpack_rows(u) for u in _load_token_slabs(xs_ref, rows)]
        x = jnp.concatenate([lo.astype(BF16) for lo, _ in halves] + [hi.astype(BF16) for _, hi in halves], axis=1)
        gte = _dot(x, wg_bf[...])
        up = _dot(x, wu_bf[...])
        hb = _silu(gte) * up
        _store_token_slabs(o_ref, _pack_rows(_dot(hb.astype(BF16), wd_bf[...])))

    @pl.when(jnp.logical_not(active))
    def _():
        o_ref[...] = jnp.zeros_like(o_ref)


def _experts(xs, plan, gate_w, up_w, down_w, layer, rows=MOE_ROWS):
    n_rows = xs.shape[0] // SLAB
    d = gate_w.shape[-2]
    hid = gate_w.shape[-1]
    assert d == 2 * SLAB * LANES and MOE_WSLOTS == 3 and n_rows // rows <= LANES
    x_map = lambda b, plan_ref: (jnp.minimum(b, plan_ref[PLAN_NUSED, 0] - 1), 0)
    grid_spec = pltpu.PrefetchScalarGridSpec(
        num_scalar_prefetch=1,
        grid=(n_rows // rows,),
        in_specs=[pl.BlockSpec((rows * SLAB, LANES), x_map),
                  pl.BlockSpec(memory_space=pl.ANY),
                  pl.BlockSpec(memory_space=pl.ANY),
                  pl.BlockSpec(memory_space=pl.ANY)],
        out_specs=pl.BlockSpec((rows * SLAB, LANES), lambda b, *_: (b, 0)),
        scratch_shapes=[pltpu.VMEM((MOE_WSLOTS, d, hid), F32), pltpu.VMEM((MOE_WSLOTS, d, hid), F32),
                        pltpu.VMEM((MOE_WSLOTS, hid, d), F32),
                        pltpu.VMEM((d, hid), BF16), pltpu.VMEM((d, hid), BF16), pltpu.VMEM((hid, d), BF16),
                        pltpu.SemaphoreType.DMA((MOE_WSLOTS, 3))],
    )
    return pl.pallas_call(
        functools.partial(_expert_kernel, layer=layer),
        out_shape=jax.ShapeDtypeStruct((n_rows * SLAB, LANES), U32),
        grid_spec=grid_spec,
        compiler_params=_cparams(1),
        name="moe_experts",
    )(plan, xs, gate_w, up_w, down_w)


def _combine_kernel(*refs, t, mode):
    dest_ref, dest_next_ref, h_ref, rg_ref, ys_ref, g_ref, o_ref = refs[:7]
    rest = list(refs[7:])
    hn_ref = rest.pop(0) if mode == "norm_bf16" else None
    buf00, buf01, buf10, buf11, sems = rest
    _combine_tile(dest_ref, dest_next_ref, h_ref, rg_ref, ys_ref, o_ref, ((buf00, buf01), (buf10, buf11)), sems, t)
    normed = _rms(o_ref[...], g_ref[...])
    if mode == "norm_bf16":
        hn_ref[...] = normed.astype(BF16)
    else:
        o_ref[...] = normed


def _combine_tile(dest_ref, dest_next_ref, h_ref, rg_ref, ys_ref, o_ref, bufs, sems, t):
    i = pl.program_id(0)
    n = pl.num_programs(0)
    half = SLAB * LANES

    def row_copy(dref, slot, r, k):
        return pltpu.make_async_copy(_slab(ys_ref, dref[0, k * t + r]), _slab(bufs[slot][k], r), sems.at[slot])

    def issue(dref, slot):
        def start(r, c):
            row_copy(dref, slot, r, 0).start(priority=0)
            row_copy(dref, slot, r, 1).start(priority=1)
            return c
        lax.fori_loop(0, t, start, 0, unroll=8)

    def drain(slot):
        for k in range(2):
            pltpu.make_async_copy(ys_ref.at[pl.ds(0, t * SLAB)], bufs[slot][k], sems.at[slot]).wait()

    @pl.when(i == 0)
    def _():
        issue(dest_ref, 0)

    for slot in (0, 1):
        @pl.when((i % 2 == slot) & (i + 1 < n))
        def _():
            issue(dest_next_ref, 1 - slot)

        @pl.when(i % 2 == slot)
        def _():
            drain(slot)
            rg = rg_ref[...]
            g0 = rg[:, 0:1]
            g1 = rg[:, 1:2]
            for c, (ua, ub) in enumerate(zip(_load_token_slabs(bufs[slot][0], t), _load_token_slabs(bufs[slot][1], t))):
                a_lo, a_hi = _unpack_rows(ua)
                b_lo, b_hi = _unpack_rows(ub)
                lo_cols = slice(c * LANES, (c + 1) * LANES)
                hi_cols = slice(half + c * LANES, half + (c + 1) * LANES)
                o_ref[:, lo_cols] = h_ref[:, lo_cols] + (g0 * a_lo + g1 * b_lo)
                o_ref[:, hi_cols] = h_ref[:, hi_cols] + (g0 * a_hi + g1 * b_hi)


def _combine(h, rg, ys, dest, g_next, mode, t=COMBINE_TILE):
    n, d = h.shape
    assert d == 2 * SLAB * LANES
    last = n // t - 1
    in_specs = [pl.BlockSpec((None, 1, 2 * t), lambda i: (i, 0, 0), memory_space=pltpu.SMEM),
                pl.BlockSpec((None, 1, 2 * t), lambda i: (jnp.minimum(i + 1, last), 0, 0),
                             memory_space=pltpu.SMEM),
                pl.BlockSpec((t, d), lambda i: (i, 0)),
                pl.BlockSpec((t, LANES), lambda i: (i, 0)),
                pl.BlockSpec(memory_space=pl.ANY),
                pl.BlockSpec((1, d), lambda i: (0, 0))]
    args = [dest, dest, h, rg, ys, g_next.reshape(1, d)]
    tile = pl.BlockSpec((t, d), lambda i: (i, 0))
    assert mode in ("norm_bf16", "final")
    if mode == "norm_bf16":
        out_shape = (jax.ShapeDtypeStruct((n, d), F32), jax.ShapeDtypeStruct((n, d), BF16))
        out_specs = (tile, tile)
    else:
        out_shape = jax.ShapeDtypeStruct((n, d), F32)
        out_specs = tile
    return pl.pallas_call(
        functools.partial(_combine_kernel, t=t, mode=mode),
        out_shape=out_shape,
        grid=(n // t,),
        in_specs=in_specs,
        out_specs=out_specs,
        scratch_shapes=[pltpu.VMEM((t * SLAB, LANES), U32)] * 4 + [pltpu.SemaphoreType.DMA((2,))],
        compiler_params=_cparams(1),
        name="moe_combine",
    )(*args)


def _moe_experts_path(hn, ri, cnt, layer, gate_w, up_w, down_w):
    n = hn.shape[0] // SLAB
    n_rows = -(-(n * 2 + MOE_EXPERTS * (MOE_ROWS - 1)) // MOE_ROWS) * MOE_ROWS
    dest, plan = _moe_plan(ri, cnt)
    xs = _dispatch(hn, _dest_tiles(dest, DISPATCH_TILE), plan, n_rows)
    return _experts(xs, plan, gate_w, up_w, down_w, layer), _dest_tiles(dest, COMBINE_TILE)


def _dest_tiles(dest, t):
    n = dest.shape[1]
    return dest[0:2].reshape(2, n // t, t).transpose(1, 0, 2).reshape(n // t, 1, 2 * t)


PLAN_EXPERT = 0
PLAN_FIRST = 1
PLAN_SLOT = 2
PLAN_AHEAD = 3
PLAN_SECOND = 4
PLAN_NUSED = 5
PLAN_PAD_START = 6
PLAN_PAD_LEN = 7


def _plan_kernel(ri_ref, cnt_ref, tri_ref, dest_ref, plan_ref, *, chunk):
    e_lane = lax.broadcasted_iota(I32, (LANES, LANES), 1)
    e_sub = lax.broadcasted_iota(I32, (LANES, LANES), 0)
    tri = tri_ref[...]
    rows = float(MOE_ROWS)

    cnt_l = jnp.where(e_lane < MOE_EXPERTS, jnp.broadcast_to(cnt_ref[0:1, :], (LANES, LANES)), 0.0)
    cnt_s = cnt_l.T
    nblk_l = jnp.floor((cnt_l + (rows - 1.0)) * (1.0 / rows))
    nblk_s = jnp.floor((cnt_s + (rows - 1.0)) * (1.0 / rows))
    end_l = _dot_nt(nblk_l.astype(BF16), tri)
    end_s = _dot(tri, nblk_s.astype(BF16))
    start_l = end_l - nblk_l
    start_s = end_s - nblk_s
    row_start_l = start_l * rows
    n_used = end_l[0:1, LANES - 1:LANES]

    row_start_bf = row_start_l[0:SLAB, :].astype(BF16)
    n = ri_ref.shape[1]
    for c in range(n // chunk):
        cols = slice(c * chunk, (c + 1) * chunk)
        parts = []
        for k in range(2):
            onehot = (lax.broadcasted_iota(I32, (LANES, chunk), 0) == ri_ref[k:k + 1, cols])
            base = _dot(row_start_bf, jnp.where(onehot, 1.0, 0.0).astype(BF16))[0:1, :]
            parts.append(base.astype(I32) + ri_ref[2 + k:3 + k, cols])
        dest_ref[:, cols] = jnp.concatenate(parts + [jnp.zeros((SLAB - 2, chunk), I32)], axis=0)

    blk = e_lane.astype(F32)
    valid = e_sub < MOE_EXPERTS
    block_exp = jnp.sum(jnp.where(valid & (end_s <= blk), 1.0, 0.0), axis=0, keepdims=True)
    block_exp = jnp.minimum(block_exp, float(MOE_EXPERTS - 1))
    mine = e_sub.astype(F32) == block_exp
    blk_row = blk[0:1, :]
    first = (blk_row == jnp.sum(jnp.where(mine, start_s, 0.0), axis=0, keepdims=True)) & (blk_row < n_used)
    nonempty_s = nblk_s > 0.0
    ordinal_s = _dot(tri, jnp.where(nonempty_s, 1.0, 0.0).astype(BF16)) - 1.0
    pos = jnp.sum(jnp.where(mine, ordinal_s, 0.0), axis=0, keepdims=True)
    slot = pos - MOE_WSLOTS * jnp.floor((pos + 0.5) * (1.0 / MOE_WSLOTS))

    def expert_at(target):
        hit = nonempty_s & (ordinal_s == target)
        found = jnp.sum(jnp.where(hit, 1.0, 0.0), axis=0, keepdims=True) > 0.0
        return jnp.where(found, jnp.sum(jnp.where(hit, e_sub.astype(F32), 0.0), axis=0, keepdims=True), -1.0)

    ahead = expert_at(pos + (MOE_WSLOTS - 1.0))
    second = expert_at(jnp.ones((1, LANES), F32))
    pad_start = (row_start_l + cnt_l)[0:1, :]
    pad_len = (nblk_l * rows - cnt_l)[0:1, :]
    table = [block_exp, jnp.where(first, 1.0, 0.0), slot, ahead, second,
             jnp.broadcast_to(n_used, (1, LANES)), pad_start, pad_len]
    plan_ref[...] = jnp.concatenate(table, axis=0).astype(I32)


def _moe_plan(ri, cnt, chunk=PLAN_CHUNK):
    n = ri.shape[1]
    tri = jnp.tril(jnp.ones((LANES, LANES), BF16))
    return pl.pallas_call(
        functools.partial(_plan_kernel, chunk=chunk),
        out_shape=(jax.ShapeDtypeStruct((SLAB, n), I32), jax.ShapeDtypeStruct((SLAB, LANES), I32)),
        grid=(1,),
        in_specs=[pl.BlockSpec((SLAB, n), lambda i: (0, 0)), pl.BlockSpec((SLAB, LANES), lambda i: (0, 0)),
                  pl.BlockSpec((LANES, LANES), lambda i: (0, 0))],
        out_specs=(pl.BlockSpec((SLAB, n), lambda i: (0, 0)), pl.BlockSpec((SLAB, LANES), lambda i: (0, 0))),
        compiler_params=_cparams(1),
        name="moe_plan",
    )(ri, cnt, tri)


def _ssd_kernel(z_ref, x_ref, b_ref, c_ref, dt_ref, cwx_ref, cwb_ref, cwc_ref, cbx_ref, cbb_ref, cbc_ref,
                dtb_ref, a_ref, dexp_ref, nw_ref, exp_ref, tri_ref, o_ref,
                xbuf, bbuf, cbuf, xc_s, xdt_s, xw_s, b_s, c_s, eacs_s, acs_s, acst_s, y_s, st_s,
                *, L, hp, n_state, groups):
    ci = pl.program_id(0)
    d_inner = x_ref.shape[1]
    gw = d_inner // groups
    kh = gw // hp
    H0 = CONV_HALO

    @pl.when(ci == 0)
    def _():
        xbuf[0:H0, :] = jnp.zeros((H0, xbuf.shape[1]), F32)
        bbuf[0:H0, :] = jnp.zeros((H0, bbuf.shape[1]), F32)
        cbuf[0:H0, :] = jnp.zeros((H0, cbuf.shape[1]), F32)
        st_s[...] = jnp.zeros_like(st_s)

    def conv_silu(in_ref, buf, cw_ref, cb_ref):
        buf[H0:H0 + L, :] = in_ref[...].astype(F32)
        acc = cb_ref[...] + cw_ref[SSD_CONV - 1:SSD_CONV, :] * buf[H0:H0 + L, :]
        for j in range(SSD_CONV - 1):
            off = H0 - (SSD_CONV - 1) + j
            acc = acc + cw_ref[j:j + 1, :] * buf[off:off + L, :]
        buf[0:H0, :] = buf[L:L + H0, :]
        return _silu(acc)

    xc_s[...] = conv_silu(x_ref, xbuf, cwx_ref, cbx_ref)
    b_s[...] = conv_silu(b_ref, bbuf, cwb_ref, cbb_ref).astype(BF16)
    c_s[...] = conv_silu(c_ref, cbuf, cwc_ref, cbc_ref).astype(BF16)

    dt = _softplus(dt_ref[...] + dtb_ref[...])
    d_a = dt * a_ref[...]
    acs = _dot3_lhs_exact(tri_ref[...], d_a)
    acs_s[...] = acs
    acst_s[...] = acs.T
    emat = exp_ref[...]
    acs_last = acs[L - 1:L, :]
    to_end = dt * jnp.exp(acs_last - acs)
    xc = xc_s[...]
    xdt_s[...] = (xc * _dot2_rhs_exact(dt, emat)).astype(BF16)
    xw_s[...] = (xc * _dot2_rhs_exact(to_end, emat)).astype(BF16)
    eacs_s[...] = _dot2_rhs_exact(jnp.exp(acs), emat)
    cdecay = _dot2_rhs_exact(jnp.broadcast_to(jnp.exp(acs_last), (SLAB, LANES)), emat)[0:1, :]

    rr = lax.broadcasted_iota(I32, (L, L), 0)
    cc = lax.broadcasted_iota(I32, (L, L), 1)
    causal = rr >= cc
    lane2 = lax.broadcasted_iota(I32, (L, 2 * hp), 1)

    for g in range(groups):
        bg = b_s[:, g * n_state:(g + 1) * n_state]
        cg = c_s[:, g * n_state:(g + 1) * n_state]
        cb = _dot_nt(cg, bg)
        st_prev = st_s[g]
        y_off = _dot(cg, st_prev.astype(BF16)) * eacs_s[:, g * gw:(g + 1) * gw]
        for kp in range(kh // 2):
            c0 = g * gw + kp * 2 * hp
            slab = xdt_s[:, c0:c0 + 2 * hp]
            ms, xs = [], []
            for half in range(2):
                hd = g * kh + kp * 2 + half
                seg = acs_s[:, hd:hd + 1] - acst_s[hd:hd + 1, :]
                decay = jnp.exp(jnp.where(causal, seg, -jnp.inf))
                ms.append((cb * decay).astype(BF16))
                keep = (lane2 < hp) if half == 0 else (lane2 >= hp)
                xs.append(jnp.where(keep, slab, jnp.zeros_like(slab)))
            y_pair = _dot(jnp.concatenate(ms, axis=1), jnp.concatenate(xs, axis=0))
            y_s[:, c0:c0 + 2 * hp] = y_pair + y_off[:, kp * 2 * hp:(kp + 1) * 2 * hp]
        bgt = bg.astype(F32).T.astype(BF16)
        s_new = _dot(bgt, xw_s[:, g * gw:(g + 1) * gw])
        st_s[g] = cdecay[:, g * gw:(g + 1) * gw] * st_prev + s_new

    y = y_s[...] + dexp_ref[...] * xc_s[...]
    z = z_ref[...].astype(F32)
    y = y * _silu(z)
    for g in range(groups):
        yg = y[:, g * gw:(g + 1) * gw]
        yg = yg * lax.rsqrt(jnp.mean(yg * yg, axis=-1, keepdims=True) + RMS_EPS)
        o_ref[:, g * gw:(g + 1) * gw] = (yg * nw_ref[:, g * gw:(g + 1) * gw]).astype(o_ref.dtype)


def _ssd_scan(zx, dt_raw, conv_w, conv_b, dt_bias, a_log, d_skip, norm_w, d_inner):
    s = zx.shape[0]
    L = SSD_CHUNK
    hp, n_state, groups = SSD_HEAD_DIM, SSD_STATE, SSD_GROUPS
    heads = d_inner // hp
    gn = groups * n_state
    gw = d_inner // groups
    assert d_inner % gn == 0 and heads <= LANES
    pad = LANES - heads
    dtb = jnp.pad(dt_bias, (0, pad)).reshape(1, LANES)
    a_neg = jnp.pad(-jnp.exp(a_log), (0, pad)).reshape(1, LANES)
    dexp = jnp.repeat(d_skip, hp).reshape(1, d_inner)
    emat = (jnp.arange(LANES)[:, None] == (jnp.arange(d_inner)[None, :] // hp)).astype(BF16)
    tri = jnp.tril(jnp.ones((L, L), BF16))
    cb2 = conv_b.reshape(1, -1)
    xb = d_inner // gn
    row = lambda c: (c, 0)
    const = lambda c: (0, 0)
    in_specs = [
        pl.BlockSpec((L, d_inner), row),
        pl.BlockSpec((L, d_inner), lambda c: (c, 1)),
        pl.BlockSpec((L, gn), lambda c: (c, 2 * xb)),
        pl.BlockSpec((L, gn), lambda c: (c, 2 * xb + 1)),
        pl.BlockSpec((L, LANES), row),
        pl.BlockSpec((SSD_CONV, d_inner), const),
        pl.BlockSpec((SSD_CONV, gn), lambda c: (0, xb)),
        pl.BlockSpec((SSD_CONV, gn), lambda c: (0, xb + 1)),
        pl.BlockSpec((1, d_inner), const),
        pl.BlockSpec((1, gn), lambda c: (0, xb)),
        pl.BlockSpec((1, gn), lambda c: (0, xb + 1)),
        pl.BlockSpec((1, LANES), const),
        pl.BlockSpec((1, LANES), const),
        pl.BlockSpec((1, d_inner), const),
        pl.BlockSpec((1, d_inner), const),
        pl.BlockSpec((LANES, d_inner), const),
        pl.BlockSpec((L, L), const),
    ]
    scratch = [
        pltpu.VMEM((CONV_HALO + L, d_inner), F32), pltpu.VMEM((CONV_HALO + L, gn), F32),
        pltpu.VMEM((CONV_HALO + L, gn), F32),
        pltpu.VMEM((L, d_inner), F32),
        pltpu.VMEM((L, d_inner), BF16),
        pltpu.VMEM((L, d_inner), BF16),
        pltpu.VMEM((L, gn), BF16), pltpu.VMEM((L, gn), BF16),
        pltpu.VMEM((L, d_inner), F32),
        pltpu.VMEM((L, LANES), F32), pltpu.VMEM((LANES, L), F32),
        pltpu.VMEM((L, d_inner), F32),
        pltpu.VMEM((groups, n_state, gw), F32),
    ]
    return pl.pallas_call(
        functools.partial(_ssd_kernel, L=L, hp=hp, n_state=n_state, groups=groups),
        out_shape=jax.ShapeDtypeStruct((s, d_inner), BF16),
        grid=(s // L,),
        in_specs=in_specs,
        out_specs=pl.BlockSpec((L, d_inner), row),
        scratch_shapes=scratch,
        compiler_params=_cparams(1),
        name="ssd_scan",
    )(zx, zx, zx, zx, dt_raw, conv_w, conv_w, conv_w, cb2, cb2, cb2, dtb, a_neg, dexp,
      norm_w.reshape(1, d_inner), emat, tri)


def _ssd_mixer(h, hn, j, in_w, conv_w, conv_b, dt_bias, a_log, d_skip, norm_w, out_w):
    d_inner = out_w.shape[1]
    heads = d_inner // SSD_HEAD_DIM
    wide = in_w.shape[2] - heads
    wt = jnp.swapaxes(in_w, 1, 2)
    zx = _matmul_t(hn, wt, layer=j, row0=0, nrows=wide, tn=MM_TN, tm=MM_TM_IN, out_dtype=BF16, name="ssd_in_proj")
    dt_raw = _matmul_t(hn, wt, layer=j, row0=wide, nrows=heads, tn=heads, tm=MM_TM, name="ssd_dt_proj")
    y = _ssd_scan(zx, dt_raw, conv_w, conv_b, dt_bias, a_log, d_skip, norm_w, d_inner)
    return _matmul(y, out_w, layer=j, tn=MM_TN // 2, tm=MM_TM, residual=h, name="ssd_out_proj")


def _fox_cum_kernel(f_ref, b_ref, tri_ref, o_ref, carry):
    @pl.when(pl.program_id(0) == 0)
    def _():
        carry[...] = jnp.zeros_like(carry)

    x = f_ref[...] + b_ref[...]
    log_f = jnp.minimum(x, 0.0) - jnp.log1p(jnp.exp(-jnp.abs(x)))
    cs = _dot3_rhs_exact(log_f, tri_ref[...]) + carry[...]
    o_ref[...] = cs
    carry[...] = cs[:, cs.shape[1] - 1:]


def _fox_cum(f_t, bias, tk=CUM_TILE):
    nh, s = f_t.shape
    triu = jnp.triu(jnp.ones((tk, tk), BF16))
    return pl.pallas_call(
        _fox_cum_kernel,
        out_shape=jax.ShapeDtypeStruct((nh, s), F32),
        grid=(s // tk,),
        in_specs=[pl.BlockSpec((nh, tk), lambda i: (0, i)),
                  pl.BlockSpec((nh, 1), lambda i: (0, 0)),
                  pl.BlockSpec((tk, tk), lambda i: (0, 0))],
        out_specs=pl.BlockSpec((nh, tk), lambda i: (0, i)),
        scratch_shapes=[pltpu.VMEM((nh, 1), F32)],
        compiler_params=_cparams(1),
        name="fox_cum",
    )(f_t, bias.reshape(nh, 1), triu)


def _fox_attn_kernel(q_ref, k_ref, v_ref, cum_ref, o_ref, m_s, acc_s, vext, *, tq, tk, sr):
    qi = pl.program_id(1)
    dh = q_ref.shape[1]
    n_sub = tq // sr

    @pl.when(qi == 0)
    def _():
        vext[:, 0:dh] = v_ref[...]
        vext[:, dh:2 * dh] = jnp.ones((vext.shape[0], dh), BF16)

    q0 = pl.multiple_of(qi * tq, tq)
    c_ref = cum_ref[:, pl.ds(q0, LANES)][:, 0:1]
    m_s[...] = jnp.full_like(m_s, -jnp.inf)
    acc_s[...] = jnp.zeros_like(acc_s)

    def sub_step(r, k0, width, bias, diagonal):
        rows = slice(r * sr, (r + 1) * sr)
        s = _dot_nt(q_ref[rows, :], k_ref[pl.ds(k0, width), :]) + bias
        if diagonal:
            keep = lax.broadcasted_iota(I32, (sr, width), 0) >= lax.broadcasted_iota(I32, (sr, width), 1)
            s = jnp.where(keep, s, -jnp.inf)
        m_prev = m_s[rows, :]
        m_new = jnp.maximum(m_prev, jnp.max(s, axis=-1, keepdims=True))
        alpha = jnp.exp2(m_prev - m_new)
        p = jnp.exp2(s - jnp.concatenate([m_new] * (width // dh), axis=1)).astype(BF16)
        pv = _dot(p, vext[pl.ds(k0, width), :])
        acc_s[rows, :] = jnp.concatenate([alpha, alpha], axis=1) * acc_s[rows, :] + pv
        m_s[rows, :] = m_new

    def key_bias(k0, width):
        return (c_ref - cum_ref[:, pl.ds(k0, width)]) * LOG2E

    def body(j, c):
        k0 = pl.multiple_of(j * tk, tk)
        bias = key_bias(k0, tk)
        for r in range(n_sub):
            sub_step(r, k0, tk, bias, False)
        return c

    lax.fori_loop(0, qi * (tq // tk), body, 0)
    per = tk // sr
    for c in range(tq // tk):
        kc = pl.multiple_of(q0 + c * tk, tk)
        bias = key_bias(kc, tk)
        for r in range((c + 1) * per, n_sub):
            sub_step(r, kc, tk, bias, False)
        for r in range(c * per, (c + 1) * per):
            w = (r - c * per) * sr
            if w > 0:
                sub_step(r, kc, w, bias[:, 0:w], False)
            sub_step(r, pl.multiple_of(kc + w, sr), sr, bias[:, w:w + sr], True)
    acc = acc_s[...]
    o_ref[...] = (acc[:, 0:dh] / acc[:, dh:2 * dh]).astype(o_ref.dtype)


def _fox_attention(qkv, cum, nh, tq=FOX_TQ, tk=FOX_TK, sr=FOX_SUB):
    s = qkv.shape[0]
    dh = FOX_HEAD_DIM
    assert s % tq == 0 and tq % tk == 0 and tq % sr == 0 and tk % dh == 0 and dh == LANES
    cum3 = cum.reshape(nh, 1, s)
    return pl.pallas_call(
        functools.partial(_fox_attn_kernel, tq=tq, tk=tk, sr=sr),
        out_shape=jax.ShapeDtypeStruct((s, nh * dh), BF16),
        grid=(nh, s // tq),
        in_specs=[pl.BlockSpec((tq, dh), lambda h, i: (i, h)),
                  pl.BlockSpec((s, dh), lambda h, i: (0, nh + h)),
                  pl.BlockSpec((s, dh), lambda h, i: (0, 2 * nh + h)),
                  pl.BlockSpec((None, 1, s), lambda h, i: (h, 0, 0))],
        out_specs=pl.BlockSpec((tq, dh), lambda h, i: (i, h)),
        scratch_shapes=[pltpu.VMEM((tq, dh), F32), pltpu.VMEM((tq, 2 * dh), F32), pltpu.VMEM((s, 2 * dh), BF16)],
        compiler_params=_cparams(2),
        name="fox_attention",
    )(qkv, qkv, qkv, cum3)


def _fox_mixer(h, hn, j, in_w, f_bias, out_w):
    d = hn.shape[1]
    nh = d // FOX_HEAD_DIM
    q_scale = FOX_HEAD_DIM ** -0.5 * LOG2E
    colscale = jnp.concatenate([jnp.full((1, d), q_scale, F32), jnp.ones((1, 2 * d), F32)], axis=1)
    wt = jnp.swapaxes(in_w, 1, 2)
    qkv = _matmul_t(hn, wt, layer=j, row0=0, nrows=3 * d, tn=MM_TN, tm=MM_TM_IN, out_dtype=BF16, colscale=colscale,
                    name="fox_in_proj")
    f_raw = _matmul_t(hn, wt, layer=j, row0=3 * d, nrows=nh, tn=nh, tm=MM_TM, name="fox_f_proj")
    cum = _fox_cum(f_raw[:, :nh].T, f_bias)
    o = _fox_attention(qkv, cum, nh)
    return _matmul(o, out_w, layer=j, tn=MM_TN, tm=MM_TM, residual=h, name="fox_out_proj")


def _norm_kernel(h_ref, g_ref, o_ref):
    o_ref[...] = _rms(h_ref[...], g_ref[...]).astype(o_ref.dtype)


def _norm(h, g, out_dtype, tm=ROUTE_TILE):
    n, d = h.shape
    return pl.pallas_call(
        _norm_kernel,
        out_shape=jax.ShapeDtypeStruct((n, d), out_dtype),
        grid=(n // tm,),
        in_specs=[pl.BlockSpec((tm, d), lambda i: (i, 0)), pl.BlockSpec((1, d), lambda i: (0, 0))],
        out_specs=pl.BlockSpec((tm, d), lambda i: (i, 0)),
        compiler_params=_cparams(1),
        name="rmsnorm",
    )(h, g.reshape(1, d))


def kernel(x, norm_mix, norm_ffn, final_norm, pool_w, pool_scale, ssd_in_w, ssd_conv_w, ssd_conv_b,
           ssd_dt_bias, ssd_a_log, ssd_d_skip, ssd_norm, ssd_out_w, fox_in_w, fox_f_bias, fox_out_w,
           moe_group_w, moe_group_b, moe_router_w, moe_router_b, moe_gate_w, moe_up_w, moe_down_w):
    bsz, s, d = x.shape
    assert bsz == 1
    depth = norm_mix.shape[0]
    h = x.reshape(s, d)
    hn = None
    pending = None
    ops = _route_operands(moe_group_w, moe_group_b, moe_router_w, moe_router_b)
    for i in range(depth):
        kind, j = i % 3, i // 3
        if kind == 0:
            h, hn_rows, ri, rg, cnt = _pool_route(h, pending, norm_mix[i], pool_w, pool_scale, j, norm_ffn[i], ops, i)
            pending = None
        else:
            if hn is None:
                hn = _norm(h, norm_mix[i], BF16)
            if kind == 1:
                h = _ssd_mixer(h, hn, j, ssd_in_w, ssd_conv_w[j], ssd_conv_b[j], ssd_dt_bias[j], ssd_a_log[j],
                               ssd_d_skip[j], ssd_norm[j], ssd_out_w)
            else:
                h = _fox_mixer(h, hn, j, fox_in_w, fox_f_bias[j], fox_out_w)
            hn_rows, ri, rg, cnt = _route(h, norm_ffn[i], ops, i)
        ys, dest = _moe_experts_path(hn_rows, ri, cnt, i, moe_gate_w, moe_up_w, moe_down_w)
        hn = None
        if i == depth - 1:
            h = _combine(h, rg, ys, dest, final_norm, "final")
        elif (i + 1) % 3 == 0:
            pending = (rg, ys, dest)
        else:
            h, hn = _combine(h, rg, ys, dest, norm_mix[i + 1], "norm_bf16")
    return h.reshape(bsz, s, d)
```
